```python
import math
import jax, jax.numpy as jnp
from jax import lax
import numpy as np

D_MODEL = 1024
BATCH = 32
SEQ = 256
DEPTH = 2
DEC_BATCH = 4
DEC_SEQ = 2048
PAST_LEN = 512

GRID_W = 64
HEAD_DIM = 64
BRANCH_W = 256
N_BRANCH = 4
Q_BLOCK = 128
A_HEADS = 4
A_KV_HEADS = 2
A_GROUP = A_HEADS // A_KV_HEADS
A_WINDOW = 128
MLA_HEADS = 4
MLA_Q_LORA = 256
MLA_KV_LORA = 128
MLA_NOPE = 64
MLA_ROPE = 32
MLA_V = 64
S5_CH = 16
S5_GROUPS = BRANCH_W // S5_CH
S5_STATE = 64
NAT_HEADS = 4
NAT_ROWS = 8
NAT_COLS = 16
ROPE_BASE = 10000.0
EPS = 1e-6
NEG = -1e30
IN_SIZES = (A_HEADS * HEAD_DIM, A_KV_HEADS * HEAD_DIM, A_KV_HEADS * HEAD_DIM, BRANCH_W,
            MLA_Q_LORA, MLA_KV_LORA, MLA_ROPE, BRANCH_W,
            BRANCH_W, BRANCH_W,
            NAT_HEADS * HEAD_DIM, NAT_HEADS * HEAD_DIM, NAT_HEADS * HEAD_DIM, BRANCH_W)
IN_COLS = sum(IN_SIZES)

kernel_name = 'hybrid_diffusion_prefix_trunk'


def rmsnorm(x, g):
    x32 = x.astype(jnp.float32)
    y = x32 * lax.rsqrt(jnp.mean(x32 * x32, axis=-1, keepdims=True) + EPS)
    return (y * g.astype(jnp.float32)).astype(x.dtype)


def rope_1d(x, pos):
    half = x.shape[-1] // 2
    inv = ROPE_BASE ** (-jnp.arange(half, dtype=jnp.float32) / half)
    ang = pos[:, None] * inv[None, :]
    cos = jnp.cos(ang)[:, None, :]
    sin = jnp.sin(ang)[:, None, :]
    x32 = x.astype(jnp.float32)
    x1, x2 = x32[..., :half], x32[..., half:]
    return jnp.concatenate([x1 * cos - x2 * sin, x1 * sin + x2 * cos], axis=-1).astype(x.dtype)


def axial_rope(x):
    L = x.shape[1]
    t = jnp.arange(L)
    row = (t // GRID_W).astype(jnp.float32)
    col = (t % GRID_W).astype(jnp.float32)
    half = x.shape[-1] // 2
    return jnp.concatenate([rope_1d(x[..., :half], row), rope_1d(x[..., half:], col)], axis=-1)


def softmax_sink(s, sink_col):
    m = jnp.maximum(jnp.max(s, axis=-1, keepdims=True), sink_col)
    e = jnp.exp(s - m)
    return e / (jnp.sum(e, axis=-1, keepdims=True) + jnp.exp(sink_col - m))


def dense_attention(q, k, v, sink=None):
    B, Lq, Hk, G, dq = q.shape
    nb = Lq // Q_BLOCK
    scale = dq ** -0.5
    qb = jnp.moveaxis(q.reshape(B, nb, Q_BLOCK, Hk, G, dq), 1, 0)

    def block(qi):
        s = jnp.einsum('bqhgd,bkhd->bhgqk', qi, k).astype(jnp.float32) * scale
        if sink is None:
            p = jax.nn.softmax(s, axis=-1)
        else:
            p = softmax_sink(s, sink.astype(jnp.float32).reshape(Hk, G)[None, :, :, None, None])
        return jnp.einsum('bhgqk,bkhd->bqhgd', p.astype(v.dtype), v)

    o = lax.map(block, qb)
    return jnp.moveaxis(o, 0, 1).reshape(B, Lq, Hk * G * v.shape[-1])


def window_attention_latent(q, k, v, k_ctx, v_ctx, sink):
    B, L, Hk, G, d = q.shape
    nb = L // Q_BLOCK
    scale = d ** -0.5
    pad = ((0, 0), (Q_BLOCK, Q_BLOCK), (0, 0), (0, 0))
    kp = jnp.pad(k, pad).reshape(B, nb + 2, Q_BLOCK, Hk, d)
    vp = jnp.pad(v, pad).reshape(B, nb + 2, Q_BLOCK, Hk, d)
    kw = jnp.concatenate([kp[:, :-2], kp[:, 1:-1], kp[:, 2:]], axis=2)
    vw = jnp.concatenate([vp[:, :-2], vp[:, 1:-1], vp[:, 2:]], axis=2)
    blk = jnp.arange(nb)[:, None] * Q_BLOCK
    qpos = blk + jnp.arange(Q_BLOCK)[None, :]
    kpos = blk - Q_BLOCK + jnp.arange(3 * Q_BLOCK)[None, :]
    mask = ((jnp.abs(qpos[:, :, None] - kpos[:, None, :]) <= A_WINDOW)
            & (kpos >= 0)[:, None, :] & (kpos < L)[:, None, :])
    qb = q.reshape(B, nb, Q_BLOCK, Hk, G, d)
    s_win = jnp.einsum('bnqhgd,bnkhd->bnhgqk', qb, kw).astype(jnp.float32) * scale
    s_win = jnp.where(mask[None, :, None, None], s_win, NEG)
    s_ctx = jnp.einsum('bnqhgd,bkhd->bnhgqk', qb, k_ctx).astype(jnp.float32) * scale
    sink_col = sink.astype(jnp.float32).reshape(Hk, G)[None, None, :, :, None, None]
    p = softmax_sink(jnp.concatenate([s_win, s_ctx], axis=-1), sink_col).astype(v.dtype)
    nw = 3 * Q_BLOCK
    o = (jnp.einsum('bnhgqk,bnkhd->bnqhgd', p[..., :nw], vw)
         + jnp.einsum('bnhgqk,bkhd->bnqhgd', p[..., nw:], v_ctx))
    return o.reshape(B, L, Hk * G * d)


def neighbourhood_attention_latent(q, k, v, k_ctx, v_ctx, rpb):
    B, L, H, d = q.shape
    rows = L // GRID_W
    kr_n = min(NAT_ROWS, rows)
    scale = d ** -0.5
    qg = q.reshape(B, rows, GRID_W, H, d)
    kg = k.reshape(B, rows, GRID_W, H, d)
    vg = v.reshape(B, rows, GRID_W, H, d)
    r = jnp.arange(rows)
    row_start = jnp.clip(r - kr_n // 2, 0, rows - kr_n)
    row_idx = row_start[:, None] + jnp.arange(kr_n)[None, :]
    kr = kg[:, row_idx]
    vr = vg[:, row_idx]
    cidx = jnp.arange(GRID_W)
    col_start = jnp.clip(cidx - NAT_COLS // 2, 0, GRID_W - NAT_COLS)
    rel = cidx[None, :] - col_start[:, None]
    col_mask = (rel >= 0) & (rel < NAT_COLS)
    col_off = jnp.clip(cidx[None, :] - cidx[:, None], -(NAT_COLS - 1), NAT_COLS - 1) + NAT_COLS - 1
    row_off = row_idx - r[:, None] + NAT_ROWS - 1
    bias = rpb[:, row_off[:, :, None, None], col_off[None, None, :, :]]
    bias = jnp.transpose(bias, (1, 0, 3, 2, 4)).astype(jnp.float32)
    s_lat = jnp.einsum('brqhd,brjkhd->brhqjk', qg, kr).astype(jnp.float32) * scale + bias[None]
    s_lat = jnp.where(col_mask[:, None, :], s_lat, NEG)
    n_lat = kr_n * GRID_W
    s_ctx = jnp.einsum('brqhd,bkhd->brhqk', qg, k_ctx).astype(jnp.float32) * scale
    p = jax.nn.softmax(jnp.concatenate([s_lat.reshape(B, rows, H, GRID_W, n_lat), s_ctx], axis=-1), axis=-1)
    p = p.astype(v.dtype)
    p_lat = p[..., :n_lat].reshape(B, rows, H, GRID_W, kr_n, GRID_W)
    o = (jnp.einsum('brhqjk,brjkhd->brqhd', p_lat, vr)
         + jnp.einsum('brhqk,bkhd->brqhd', p[..., n_lat:], v_ctx))
    return o.reshape(B, L, H * d)


def s5_scan(u, a_bar, b_bar, h0):
    bu = jnp.einsum('blgc,gpc->blgp', u, b_bar)
    a = jnp.broadcast_to(a_bar, bu.shape)

    def combine(x, y):
        a1, b1 = x
        a2, b2 = y
        return a1 * a2, a2 * b1 + b2

    a_cum, h = lax.associative_scan(combine, (a, bu), axis=1)
    return h + a_cum * h0[:, None]


def s5_readout(h, c_re, c_im):
    return (jnp.einsum('blgp,gcp->blgc', jnp.real(h), c_re.astype(jnp.float32))
            - jnp.einsum('blgp,gcp->blgc', jnp.imag(h), c_im.astype(jnp.float32)))


def s5_mixer(u, lp, h0_f, h0_b):
    B, L, _ = u.shape
    ug = u.astype(jnp.float32).reshape(B, L, S5_GROUPS, S5_CH)
    lam = lax.complex(jnp.minimum(lp['s5_lam_re'].astype(jnp.float32), -1e-4),
                      lp['s5_lam_im'].astype(jnp.float32))
    dt = jnp.exp(lp['s5_log_dt'].astype(jnp.float32))[..., None]
    a_bar = jnp.exp(lam * dt)
    b_c = lax.complex(lp['s5_b_re'].astype(jnp.float32), lp['s5_b_im'].astype(jnp.float32))
    b_bar = ((a_bar - 1.0) / lam)[..., None] * b_c
    uc = ug.astype(jnp.complex64)
    h_f = s5_scan(uc, a_bar[0], b_bar[0], h0_f)
    h_b = jnp.flip(s5_scan(jnp.flip(uc, axis=1), a_bar[1], b_bar[1], h0_b), axis=1)
    y = (s5_readout(h_f, lp['s5_c_re'][0], lp['s5_c_im'][0])
         + s5_readout(h_b, lp['s5_c_re'][1], lp['s5_c_im'][1])
         + lp['s5_d'].astype(jnp.float32).reshape(S5_GROUPS, S5_CH) * ug)
    y = jax.nn.gelu(y.reshape(B, L, BRANCH_W))
    g = y @ lp['s5_w_glu'].astype(jnp.float32)
    y = g[..., :BRANCH_W] * jax.nn.sigmoid(g[..., BRANCH_W:])
    return y.astype(u.dtype), h_f[:, -1], h_b[:, 0]


def split_in(z):
    idx = [int(i) for i in np.cumsum(IN_SIZES)[:-1]]
    return jnp.split(z, idx, axis=-1)


def ada_pre(x, cond, lp):
    mod = jax.nn.silu(cond) @ lp['w_mod'] + lp['b_mod']
    shift, scale, gate = jnp.split(mod, 3, axis=-1)
    h = rmsnorm(x, lp['norm_g']) * (1.0 + scale) + shift
    return h, gate


def ada_post(x, h, gate, outs, gate_paths, lp):
    B, L, _ = x.shape
    br = jnp.stack([o.astype(x.dtype) * jax.nn.silu(g) for o, g in zip(outs, gate_paths)], axis=-2)
    proj = jnp.einsum('blkw,kwd->blkd', br, lp['w_branch'])
    mg = jax.nn.sigmoid(h @ lp['w_merge']).reshape(B, L, N_BRANCH, D_MODEL)
    y = jnp.sum(mg * proj, axis=-2) @ lp['w_out']
    return x + gate * y


def mla_queries(cq, lp):
    B, L, _ = cq.shape
    q = rmsnorm(cq, lp['mla_q_norm']) @ lp['mla_w_q_up']
    return q.reshape(B, L, MLA_HEADS, MLA_NOPE + MLA_ROPE)


def mla_kv(ckv_n, kpe, w_kv_up):
    B, L, _ = ckv_n.shape
    kv = (ckv_n @ w_kv_up).reshape(B, L, MLA_HEADS, MLA_NOPE + MLA_V)
    kpe_h = jnp.broadcast_to(kpe[:, :, None, :], (B, L, MLA_HEADS, MLA_ROPE)).astype(kv.dtype)
    k = jnp.concatenate([kv[..., :MLA_NOPE], kpe_h], axis=-1)
    return k, kv[..., MLA_NOPE:]


def layer_context(x, cond, lp):
    B, L, _ = x.shape
    h, gate = ada_pre(x, cond, lp)
    (a_q, a_k, a_v, a_g, b_cq, b_ckv, b_kpe, b_g, c_u, c_g, d_q, d_k, d_v, d_g) = split_in(h @ lp['w_in'])
    ka = a_k.reshape(B, L, A_KV_HEADS, HEAD_DIM)
    va = a_v.reshape(B, L, A_KV_HEADS, HEAD_DIM)
    oa = dense_attention(a_q.reshape(B, L, A_KV_HEADS, A_GROUP, HEAD_DIM), ka, va, lp['a_sink'])
    qm = mla_queries(b_cq, lp)
    ckv = rmsnorm(b_ckv, lp['mla_kv_norm'])
    km, vm = mla_kv(ckv, b_kpe, lp['mla_w_kv_up'])
    ob = dense_attention(qm[:, :, :, None], km, vm)
    zero = jnp.zeros((B, S5_GROUPS, S5_STATE), jnp.complex64)
    oc, hf, hb = s5_mixer(c_u, lp, zero, zero)
    kd = d_k.reshape(B, L, NAT_HEADS, HEAD_DIM)
    vd = d_v.reshape(B, L, NAT_HEADS, HEAD_DIM)
    od = dense_attention(d_q.reshape(B, L, NAT_HEADS, 1, HEAD_DIM), kd, vd)
    x = ada_post(x, h, gate, (oa, ob, oc, od), (a_g, b_g, c_g, d_g), lp)
    s = jnp.stack([hf, hb], axis=1)
    return x, (ka, va, ckv, b_kpe, kd, vd, jnp.real(s), jnp.imag(s))


def layer_latent(x, cond, cache, lp):
    k_a_ctx, v_a_ctx, ckv_ctx, kpe_ctx, k_n_ctx, v_n_ctx, s_re, s_im = cache
    B, L, _ = x.shape
    h, gate = ada_pre(x, cond, lp)
    (a_q, a_k, a_v, a_g, b_cq, b_ckv, b_kpe, b_g, c_u, c_g, d_q, d_k, d_v, d_g) = split_in(h @ lp['w_in'])
    qa = axial_rope(a_q.reshape(B, L, A_HEADS, HEAD_DIM)).reshape(B, L, A_KV_HEADS, A_GROUP, HEAD_DIM)
    ka = axial_rope(a_k.reshape(B, L, A_KV_HEADS, HEAD_DIM))
    va = a_v.reshape(B, L, A_KV_HEADS, HEAD_DIM)
    oa = window_attention_latent(qa, ka, va, k_a_ctx, v_a_ctx, lp['a_sink'])
    qm = mla_queries(b_cq, lp)
    qm = jnp.concatenate([qm[..., :MLA_NOPE], axial_rope(qm[..., MLA_NOPE:])], axis=-1)
    kpe_lat = axial_rope(b_kpe[:, :, None, :])[:, :, 0]
    k_lat, v_lat = mla_kv(rmsnorm(b_ckv, lp['mla_kv_norm']), kpe_lat, lp['mla_w_kv_up'])
    k_cx, v_cx = mla_kv(ckv_ctx, kpe_ctx, lp['mla_w_kv_up'])
    ob = dense_attention(qm[:, :, :, None], jnp.concatenate([k_lat, k_cx.astype(k_lat.dtype)], axis=1),
                         jnp.concatenate([v_lat, v_cx.astype(v_lat.dtype)], axis=1))
    h0 = lax.complex(s_re.astype(jnp.float32), s_im.astype(jnp.float32))
    oc, _, _ = s5_mixer(c_u, lp, h0[:, 0], h0[:, 1])
    od = neighbourhood_attention_latent(d_q.reshape(B, L, NAT_HEADS, HEAD_DIM),
                                        d_k.reshape(B, L, NAT_HEADS, HEAD_DIM),
                                        d_v.reshape(B, L, NAT_HEADS, HEAD_DIM),
                                        k_n_ctx, v_n_ctx, lp['na_rpb'])
    return ada_post(x, h, gate, (oa, ob, oc, od), (a_g, b_g, c_g, d_g), lp)


def setup_inputs(seed: int = 0) -> dict:
    key = jax.random.key(seed)
    keys = iter(jax.random.split(key, 48))

    def nrm(shape, scale):
        return jax.random.normal(next(keys), shape, jnp.float32) * scale

    D = D_MODEL
    inp = {}
    inp['x_prompt'] = nrm((BATCH, SEQ, D), 1.0)
    inp['x_sample'] = nrm((DEC_BATCH, DEC_SEQ, D), 1.0)
    inp['cache_a_k'] = nrm((DEC_BATCH, DEPTH, PAST_LEN, A_KV_HEADS, HEAD_DIM), 1.0)
    inp['cache_a_v'] = nrm((DEC_BATCH, DEPTH, PAST_LEN, A_KV_HEADS, HEAD_DIM), 1.0)
    inp['cache_mla_ckv'] = nrm((DEC_BATCH, DEPTH, PAST_LEN, MLA_KV_LORA), 1.0)
    inp['cache_mla_kpe'] = nrm((DEC_BATCH, DEPTH, PAST_LEN, MLA_ROPE), 1.0)
    inp['cache_na_k'] = nrm((DEC_BATCH, DEPTH, PAST_LEN, NAT_HEADS, HEAD_DIM), 1.0)
    inp['cache_na_v'] = nrm((DEC_BATCH, DEPTH, PAST_LEN, NAT_HEADS, HEAD_DIM), 1.0)
    inp['state_s5_re'] = nrm((DEC_BATCH, DEPTH, 2, S5_GROUPS, S5_STATE), 0.1)
    inp['state_s5_im'] = nrm((DEC_BATCH, DEPTH, 2, S5_GROUPS, S5_STATE), 0.1)
    inp['c'] = nrm((DEC_BATCH, D), 1.0)
    inp['c_ctx'] = nrm((D,), 1.0)
    inp['w_mod'] = nrm((DEPTH, D, 3 * D), D ** -0.5)
    inp['b_mod'] = nrm((DEPTH, 3 * D), 0.01)
    inp['norm_g'] = 1.0 + nrm((DEPTH, D), 0.01)
    inp['w_in'] = nrm((DEPTH, D, IN_COLS), D ** -0.5)
    inp['w_merge'] = nrm((DEPTH, D, N_BRANCH * D), D ** -0.5)
    inp['a_sink'] = nrm((DEPTH, A_HEADS), 0.5)
    inp['mla_q_norm'] = 1.0 + nrm((DEPTH, MLA_Q_LORA), 0.01)
    inp['mla_w_q_up'] = nrm((DEPTH, MLA_Q_LORA, MLA_HEADS * (MLA_NOPE + MLA_ROPE)), MLA_Q_LORA ** -0.5)
    inp['mla_kv_norm'] = 1.0 + nrm((DEPTH, MLA_KV_LORA), 0.01)
    inp['mla_w_kv_up'] = nrm((DEPTH, MLA_KV_LORA, MLA_HEADS * (MLA_NOPE + MLA_V)), MLA_KV_LORA ** -0.5)
    inp['s5_lam_re'] = -0.5 + nrm((DEPTH, 2, S5_GROUPS, S5_STATE), 0.01)
    inp['s5_lam_im'] = (math.pi * jnp.arange(S5_STATE, dtype=jnp.float32)[None, None, None, :]
                        + nrm((DEPTH, 2, S5_GROUPS, S5_STATE), 0.01))
    inp['s5_log_dt'] = jax.random.uniform(next(keys), (DEPTH, 2, S5_GROUPS), jnp.float32,
                                          minval=math.log(1e-3), maxval=math.log(1e-1))
    inp['s5_b_re'] = nrm((DEPTH, 2, S5_GROUPS, S5_STATE, S5_CH), (2 * S5_CH) ** -0.5)
    inp['s5_b_im'] = nrm((DEPTH, 2, S5_GROUPS, S5_STATE, S5_CH), (2 * S5_CH) ** -0.5)
    inp['s5_c_re'] = nrm((DEPTH, 2, S5_GROUPS, S5_CH, S5_STATE), S5_STATE ** -0.5)
    inp['s5_c_im'] = nrm((DEPTH, 2, S5_GROUPS, S5_CH, S5_STATE), S5_STATE ** -0.5)
    inp['s5_d'] = nrm((DEPTH, BRANCH_W), 1.0)
    inp['s5_w_glu'] = nrm((DEPTH, BRANCH_W, 2 * BRANCH_W), BRANCH_W ** -0.5)
    inp['na_rpb'] = nrm((DEPTH, NAT_HEADS, 2 * NAT_ROWS - 1, 2 * NAT_COLS - 1), 0.1)
    inp['w_branch'] = nrm((DEPTH, N_BRANCH, BRANCH_W, D), BRANCH_W ** -0.5)
    inp['w_out'] = nrm((DEPTH, D, D), D ** -0.5)
    inp['final_norm_g'] = 1.0 + nrm((D,), 0.01)
    return inp


def reference(x_prompt, x_sample, cache_a_k, cache_a_v, cache_mla_ckv, cache_mla_kpe, cache_na_k, cache_na_v,
              state_s5_re, state_s5_im, c, c_ctx, w_mod, b_mod, norm_g, w_in, w_merge, a_sink,
              mla_q_norm, mla_w_q_up, mla_kv_norm, mla_w_kv_up, s5_lam_re, s5_lam_im, s5_log_dt,
              s5_b_re, s5_b_im, s5_c_re, s5_c_im, s5_d, s5_w_glu, na_rpb, w_branch, w_out, final_norm_g):
    yp = x_prompt
    ys = x_sample
    ctx_states = []
    for i in range(DEPTH):
        lp = dict(w_mod=w_mod[i], b_mod=b_mod[i], norm_g=norm_g[i], w_in=w_in[i], w_merge=w_merge[i],
                  a_sink=a_sink[i], mla_q_norm=mla_q_norm[i], mla_w_q_up=mla_w_q_up[i],
                  mla_kv_norm=mla_kv_norm[i], mla_w_kv_up=mla_w_kv_up[i], s5_lam_re=s5_lam_re[i],
                  s5_lam_im=s5_lam_im[i], s5_log_dt=s5_log_dt[i], s5_b_re=s5_b_re[i], s5_b_im=s5_b_im[i],
                  s5_c_re=s5_c_re[i], s5_c_im=s5_c_im[i], s5_d=s5_d[i], s5_w_glu=s5_w_glu[i],
                  na_rpb=na_rpb[i], w_branch=w_branch[i], w_out=w_out[i])
        yp, st = layer_context(yp, c_ctx, lp)
        ctx_states.append(st)
        cache_i = (cache_a_k[:, i], cache_a_v[:, i], cache_mla_ckv[:, i], cache_mla_kpe[:, i],
                   cache_na_k[:, i], cache_na_v[:, i], state_s5_re[:, i], state_s5_im[:, i])
        ys = layer_latent(ys, c[:, None, :], cache_i, lp)
    y_prompt = rmsnorm(yp, final_norm_g)
    y_sample = rmsnorm(ys, final_norm_g)
    new_a_k = jnp.stack([st[0] for st in ctx_states], axis=1)
    new_a_v = jnp.stack([st[1] for st in ctx_states], axis=1)
    new_mla_ckv = jnp.stack([st[2] for st in ctx_states], axis=1)
    new_mla_kpe = jnp.stack([st[3] for st in ctx_states], axis=1)
    new_na_k = jnp.stack([st[4] for st in ctx_states], axis=1)
    new_na_v = jnp.stack([st[5] for st in ctx_states], axis=1)
    new_s5_re = jnp.stack([st[6] for st in ctx_states], axis=1)
    new_s5_im = jnp.stack([st[7] for st in ctx_states], axis=1)
    return (y_prompt, y_sample, new_a_k, new_a_v, new_mla_ckv, new_mla_kpe, new_na_k, new_na_v, new_s5_re, new_s5_im)
```

```python
import functools
import math

import numpy as np
import jax
import jax.numpy as jnp
from jax import lax
from jax.experimental import pallas as pl
from jax.experimental.pallas import tpu as pltpu

F32 = jnp.float32
BF16 = jnp.bfloat16

D_MODEL = 1024
BATCH = 32
SEQ = 256
DEPTH = 2
DEC_BATCH = 4
DEC_SEQ = 2048
PAST_LEN = 512
GRID_W = 64
HEAD_DIM = 64
BRANCH_W = 256
N_BRANCH = 4
Q_BLOCK = 128
A_HEADS = 4
A_KV_HEADS = 2
A_GROUP = A_HEADS // A_KV_HEADS
A_WINDOW = 128
MLA_HEADS = 4
MLA_Q_LORA = 256
MLA_KV_LORA = 128
MLA_NOPE = 64
MLA_ROPE = 32
MLA_V = 64
S5_CH = 16
S5_GROUPS = BRANCH_W // S5_CH
S5_STATE = 64
NAT_HEADS = 4
NAT_ROWS = 8
NAT_COLS = 16
ROPE_BASE = 10000.0
EPS = 1e-6
NEG = -1e30

LANE = 128

Z_AQ, Z_AK, Z_AV = 0, 256, 384
Z_DQ, Z_DK, Z_DV = 512, 768, 1024
Z_QN, Z_QR, Z_CKV = 1280, 1536, 1664
Z_CU = 1792
Z_AG, Z_BG, Z_CG, Z_DG = 2048, 2304, 2560, 2816
Z_KPE = 3072
Z_W = 3200
R_CQ, R_CKV, R_CU, R_KPE, R_W = 1280, 1536, 1664, 2944, 3072

S5_T = 32
S5_TC = S5_T * S5_CH
S5_ROWS_CTX = BATCH * SEQ // S5_T
S5_ROWS_LAT = DEC_BATCH * DEC_SEQ // S5_T
S5_NK_CTX = SEQ // S5_T
S5_NK_LAT = DEC_SEQ // S5_T

TM = 512
TQ_MLA = 256


def _cparams(sem, vmem_mb):
    return pltpu.CompilerParams(dimension_semantics=sem, vmem_limit_bytes=vmem_mb * 1024 * 1024)


def _sigmoid(x):
    return 1.0 / (1.0 + jnp.exp(-x))


def _silu(x):
    return x * _sigmoid(x)


def _rms(x, g):
    return x * lax.rsqrt(jnp.mean(x * x, axis=-1, keepdims=True) + EPS) * g


def _dot(a, b):
    return jnp.dot(a, b, preferred_element_type=F32)


def _dot_nt(a, b):
    return lax.dot_general(a, b, (((1,), (1,)), ((), ())), preferred_element_type=F32)


def _mod_kernel(c_ref, w_ref, b_ref, o_ref):
    s = _silu(c_ref[...])
    o_ref[0] = _dot(s.astype(BF16), w_ref[0].astype(BF16)) + b_ref[0]


def _mod_call(conds, w_mod, b_mod):
    nc = 512
    return pl.pallas_call(
        _mod_kernel,
        out_shape=jax.ShapeDtypeStruct((DEPTH, 8, 3 * D_MODEL), F32),
        grid=(DEPTH, 3 * D_MODEL // nc),
        in_specs=[
            pl.BlockSpec((8, D_MODEL), lambda i, j: (0, 0)),
            pl.BlockSpec((1, D_MODEL, nc), lambda i, j: (i, 0, j)),
            pl.BlockSpec((1, 1, nc), lambda i, j: (i, 0, j)),
        ],
        out_specs=pl.BlockSpec((1, 8, nc), lambda i, j: (i, 0, j)),
        compiler_params=_cparams(("arbitrary", "arbitrary"), 32),
        name="mod_rows",
    )(conds, w_mod, b_mod.reshape(DEPTH, 1, 3 * D_MODEL))


def _ada_h(x, mod_ref, ng_ref):
    shift = mod_ref[0, :, 0:D_MODEL]
    scale = mod_ref[0, :, D_MODEL:2 * D_MODEL]
    return _rms(x, ng_ref[...]) * (1.0 + scale) + shift


def _rope_block(xs, tab_ref, shift):
    return (xs * tab_ref[0] + pltpu.roll(xs, shift, 1) * tab_ref[1]
            + pltpu.roll(xs, LANE - shift, 1) * tab_ref[2])


def _inproj_kernel(*refs, rope):
    if rope:
        x_ref, mod_ref, ng_ref, w_ref, qn_ref, wq_ref, kvn_ref, ta_ref, tm_ref, z_ref = refs
    else:
        x_ref, mod_ref, ng_ref, w_ref, qn_ref, wq_ref, kvn_ref, z_ref = refs
    h = _ada_h(x_ref[0], mod_ref, ng_ref)
    raw = _dot(h.astype(BF16), w_ref[...])
    if rope:
        for j in range(Z_AV // LANE):
            z_ref[0, :, j * LANE:(j + 1) * LANE] = _rope_block(raw[:, j * LANE:(j + 1) * LANE], ta_ref, 16)
    else:
        z_ref[0, :, 0:Z_AV] = raw[:, 0:Z_AV]
    z_ref[0, :, Z_AV:Z_QN] = raw[:, Z_AV:R_CQ]
    qn = _rms(raw[:, R_CQ:R_CKV], qn_ref[...])
    q = _dot(qn.astype(BF16), wq_ref[...])
    z_ref[0, :, Z_QN:Z_QR] = q[:, 0:256]
    qr = q[:, 256:384]
    kp = raw[:, R_KPE:R_W]
    if rope:
        qr = _rope_block(qr, tm_ref, 8)
        kp = _rope_block(kp, tm_ref, 8)
    z_ref[0, :, Z_QR:Z_CKV] = qr
    z_ref[0, :, Z_CKV:Z_CU] = _rms(raw[:, R_CKV:R_CU], kvn_ref[...])
    z_ref[0, :, Z_CU:Z_KPE] = raw[:, R_CU:R_KPE]
    z_ref[0, :, Z_KPE:Z_W] = kp


def _inproj_call(x, mod, ng, w_raw, qnorm, wq, kvnorm, tabs, name):
    nb, length, _ = x.shape
    rope = tabs is not None
    full2 = lambda b, t: (0, 0)
    in_specs = [
        pl.BlockSpec((1, TM, D_MODEL), lambda b, t: (b, t, 0)),
        pl.BlockSpec((1, 1, 3 * D_MODEL), lambda b, t: (b, 0, 0)),
        pl.BlockSpec((1, D_MODEL), full2),
        pl.BlockSpec((D_MODEL, R_W), full2),
        pl.BlockSpec((1, MLA_Q_LORA), full2),
        pl.BlockSpec((MLA_Q_LORA, 384), full2),
        pl.BlockSpec((1, MLA_KV_LORA), full2),
    ]
    args = [x, mod, ng, w_raw, qnorm, wq, kvnorm]
    if rope:
        in_specs += [pl.BlockSpec((3, TM, LANE), lambda b, t: (0, t, 0))] * 2
        args += list(tabs)
    return pl.pallas_call(
        functools.partial(_inproj_kernel, rope=rope),
        out_shape=jax.ShapeDtypeStruct((nb, length, Z_W), F32),
        grid=(nb, length // TM),
        in_specs=in_specs,
        out_specs=pl.BlockSpec((1, TM, Z_W), lambda b, t: (b, t, 0)),
        compiler_params=_cparams(("arbitrary", "arbitrary"), 56),
        name=name,
    )(*args)


def _softmax_pv(s_list, v_list, sink=None):
    m = jnp.max(s_list[0], axis=-1, keepdims=True)
    for s in s_list[1:]:
        m = jnp.maximum(m, jnp.max(s, axis=-1, keepdims=True))
    if sink is not None:
        m = jnp.maximum(m, sink)
    den = None
    o = None
    for s, v in zip(s_list, v_list):
        e = jnp.exp(s - m)
        d = jnp.sum(e, axis=-1, keepdims=True)
        pv = _dot(e.astype(BF16), v)
        den = d if den is None else den + d
        o = pv if o is None else o + pv
    if sink is not None:
        den = den + jnp.exp(sink - m)
    return o / den


def _ctx_attn_kernel(sink_ref, z_ref, wkv_ref, oa_ref, ob_ref, od_ref):
    hd = HEAD_DIM
    for h in range(A_HEADS):
        hk = h // A_GROUP
        q = z_ref[0, :, Z_AQ + hd * h:Z_AQ + hd * (h + 1)].astype(BF16)
        k = z_ref[0, :, Z_AK + hd * hk:Z_AK + hd * (hk + 1)].astype(BF16)
        v = z_ref[0, :, Z_AV + hd * hk:Z_AV + hd * (hk + 1)].astype(BF16)
        s = _dot_nt(q, k) * (hd ** -0.5)
        o = _softmax_pv([s], [v], sink=sink_ref[h])
        g = z_ref[0, :, Z_AG + hd * h:Z_AG + hd * (h + 1)]
        oa_ref[0, :, hd * h:hd * (h + 1)] = (o * _silu(g)).astype(BF16)
    kv = _dot(z_ref[0, :, Z_CKV:Z_CKV + MLA_KV_LORA].astype(BF16), wkv_ref[...])
    kpe = z_ref[0, :, Z_KPE:Z_KPE + MLA_ROPE].astype(BF16)
    for h in range(MLA_HEADS):
        qn = z_ref[0, :, Z_QN + 64 * h:Z_QN + 64 * (h + 1)].astype(BF16)
        qr = z_ref[0, :, Z_QR + 32 * h:Z_QR + 32 * (h + 1)].astype(BF16)
        kn = kv[:, 128 * h:128 * h + 64].astype(BF16)
        v = kv[:, 128 * h + 64:128 * (h + 1)].astype(BF16)
        s = (_dot_nt(qn, kn) + _dot_nt(qr, kpe)) * ((MLA_NOPE + MLA_ROPE) ** -0.5)
        o = _softmax_pv([s], [v])
        g = z_ref[0, :, Z_BG + 64 * h:Z_BG + 64 * (h + 1)]
        ob_ref[0, :, 64 * h:64 * (h + 1)] = (o * _silu(g)).astype(BF16)
    for h in range(NAT_HEADS):
        q = z_ref[0, :, Z_DQ + hd * h:Z_DQ + hd * (h + 1)].astype(BF16)
        k = z_ref[0, :, Z_DK + hd * h:Z_DK + hd * (h + 1)].astype(BF16)
        v = z_ref[0, :, Z_DV + hd * h:Z_DV + hd * (h + 1)].astype(BF16)
        s = _dot_nt(q, k) * (hd ** -0.5)
        o = _softmax_pv([s], [v])
        g = z_ref[0, :, Z_DG + hd * h:Z_DG + hd * (h + 1)]
        od_ref[0, :, hd * h:hd * (h + 1)] = (o * _silu(g)).astype(BF16)


def _ctx_attn_call(z, sink, wkv):
    out = jax.ShapeDtypeStruct((BATCH, SEQ, BRANCH_W), BF16)
    ospec = pl.BlockSpec((1, SEQ, BRANCH_W), lambda b: (b, 0, 0))
    return pl.pallas_call(
        _ctx_attn_kernel,
        out_shape=(out, out, out),
        grid=(BATCH,),
        in_specs=[
            pl.BlockSpec(memory_space=pltpu.SMEM),
            pl.BlockSpec((1, SEQ, Z_W), lambda b: (b, 0, 0)),
            pl.BlockSpec((MLA_KV_LORA, 512), lambda b: (0, 0)),
        ],
        out_specs=(ospec, ospec, ospec),
        compiler_params=_cparams(("arbitrary",), 40),
        name="ctx_attn",
    )(sink, z, wkv)


def _win_attn_kernel(sink_ref, q_ref, kv_ref, g_ref, ck_ref, cv_ref, o_ref):
    hd = HEAD_DIM
    n = pl.program_id(1)
    nb = DEC_SEQ // Q_BLOCK
    blocks = []
    for off in (-1, 0, 1):
        start = pl.multiple_of(jnp.clip(n + off, 0, nb - 1) * Q_BLOCK, Q_BLOCK)
        blocks.append(kv_ref[0, pl.ds(start, Q_BLOCK), :])
    kvw = jnp.concatenate(blocks, axis=0).astype(BF16)
    qpos = n * Q_BLOCK + lax.broadcasted_iota(jnp.int32, (Q_BLOCK, 3 * Q_BLOCK), 0)
    kpos = (n - 1) * Q_BLOCK + lax.broadcasted_iota(jnp.int32, (Q_BLOCK, 3 * Q_BLOCK), 1)
    mask = (jnp.abs(qpos - kpos) <= A_WINDOW) & (kpos >= 0) & (kpos < DEC_SEQ)
    ck = ck_ref[0, 0].astype(BF16)
    cv = cv_ref[0, 0].astype(BF16)
    for h in range(A_HEADS):
        hk = h // A_GROUP
        q = q_ref[0, :, hd * h:hd * (h + 1)].astype(BF16)
        kw = kvw[:, hd * hk:hd * (hk + 1)]
        vw = kvw[:, 2 * hd + hd * hk:2 * hd + hd * (hk + 1)]
        s_win = jnp.where(mask, _dot_nt(q, kw) * (hd ** -0.5), NEG)
        s_ctx = _dot_nt(q, ck[:, hd * hk:hd * (hk + 1)]) * (hd ** -0.5)
        o = _softmax_pv([s_win, s_ctx], [vw, cv[:, hd * hk:hd * (hk + 1)]], sink=sink_ref[h])
        g = g_ref[0, :, hd * h:hd * (h + 1)]
        o_ref[0, :, hd * h:hd * (h + 1)] = (o * _silu(g)).astype(BF16)


def _win_attn_call(z, sink, cache_k, cache_v, layer):
    kvw = A_KV_HEADS * HEAD_DIM
    return pl.pallas_call(
        _win_attn_kernel,
        out_shape=jax.ShapeDtypeStruct((DEC_BATCH, DEC_SEQ, BRANCH_W), BF16),
        grid=(DEC_BATCH, DEC_SEQ // Q_BLOCK),
        in_specs=[
            pl.BlockSpec(memory_space=pltpu.SMEM),
            pl.BlockSpec((1, Q_BLOCK, 256), lambda b, n: (b, n, Z_AQ // 256)),
            pl.BlockSpec((1, DEC_SEQ, 256), lambda b, n: (b, 0, Z_AK // 256)),
            pl.BlockSpec((1, Q_BLOCK, 256), lambda b, n: (b, n, Z_AG // 256)),
            pl.BlockSpec((1, 1, PAST_LEN, kvw), lambda b, n: (b, layer, 0, 0)),
            pl.BlockSpec((1, 1, PAST_LEN, kvw), lambda b, n: (b, layer, 0, 0)),
        ],
        out_specs=pl.BlockSpec((1, Q_BLOCK, BRANCH_W), lambda b, n: (b, n, 0)),
        compiler_params=_cparams(("arbitrary", "arbitrary"), 32),
        name="lat_window_attn",
    )(sink, z, z, z, cache_k, cache_v)


def _nat_kernel(q_ref, k_ref, v_ref, g_ref, ck_ref, cv_ref, bias_ref, o_ref):
    hd = HEAD_DIM
    rows = DEC_SEQ // GRID_W
    kr = NAT_ROWS
    r = pl.program_id(1)
    rs = jnp.clip(r - kr // 2, 0, rows - kr)
    start = pl.multiple_of(rs * GRID_W, GRID_W)
    kl = k_ref[0, pl.ds(start, kr * GRID_W), :].astype(BF16)
    vl = v_ref[0, pl.ds(start, kr * GRID_W), :].astype(BF16)
    ck = ck_ref[0, 0].astype(BF16)
    cv = cv_ref[0, 0].astype(BF16)
    shape = (GRID_W, kr * GRID_W)
    qc = lax.broadcasted_iota(jnp.int32, shape, 0)
    kc = lax.broadcasted_iota(jnp.int32, shape, 1) & (GRID_W - 1)
    rel = kc - jnp.clip(qc - NAT_COLS // 2, 0, GRID_W - NAT_COLS)
    mask = (rel >= 0) & (rel < NAT_COLS)
    bidx = rs - r + (NAT_ROWS - 1)
    for h in range(NAT_HEADS):
        q = q_ref[0, :, hd * h:hd * (h + 1)].astype(BF16)
        s_lat = _dot_nt(q, kl[:, hd * h:hd * (h + 1)]) * (hd ** -0.5) + bias_ref[bidx, h]
        s_lat = jnp.where(mask, s_lat, NEG)
        s_ctx = _dot_nt(q, ck[:, hd * h:hd * (h + 1)]) * (hd ** -0.5)
        o = _softmax_pv([s_lat, s_ctx], [vl[:, hd * h:hd * (h + 1)], cv[:, hd * h:hd * (h + 1)]])
        g = g_ref[0, :, hd * h:hd * (h + 1)]
        o_ref[0, :, hd * h:hd * (h + 1)] = (o * _silu(g)).astype(BF16)


def _nat_bias_table(rpb):
    cidx = np.arange(GRID_W)
    col_off = np.clip(cidx[None, :] - cidx[:, None], -(NAT_COLS - 1), NAT_COLS - 1) + NAT_COLS - 1
    row_off = np.arange(NAT_ROWS)[:, None] + np.arange(NAT_ROWS)[None, :]
    tab = rpb[:, row_off[:, :, None, None], col_off[None, None, :, :]]
    tab = jnp.transpose(tab, (1, 0, 3, 2, 4))
    return tab.reshape(NAT_ROWS, NAT_HEADS, GRID_W, NAT_ROWS * GRID_W).astype(F32)


def _nat_call(z, cache_k, cache_v, bias, layer):
    hw = NAT_HEADS * HEAD_DIM
    return pl.pallas_call(
        _nat_kernel,
        out_shape=jax.ShapeDtypeStruct((DEC_BATCH, DEC_SEQ, BRANCH_W), BF16),
        grid=(DEC_BATCH, DEC_SEQ // GRID_W),
        in_specs=[
            pl.BlockSpec((1, GRID_W, 256), lambda b, r: (b, r, Z_DQ // 256)),
            pl.BlockSpec((1, DEC_SEQ, 256), lambda b, r: (b, 0, Z_DK // 256)),
            pl.BlockSpec((1, DEC_SEQ, 256), lambda b, r: (b, 0, Z_DV // 256)),
            pl.BlockSpec((1, GRID_W, 256), lambda b, r: (b, r, Z_DG // 256)),
            pl.BlockSpec((1, 1, PAST_LEN, hw), lambda b, r: (b, layer, 0, 0)),
            pl.BlockSpec((1, 1, PAST_LEN, hw), lambda b, r: (b, layer, 0, 0)),
            pl.BlockSpec(bias.shape, lambda b, r: (0, 0, 0, 0)),
        ],
        out_specs=pl.BlockSpec((1, GRID_W, BRANCH_W), lambda b, r: (b, r, 0)),
        compiler_params=_cparams(("arbitrary", "arbitrary"), 40),
        name="lat_nat_attn",
    )(z, z, z, z, cache_k, cache_v, bias)


def _mla_kernel(qn_ref, qr_ref, ckv_ref, kpe_ref, g_ref, cckv_ref, ckpe_ref, wkv_ref, o_ref, kv_s, kpe_s):
    nlat = DEC_SEQ

    @pl.when(pl.program_id(1) == 0)
    def _():
        rows = 512
        for c in range(nlat // rows):
            blk = ckv_ref[0, c * rows:(c + 1) * rows, :].astype(BF16)
            kv_s[c * rows:(c + 1) * rows, :] = _dot(blk, wkv_ref[...]).astype(BF16)
        kv_s[nlat:nlat + PAST_LEN, :] = _dot(cckv_ref[0, 0].astype(BF16), wkv_ref[...]).astype(BF16)
        kpe_s[0:nlat, :] = kpe_ref[0, :, 0:MLA_ROPE].astype(BF16)
        kpe_s[nlat:nlat + PAST_LEN, :] = ckpe_ref[0, 0].astype(BF16)

    kpe = kpe_s[...]
    for h in range(MLA_HEADS):
        qn = qn_ref[0, :, 64 * h:64 * (h + 1)].astype(BF16)
        qr = qr_ref[0, :, 32 * h:32 * (h + 1)].astype(BF16)
        kn = kv_s[:, 128 * h:128 * h + 64]
        v = kv_s[:, 128 * h + 64:128 * (h + 1)]
        s = (_dot_nt(qn, kn) + _dot_nt(qr, kpe)) * ((MLA_NOPE + MLA_ROPE) ** -0.5)
        o = _softmax_pv([s], [v])
        g = g_ref[0, :, 64 * h:64 * (h + 1)]
        o_ref[0, :, 64 * h:64 * (h + 1)] = (o * _silu(g)).astype(BF16)


def _mla_call(z, cache_ckv, cache_kpe, wkv, layer):
    nkeys = DEC_SEQ + PAST_LEN
    return pl.pallas_call(
        _mla_kernel,
        out_shape=jax.ShapeDtypeStruct((DEC_BATCH, DEC_SEQ, BRANCH_W), BF16),
        grid=(DEC_BATCH, DEC_SEQ // TQ_MLA),
        in_specs=[
            pl.BlockSpec((1, TQ_MLA, 256), lambda b, t: (b, t, Z_QN // 256)),
            pl.BlockSpec((1, TQ_MLA, LANE), lambda b, t: (b, t, Z_QR // LANE)),
            pl.BlockSpec((1, DEC_SEQ, LANE), lambda b, t: (b, 0, Z_CKV // LANE)),
            pl.BlockSpec((1, DEC_SEQ, LANE), lambda b, t: (b, 0, Z_KPE // LANE)),
            pl.BlockSpec((1, TQ_MLA, 256), lambda b, t: (b, t, Z_BG // 256)),
            pl.BlockSpec((1, 1, PAST_LEN, MLA_KV_LORA), lambda b, t: (b, layer, 0, 0)),
            pl.BlockSpec((1, 1, PAST_LEN, MLA_ROPE), lambda b, t: (b, layer, 0, 0)),
            pl.BlockSpec((MLA_KV_LORA, 512), lambda b, t: (0, 0)),
        ],
        out_specs=pl.BlockSpec((1, TQ_MLA, BRANCH_W), lambda b, t: (b, t, 0)),
        scratch_shapes=[pltpu.VMEM((nkeys, 512), BF16), pltpu.VMEM((nkeys, MLA_ROPE), BF16)],
        compiler_params=_cparams(("arbitrary", "arbitrary"), 48),
        name="lat_mla_attn",
    )(z, z, z, z, z, cache_ckv, cache_kpe, wkv)


def _s5_mats(lam_re, lam_im, log_dt, b_re, b_im, c_re, c_im):
    t = S5_T
    hp = lax.Precision.HIGHEST
    lre = jnp.minimum(lam_re, -1e-4)
    lim = lam_im
    dt = jnp.exp(log_dt)[..., None]
    er, ei = lre * dt, lim * dt
    mag = jnp.exp(er)
    are, aim = mag * jnp.cos(ei), mag * jnp.sin(ei)
    den = lre * lre + lim * lim
    qre = ((are - 1.0) * lre + aim * lim) / den
    qim = (aim * lre - (are - 1.0) * lim) / den
    bbr = qre[..., None] * b_re - qim[..., None] * b_im
    bbi = qre[..., None] * b_im + qim[..., None] * b_re
    n = jnp.arange(t + 1, dtype=F32)[:, None, None, None]
    pmag = jnp.exp(er[None] * n)
    pr, pi = pmag * jnp.cos(ei[None] * n), pmag * jnp.sin(ei[None] * n)

    cbr = c_re[..., None] * bbr[:, :, None] - c_im[..., None] * bbi[:, :, None]
    cbi = c_re[..., None] * bbi[:, :, None] + c_im[..., None] * bbr[:, :, None]
    kk = (jnp.einsum('tdgp,dgcpk->tdgck', pr[:t], cbr, precision=hp)
          - jnp.einsum('tdgp,dgcpk->tdgck', pi[:t], cbi, precision=hp))
    ii = np.arange(t)
    idx = ii[None, :] - ii[:, None]
    mf = kk[np.clip(idx, 0, t - 1), 0] * jnp.asarray((idx >= 0)[:, :, None, None, None], F32)
    mb = kk[np.clip(-idx, 0, t - 1), 1] * jnp.asarray((idx <= 0)[:, :, None, None, None], F32)
    m_intra = jnp.transpose(mf + mb, (2, 0, 4, 1, 3)).reshape(S5_GROUPS, S5_TC, S5_TC)

    def state_cols(pw_r, pw_i, d):
        sr = pw_r[..., None] * bbr[d][None] - pw_i[..., None] * bbi[d][None]
        si = pw_r[..., None] * bbi[d][None] + pw_i[..., None] * bbr[d][None]
        f = lambda a: jnp.transpose(a, (1, 0, 3, 2)).reshape(S5_GROUPS, S5_TC, S5_STATE)
        return f(sr), f(si)

    sfr, sfi = state_cols(pr[:t, 0][::-1], pi[:t, 0][::-1], 0)
    sbr, sbi = state_cols(pr[:t, 1], pi[:t, 1], 1)
    m_state = jnp.concatenate([sfr, sbr, sfi, sbi], axis=-1)

    def out_rows(pw_r, pw_i, d):
        wr = c_re[d][None] * pw_r[:, :, None, :] - c_im[d][None] * pw_i[:, :, None, :]
        wi = c_re[d][None] * pw_i[:, :, None, :] + c_im[d][None] * pw_r[:, :, None, :]
        f = lambda a: jnp.transpose(a, (1, 3, 0, 2)).reshape(S5_GROUPS, S5_STATE, S5_TC)
        return f(wr), f(-wi)

    ofr, ofi = out_rows(pr[1:t + 1, 0], pi[1:t + 1, 0], 0)
    obr, obi = out_rows(pr[1:t + 1, 1][::-1], pi[1:t + 1, 1][::-1], 1)
    m_out = jnp.concatenate([ofr, obr, ofi, obi], axis=1)

    a_t = jnp.stack([jnp.concatenate([pr[t, 0], pr[t, 1]], axis=-1),
                     jnp.concatenate([pi[t, 0], pi[t, 1]], axis=-1)], axis=1)
    return m_intra.astype(BF16), m_state.astype(BF16), m_out.astype(BF16), a_t


def _s5_kernel(u_ref, mi_ref, ms_ref, mo_ref, at_ref, h0_ref, y_ref, hend_ref, s_s, hst_s):
    p = S5_STATE
    u = u_ref[0]
    s_s[...] = _dot(u, ms_ref[0])
    are = at_ref[0, 0:1, :]
    aim = at_ref[0, 1:2, :]
    fwd_lane = lax.broadcasted_iota(jnp.int32, (1, 2 * p), 1) < p

    def scan(row0, nk, nb, hre, him):
        for st in range(nk):
            rf = slice(row0 + st * nb, row0 + (st + 1) * nb)
            rb = slice(row0 + (nk - 1 - st) * nb, row0 + (nk - st) * nb)
            hst_s[rf, 0:p] = hre[:, 0:p]
            hst_s[rb, p:2 * p] = hre[:, p:2 * p]
            hst_s[rf, 2 * p:3 * p] = him[:, 0:p]
            hst_s[rb, 3 * p:4 * p] = him[:, p:2 * p]
            sre = jnp.where(fwd_lane, s_s[rf, 0:2 * p], s_s[rb, 0:2 * p])
            sim = jnp.where(fwd_lane, s_s[rf, 2 * p:4 * p], s_s[rb, 2 * p:4 * p])
            hre, him = are * hre - aim * him + sre, are * him + aim * hre + sim
        return hre, him

    zero = jnp.zeros((BATCH, 2 * p), F32)
    hre, him = scan(0, S5_NK_CTX, BATCH, zero, zero)
    hend_ref[0] = jnp.concatenate([hre, him], axis=1)
    scan(S5_ROWS_CTX, S5_NK_LAT, DEC_BATCH, h0_ref[0, :, 0:2 * p], h0_ref[0, :, 2 * p:4 * p])
    y_ref[0] = _dot(u, mi_ref[0]) + _dot(hst_s[...].astype(BF16), mo_ref[0])


def _s5_call(ut, m_intra, m_state, m_out, a_t, h0):
    rows = S5_ROWS_CTX + S5_ROWS_LAT
    g3 = lambda g: (g, 0, 0)
    return pl.pallas_call(
        _s5_kernel,
        out_shape=(jax.ShapeDtypeStruct((S5_GROUPS, rows, S5_TC), F32),
                   jax.ShapeDtypeStruct((S5_GROUPS, BATCH, 4 * S5_STATE), F32)),
        grid=(S5_GROUPS,),
        in_specs=[
            pl.BlockSpec((1, rows, S5_TC), g3),
            pl.BlockSpec((1, S5_TC, S5_TC), g3),
            pl.BlockSpec((1, S5_TC, 4 * S5_STATE), g3),
            pl.BlockSpec((1, 4 * S5_STATE, S5_TC), g3),
            pl.BlockSpec((1, 2, 2 * S5_STATE), g3),
            pl.BlockSpec((1, DEC_BATCH, 4 * S5_STATE), g3),
        ],
        out_specs=(pl.BlockSpec((1, rows, S5_TC), g3), pl.BlockSpec((1, BATCH, 4 * S5_STATE), g3)),
        scratch_shapes=[pltpu.VMEM((rows, 4 * S5_STATE), F32), pltpu.VMEM((rows, 4 * S5_STATE), F32)],
        compiler_params=_cparams(("arbitrary",), 32),
        name="s5_chunks",
    )(ut, m_intra, m_state, m_out, a_t, h0)


def _to_chunk_rows(u, nk):
    b = u.shape[0]
    u = u.reshape(b, nk, S5_T, S5_GROUPS, S5_CH)
    return jnp.transpose(u, (3, 1, 0, 2, 4)).reshape(S5_GROUPS, nk * b, S5_TC)


def _from_chunk_rows(y, b, nk):
    y = y.reshape(S5_GROUPS, nk, b, S5_T, S5_CH)
    return jnp.transpose(y, (2, 1, 3, 0, 4)).reshape(b, nk * S5_T, BRANCH_W)


def _post_kernel(x_ref, mod_ref, ng_ref, ba_ref, bb_ref, bd_ref, ys_ref, u_ref, cg_ref, sd_ref, wglu_ref,
                 wm_ref, wbr_ref, wo_ref, fg_ref, o_ref, *, final):
    x = x_ref[0]
    hb = _ada_h(x, mod_ref, ng_ref).astype(BF16)
    yv = ys_ref[0] + sd_ref[...] * u_ref[0]
    gel = 0.5 * yv * (1.0 + jnp.tanh(math.sqrt(2.0 / math.pi) * (yv + 0.044715 * (yv * yv * yv))))
    gl = _dot(gel.astype(BF16), wglu_ref[...])
    oc = gl[:, 0:BRANCH_W] * _sigmoid(gl[:, BRANCH_W:2 * BRANCH_W])
    bc = (oc * _silu(cg_ref[0])).astype(BF16)
    acc = None
    for k, br in enumerate((ba_ref[0], bb_ref[0], bc, bd_ref[0])):
        proj = _dot(br, wbr_ref[k])
        mg = _sigmoid(_dot(hb, wm_ref[:, k * D_MODEL:(k + 1) * D_MODEL]))
        acc = mg * proj if acc is None else acc + mg * proj
    y = _dot(acc.astype(BF16), wo_ref[...])
    out = x + mod_ref[0, :, 2 * D_MODEL:3 * D_MODEL] * y
    if final:
        out = _rms(out, fg_ref[...])
    o_ref[0] = out


def _post_call(x, mod, ng, ba, bb, bd, ys, z, s5d, wglu, wm, wbr, wo, fg, final, name):
    nb, length, _ = x.shape
    tok = lambda b, t: (b, t, 0)
    full2 = lambda b, t: (0, 0)
    br_spec = pl.BlockSpec((1, TM, BRANCH_W), tok)
    return pl.pallas_call(
        functools.partial(_post_kernel, final=final),
        out_shape=jax.ShapeDtypeStruct((nb, length, D_MODEL), F32),
        grid=(nb, length // TM),
        in_specs=[
            pl.BlockSpec((1, TM, D_MODEL), tok),
            pl.BlockSpec((1, 1, 3 * D_MODEL), lambda b, t: (b, 0, 0)),
            pl.BlockSpec((1, D_MODEL), full2),
            br_spec, br_spec, br_spec, br_spec,
            pl.BlockSpec((1, TM, 256), lambda b, t: (b, t, Z_CU // 256)),
            pl.BlockSpec((1, TM, 256), lambda b, t: (b, t, Z_CG // 256)),
            pl.BlockSpec((1, BRANCH_W), full2),
            pl.BlockSpec((BRANCH_W, 2 * BRANCH_W), full2),
            pl.BlockSpec((D_MODEL, N_BRANCH * D_MODEL), full2),
            pl.BlockSpec((N_BRANCH, BRANCH_W, D_MODEL), lambda b, t: (0, 0, 0)),
            pl.BlockSpec((D_MODEL, D_MODEL), full2),
            pl.BlockSpec((1, D_MODEL), full2),
        ],
        out_specs=pl.BlockSpec((1, TM, D_MODEL), tok),
        compiler_params=_cparams(("arbitrary", "arbitrary"), 56),
        name=name,
    )(x, mod, ng, ba, bb, bd, ys, z, z, s5d, wglu, wm, wbr, wo, fg)


def _rope_tables():
    t = np.arange(DEC_SEQ)
    row = (t // GRID_W).astype(np.float64)
    col = (t % GRID_W).astype(np.float64)

    def table(half):
        inv = ROPE_BASE ** (-np.arange(half, dtype=np.float64) / half)
        zeros = np.zeros((DEC_SEQ, half))
        cs, s_up, s_lo = [], [], []
        for pos in (row, col):
            ang = pos[:, None] * inv[None, :]
            c, s = np.cos(ang), np.sin(ang)
            cs += [c, c]
            s_up += [zeros, s]
            s_lo += [-s, zeros]
        reps = LANE // (4 * half)
        f = lambda parts: np.tile(np.concatenate(parts, axis=1), (1, reps))
        return jnp.asarray(np.stack([f(cs), f(s_up), f(s_lo)]), F32)

    return table(16), table(8)


def _reorder_w_in(w):
    pad = jnp.zeros((D_MODEL, R_W - R_KPE - MLA_ROPE), w.dtype)
    return jnp.concatenate([w[:, 0:512], w[:, 1952:2720], w[:, 768:1152], w[:, 1440:1696], w[:, 512:768],
                            w[:, 1184:1440], w[:, 1696:1952], w[:, 2720:2976], w[:, 1152:1184], pad], axis=1)


def _reorder_w_q_up(w):
    w = w.reshape(MLA_Q_LORA, MLA_HEADS, MLA_NOPE + MLA_ROPE)
    return jnp.concatenate([w[:, :, :MLA_NOPE].reshape(MLA_Q_LORA, -1), w[:, :, MLA_NOPE:].reshape(MLA_Q_LORA, -1)],
                           axis=1)


def kernel(x_prompt, x_sample, cache_a_k, cache_a_v, cache_mla_ckv, cache_mla_kpe, cache_na_k, cache_na_v,
           state_s5_re, state_s5_im, c, c_ctx, w_mod, b_mod, norm_g, w_in, w_merge, a_sink,
           mla_q_norm, mla_w_q_up, mla_kv_norm, mla_w_kv_up, s5_lam_re, s5_lam_im, s5_log_dt,
           s5_b_re, s5_b_im, s5_c_re, s5_c_im, s5_d, s5_w_glu, na_rpb, w_branch, w_out, final_norm_g):
    n_ctx = BATCH * SEQ
    conds = jnp.concatenate([c, c_ctx[None, :], jnp.zeros((3, D_MODEL), F32)], axis=0)
    mods = _mod_call(conds, w_mod, b_mod)
    tabs = _rope_tables()
    ck_a = cache_a_k.reshape(DEC_BATCH, DEPTH, PAST_LEN, A_KV_HEADS * HEAD_DIM)
    cv_a = cache_a_v.reshape(DEC_BATCH, DEPTH, PAST_LEN, A_KV_HEADS * HEAD_DIM)
    ck_n = cache_na_k.reshape(DEC_BATCH, DEPTH, PAST_LEN, NAT_HEADS * HEAD_DIM)
    cv_n = cache_na_v.reshape(DEC_BATCH, DEPTH, PAST_LEN, NAT_HEADS * HEAD_DIM)
    fg = final_norm_g.reshape(1, D_MODEL)

    yp = x_prompt.reshape(1, n_ctx, D_MODEL)
    ys = x_sample
    outs = {k: [] for k in ('ak', 'av', 'ckv', 'kpe', 'nk', 'nv', 're', 'im')}
    for i in range(DEPTH):
        mod_lat = mods[i, 0:DEC_BATCH].reshape(DEC_BATCH, 1, 3 * D_MODEL)
        mod_ctx = mods[i, DEC_BATCH:DEC_BATCH + 1].reshape(1, 1, 3 * D_MODEL)
        ng = norm_g[i].reshape(1, D_MODEL)
        w_raw = _reorder_w_in(w_in[i]).astype(BF16)
        wq = _reorder_w_q_up(mla_w_q_up[i]).astype(BF16)
        wkv = mla_w_kv_up[i].astype(BF16)
        qnorm = mla_q_norm[i].reshape(1, MLA_Q_LORA)
        kvnorm = mla_kv_norm[i].reshape(1, MLA_KV_LORA)
        wm = w_merge[i].astype(BF16)
        wbr = w_branch[i].astype(BF16)
        wo = w_out[i].astype(BF16)
        wglu = s5_w_glu[i].astype(BF16)
        s5d = s5_d[i].reshape(1, BRANCH_W)

        z_ctx = _inproj_call(yp, mod_ctx, ng, w_raw, qnorm, wq, kvnorm, None, "inproj_ctx")
        z_lat = _inproj_call(ys, mod_lat, ng, w_raw, qnorm, wq, kvnorm, tabs, "inproj_lat")
        zc = z_ctx.reshape(BATCH, SEQ, Z_W)

        m_intra, m_state, m_out, a_t = _s5_mats(s5_lam_re[i], s5_lam_im[i], s5_log_dt[i], s5_b_re[i], s5_b_im[i],
                                                s5_c_re[i], s5_c_im[i])
        ut = jnp.concatenate([_to_chunk_rows(zc[:, :, Z_CU:Z_CU + BRANCH_W], S5_NK_CTX),
                              _to_chunk_rows(z_lat[:, :, Z_CU:Z_CU + BRANCH_W], S5_NK_LAT)], axis=1).astype(BF16)
        sre, sim = state_s5_re[:, i], state_s5_im[:, i]
        h0 = jnp.transpose(jnp.concatenate([sre[:, 0], sre[:, 1], sim[:, 0], sim[:, 1]], axis=-1), (1, 0, 2))
        y_s5, hend = _s5_call(ut, m_intra, m_state, m_out, a_t, h0)
        ys5_ctx = _from_chunk_rows(y_s5[:, :S5_ROWS_CTX], BATCH, S5_NK_CTX).reshape(1, n_ctx, BRANCH_W)
        ys5_lat = _from_chunk_rows(y_s5[:, S5_ROWS_CTX:], DEC_BATCH, S5_NK_LAT)
        hend = jnp.transpose(hend.reshape(S5_GROUPS, BATCH, 2, 2, S5_STATE), (1, 2, 3, 0, 4))
        outs['re'].append(hend[:, 0])
        outs['im'].append(hend[:, 1])

        oa_c, ob_c, od_c = _ctx_attn_call(zc, a_sink[i], wkv)
        oa_l = _win_attn_call(z_lat, a_sink[i], ck_a, cv_a, i)
        ob_l = _mla_call(z_lat, cache_mla_ckv, cache_mla_kpe, wkv, i)
        od_l = _nat_call(z_lat, ck_n, cv_n, _nat_bias_table(na_rpb[i]), i)

        flat = lambda a: a.reshape(1, n_ctx, BRANCH_W)
        final = i == DEPTH - 1
        yp = _post_call(yp, mod_ctx, ng, flat(oa_c), flat(ob_c), flat(od_c), ys5_ctx, z_ctx, s5d, wglu, wm, wbr, wo,
                        fg, final, "post_ctx")
        ys = _post_call(ys, mod_lat, ng, oa_l, ob_l, od_l, ys5_lat, z_lat, s5d, wglu, wm, wbr, wo, fg, final,
                        "post_lat")

        outs['ak'].append(zc[:, :, Z_AK:Z_AK + 128].reshape(BATCH, SEQ, A_KV_HEADS, HEAD_DIM))
        outs['av'].append(zc[:, :, Z_AV:Z_AV + 128].reshape(BATCH, SEQ, A_KV_HEADS, HEAD_DIM))
        outs['ckv'].append(zc[:, :, Z_CKV:Z_CKV + MLA_KV_LORA])
        outs['kpe'].append(zc[:, :, Z_KPE:Z_KPE + MLA_ROPE])
        outs['nk'].append(zc[:, :, Z_DK:Z_DK + 256].reshape(BATCH, SEQ, NAT_HEADS, HEAD_DIM))
        outs['nv'].append(zc[:, :, Z_DV:Z_DV + 256].reshape(BATCH, SEQ, NAT_HEADS, HEAD_DIM))

    st = lambda k: jnp.stack(outs[k], axis=1)
    return (yp.reshape(BATCH, SEQ, D_MODEL), ys, st('ak'), st('av'), st('ckv'), st('kpe'), st('nk'), st('nv'),
            st('re'), st('im'))
```

```python
import functools
import math

import numpy as np
import jax
import jax.numpy as jnp
from jax import lax
from jax.experimental import pallas as pl
from jax.experimental.pallas import tpu as pltpu

F32 = jnp.float32
BF16 = jnp.bfloat16

D_MODEL = 1024
BATCH = 32
SEQ = 256
DEPTH = 2
DEC_BATCH = 4
DEC_SEQ = 2048
PAST_LEN = 512
GRID_W = 64
HEAD_DIM = 64
BRANCH_W = 256
N_BRANCH = 4
Q_BLOCK = 128
A_HEADS = 4
A_KV_HEADS = 2
A_GROUP = A_HEADS // A_KV_HEADS
A_WINDOW = 128
MLA_HEADS = 4
MLA_Q_LORA = 256
MLA_KV_LORA = 128
MLA_NOPE = 64
MLA_ROPE = 32
MLA_V = 64
S5_CH = 16
S5_GROUPS = BRANCH_W // S5_CH
S5_STATE = 64
NAT_HEADS = 4
NAT_ROWS = 8
NAT_COLS = 16
ROPE_BASE = 10000.0
EPS = 1e-6
NEG = -1e30

LANE = 128

Z_AQ, Z_AK, Z_AV = 0, 256, 384
Z_DQ, Z_DK, Z_DV = 512, 768, 1024
Z_QN, Z_QR, Z_CKV = 1280, 1536, 1664
Z_CU = 1792
Z_AG, Z_BG, Z_CG, Z_DG = 2048, 2304, 2560, 2816
Z_KPE = 3072
Z_W = 3200
R_CQ, R_CKV, R_CU, R_KPE, R_W = 1280, 1536, 1664, 2944, 3072

S5_T = 32
S5_TC = S5_T * S5_CH
S5_ROWS_CTX = BATCH * SEQ // S5_T
S5_ROWS_LAT = DEC_BATCH * DEC_SEQ // S5_T
S5_NK_CTX = SEQ // S5_T
S5_NK_LAT = DEC_SEQ // S5_T

TM = 512
TQ_MLA = 256


def _cparams(sem, vmem_mb):
    return pltpu.CompilerParams(dimension_semantics=sem, vmem_limit_bytes=vmem_mb * 1024 * 1024)


def _sigmoid(x):
    return 1.0 / (1.0 + jnp.exp(-x))


def _silu(x):
    return x * _sigmoid(x)


def _rms(x, g):
    return x * lax.rsqrt(jnp.mean(x * x, axis=-1, keepdims=True) + EPS) * g


def _dot(a, b):
    return jnp.dot(a, b, preferred_element_type=F32)


def _dot_nt(a, b):
    return lax.dot_general(a, b, (((1,), (1,)), ((), ())), preferred_element_type=F32)


def _mod_kernel(c_ref, w_ref, b_ref, o_ref):
    s = _silu(c_ref[...])
    o_ref[0] = _dot(s.astype(BF16), w_ref[0].astype(BF16)) + b_ref[0]


def _mod_call(conds, w_mod, b_mod):
    nc = 512
    return pl.pallas_call(
        _mod_kernel,
        out_shape=jax.ShapeDtypeStruct((DEPTH, 8, 3 * D_MODEL), F32),
        grid=(DEPTH, 3 * D_MODEL // nc),
        in_specs=[
            pl.BlockSpec((8, D_MODEL), lambda i, j: (0, 0)),
            pl.BlockSpec((1, D_MODEL, nc), lambda i, j: (i, 0, j)),
            pl.BlockSpec((1, 1, nc), lambda i, j: (i, 0, j)),
        ],
        out_specs=pl.BlockSpec((1, 8, nc), lambda i, j: (i, 0, j)),
        compiler_params=_cparams(("arbitrary", "arbitrary"), 32),
        name="mod_rows",
    )(conds, w_mod, b_mod.reshape(DEPTH, 1, 3 * D_MODEL))


def _ada_h(x, mod_ref, ng_ref):
    shift = mod_ref[0, :, 0:D_MODEL]
    scale = mod_ref[0, :, D_MODEL:2 * D_MODEL]
    return _rms(x, ng_ref[...]) * (1.0 + scale) + shift


def _rope_block(xs, tab_ref, shift):
    return (xs * tab_ref[0] + pltpu.roll(xs, shift, 1) * tab_ref[1]
            + pltpu.roll(xs, LANE - shift, 1) * tab_ref[2])


def _inproj_kernel(*refs, rope):
    if rope:
        x_ref, mod_ref, ng_ref, w_ref, qn_ref, wq_ref, kvn_ref, ta_ref, tm_ref, z_ref = refs
    else:
        x_ref, mod_ref, ng_ref, w_ref, qn_ref, wq_ref, kvn_ref, z_ref = refs
    h = _ada_h(x_ref[0], mod_ref, ng_ref)
    raw = _dot(h.astype(BF16), w_ref[...])
    if rope:
        for j in range(Z_AV // LANE):
            z_ref[0, :, j * LANE:(j + 1) * LANE] = _rope_block(raw[:, j * LANE:(j + 1) * LANE], ta_ref, 16)
    else:
        z_ref[0, :, 0:Z_AV] = raw[:, 0:Z_AV]
    z_ref[0, :, Z_AV:Z_QN] = raw[:, Z_AV:R_CQ]
    qn = _rms(raw[:, R_CQ:R_CKV], qn_ref[...])
    q = _dot(qn.astype(BF16), wq_ref[...])
    z_ref[0, :, Z_QN:Z_QR] = q[:, 0:256]
    qr = q[:, 256:384]
    kp = raw[:, R_KPE:R_W]
    if rope:
        qr = _rope_block(qr, tm_ref, 8)
        kp = _rope_block(kp, tm_ref, 8)
    z_ref[0, :, Z_QR:Z_CKV] = qr
    z_ref[0, :, Z_CKV:Z_CU] = _rms(raw[:, R_CKV:R_CU], kvn_ref[...])
    z_ref[0, :, Z_CU:Z_KPE] = raw[:, R_CU:R_KPE]
    z_ref[0, :, Z_KPE:Z_W] = kp


def _inproj_call(x, mod, ng, w_raw, qnorm, wq, kvnorm, tabs, name):
    nb, length, _ = x.shape
    rope = tabs is not None
    full2 = lambda b, t: (0, 0)
    in_specs = [
        pl.BlockSpec((1, TM, D_MODEL), lambda b, t: (b, t, 0)),
        pl.BlockSpec((1, 1, 3 * D_MODEL), lambda b, t: (b, 0, 0)),
        pl.BlockSpec((1, D_MODEL), full2),
        pl.BlockSpec((D_MODEL, R_W), full2),
        pl.BlockSpec((1, MLA_Q_LORA), full2),
        pl.BlockSpec((MLA_Q_LORA, 384), full2),
        pl.BlockSpec((1, MLA_KV_LORA), full2),
    ]
    args = [x, mod, ng, w_raw, qnorm, wq, kvnorm]
    if rope:
        in_specs += [pl.BlockSpec((3, TM, LANE), lambda b, t: (0, t, 0))] * 2
        args += list(tabs)
    return pl.pallas_call(
        functools.partial(_inproj_kernel, rope=rope),
        out_shape=jax.ShapeDtypeStruct((nb, length, Z_W), F32),
        grid=(nb, length // TM),
        in_specs=in_specs,
        out_specs=pl.BlockSpec((1, TM, Z_W), lambda b, t: (b, t, 0)),
        compiler_params=_cparams(("arbitrary", "arbitrary"), 56),
        name=name,
    )(*args)


def _softmax_pv(s_list, v_list, sink=None):
    m = jnp.max(s_list[0], axis=-1, keepdims=True)
    for s in s_list[1:]:
        m = jnp.maximum(m, jnp.max(s, axis=-1, keepdims=True))
    if sink is not None:
        m = jnp.maximum(m, sink)
    den = None
    o = None
    for s, v in zip(s_list, v_list):
        e = jnp.exp(s - m)
        d = jnp.sum(e, axis=-1, keepdims=True)
        pv = _dot(e.astype(BF16), v)
        den = d if den is None else den + d
        o = pv if o is None else o + pv
    if sink is not None:
        den = den + jnp.exp(sink - m)
    return o / den


def _ctx_attn_kernel(sink_ref, z_ref, wkv_ref, oa_ref, ob_ref, od_ref):
    hd = HEAD_DIM
    for h in range(A_HEADS):
        hk = h // A_GROUP
        q = z_ref[0, :, Z_AQ + hd * h:Z_AQ + hd * (h + 1)].astype(BF16)
        k = z_ref[0, :, Z_AK + hd * hk:Z_AK + hd * (hk + 1)].astype(BF16)
        v = z_ref[0, :, Z_AV + hd * hk:Z_AV + hd * (hk + 1)].astype(BF16)
        s = _dot_nt(q, k) * (hd ** -0.5)
        o = _softmax_pv([s], [v], sink=sink_ref[h])
        g = z_ref[0, :, Z_AG + hd * h:Z_AG + hd * (h + 1)]
        oa_ref[0, :, hd * h:hd * (h + 1)] = (o * _silu(g)).astype(BF16)
    kv = _dot(z_ref[0, :, Z_CKV:Z_CKV + MLA_KV_LORA].astype(BF16), wkv_ref[...])
    kpe = z_ref[0, :, Z_KPE:Z_KPE + MLA_ROPE].astype(BF16)
    for h in range(MLA_HEADS):
        qn = z_ref[0, :, Z_QN + 64 * h:Z_QN + 64 * (h + 1)].astype(BF16)
        qr = z_ref[0, :, Z_QR + 32 * h:Z_QR + 32 * (h + 1)].astype(BF16)
        kn = kv[:, 128 * h:128 * h + 64].astype(BF16)
        v = kv[:, 128 * h + 64:128 * (h + 1)].astype(BF16)
        s = (_dot_nt(qn, kn) + _dot_nt(qr, kpe)) * ((MLA_NOPE + MLA_ROPE) ** -0.5)
        o = _softmax_pv([s], [v])
        g = z_ref[0, :, Z_BG + 64 * h:Z_BG + 64 * (h + 1)]
        ob_ref[0, :, 64 * h:64 * (h + 1)] = (o * _silu(g)).astype(BF16)
    for h in range(NAT_HEADS):
        q = z_ref[0, :, Z_DQ + hd * h:Z_DQ + hd * (h + 1)].astype(BF16)
        k = z_ref[0, :, Z_DK + hd * h:Z_DK + hd * (h + 1)].astype(BF16)
        v = z_ref[0, :, Z_DV + hd * h:Z_DV + hd * (h + 1)].astype(BF16)
        s = _dot_nt(q, k) * (hd ** -0.5)
        o = _softmax_pv([s], [v])
        g = z_ref[0, :, Z_DG + hd * h:Z_DG + hd * (h + 1)]
        od_ref[0, :, hd * h:hd * (h + 1)] = (o * _silu(g)).astype(BF16)


def _ctx_attn_call(z, sink, wkv):
    out = jax.ShapeDtypeStruct((BATCH, SEQ, BRANCH_W), BF16)
    ospec = pl.BlockSpec((1, SEQ, BRANCH_W), lambda b: (b, 0, 0))
    return pl.pallas_call(
        _ctx_attn_kernel,
        out_shape=(out, out, out),
        grid=(BATCH,),
        in_specs=[
            pl.BlockSpec(memory_space=pltpu.SMEM),
            pl.BlockSpec((1, SEQ, Z_W), lambda b: (b, 0, 0)),
            pl.BlockSpec((MLA_KV_LORA, 512), lambda b: (0, 0)),
        ],
        out_specs=(ospec, ospec, ospec),
        compiler_params=_cparams(("arbitrary",), 40),
        name="ctx_attn",
    )(sink, z, wkv)


def _win_attn_kernel(sink_ref, q_ref, kv_ref, g_ref, ck_ref, cv_ref, o_ref):
    hd = HEAD_DIM
    n = pl.program_id(1)
    nb = DEC_SEQ // Q_BLOCK
    blocks = []
    for off in (-1, 0, 1):
        start = pl.multiple_of(jnp.clip(n + off, 0, nb - 1) * Q_BLOCK, Q_BLOCK)
        blocks.append(kv_ref[0, pl.ds(start, Q_BLOCK), :])
    kvw = jnp.concatenate(blocks, axis=0).astype(BF16)
    qpos = n * Q_BLOCK + lax.broadcasted_iota(jnp.int32, (Q_BLOCK, 3 * Q_BLOCK), 0)
    kpos = (n - 1) * Q_BLOCK + lax.broadcasted_iota(jnp.int32, (Q_BLOCK, 3 * Q_BLOCK), 1)
    mask = (jnp.abs(qpos - kpos) <= A_WINDOW) & (kpos >= 0) & (kpos < DEC_SEQ)
    ck = ck_ref[0, 0].astype(BF16)
    cv = cv_ref[0, 0].astype(BF16)
    for h in range(A_HEADS):
        hk = h // A_GROUP
        q = q_ref[0, :, hd * h:hd * (h + 1)].astype(BF16)
        kw = kvw[:, hd * hk:hd * (hk + 1)]
        vw = kvw[:, 2 * hd + hd * hk:2 * hd + hd * (hk + 1)]
        s_win = jnp.where(mask, _dot_nt(q, kw) * (hd ** -0.5), NEG)
        s_ctx = _dot_nt(q, ck[:, hd * hk:hd * (hk + 1)]) * (hd ** -0.5)
        o = _softmax_pv([s_win, s_ctx], [vw, cv[:, hd * hk:hd * (hk + 1)]], sink=sink_ref[h])
        g = g_ref[0, :, hd * h:hd * (h + 1)]
        o_ref[0, :, hd * h:hd * (h + 1)] = (o * _silu(g)).astype(BF16)


def _win_attn_call(z, sink, cache_k, cache_v, layer):
    kvw = A_KV_HEADS * HEAD_DIM
    return pl.pallas_call(
        _win_attn_kernel,
        out_shape=jax.ShapeDtypeStruct((DEC_BATCH, DEC_SEQ, BRANCH_W), BF16),
        grid=(DEC_BATCH, DEC_SEQ // Q_BLOCK),
        in_specs=[
            pl.BlockSpec(memory_space=pltpu.SMEM),
            pl.BlockSpec((1, Q_BLOCK, 256), lambda b, n: (b, n, Z_AQ // 256)),
            pl.BlockSpec((1, DEC_SEQ, 256), lambda b, n: (b, 0, Z_AK // 256)),
            pl.BlockSpec((1, Q_BLOCK, 256), lambda b, n: (b, n, Z_AG // 256)),
            pl.BlockSpec((1, 1, PAST_LEN, kvw), lambda b, n: (b, layer, 0, 0)),
            pl.BlockSpec((1, 1, PAST_LEN, kvw), lambda b, n: (b, layer, 0, 0)),
        ],
        out_specs=pl.BlockSpec((1, Q_BLOCK, BRANCH_W), lambda b, n: (b, n, 0)),
        compiler_params=_cparams(("arbitrary", "arbitrary"), 32),
        name="lat_window_attn",
    )(sink, z, z, z, cache_k, cache_v)


def _nat_kernel(q_ref, k_ref, v_ref, g_ref, ck_ref, cv_ref, bias_ref, o_ref):
    hd = HEAD_DIM
    rows = DEC_SEQ // GRID_W
    kr = NAT_ROWS
    r = pl.program_id(1)
    rs = jnp.clip(r - kr // 2, 0, rows - kr)
    start = pl.multiple_of(rs * GRID_W, GRID_W)
    kl = k_ref[0, pl.ds(start, kr * GRID_W), :].astype(BF16)
    vl = v_ref[0, pl.ds(start, kr * GRID_W), :].astype(BF16)
    ck = ck_ref[0, 0].astype(BF16)
    cv = cv_ref[0, 0].astype(BF16)
    shape = (GRID_W, kr * GRID_W)
    qc = lax.broadcasted_iota(jnp.int32, shape, 0)
    kc = lax.broadcasted_iota(jnp.int32, shape, 1) & (GRID_W - 1)
    rel = kc - jnp.clip(qc - NAT_COLS // 2, 0, GRID_W - NAT_COLS)
    mask = (rel >= 0) & (rel < NAT_COLS)
    bidx = rs - r + (NAT_ROWS - 1)
    for h in range(NAT_HEADS):
        q = q_ref[0, :, hd * h:hd * (h + 1)].astype(BF16)
        bias = jnp.concatenate([bias_ref[h, bidx + 2 * m] for m in range(kr // 2)], axis=1)
        s_lat = _dot_nt(q, kl[:, hd * h:hd * (h + 1)]) * (hd ** -0.5) + bias
        s_lat = jnp.where(mask, s_lat, NEG)
        s_ctx = _dot_nt(q, ck[:, hd * h:hd * (h + 1)]) * (hd ** -0.5)
        o = _softmax_pv([s_lat, s_ctx], [vl[:, hd * h:hd * (h + 1)], cv[:, hd * h:hd * (h + 1)]])
        g = g_ref[0, :, hd * h:hd * (h + 1)]
        o_ref[0, :, hd * h:hd * (h + 1)] = (o * _silu(g)).astype(BF16)


def _nat_bias_table(rpb):
    nc = NAT_COLS - 1
    rep = lambda a, n: jnp.broadcast_to(a, a.shape[:-1] + (n,))
    ring = jnp.concatenate([rpb[..., nc:], rep(rpb[..., -1:], GRID_W - 1 - nc), rep(rpb[..., :1], GRID_W - nc),
                            rpb[..., :nc]], axis=-1)
    w = 2 * GRID_W - 1
    tq = jnp.tile(ring, (1, 1, GRID_W))[..., :GRID_W * w].reshape(NAT_HEADS, 2 * NAT_ROWS - 1, GRID_W, w)
    tq = tq[..., :GRID_W]
    nxt = jnp.concatenate([tq[:, 1:], jnp.zeros_like(tq[:, :1])], axis=1)
    return jnp.concatenate([tq, nxt], axis=-1).astype(F32)


def _nat_call(z, cache_k, cache_v, bias, layer):
    hw = NAT_HEADS * HEAD_DIM
    return pl.pallas_call(
        _nat_kernel,
        out_shape=jax.ShapeDtypeStruct((DEC_BATCH, DEC_SEQ, BRANCH_W), BF16),
        grid=(DEC_BATCH, DEC_SEQ // GRID_W),
        in_specs=[
            pl.BlockSpec((1, GRID_W, 256), lambda b, r: (b, r, Z_DQ // 256)),
            pl.BlockSpec((1, DEC_SEQ, 256), lambda b, r: (b, 0, Z_DK // 256)),
            pl.BlockSpec((1, DEC_SEQ, 256), lambda b, r: (b, 0, Z_DV // 256)),
            pl.BlockSpec((1, GRID_W, 256), lambda b, r: (b, r, Z_DG // 256)),
            pl.BlockSpec((1, 1, PAST_LEN, hw), lambda b, r: (b, layer, 0, 0)),
            pl.BlockSpec((1, 1, PAST_LEN, hw), lambda b, r: (b, layer, 0, 0)),
            pl.BlockSpec(bias.shape, lambda b, r: (0, 0, 0, 0)),
        ],
        out_specs=pl.BlockSpec((1, GRID_W, BRANCH_W), lambda b, r: (b, r, 0)),
        compiler_params=_cparams(("arbitrary", "arbitrary"), 40),
        name="lat_nat_attn",
    )(z, z, z, z, cache_k, cache_v, bias)


def _mla_kernel(qn_ref, qr_ref, ckv_ref, kpe_ref, g_ref, cckv_ref, ckpe_ref, wkv_ref, o_ref, kv_s, kpe_s):
    nlat = DEC_SEQ

    @pl.when(pl.program_id(1) == 0)
    def _():
        rows = 512
        for c in range(nlat // rows):
            blk = ckv_ref[0, c * rows:(c + 1) * rows, :].astype(BF16)
            kv_s[c * rows:(c + 1) * rows, :] = _dot(blk, wkv_ref[...]).astype(BF16)
        kv_s[nlat:nlat + PAST_LEN, :] = _dot(cckv_ref[0, 0].astype(BF16), wkv_ref[...]).astype(BF16)
        kpe_s[0:nlat, :] = kpe_ref[0, :, 0:MLA_ROPE].astype(BF16)
        kpe_s[nlat:nlat + PAST_LEN, :] = ckpe_ref[0, 0].astype(BF16)

    kpe = kpe_s[...]
    for h in range(MLA_HEADS):
        qn = qn_ref[0, :, 64 * h:64 * (h + 1)].astype(BF16)
        qr = qr_ref[0, :, 32 * h:32 * (h + 1)].astype(BF16)
        kn = kv_s[:, 128 * h:128 * h + 64]
        v = kv_s[:, 128 * h + 64:128 * (h + 1)]
        s = (_dot_nt(qn, kn) + _dot_nt(qr, kpe)) * ((MLA_NOPE + MLA_ROPE) ** -0.5)
        o = _softmax_pv([s], [v])
        g = g_ref[0, :, 64 * h:64 * (h + 1)]
        o_ref[0, :, 64 * h:64 * (h + 1)] = (o * _silu(g)).astype(BF16)


def _mla_call(z, cache_ckv, cache_kpe, wkv, layer):
    nkeys = DEC_SEQ + PAST_LEN
    return pl.pallas_call(
        _mla_kernel,
        out_shape=jax.ShapeDtypeStruct((DEC_BATCH, DEC_SEQ, BRANCH_W), BF16),
        grid=(DEC_BATCH, DEC_SEQ // TQ_MLA),
        in_specs=[
            pl.BlockSpec((1, TQ_MLA, 256), lambda b, t: (b, t, Z_QN // 256)),
            pl.BlockSpec((1, TQ_MLA, LANE), lambda b, t: (b, t, Z_QR // LANE)),
            pl.BlockSpec((1, DEC_SEQ, LANE), lambda b, t: (b, 0, Z_CKV // LANE)),
            pl.BlockSpec((1, DEC_SEQ, LANE), lambda b, t: (b, 0, Z_KPE // LANE)),
            pl.BlockSpec((1, TQ_MLA, 256), lambda b, t: (b, t, Z_BG // 256)),
            pl.BlockSpec((1, 1, PAST_LEN, MLA_KV_LORA), lambda b, t: (b, layer, 0, 0)),
            pl.BlockSpec((1, 1, PAST_LEN, MLA_ROPE), lambda b, t: (b, layer, 0, 0)),
            pl.BlockSpec((MLA_KV_LORA, 512), lambda b, t: (0, 0)),
        ],
        out_specs=pl.BlockSpec((1, TQ_MLA, BRANCH_W), lambda b, t: (b, t, 0)),
        scratch_shapes=[pltpu.VMEM((nkeys, 512), BF16), pltpu.VMEM((nkeys, MLA_ROPE), BF16)],
        compiler_params=_cparams(("arbitrary", "arbitrary"), 48),
        name="lat_mla_attn",
    )(z, z, z, z, z, cache_ckv, cache_kpe, wkv)


def _s5_mats(lam_re, lam_im, log_dt, b_re, b_im, c_re, c_im):
    t = S5_T
    hp = lax.Precision.HIGHEST
    lre = jnp.minimum(lam_re, -1e-4)
    lim = lam_im
    dt = jnp.exp(log_dt)[..., None]
    er, ei = lre * dt, lim * dt
    mag = jnp.exp(er)
    are, aim = mag * jnp.cos(ei), mag * jnp.sin(ei)
    den = lre * lre + lim * lim
    qre = ((are - 1.0) * lre + aim * lim) / den
    qim = (aim * lre - (are - 1.0) * lim) / den
    bbr = qre[..., None] * b_re - qim[..., None] * b_im
    bbi = qre[..., None] * b_im + qim[..., None] * b_re
    n = jnp.arange(t + 1, dtype=F32)[:, None, None, None]
    pmag = jnp.exp(er[None] * n)
    pr, pi = pmag * jnp.cos(ei[None] * n), pmag * jnp.sin(ei[None] * n)

    cbr = c_re[..., None] * bbr[:, :, None] - c_im[..., None] * bbi[:, :, None]
    cbi = c_re[..., None] * bbi[:, :, None] + c_im[..., None] * bbr[:, :, None]
    kk = (jnp.einsum('tdgp,dgcpk->dgktc', pr[:t], cbr, precision=hp)
          - jnp.einsum('tdgp,dgcpk->dgktc', pi[:t], cbi, precision=hp))
    ring = jnp.concatenate([kk[0][:, :, 0:1] + kk[1][:, :, 0:1], kk[0][:, :, 1:], jnp.zeros_like(kk[0][:, :, 0:1]),
                            kk[1][:, :, 1:][:, :, ::-1]], axis=2).reshape(S5_GROUPS, S5_CH, 2 * S5_TC)
    w = 2 * S5_TC - S5_CH
    rows = jnp.tile(ring, (1, 1, t))[..., :t * w].reshape(S5_GROUPS, S5_CH, t, w)[..., :S5_TC]
    m_intra = jnp.transpose(rows, (0, 2, 1, 3)).reshape(S5_GROUPS, S5_TC, S5_TC)

    def state_cols(pw_r, pw_i, d):
        sr = pw_r[..., None] * bbr[d][None] - pw_i[..., None] * bbi[d][None]
        si = pw_r[..., None] * bbi[d][None] + pw_i[..., None] * bbr[d][None]
        f = lambda a: jnp.transpose(a, (1, 0, 3, 2)).reshape(S5_GROUPS, S5_TC, S5_STATE)
        return f(sr), f(si)

    sfr, sfi = state_cols(pr[:t, 0][::-1], pi[:t, 0][::-1], 0)
    sbr, sbi = state_cols(pr[:t, 1], pi[:t, 1], 1)
    m_state = jnp.concatenate([sfr, sbr, sfi, sbi], axis=-1)

    def out_rows(pw_r, pw_i, d):
        wr = c_re[d][None] * pw_r[:, :, None, :] - c_im[d][None] * pw_i[:, :, None, :]
        wi = c_re[d][None] * pw_i[:, :, None, :] + c_im[d][None] * pw_r[:, :, None, :]
        f = lambda a: jnp.transpose(a, (1, 3, 0, 2)).reshape(S5_GROUPS, S5_STATE, S5_TC)
        return f(wr), f(-wi)

    ofr, ofi = out_rows(pr[1:t + 1, 0], pi[1:t + 1, 0], 0)
    obr, obi = out_rows(pr[1:t + 1, 1][::-1], pi[1:t + 1, 1][::-1], 1)
    m_out = jnp.concatenate([ofr, obr, ofi, obi], axis=1)

    a_t = jnp.stack([jnp.concatenate([pr[t, 0], pr[t, 1]], axis=-1),
                     jnp.concatenate([pi[t, 0], pi[t, 1]], axis=-1)], axis=1)
    return m_intra.astype(BF16), m_state.astype(BF16), m_out.astype(BF16), a_t


def _s5_kernel(u_ref, mi_ref, ms_ref, mo_ref, at_ref, h0_ref, y_ref, hend_ref, s_s, hst_s):
    p = S5_STATE
    u = u_ref[0]
    s_s[...] = _dot(u, ms_ref[0])
    are = at_ref[0, 0:1, :]
    aim = at_ref[0, 1:2, :]
    fwd_lane = lax.broadcasted_iota(jnp.int32, (1, 2 * p), 1) < p

    def scan(row0, nk, nb, hre, him):
        for st in range(nk):
            rf = slice(row0 + st * nb, row0 + (st + 1) * nb)
            rb = slice(row0 + (nk - 1 - st) * nb, row0 + (nk - st) * nb)
            hst_s[rf, 0:p] = hre[:, 0:p]
            hst_s[rb, p:2 * p] = hre[:, p:2 * p]
            hst_s[rf, 2 * p:3 * p] = him[:, 0:p]
            hst_s[rb, 3 * p:4 * p] = him[:, p:2 * p]
            sre = jnp.where(fwd_lane, s_s[rf, 0:2 * p], s_s[rb, 0:2 * p])
            sim = jnp.where(fwd_lane, s_s[rf, 2 * p:4 * p], s_s[rb, 2 * p:4 * p])
            hre, him = are * hre - aim * him + sre, are * him + aim * hre + sim
        return hre, him

    zero = jnp.zeros((BATCH, 2 * p), F32)
    hre, him = scan(0, S5_NK_CTX, BATCH, zero, zero)
    hend_ref[0] = jnp.concatenate([hre, him], axis=1)
    scan(S5_ROWS_CTX, S5_NK_LAT, DEC_BATCH, h0_ref[0, :, 0:2 * p], h0_ref[0, :, 2 * p:4 * p])
    y_ref[0] = _dot(u, mi_ref[0]) + _dot(hst_s[...].astype(BF16), mo_ref[0])


def _s5_call(ut, m_intra, m_state, m_out, a_t, h0):
    rows = S5_ROWS_CTX + S5_ROWS_LAT
    g3 = lambda g: (g, 0, 0)
    return pl.pallas_call(
        _s5_kernel,
        out_shape=(jax.ShapeDtypeStruct((S5_GROUPS, rows, S5_TC), F32),
                   jax.ShapeDtypeStruct((S5_GROUPS, BATCH, 4 * S5_STATE), F32)),
        grid=(S5_GROUPS,),
        in_specs=[
            pl.BlockSpec((1, rows, S5_TC), g3),
            pl.BlockSpec((1, S5_TC, S5_TC), g3),
            pl.BlockSpec((1, S5_TC, 4 * S5_STATE), g3),
            pl.BlockSpec((1, 4 * S5_STATE, S5_TC), g3),
            pl.BlockSpec((1, 2, 2 * S5_STATE), g3),
            pl.BlockSpec((1, DEC_BATCH, 4 * S5_STATE), g3),
        ],
        out_specs=(pl.BlockSpec((1, rows, S5_TC), g3), pl.BlockSpec((1, BATCH, 4 * S5_STATE), g3)),
        scratch_shapes=[pltpu.VMEM((rows, 4 * S5_STATE), F32), pltpu.VMEM((rows, 4 * S5_STATE), F32)],
        compiler_params=_cparams(("arbitrary",), 32),
        name="s5_chunks",
    )(ut, m_intra, m_state, m_out, a_t, h0)


def _to_chunk_rows(u, nk):
    b = u.shape[0]
    u = u.reshape(b, nk, S5_T, S5_GROUPS, S5_CH)
    return jnp.transpose(u, (3, 1, 0, 2, 4)).reshape(S5_GROUPS, nk * b, S5_TC)


def _from_chunk_rows(y, b, nk):
    y = y.reshape(S5_GROUPS, nk, b, S5_T, S5_CH)
    return jnp.transpose(y, (2, 1, 3, 0, 4)).reshape(b, nk * S5_T, BRANCH_W)


def _post_kernel(x_ref, mod_ref, ng_ref, ba_ref, bb_ref, bd_ref, ys_ref, u_ref, cg_ref, sd_ref, wglu_ref,
                 wm_ref, wbr_ref, wo_ref, fg_ref, o_ref, *, final):
    x = x_ref[0]
    hb = _ada_h(x, mod_ref, ng_ref).astype(BF16)
    yv = ys_ref[0] + sd_ref[...] * u_ref[0]
    gel = 0.5 * yv * (1.0 + jnp.tanh(math.sqrt(2.0 / math.pi) * (yv + 0.044715 * (yv * yv * yv))))
    gl = _dot(gel.astype(BF16), wglu_ref[...])
    oc = gl[:, 0:BRANCH_W] * _sigmoid(gl[:, BRANCH_W:2 * BRANCH_W])
    bc = (oc * _silu(cg_ref[0])).astype(BF16)
    acc = None
    for k, br in enumerate((ba_ref[0], bb_ref[0], bc, bd_ref[0])):
        proj = _dot(br, wbr_ref[k])
        mg = _sigmoid(_dot(hb, wm_ref[:, k * D_MODEL:(k + 1) * D_MODEL]))
        acc = mg * proj if acc is None else acc + mg * proj
    y = _dot(acc.astype(BF16), wo_ref[...])
    out = x + mod_ref[0, :, 2 * D_MODEL:3 * D_MODEL] * y
    if final:
        out = _rms(out, fg_ref[...])
    o_ref[0] = out


def _post_call(x, mod, ng, ba, bb, bd, ys, z, s5d, wglu, wm, wbr, wo, fg, final, name):
    nb, length, _ = x.shape
    tok = lambda b, t: (b, t, 0)
    full2 = lambda b, t: (0, 0)
    br_spec = pl.BlockSpec((1, TM, BRANCH_W), tok)
    return pl.pallas_call(
        functools.partial(_post_kernel, final=final),
        out_shape=jax.ShapeDtypeStruct((nb, length, D_MODEL), F32),
        grid=(nb, length // TM),
        in_specs=[
            pl.BlockSpec((1, TM, D_MODEL), tok),
            pl.BlockSpec((1, 1, 3 * D_MODEL), lambda b, t: (b, 0, 0)),
            pl.BlockSpec((1, D_MODEL), full2),
            br_spec, br_spec, br_spec, br_spec,
            pl.BlockSpec((1, TM, 256), lambda b, t: (b, t, Z_CU // 256)),
            pl.BlockSpec((1, TM, 256), lambda b, t: (b, t, Z_CG // 256)),
            pl.BlockSpec((1, BRANCH_W), full2),
            pl.BlockSpec((BRANCH_W, 2 * BRANCH_W), full2),
            pl.BlockSpec((D_MODEL, N_BRANCH * D_MODEL), full2),
            pl.BlockSpec((N_BRANCH, BRANCH_W, D_MODEL), lambda b, t: (0, 0, 0)),
            pl.BlockSpec((D_MODEL, D_MODEL), full2),
            pl.BlockSpec((1, D_MODEL), full2),
        ],
        out_specs=pl.BlockSpec((1, TM, D_MODEL), tok),
        compiler_params=_cparams(("arbitrary", "arbitrary"), 56),
        name=name,
    )(x, mod, ng, ba, bb, bd, ys, z, z, s5d, wglu, wm, wbr, wo, fg)


def _rope_tables():
    t = np.arange(DEC_SEQ)
    row = (t // GRID_W).astype(np.float64)
    col = (t % GRID_W).astype(np.float64)

    def table(half):
        inv = ROPE_BASE ** (-np.arange(half, dtype=np.float64) / half)
        zeros = np.zeros((DEC_SEQ, half))
        cs, s_up, s_lo = [], [], []
        for pos in (row, col):
            ang = pos[:, None] * inv[None, :]
            c, s = np.cos(ang), np.sin(ang)
            cs += [c, c]
            s_up += [zeros, s]
            s_lo += [-s, zeros]
        reps = LANE // (4 * half)
        f = lambda parts: np.tile(np.concatenate(parts, axis=1), (1, reps))
        return jnp.asarray(np.stack([f(cs), f(s_up), f(s_lo)]), F32)

    return table(16), table(8)


def _reorder_w_in(w):
    pad = jnp.zeros((D_MODEL, R_W - R_KPE - MLA_ROPE), w.dtype)
    return jnp.concatenate([w[:, 0:512], w[:, 1952:2720], w[:, 768:1152], w[:, 1440:1696], w[:, 512:768],
                            w[:, 1184:1440], w[:, 1696:1952], w[:, 2720:2976], w[:, 1152:1184], pad], axis=1)


def _reorder_w_q_up(w):
    w = w.reshape(MLA_Q_LORA, MLA_HEADS, MLA_NOPE + MLA_ROPE)
    return jnp.concatenate([w[:, :, :MLA_NOPE].reshape(MLA_Q_LORA, -1), w[:, :, MLA_NOPE:].reshape(MLA_Q_LORA, -1)],
                           axis=1)


def kernel(x_prompt, x_sample, cache_a_k, cache_a_v, cache_mla_ckv, cache_mla_kpe, cache_na_k, cache_na_v,
           state_s5_re, state_s5_im, c, c_ctx, w_mod, b_mod, norm_g, w_in, w_merge, a_sink,
           mla_q_norm, mla_w_q_up, mla_kv_norm, mla_w_kv_up, s5_lam_re, s5_lam_im, s5_log_dt,
           s5_b_re, s5_b_im, s5_c_re, s5_c_im, s5_d, s5_w_glu, na_rpb, w_branch, w_out, final_norm_g):
    n_ctx = BATCH * SEQ
    conds = jnp.concatenate([c, c_ctx[None, :], jnp.zeros((3, D_MODEL), F32)], axis=0)
    mods = _mod_call(conds, w_mod, b_mod)
    tabs = _rope_tables()
    ck_a = cache_a_k.reshape(DEC_BATCH, DEPTH, PAST_LEN, A_KV_HEADS * HEAD_DIM)
    cv_a = cache_a_v.reshape(DEC_BATCH, DEPTH, PAST_LEN, A_KV_HEADS * HEAD_DIM)
    ck_n = cache_na_k.reshape(DEC_BATCH, DEPTH, PAST_LEN, NAT_HEADS * HEAD_DIM)
    cv_n = cache_na_v.reshape(DEC_BATCH, DEPTH, PAST_LEN, NAT_HEADS * HEAD_DIM)
    fg = final_norm_g.reshape(1, D_MODEL)

    yp = x_prompt.reshape(1, n_ctx, D_MODEL)
    ys = x_sample
    outs = {k: [] for k in ('ak', 'av', 'ckv', 'kpe', 'nk', 'nv', 're', 'im')}
    for i in range(DEPTH):
        mod_lat = mods[i, 0:DEC_BATCH].reshape(DEC_BATCH, 1, 3 * D_MODEL)
        mod_ctx = mods[i, DEC_BATCH:DEC_BATCH + 1].reshape(1, 1, 3 * D_MODEL)
        ng = norm_g[i].reshape(1, D_MODEL)
        w_raw = _reorder_w_in(w_in[i]).astype(BF16)
        wq = _reorder_w_q_up(mla_w_q_up[i]).astype(BF16)
        wkv = mla_w_kv_up[i].astype(BF16)
        qnorm = mla_q_norm[i].reshape(1, MLA_Q_LORA)
        kvnorm = mla_kv_norm[i].reshape(1, MLA_KV_LORA)
        wm = w_merge[i].astype(BF16)
        wbr = w_branch[i].astype(BF16)
        wo = w_out[i].astype(BF16)
        wglu = s5_w_glu[i].astype(BF16)
        s5d = s5_d[i].reshape(1, BRANCH_W)

        z_ctx = _inproj_call(yp, mod_ctx, ng, w_raw, qnorm, wq, kvnorm, None, "inproj_ctx")
        z_lat = _inproj_call(ys, mod_lat, ng, w_raw, qnorm, wq, kvnorm, tabs, "inproj_lat")
        zc = z_ctx.reshape(BATCH, SEQ, Z_W)

        m_intra, m_state, m_out, a_t = _s5_mats(s5_lam_re[i], s5_lam_im[i], s5_log_dt[i], s5_b_re[i], s5_b_im[i],
                                                s5_c_re[i], s5_c_im[i])
        ut = jnp.concatenate([_to_chunk_rows(zc[:, :, Z_CU:Z_CU + BRANCH_W], S5_NK_CTX),
                              _to_chunk_rows(z_lat[:, :, Z_CU:Z_CU + BRANCH_W], S5_NK_LAT)], axis=1).astype(BF16)
        sre, sim = state_s5_re[:, i], state_s5_im[:, i]
        h0 = jnp.transpose(jnp.concatenate([sre[:, 0], sre[:, 1], sim[:, 0], sim[:, 1]], axis=-1), (1, 0, 2))
        y_s5, hend = _s5_call(ut, m_intra, m_state, m_out, a_t, h0)
        ys5_ctx = _from_chunk_rows(y_s5[:, :S5_ROWS_CTX], BATCH, S5_NK_CTX).reshape(1, n_ctx, BRANCH_W)
        ys5_lat = _from_chunk_rows(y_s5[:, S5_ROWS_CTX:], DEC_BATCH, S5_NK_LAT)
        hend = jnp.transpose(hend.reshape(S5_GROUPS, BATCH, 2, 2, S5_STATE), (1, 2, 3, 0, 4))
        outs['re'].append(hend[:, 0])
        outs['im'].append(hend[:, 1])

        oa_c, ob_c, od_c = _ctx_attn_call(zc, a_sink[i], wkv)
        oa_l = _win_attn_call(z_lat, a_sink[i], ck_a, cv_a, i)
        ob_l = _mla_call(z_lat, cache_mla_ckv, cache_mla_kpe, wkv, i)
        od_l = _nat_call(z_lat, ck_n, cv_n, _nat_bias_table(na_rpb[i]), i)

        flat = lambda a: a.reshape(1, n_ctx, BRANCH_W)
        final = i == DEPTH - 1
        yp = _post_call(yp, mod_ctx, ng, flat(oa_c), flat(ob_c), flat(od_c), ys5_ctx, z_ctx, s5d, wglu, wm, wbr, wo,
                        fg, final, "post_ctx")
        ys = _post_call(ys, mod_lat, ng, oa_l, ob_l, od_l, ys5_lat, z_lat, s5d, wglu, wm, wbr, wo, fg, final,
                        "post_lat")

        outs['ak'].append(zc[:, :, Z_AK:Z_AK + 128].reshape(BATCH, SEQ, A_KV_HEADS, HEAD_DIM))
        outs['av'].append(zc[:, :, Z_AV:Z_AV + 128].reshape(BATCH, SEQ, A_KV_HEADS, HEAD_DIM))
        outs['ckv'].append(zc[:, :, Z_CKV:Z_CKV + MLA_KV_LORA])
        outs['kpe'].append(zc[:, :, Z_KPE:Z_KPE + MLA_ROPE])
        outs['nk'].append(zc[:, :, Z_DK:Z_DK + 256].reshape(BATCH, SEQ, NAT_HEADS, HEAD_DIM))
        outs['nv'].append(zc[:, :, Z_DV:Z_DV + 256].reshape(BATCH, SEQ, NAT_HEADS, HEAD_DIM))

    st = lambda k: jnp.stack(outs[k], axis=1)
    return (yp.reshape(BATCH, SEQ, D_MODEL), ys, st('ak'), st('av'), st('ckv'), st('kpe'), st('nk'), st('nv'),
            st('re'), st('im'))
```

```python
import functools
import math

import numpy as np
import jax
import jax.numpy as jnp
from jax import lax
from jax.experimental import pallas as pl
from jax.experimental.pallas import tpu as pltpu

F32 = jnp.float32
BF16 = jnp.bfloat16

D_MODEL = 1024
BATCH = 32
SEQ = 256
DEPTH = 2
DEC_BATCH = 4
DEC_SEQ = 2048
PAST_LEN = 512
GRID_W = 64
HEAD_DIM = 64
BRANCH_W = 256
N_BRANCH = 4
Q_BLOCK = 128
A_HEADS = 4
A_KV_HEADS = 2
A_GROUP = A_HEADS // A_KV_HEADS
A_WINDOW = 128
MLA_HEADS = 4
MLA_Q_LORA = 256
MLA_KV_LORA = 128
MLA_NOPE = 64
MLA_ROPE = 32
MLA_V = 64
S5_CH = 16
S5_GROUPS = BRANCH_W // S5_CH
S5_STATE = 64
NAT_HEADS = 4
NAT_ROWS = 8
NAT_COLS = 16
ROPE_BASE = 10000.0
EPS = 1e-6
NEG = -1e30

LANE = 128

R_AV, R_CQ, R_CKV, R_KPE, R_CU, R_G, R_W = 384, 1280, 1536, 1664, 1792, 2048, 3072
KPE_LANE = 64
Z_AQ, Z_AK, Z_AV = 0, 256, 384
Z_DQ, Z_DK, Z_DV = 512, 768, 1024
Z_CKV, Z_KPE = 1280, 1408
Z_QM = 1536
Z_AG, Z_BG, Z_CG, Z_DG = 2048, 2304, 2560, 2816
Z_CU = 3072
Z_W = 3328

S5_T = 32
S5_TC = S5_T * S5_CH
S5_ROWS = 256
S5_NK_CTX = SEQ // S5_T
S5_NK_LAT = DEC_SEQ // S5_T

TM = 512
TM_CHUNKS = TM // S5_T
TQ_MLA = 256
NAT_RPS = 4
WIN_BPS = 2


def _cparams(sem, vmem_mb):
    return pltpu.CompilerParams(dimension_semantics=sem, vmem_limit_bytes=vmem_mb * 1024 * 1024)


def _sigmoid(x):
    return 1.0 / (1.0 + jnp.exp(-x))


def _silu(x):
    return x * _sigmoid(x)


def _rms(x, g):
    return x * lax.rsqrt(jnp.mean(x * x, axis=-1, keepdims=True) + EPS) * g


def _dot(a, b):
    return jnp.dot(a, b, preferred_element_type=F32)


def _dot_nt(a, b):
    return lax.dot_general(a, b, (((1,), (1,)), ((), ())), preferred_element_type=F32)


def _slot_transpose(vs):
    lane = lax.broadcasted_iota(jnp.int32, vs[0].shape, 1)
    vs = list(vs)
    for d in (4, 2, 1):
        keep = (lane & (S5_CH * d)) == 0
        nxt = list(vs)
        for lo in range(8):
            if lo & d:
                continue
            hi = lo + d
            nxt[lo] = jnp.where(keep, vs[lo], pltpu.roll(vs[hi], S5_CH * d, 1))
            nxt[hi] = jnp.where(keep, pltpu.roll(vs[lo], LANE - S5_CH * d, 1), vs[hi])
        vs = nxt
    return vs


def _mod_kernel(c_ref, w_ref, b_ref, o_ref):
    s = _silu(c_ref[...])
    o_ref[0] = _dot(s.astype(BF16), w_ref[0].astype(BF16)) + b_ref[0]


def _mod_call(conds, w_mod, b_mod):
    nc = 512
    return pl.pallas_call(
        _mod_kernel,
        out_shape=jax.ShapeDtypeStruct((DEPTH, 8, 3 * D_MODEL), F32),
        grid=(DEPTH, 3 * D_MODEL // nc),
        in_specs=[
            pl.BlockSpec((8, D_MODEL), lambda i, j: (0, 0)),
            pl.BlockSpec((1, D_MODEL, nc), lambda i, j: (i, 0, j)),
            pl.BlockSpec((1, 1, nc), lambda i, j: (i, 0, j)),
        ],
        out_specs=pl.BlockSpec((1, 8, nc), lambda i, j: (i, 0, j)),
        compiler_params=_cparams(("arbitrary", "arbitrary"), 32),
        name="mod_rows",
    )(conds, w_mod, b_mod.reshape(DEPTH, 1, 3 * D_MODEL))


def _ada_h(x, mod_ref, ng_ref):
    shift = mod_ref[0, :, 0:D_MODEL]
    scale = mod_ref[0, :, D_MODEL:2 * D_MODEL]
    return _rms(x, ng_ref[...]) * (1.0 + scale) + shift


def _rope_block(xs, tab_ref, shift):
    return (xs * tab_ref[0] + pltpu.roll(xs, shift, 1) * tab_ref[1]
            + pltpu.roll(xs, LANE - shift, 1) * tab_ref[2])


N_CACHE = 6


def _inproj_kernel(*refs, rope):
    x_ref, mod_ref, ng_ref, w_ref, qn_ref, wq_ref, kvn_ref = refs[:7]
    if rope:
        ta_ref, tm_ref, z_ref, ut_ref, u_s = refs[7:]
    else:
        z_ref, ut_ref, ak_ref, av_ref, ckv_ref, kpe_ref, nk_ref, nv_ref, u_s = refs[7 + N_CACHE:]
    h = _ada_h(x_ref[0], mod_ref, ng_ref)
    raw = _dot(h.astype(BF16), w_ref[...])
    if rope:
        for j in range(R_AV // LANE):
            z_ref[0, :, j * LANE:(j + 1) * LANE] = _rope_block(raw[:, j * LANE:(j + 1) * LANE], ta_ref,
                                                               16).astype(BF16)
    else:
        z_ref[0, :, 0:R_AV] = raw[:, 0:R_AV].astype(BF16)
    z_ref[0, :, R_AV:R_CQ] = raw[:, R_AV:R_CQ].astype(BF16)
    qn = _rms(raw[:, R_CQ:R_CKV], qn_ref[...])
    q = _dot(qn.astype(BF16), wq_ref[...])
    kp = raw[:, R_KPE:R_CU]
    ckv = _rms(raw[:, R_CKV:R_KPE], kvn_ref[...])
    for hh in range(MLA_HEADS):
        qh = q[:, hh * LANE:(hh + 1) * LANE]
        if rope:
            qh = _rope_block(qh, tm_ref, 8)
        z_ref[0, :, Z_QM + hh * LANE:Z_QM + (hh + 1) * LANE] = qh.astype(BF16)
    z_ref[0, :, Z_CKV:Z_KPE] = ckv.astype(BF16)
    z_ref[0, :, Z_KPE:Z_QM] = (_rope_block(kp, tm_ref, 8) if rope else kp).astype(BF16)
    z_ref[0, :, Z_AG:Z_CU] = raw[:, R_G:R_W].astype(BF16)
    z_ref[0, :, Z_CU:Z_W] = raw[:, R_CU:R_G].astype(BF16)
    if not rope:
        half = TM // 2
        for e in range(2):
            rows = slice(e * half, (e + 1) * half)
            ak_ref[e, 0] = raw[rows, Z_AK:Z_AV]
            av_ref[e, 0] = raw[rows, Z_AV:Z_DQ]
            nk_ref[e, 0] = raw[rows, Z_DK:Z_DV]
            nv_ref[e, 0] = raw[rows, Z_DV:R_CQ]
            ckv_ref[e, 0] = ckv[rows]
            kpe_ref[e, 0] = kp[rows, KPE_LANE:KPE_LANE + MLA_ROPE]
    for hf in range(2):
        u_s[hf] = raw[:, R_CU + hf * LANE:R_CU + (hf + 1) * LANE]
    for m in range(S5_T // 8):
        for hf in range(2):
            outs = _slot_transpose([u_s[hf, pl.ds(8 * m + s, TM_CHUNKS, stride=S5_T), :] for s in range(8)])
            for gp in range(8):
                ut_ref[8 * hf + gp, :, m * LANE:(m + 1) * LANE] = outs[gp].astype(BF16)


def _inproj_call(x, mod, ng, w_raw, qnorm, wq, kvnorm, tabs, caches, layer, name):
    nb, length, _ = x.shape
    rope = tabs is not None
    nt = length // TM
    full2 = lambda b, t: (0, 0)
    in_specs = [
        pl.BlockSpec((1, TM, D_MODEL), lambda b, t: (b, t, 0)),
        pl.BlockSpec((1, 1, 3 * D_MODEL), lambda b, t: (b, 0, 0)),
        pl.BlockSpec((1, D_MODEL), full2),
        pl.BlockSpec((D_MODEL, R_W), full2),
        pl.BlockSpec((1, MLA_Q_LORA), full2),
        pl.BlockSpec((MLA_Q_LORA, MLA_HEADS * LANE), full2),
        pl.BlockSpec((1, MLA_KV_LORA), full2),
    ]
    args = [x, mod, ng, w_raw, qnorm, wq, kvnorm]
    out_shape = [jax.ShapeDtypeStruct((nb, length, Z_W), BF16),
                 jax.ShapeDtypeStruct((S5_GROUPS, S5_ROWS, S5_TC), BF16)]
    out_specs = [pl.BlockSpec((1, TM, Z_W), lambda b, t: (b, t, 0)),
                 pl.BlockSpec((S5_GROUPS, TM_CHUNKS, S5_TC), lambda b, t: (0, b * nt + t, 0))]
    aliases = {}
    if rope:
        in_specs += [pl.BlockSpec((3, TM, LANE), lambda b, t: (0, t, 0))] * 2
        args += list(tabs)
    else:
        for k, cbuf in enumerate(caches):
            in_specs.append(pl.BlockSpec(memory_space=pl.ANY))
            args.append(cbuf)
            out_shape.append(jax.ShapeDtypeStruct(cbuf.shape, cbuf.dtype))
            out_specs.append(pl.BlockSpec((2, 1, SEQ, cbuf.shape[-1]), lambda b, t: (t, layer, 0, 0)))
            aliases[7 + k] = 2 + k
    return pl.pallas_call(
        functools.partial(_inproj_kernel, rope=rope),
        out_shape=tuple(out_shape),
        grid=(nb, nt),
        in_specs=in_specs,
        out_specs=tuple(out_specs),
        scratch_shapes=[pltpu.VMEM((2, TM, LANE), F32)],
        input_output_aliases=aliases,
        compiler_params=_cparams(("arbitrary", "arbitrary"), 56),
        name=name,
    )(*args)


def _softmax_pv(s_list, v_list, sink=None):
    m = jnp.max(s_list[0], axis=-1, keepdims=True)
    for s in s_list[1:]:
        m = jnp.maximum(m, jnp.max(s, axis=-1, keepdims=True))
    if sink is not None:
        m = jnp.maximum(m, sink)
    den = None
    o = None
    for s, v in zip(s_list, v_list):
        e = jnp.exp(s - m)
        d = jnp.sum(e, axis=-1, keepdims=True)
        pv = _dot(e.astype(BF16), v)
        den = d if den is None else den + d
        o = pv if o is None else o + pv
    if sink is not None:
        den = den + jnp.exp(sink - m)
    return o / den


def _mla_keys(kv, kpe_blk):
    lane = lax.broadcasted_iota(jnp.int32, kpe_blk.shape, 1)
    return [jnp.where(lane < MLA_NOPE, kv[:, h * LANE:(h + 1) * LANE], kpe_blk).astype(BF16)
            for h in range(MLA_HEADS)]


def _ctx_attn_kernel(sink_ref, z_ref, wkv_ref, oa_ref, ob_ref, od_ref):
    hd = HEAD_DIM
    gate = lambda c0: _silu(z_ref[0, :, c0:c0 + hd].astype(F32))
    for h in range(A_HEADS):
        hk = h // A_GROUP
        q = z_ref[0, :, Z_AQ + hd * h:Z_AQ + hd * (h + 1)]
        k = z_ref[0, :, Z_AK + hd * hk:Z_AK + hd * (hk + 1)]
        v = z_ref[0, :, Z_AV + hd * hk:Z_AV + hd * (hk + 1)]
        o = _softmax_pv([_dot_nt(q, k) * (hd ** -0.5)], [v], sink=sink_ref[h])
        oa_ref[0, :, hd * h:hd * (h + 1)] = (o * gate(Z_AG + hd * h)).astype(BF16)
    kv = _dot(z_ref[0, :, Z_CKV:Z_KPE], wkv_ref[...])
    keys = _mla_keys(kv, z_ref[0, :, Z_KPE:Z_QM].astype(F32))
    for h in range(MLA_HEADS):
        q = z_ref[0, :, Z_QM + LANE * h:Z_QM + LANE * (h + 1)]
        v = kv[:, LANE * h + MLA_NOPE:LANE * (h + 1)].astype(BF16)
        o = _softmax_pv([_dot_nt(q, keys[h]) * ((MLA_NOPE + MLA_ROPE) ** -0.5)], [v])
        ob_ref[0, :, hd * h:hd * (h + 1)] = (o * gate(Z_BG + hd * h)).astype(BF16)
    for h in range(NAT_HEADS):
        q = z_ref[0, :, Z_DQ + hd * h:Z_DQ + hd * (h + 1)]
        k = z_ref[0, :, Z_DK + hd * h:Z_DK + hd * (h + 1)]
        v = z_ref[0, :, Z_DV + hd * h:Z_DV + hd * (h + 1)]
        o = _softmax_pv([_dot_nt(q, k) * (hd ** -0.5)], [v])
        od_ref[0, :, hd * h:hd * (h + 1)] = (o * gate(Z_DG + hd * h)).astype(BF16)


def _ctx_attn_call(z, sink, wkv):
    out = jax.ShapeDtypeStruct((BATCH, SEQ, BRANCH_W), BF16)
    ospec = pl.BlockSpec((1, SEQ, BRANCH_W), lambda b: (b, 0, 0))
    return pl.pallas_call(
        _ctx_attn_kernel,
        out_shape=(out, out, out),
        grid=(BATCH,),
        in_specs=[
            pl.BlockSpec(memory_space=pltpu.SMEM),
            pl.BlockSpec((1, SEQ, Z_W), lambda b: (b, 0, 0)),
            pl.BlockSpec((MLA_KV_LORA, 512), lambda b: (0, 0)),
        ],
        out_specs=(ospec, ospec, ospec),
        compiler_params=_cparams(("arbitrary",), 40),
        name="ctx_attn",
    )(sink, z, wkv)


def _win_attn_kernel(sink_ref, q_ref, kv_ref, g_ref, ck_ref, cv_ref, o_ref, ck_s, cv_s):
    hd = HEAD_DIM
    nb = DEC_SEQ // Q_BLOCK

    @pl.when(pl.program_id(1) == 0)
    def _():
        ck_s[...] = ck_ref[0, 0].astype(BF16)
        cv_s[...] = cv_ref[0, 0].astype(BF16)

    for qb in range(WIN_BPS):
        n = pl.program_id(1) * WIN_BPS + qb
        rows = slice(qb * Q_BLOCK, (qb + 1) * Q_BLOCK)
        blocks = []
        for off in (-1, 0, 1):
            start = pl.multiple_of(jnp.clip(n + off, 0, nb - 1) * Q_BLOCK, Q_BLOCK)
            blocks.append(kv_ref[0, pl.ds(start, Q_BLOCK), :])
        kvw = jnp.concatenate(blocks, axis=0)
        qpos = n * Q_BLOCK + lax.broadcasted_iota(jnp.int32, (Q_BLOCK, 3 * Q_BLOCK), 0)
        kpos = (n - 1) * Q_BLOCK + lax.broadcasted_iota(jnp.int32, (Q_BLOCK, 3 * Q_BLOCK), 1)
        mask = (jnp.abs(qpos - kpos) <= A_WINDOW) & (kpos >= 0) & (kpos < DEC_SEQ)
        for h in range(A_HEADS):
            hk = h // A_GROUP
            q = q_ref[0, rows, hd * h:hd * (h + 1)]
            kw = kvw[:, hd * hk:hd * (hk + 1)]
            vw = kvw[:, 2 * hd + hd * hk:2 * hd + hd * (hk + 1)]
            s_win = jnp.where(mask, _dot_nt(q, kw) * (hd ** -0.5), NEG)
            s_ctx = _dot_nt(q, ck_s[:, hd * hk:hd * (hk + 1)]) * (hd ** -0.5)
            o = _softmax_pv([s_win, s_ctx], [vw, cv_s[:, hd * hk:hd * (hk + 1)]], sink=sink_ref[h])
            g = g_ref[0, rows, hd * h:hd * (h + 1)].astype(F32)
            o_ref[0, rows, hd * h:hd * (h + 1)] = (o * _silu(g)).astype(BF16)


def _win_attn_call(z, sink, cache_k, cache_v, layer):
    kvw = A_KV_HEADS * HEAD_DIM
    qrows = WIN_BPS * Q_BLOCK
    return pl.pallas_call(
        _win_attn_kernel,
        out_shape=jax.ShapeDtypeStruct((DEC_BATCH, DEC_SEQ, BRANCH_W), BF16),
        grid=(DEC_BATCH, DEC_SEQ // qrows),
        in_specs=[
            pl.BlockSpec(memory_space=pltpu.SMEM),
            pl.BlockSpec((1, qrows, 256), lambda b, n: (b, n, Z_AQ // 256)),
            pl.BlockSpec((1, DEC_SEQ, 256), lambda b, n: (b, 0, Z_AK // 256)),
            pl.BlockSpec((1, qrows, 256), lambda b, n: (b, n, Z_AG // 256)),
            pl.BlockSpec((1, 1, PAST_LEN, kvw), lambda b, n: (b, layer, 0, 0)),
            pl.BlockSpec((1, 1, PAST_LEN, kvw), lambda b, n: (b, layer, 0, 0)),
        ],
        out_specs=pl.BlockSpec((1, qrows, BRANCH_W), lambda b, n: (b, n, 0)),
        scratch_shapes=[pltpu.VMEM((PAST_LEN, kvw), BF16), pltpu.VMEM((PAST_LEN, kvw), BF16)],
        compiler_params=_cparams(("arbitrary", "arbitrary"), 32),
        name="lat_window_attn",
    )(sink, z, z, z, cache_k, cache_v)


def _nat_kernel(q_ref, k_ref, v_ref, g_ref, ck_ref, cv_ref, bias_ref, o_ref, ck_s, cv_s):
    hd = HEAD_DIM
    nrows = DEC_SEQ // GRID_W
    kr = NAT_ROWS

    @pl.when(pl.program_id(1) == 0)
    def _():
        ck_s[...] = ck_ref[0, 0].astype(BF16)
        cv_s[...] = cv_ref[0, 0].astype(BF16)

    shape = (GRID_W, kr * GRID_W)
    qc = lax.broadcasted_iota(jnp.int32, shape, 0)
    kc = lax.broadcasted_iota(jnp.int32, shape, 1) & (GRID_W - 1)
    rel = kc - jnp.clip(qc - NAT_COLS // 2, 0, GRID_W - NAT_COLS)
    mask = (rel >= 0) & (rel < NAT_COLS)
    for rr in range(NAT_RPS):
        r = pl.program_id(1) * NAT_RPS + rr
        rows = slice(rr * GRID_W, (rr + 1) * GRID_W)
        rs = jnp.clip(r - kr // 2, 0, nrows - kr)
        start = pl.multiple_of(rs * GRID_W, GRID_W)
        kl = k_ref[0, pl.ds(start, kr * GRID_W), :]
        vl = v_ref[0, pl.ds(start, kr * GRID_W), :]
        bidx = rs - r + (NAT_ROWS - 1)
        for h in range(NAT_HEADS):
            q = q_ref[0, rows, hd * h:hd * (h + 1)]
            bias = jnp.concatenate([bias_ref[h, bidx + 2 * m] for m in range(kr // 2)], axis=1)
            s_lat = _dot_nt(q, kl[:, hd * h:hd * (h + 1)]) * (hd ** -0.5) + bias
            s_lat = jnp.where(mask, s_lat, NEG)
            s_ctx = _dot_nt(q, ck_s[:, hd * h:hd * (h + 1)]) * (hd ** -0.5)
            o = _softmax_pv([s_lat, s_ctx], [vl[:, hd * h:hd * (h + 1)], cv_s[:, hd * h:hd * (h + 1)]])
            g = g_ref[0, rows, hd * h:hd * (h + 1)].astype(F32)
            o_ref[0, rows, hd * h:hd * (h + 1)] = (o * _silu(g)).astype(BF16)


def _nat_bias_table(rpb):
    nc = NAT_COLS - 1
    rep = lambda a, n: jnp.broadcast_to(a, a.shape[:-1] + (n,))
    ring = jnp.concatenate([rpb[..., nc:], rep(rpb[..., -1:], GRID_W - 1 - nc), rep(rpb[..., :1], GRID_W - nc),
                            rpb[..., :nc]], axis=-1)
    w = 2 * GRID_W - 1
    tq = jnp.tile(ring, (1, 1, GRID_W))[..., :GRID_W * w].reshape(NAT_HEADS, 2 * NAT_ROWS - 1, GRID_W, w)
    tq = tq[..., :GRID_W]
    nxt = jnp.concatenate([tq[:, 1:], jnp.zeros_like(tq[:, :1])], axis=1)
    return jnp.concatenate([tq, nxt], axis=-1).astype(F32)


def _nat_call(z, cache_k, cache_v, bias, layer):
    hw = NAT_HEADS * HEAD_DIM
    qrows = NAT_RPS * GRID_W
    return pl.pallas_call(
        _nat_kernel,
        out_shape=jax.ShapeDtypeStruct((DEC_BATCH, DEC_SEQ, BRANCH_W), BF16),
        grid=(DEC_BATCH, DEC_SEQ // qrows),
        in_specs=[
            pl.BlockSpec((1, qrows, 256), lambda b, r: (b, r, Z_DQ // 256)),
            pl.BlockSpec((1, DEC_SEQ, 256), lambda b, r: (b, 0, Z_DK // 256)),
            pl.BlockSpec((1, DEC_SEQ, 256), lambda b, r: (b, 0, Z_DV // 256)),
            pl.BlockSpec((1, qrows, 256), lambda b, r: (b, r, Z_DG // 256)),
            pl.BlockSpec((1, 1, PAST_LEN, hw), lambda b, r: (b, layer, 0, 0)),
            pl.BlockSpec((1, 1, PAST_LEN, hw), lambda b, r: (b, layer, 0, 0)),
            pl.BlockSpec(bias.shape, lambda b, r: (0, 0, 0, 0)),
        ],
        out_specs=pl.BlockSpec((1, qrows, BRANCH_W), lambda b, r: (b, r, 0)),
        scratch_shapes=[pltpu.VMEM((PAST_LEN, hw), BF16), pltpu.VMEM((PAST_LEN, hw), BF16)],
        compiler_params=_cparams(("arbitrary", "arbitrary"), 40),
        name="lat_nat_attn",
    )(z, z, z, z, cache_k, cache_v, bias)


def _mla_kernel(q_ref, ckv_ref, kpe_ref, g_ref, cckv_ref, ckpe_ref, wkv_ref, o_ref, k_s, v_s):
    nlat = DEC_SEQ

    @pl.when(pl.program_id(1) == 0)
    def _():
        rows = 512

        def fill(r0, ckv, kpe_blk):
            kv = _dot(ckv, wkv_ref[...])
            for h, kh in enumerate(_mla_keys(kv, kpe_blk)):
                k_s[h, r0:r0 + rows, :] = kh
                v_s[h, r0:r0 + rows, :] = kv[:, h * LANE + MLA_NOPE:(h + 1) * LANE].astype(BF16)

        for c in range(nlat // rows):
            fill(c * rows, ckv_ref[0, c * rows:(c + 1) * rows, :], kpe_ref[0, c * rows:(c + 1) * rows, :].astype(F32))
        ckpe = ckpe_ref[0, 0]
        ckpe_blk = jnp.concatenate([jnp.zeros((PAST_LEN, KPE_LANE), F32), ckpe,
                                    jnp.zeros((PAST_LEN, LANE - KPE_LANE - MLA_ROPE), F32)], axis=1)
        fill(nlat, cckv_ref[0, 0].astype(BF16), ckpe_blk)

    for h in range(MLA_HEADS):
        q = q_ref[0, :, LANE * h:LANE * (h + 1)]
        s = _dot_nt(q, k_s[h]) * ((MLA_NOPE + MLA_ROPE) ** -0.5)
        o = _softmax_pv([s], [v_s[h]])
        g = g_ref[0, :, 64 * h:64 * (h + 1)].astype(F32)
        o_ref[0, :, 64 * h:64 * (h + 1)] = (o * _silu(g)).astype(BF16)


def _mla_call(z, cache_ckv, cache_kpe, wkv, layer):
    nkeys = DEC_SEQ + PAST_LEN
    return pl.pallas_call(
        _mla_kernel,
        out_shape=jax.ShapeDtypeStruct((DEC_BATCH, DEC_SEQ, BRANCH_W), BF16),
        grid=(DEC_BATCH, DEC_SEQ // TQ_MLA),
        in_specs=[
            pl.BlockSpec((1, TQ_MLA, MLA_HEADS * LANE), lambda b, t: (b, t, Z_QM // (MLA_HEADS * LANE))),
            pl.BlockSpec((1, DEC_SEQ, LANE), lambda b, t: (b, 0, Z_CKV // LANE)),
            pl.BlockSpec((1, DEC_SEQ, LANE), lambda b, t: (b, 0, Z_KPE // LANE)),
            pl.BlockSpec((1, TQ_MLA, 256), lambda b, t: (b, t, Z_BG // 256)),
            pl.BlockSpec((1, 1, PAST_LEN, MLA_KV_LORA), lambda b, t: (b, layer, 0, 0)),
            pl.BlockSpec((1, 1, PAST_LEN, MLA_ROPE), lambda b, t: (b, layer, 0, 0)),
            pl.BlockSpec((MLA_KV_LORA, 512), lambda b, t: (0, 0)),
        ],
        out_specs=pl.BlockSpec((1, TQ_MLA, BRANCH_W), lambda b, t: (b, t, 0)),
        scratch_shapes=[pltpu.VMEM((MLA_HEADS, nkeys, LANE), BF16), pltpu.VMEM((MLA_HEADS, nkeys, MLA_V), BF16)],
        compiler_params=_cparams(("arbitrary", "arbitrary"), 48),
        name="lat_mla_attn",
    )(z, z, z, z, cache_ckv, cache_kpe, wkv)


def _s5_mats(lam_re, lam_im, log_dt, b_re, b_im, c_re, c_im):
    t = S5_T
    hp = lax.Precision.HIGHEST
    lre = jnp.minimum(lam_re, -1e-4)
    lim = lam_im
    dt = jnp.exp(log_dt)[..., None]
    er, ei = lre * dt, lim * dt
    mag = jnp.exp(er)
    are, aim = mag * jnp.cos(ei), mag * jnp.sin(ei)
    den = lre * lre + lim * lim
    qre = ((are - 1.0) * lre + aim * lim) / den
    qim = (aim * lre - (are - 1.0) * lim) / den
    bbr = qre[..., None] * b_re - qim[..., None] * b_im
    bbi = qre[..., None] * b_im + qim[..., None] * b_re
    n = jnp.arange(t + 1, dtype=F32)[:, None, None, None]
    pmag = jnp.exp(er[None] * n)
    pr, pi = pmag * jnp.cos(ei[None] * n), pmag * jnp.sin(ei[None] * n)

    cbr = c_re[..., None] * bbr[:, :, None] - c_im[..., None] * bbi[:, :, None]
    cbi = c_re[..., None] * bbi[:, :, None] + c_im[..., None] * bbr[:, :, None]
    kk = (jnp.einsum('tdgp,dgcpk->dgktc', pr[:t], cbr, precision=hp)
          - jnp.einsum('tdgp,dgcpk->dgktc', pi[:t], cbi, precision=hp))
    ring = jnp.concatenate([kk[0][:, :, 0:1] + kk[1][:, :, 0:1], kk[0][:, :, 1:], jnp.zeros_like(kk[0][:, :, 0:1]),
                            kk[1][:, :, 1:][:, :, ::-1]], axis=2).reshape(S5_GROUPS, S5_CH, 2 * S5_TC)
    w = 2 * S5_TC - S5_CH
    rows = jnp.tile(ring, (1, 1, t))[..., :t * w].reshape(S5_GROUPS, S5_CH, t, w)[..., :S5_TC]
    m_intra = jnp.transpose(rows, (0, 2, 1, 3)).reshape(S5_GROUPS, S5_TC, S5_TC)

    def state_cols(pw_r, pw_i, d):
        sr = pw_r[..., None] * bbr[d][None] - pw_i[..., None] * bbi[d][None]
        si = pw_r[..., None] * bbi[d][None] + pw_i[..., None] * bbr[d][None]
        f = lambda a: jnp.transpose(a, (1, 0, 3, 2)).reshape(S5_GROUPS, S5_TC, S5_STATE)
        return f(sr), f(si)

    sfr, sfi = state_cols(pr[:t, 0][::-1], pi[:t, 0][::-1], 0)
    sbr, sbi = state_cols(pr[:t, 1], pi[:t, 1], 1)
    m_state = jnp.concatenate([sfr, sbr, sfi, sbi], axis=-1)

    def out_rows(pw_r, pw_i, d):
        wr = c_re[d][None] * pw_r[:, :, None, :] - c_im[d][None] * pw_i[:, :, None, :]
        wi = c_re[d][None] * pw_i[:, :, None, :] + c_im[d][None] * pw_r[:, :, None, :]
        f = lambda a: jnp.transpose(a, (1, 3, 0, 2)).reshape(S5_GROUPS, S5_STATE, S5_TC)
        return f(wr), f(-wi)

    ofr, ofi = out_rows(pr[1:t + 1, 0], pi[1:t + 1, 0], 0)
    obr, obi = out_rows(pr[1:t + 1, 1][::-1], pi[1:t + 1, 1][::-1], 1)
    m_out = jnp.concatenate([ofr, obr, ofi, obi], axis=1)

    a_t = jnp.stack([jnp.concatenate([pr[t, 0], pr[t, 1]], axis=-1),
                     jnp.concatenate([pi[t, 0], pi[t, 1]], axis=-1)], axis=1)
    return m_intra.astype(BF16), m_state.astype(BF16), m_out.astype(BF16), a_t


def _s5_kernel(uc_ref, ul_ref, mi_ref, ms_ref, mo_ref, at_ref, h0_ref, yc_ref, yl_ref, hend_ref, s_s, hf_s, hb_s):
    p = S5_STATE
    u = jnp.concatenate([uc_ref[0], ul_ref[0]], axis=0)
    s = _dot(u, ms_ref[0])
    s_s[0] = s[:, 0:2 * p]
    s_s[1] = s[:, 2 * p:4 * p]
    are = at_ref[0, 0:1, :]
    aim = at_ref[0, 1:2, :]
    fwd_lane = lax.broadcasted_iota(jnp.int32, (1, 2 * p), 1) < p

    def scan(row0, nk, nb, hre, him):
        for st in range(nk):
            rf = pl.ds(row0 + st, nb, stride=nk)
            rb = pl.ds(row0 + nk - 1 - st, nb, stride=nk)
            hf_s[0, rf, :] = hre
            hf_s[1, rf, :] = him
            hb_s[0, rb, :] = hre
            hb_s[1, rb, :] = him
            sre = jnp.where(fwd_lane, s_s[0, rf, :], s_s[0, rb, :])
            sim = jnp.where(fwd_lane, s_s[1, rf, :], s_s[1, rb, :])
            hre, him = are * hre - aim * him + sre, are * him + aim * hre + sim
        return hre, him

    zero = jnp.zeros((BATCH, 2 * p), F32)
    hre, him = scan(0, S5_NK_CTX, BATCH, zero, zero)
    hend_ref[0] = jnp.concatenate([hre, him], axis=1)
    scan(S5_ROWS, S5_NK_LAT, DEC_BATCH, h0_ref[0, :, 0:2 * p], h0_ref[0, :, 2 * p:4 * p])
    hst = jnp.concatenate([jnp.where(fwd_lane, hf_s[0], hb_s[0]), jnp.where(fwd_lane, hf_s[1], hb_s[1])], axis=1)
    y = _dot(u, mi_ref[0]) + _dot(hst.astype(BF16), mo_ref[0])
    yc_ref[0] = y[0:S5_ROWS]
    yl_ref[0] = y[S5_ROWS:2 * S5_ROWS]


def _s5_call(ut_ctx, ut_lat, m_intra, m_state, m_out, a_t, h0):
    g3 = lambda g: (g, 0, 0)
    yshape = jax.ShapeDtypeStruct((S5_GROUPS, S5_ROWS, S5_TC), F32)
    uspec = pl.BlockSpec((1, S5_ROWS, S5_TC), g3)
    return pl.pallas_call(
        _s5_kernel,
        out_shape=(yshape, yshape, jax.ShapeDtypeStruct((S5_GROUPS, BATCH, 4 * S5_STATE), F32)),
        grid=(S5_GROUPS,),
        in_specs=[
            uspec, uspec,
            pl.BlockSpec((1, S5_TC, S5_TC), g3),
            pl.BlockSpec((1, S5_TC, 4 * S5_STATE), g3),
            pl.BlockSpec((1, 4 * S5_STATE, S5_TC), g3),
            pl.BlockSpec((1, 2, 2 * S5_STATE), g3),
            pl.BlockSpec((1, DEC_BATCH, 4 * S5_STATE), g3),
        ],
        out_specs=(uspec, uspec, pl.BlockSpec((1, BATCH, 4 * S5_STATE), g3)),
        scratch_shapes=[pltpu.VMEM((2, 2 * S5_ROWS, LANE), F32)] * 3,
        compiler_params=_cparams(("arbitrary",), 32),
        name="s5_chunks",
    )(ut_ctx, ut_lat, m_intra, m_state, m_out, a_t, h0)


def _post_kernel(x_ref, mod_ref, ng_ref, ba_ref, bb_ref, bd_ref, y_ref, u_ref, cg_ref, sd_ref, wglu_ref,
                 wm_ref, wbr_ref, wo_ref, fg_ref, o_ref, ys_s, *, final):
    x = x_ref[0]
    hb = _ada_h(x, mod_ref, ng_ref).astype(BF16)
    for m in range(S5_T // 8):
        for hf in range(2):
            vs = _slot_transpose([y_ref[8 * hf + gp, :, m * LANE:(m + 1) * LANE] for gp in range(8)])
            for s in range(8):
                ys_s[hf, pl.ds(8 * m + s, TM_CHUNKS, stride=S5_T), :] = vs[s]
    yv = jnp.concatenate([ys_s[0], ys_s[1]], axis=1) + sd_ref[...] * u_ref[0].astype(F32)
    gel = 0.5 * yv * (1.0 + jnp.tanh(math.sqrt(2.0 / math.pi) * (yv + 0.044715 * (yv * yv * yv))))
    gl = _dot(gel.astype(BF16), wglu_ref[...])
    oc = gl[:, 0:BRANCH_W] * _sigmoid(gl[:, BRANCH_W:2 * BRANCH_W])
    bc = (oc * _silu(cg_ref[0].astype(F32))).astype(BF16)
    acc = None
    for k, br in enumerate((ba_ref[0], bb_ref[0], bc, bd_ref[0])):
        proj = _dot(br, wbr_ref[k])
        mg = _sigmoid(_dot(hb, wm_ref[:, k * D_MODEL:(k + 1) * D_MODEL]))
        acc = mg * proj if acc is None else acc + mg * proj
    y = _dot(acc.astype(BF16), wo_ref[...])
    out = x + mod_ref[0, :, 2 * D_MODEL:3 * D_MODEL] * y
    if final:
        out = _rms(out, fg_ref[...])
    o_ref[0] = out


def _post_call(x, mod, ng, ba, bb, bd, y_s5, z, s5d, wglu, wm, wbr, wo, fg, final, name):
    nb, length, _ = x.shape
    nt = length // TM
    tok = lambda b, t: (b, t, 0)
    full2 = lambda b, t: (0, 0)
    br_spec = pl.BlockSpec((1, TM, BRANCH_W), tok)
    return pl.pallas_call(
        functools.partial(_post_kernel, final=final),
        out_shape=jax.ShapeDtypeStruct((nb, length, D_MODEL), F32),
        grid=(nb, nt),
        in_specs=[
            pl.BlockSpec((1, TM, D_MODEL), tok),
            pl.BlockSpec((1, 1, 3 * D_MODEL), lambda b, t: (b, 0, 0)),
            pl.BlockSpec((1, D_MODEL), full2),
            br_spec, br_spec, br_spec,
            pl.BlockSpec((S5_GROUPS, TM_CHUNKS, S5_TC), lambda b, t: (0, b * nt + t, 0)),
            pl.BlockSpec((1, TM, 256), lambda b, t: (b, t, Z_CU // 256)),
            pl.BlockSpec((1, TM, 256), lambda b, t: (b, t, Z_CG // 256)),
            pl.BlockSpec((1, BRANCH_W), full2),
            pl.BlockSpec((BRANCH_W, 2 * BRANCH_W), full2),
            pl.BlockSpec((D_MODEL, N_BRANCH * D_MODEL), full2),
            pl.BlockSpec((N_BRANCH, BRANCH_W, D_MODEL), lambda b, t: (0, 0, 0)),
            pl.BlockSpec((D_MODEL, D_MODEL), full2),
            pl.BlockSpec((1, D_MODEL), full2),
        ],
        out_specs=pl.BlockSpec((1, TM, D_MODEL), tok),
        scratch_shapes=[pltpu.VMEM((2, TM, LANE), F32)],
        compiler_params=_cparams(("arbitrary", "arbitrary"), 56),
        name=name,
    )(x, mod, ng, ba, bb, bd, y_s5, z, z, s5d, wglu, wm, wbr, wo, fg)


def _rope_tables():
    t = np.arange(DEC_SEQ)
    row = (t // GRID_W).astype(np.float64)
    col = (t % GRID_W).astype(np.float64)

    def pattern(half):
        inv = ROPE_BASE ** (-np.arange(half, dtype=np.float64) / half)
        zeros = np.zeros((DEC_SEQ, half))
        cs, s_up, s_lo = [], [], []
        for pos in (row, col):
            ang = pos[:, None] * inv[None, :]
            c, s = np.cos(ang), np.sin(ang)
            cs += [c, c]
            s_up += [zeros, s]
            s_lo += [-s, zeros]
        return [np.concatenate(parts, axis=1) for parts in (cs, s_up, s_lo)]

    tab_a = np.stack([np.tile(part, (1, LANE // HEAD_DIM)) for part in pattern(HEAD_DIM // 4)])
    ident = [np.ones, np.zeros, np.zeros]
    tab_m = np.stack([np.concatenate([fill((DEC_SEQ, KPE_LANE)), part,
                                      fill((DEC_SEQ, LANE - KPE_LANE - MLA_ROPE))], axis=1)
                      for fill, part in zip(ident, pattern(MLA_ROPE // 4))])
    return jnp.asarray(tab_a, F32), jnp.asarray(tab_m, F32)


def _reorder_w_in(w):
    zpad = lambda n: jnp.zeros((D_MODEL, n), w.dtype)
    return jnp.concatenate([w[:, 0:512], w[:, 1952:2720], w[:, 768:1152],
                            zpad(KPE_LANE), w[:, 1152:1184], zpad(LANE - KPE_LANE - MLA_ROPE),
                            w[:, 1440:1696], w[:, 512:768], w[:, 1184:1440], w[:, 1696:1952], w[:, 2720:2976]],
                           axis=1)


def _reorder_w_q_up(w):
    w = w.reshape(MLA_Q_LORA, MLA_HEADS, MLA_NOPE + MLA_ROPE)
    w = jnp.concatenate([w, jnp.zeros((MLA_Q_LORA, MLA_HEADS, LANE - MLA_NOPE - MLA_ROPE), w.dtype)], axis=-1)
    return w.reshape(MLA_Q_LORA, MLA_HEADS * LANE)


def kernel(x_prompt, x_sample, cache_a_k, cache_a_v, cache_mla_ckv, cache_mla_kpe, cache_na_k, cache_na_v,
           state_s5_re, state_s5_im, c, c_ctx, w_mod, b_mod, norm_g, w_in, w_merge, a_sink,
           mla_q_norm, mla_w_q_up, mla_kv_norm, mla_w_kv_up, s5_lam_re, s5_lam_im, s5_log_dt,
           s5_b_re, s5_b_im, s5_c_re, s5_c_im, s5_d, s5_w_glu, na_rpb, w_branch, w_out, final_norm_g):
    n_ctx = BATCH * SEQ
    conds = jnp.concatenate([c, c_ctx[None, :], jnp.zeros((3, D_MODEL), F32)], axis=0)
    mods = _mod_call(conds, w_mod, b_mod)
    tabs = _rope_tables()
    ck_a = cache_a_k.reshape(DEC_BATCH, DEPTH, PAST_LEN, A_KV_HEADS * HEAD_DIM)
    cv_a = cache_a_v.reshape(DEC_BATCH, DEPTH, PAST_LEN, A_KV_HEADS * HEAD_DIM)
    ck_n = cache_na_k.reshape(DEC_BATCH, DEPTH, PAST_LEN, NAT_HEADS * HEAD_DIM)
    cv_n = cache_na_v.reshape(DEC_BATCH, DEPTH, PAST_LEN, NAT_HEADS * HEAD_DIM)
    fg = final_norm_g.reshape(1, D_MODEL)
    caches = [jnp.zeros((BATCH, DEPTH, SEQ, w), F32) for w in (128, 128, MLA_KV_LORA, MLA_ROPE, 256, 256)]

    yp = x_prompt.reshape(1, n_ctx, D_MODEL)
    ys = x_sample
    s5_re, s5_im = [], []
    for i in range(DEPTH):
        mod_lat = mods[i, 0:DEC_BATCH].reshape(DEC_BATCH, 1, 3 * D_MODEL)
        mod_ctx = mods[i, DEC_BATCH:DEC_BATCH + 1].reshape(1, 1, 3 * D_MODEL)
        ng = norm_g[i].reshape(1, D_MODEL)
        w_raw = _reorder_w_in(w_in[i]).astype(BF16)
        wq = _reorder_w_q_up(mla_w_q_up[i]).astype(BF16)
        wkv = mla_w_kv_up[i].astype(BF16)
        qnorm = mla_q_norm[i].reshape(1, MLA_Q_LORA)
        kvnorm = mla_kv_norm[i].reshape(1, MLA_KV_LORA)
        wm = w_merge[i].astype(BF16)
        wbr = w_branch[i].astype(BF16)
        wo = w_out[i].astype(BF16)
        wglu = s5_w_glu[i].astype(BF16)
        s5d = s5_d[i].reshape(1, BRANCH_W)

        res = _inproj_call(yp, mod_ctx, ng, w_raw, qnorm, wq, kvnorm, None, caches, i, "inproj_ctx")
        z_ctx, ut_ctx, caches = res[0], res[1], list(res[2:])
        z_lat, ut_lat = _inproj_call(ys, mod_lat, ng, w_raw, qnorm, wq, kvnorm, tabs, None, i, "inproj_lat")
        zc = z_ctx.reshape(BATCH, SEQ, Z_W)

        m_intra, m_state, m_out, a_t = _s5_mats(s5_lam_re[i], s5_lam_im[i], s5_log_dt[i], s5_b_re[i], s5_b_im[i],
                                                s5_c_re[i], s5_c_im[i])
        sre, sim = state_s5_re[:, i], state_s5_im[:, i]
        h0 = jnp.transpose(jnp.concatenate([sre[:, 0], sre[:, 1], sim[:, 0], sim[:, 1]], axis=-1), (1, 0, 2))
        y_ctx, y_lat, hend = _s5_call(ut_ctx, ut_lat, m_intra, m_state, m_out, a_t, h0)
        hend = jnp.transpose(hend.reshape(S5_GROUPS, BATCH, 2, 2, S5_STATE), (1, 2, 3, 0, 4))
        s5_re.append(hend[:, 0])
        s5_im.append(hend[:, 1])

        oa_c, ob_c, od_c = _ctx_attn_call(zc, a_sink[i], wkv)
        oa_l = _win_attn_call(z_lat, a_sink[i], ck_a, cv_a, i)
        ob_l = _mla_call(z_lat, cache_mla_ckv, cache_mla_kpe, wkv, i)
        od_l = _nat_call(z_lat, ck_n, cv_n, _nat_bias_table(na_rpb[i]), i)

        flat = lambda a: a.reshape(1, n_ctx, BRANCH_W)
        final = i == DEPTH - 1
        yp = _post_call(yp, mod_ctx, ng, flat(oa_c), flat(ob_c), flat(od_c), y_ctx, z_ctx, s5d, wglu, wm, wbr, wo,
                        fg, final, "post_ctx")
        ys = _post_call(ys, mod_lat, ng, oa_l, ob_l, od_l, y_lat, z_lat, s5d, wglu, wm, wbr, wo, fg, final,
                        "post_lat")

    ak, av, ckv, kpe, nk, nv = caches
    heads = lambda a, h: a.reshape(BATCH, DEPTH, SEQ, h, HEAD_DIM)
    return (yp.reshape(BATCH, SEQ, D_MODEL), ys, heads(ak, A_KV_HEADS), heads(av, A_KV_HEADS), ckv, kpe,
            heads(nk, NAT_HEADS), heads(nv, NAT_HEADS), jnp.stack(s5_re, axis=1), jnp.stack(s5_im, axis=1))
```

```python
import functools
import math

import numpy as np
import jax
import jax.numpy as jnp
from jax import lax
from jax.experimental import pallas as pl
from jax.experimental.pallas import tpu as pltpu

F32 = jnp.float32
BF16 = jnp.bfloat16

D_MODEL = 1024
BATCH = 32
SEQ = 256
DEPTH = 2
DEC_BATCH = 4
DEC_SEQ = 2048
PAST_LEN = 512
GRID_W = 64
HEAD_DIM = 64
BRANCH_W = 256
N_BRANCH = 4
Q_BLOCK = 128
A_HEADS = 4
A_KV_HEADS = 2
A_GROUP = A_HEADS // A_KV_HEADS
A_WINDOW = 128
MLA_HEADS = 4
MLA_Q_LORA = 256
MLA_KV_LORA = 128
MLA_NOPE = 64
MLA_ROPE = 32
MLA_V = 64
S5_CH = 16
S5_GROUPS = BRANCH_W // S5_CH
S5_STATE = 64
NAT_HEADS = 4
NAT_ROWS = 8
NAT_COLS = 16
ROPE_BASE = 10000.0
EPS = 1e-6
NEG = -1e30

LANE = 128

R_AV, R_CQ, R_CKV, R_KPE, R_CU, R_G, R_W = 384, 1280, 1536, 1664, 1792, 2048, 3072
KPE_LANE = 64
A_HEAD_ORDER = (0, 2, 1, 3)
Z_AQ, Z_AK, Z_AV = 0, 256, 384
Z_DQ, Z_DK, Z_DV = 512, 768, 1024
Z_CKV, Z_KPE = 1280, 1408
Z_QM = 1536
Z_AG, Z_BG, Z_CG, Z_DG = 2048, 2304, 2560, 2816
Z_CU = 3072
Z_W = 3328

S5_T = 32
S5_TC = S5_T * S5_CH
S5_ROWS = 256
S5_NK_CTX = SEQ // S5_T
S5_NK_LAT = DEC_SEQ // S5_T

TM = 512
TM_CHUNKS = TM // S5_T
TQ_MLA = 512
NAT_RPS = 4
WIN_BPS = 2


def _cparams(sem, vmem_mb):
    return pltpu.CompilerParams(dimension_semantics=sem, vmem_limit_bytes=vmem_mb * 1024 * 1024)


def _sigmoid(x):
    return 1.0 / (1.0 + jnp.exp(-x))


def _silu(x):
    return x * _sigmoid(x)


def _rms(x, g):
    return x * lax.rsqrt(jnp.mean(x * x, axis=-1, keepdims=True) + EPS) * g


def _dot(a, b):
    return jnp.dot(a, b, preferred_element_type=F32)


def _dot_nt(a, b):
    return lax.dot_general(a, b, (((1,), (1,)), ((), ())), preferred_element_type=F32)


def _slot_transpose(vs):
    lane = lax.broadcasted_iota(jnp.int32, vs[0].shape, 1)
    vs = list(vs)
    for d in (4, 2, 1):
        keep = (lane & (S5_CH * d)) == 0
        nxt = list(vs)
        for lo in range(8):
            if lo & d:
                continue
            hi = lo + d
            nxt[lo] = jnp.where(keep, vs[lo], pltpu.roll(vs[hi], S5_CH * d, 1))
            nxt[hi] = jnp.where(keep, pltpu.roll(vs[lo], LANE - S5_CH * d, 1), vs[hi])
        vs = nxt
    return vs


def _mod_kernel(c_ref, w_ref, b_ref, o_ref):
    s = _silu(c_ref[...])
    o_ref[0] = _dot(s.astype(BF16), w_ref[0].astype(BF16)) + b_ref[0]


def _mod_call(conds, w_mod, b_mod):
    nc = 512
    return pl.pallas_call(
        _mod_kernel,
        out_shape=jax.ShapeDtypeStruct((DEPTH, 8, 3 * D_MODEL), F32),
        grid=(DEPTH, 3 * D_MODEL // nc),
        in_specs=[
            pl.BlockSpec((8, D_MODEL), lambda i, j: (0, 0)),
            pl.BlockSpec((1, D_MODEL, nc), lambda i, j: (i, 0, j)),
            pl.BlockSpec((1, 1, nc), lambda i, j: (i, 0, j)),
        ],
        out_specs=pl.BlockSpec((1, 8, nc), lambda i, j: (i, 0, j)),
        compiler_params=_cparams(("arbitrary", "arbitrary"), 32),
        name="mod_rows",
    )(conds, w_mod, b_mod.reshape(DEPTH, 1, 3 * D_MODEL))


def _ada_h(x, mod_ref, ng_ref):
    shift = mod_ref[0, :, 0:D_MODEL]
    scale = mod_ref[0, :, D_MODEL:2 * D_MODEL]
    return _rms(x, ng_ref[...]) * (1.0 + scale) + shift


def _rope_block(xs, tab_ref, shift):
    return (xs * tab_ref[0] + pltpu.roll(xs, shift, 1) * tab_ref[1]
            + pltpu.roll(xs, LANE - shift, 1) * tab_ref[2])


N_CACHE = 6


def _inproj_kernel(*refs, rope):
    x_ref, mod_ref, ng_ref, w_ref, qn_ref, wq_ref, kvn_ref = refs[:7]
    if rope:
        ta_ref, tm_ref, z_ref, ut_ref, u_s = refs[7:]
    else:
        z_ref, ut_ref, ak_ref, av_ref, ckv_ref, kpe_ref, nk_ref, nv_ref, u_s = refs[7 + N_CACHE:]
    h = _ada_h(x_ref[0], mod_ref, ng_ref)
    raw = _dot(h.astype(BF16), w_ref[...])
    if rope:
        for j in range(R_AV // LANE):
            z_ref[0, :, j * LANE:(j + 1) * LANE] = _rope_block(raw[:, j * LANE:(j + 1) * LANE], ta_ref,
                                                               16).astype(BF16)
    else:
        z_ref[0, :, 0:R_AV] = raw[:, 0:R_AV].astype(BF16)
    z_ref[0, :, R_AV:R_CQ] = raw[:, R_AV:R_CQ].astype(BF16)
    qn = _rms(raw[:, R_CQ:R_CKV], qn_ref[...])
    q = _dot(qn.astype(BF16), wq_ref[...])
    kp = raw[:, R_KPE:R_CU]
    ckv = _rms(raw[:, R_CKV:R_KPE], kvn_ref[...])
    for hh in range(MLA_HEADS):
        qh = q[:, hh * LANE:(hh + 1) * LANE]
        if rope:
            qh = _rope_block(qh, tm_ref, 8)
        z_ref[0, :, Z_QM + hh * LANE:Z_QM + (hh + 1) * LANE] = qh.astype(BF16)
    z_ref[0, :, Z_CKV:Z_KPE] = ckv.astype(BF16)
    z_ref[0, :, Z_KPE:Z_QM] = (_rope_block(kp, tm_ref, 8) if rope else kp).astype(BF16)
    z_ref[0, :, Z_AG:Z_CU] = raw[:, R_G:R_W].astype(BF16)
    z_ref[0, :, Z_CU:Z_W] = raw[:, R_CU:R_G].astype(BF16)
    if not rope:
        half = TM // 2
        for e in range(2):
            rows = slice(e * half, (e + 1) * half)
            ak_ref[e, 0] = raw[rows, Z_AK:Z_AV]
            av_ref[e, 0] = raw[rows, Z_AV:Z_DQ]
            nk_ref[e, 0] = raw[rows, Z_DK:Z_DV]
            nv_ref[e, 0] = raw[rows, Z_DV:R_CQ]
            ckv_ref[e, 0] = ckv[rows]
            kpe_ref[e, 0] = kp[rows, KPE_LANE:KPE_LANE + MLA_ROPE]
    for hf in range(2):
        u_s[hf] = raw[:, R_CU + hf * LANE:R_CU + (hf + 1) * LANE]
    for m in range(S5_T // 8):
        for hf in range(2):
            outs = _slot_transpose([u_s[hf, pl.ds(8 * m + s, TM_CHUNKS, stride=S5_T), :] for s in range(8)])
            for gp in range(8):
                ut_ref[8 * hf + gp, :, m * LANE:(m + 1) * LANE] = outs[gp].astype(BF16)


def _inproj_call(x, mod, ng, w_raw, qnorm, wq, kvnorm, tabs, caches, layer, name):
    nb, length, _ = x.shape
    rope = tabs is not None
    nt = length // TM
    full2 = lambda b, t: (0, 0)
    in_specs = [
        pl.BlockSpec((1, TM, D_MODEL), lambda b, t: (b, t, 0)),
        pl.BlockSpec((1, 1, 3 * D_MODEL), lambda b, t: (b, 0, 0)),
        pl.BlockSpec((1, D_MODEL), full2),
        pl.BlockSpec((D_MODEL, R_W), full2),
        pl.BlockSpec((1, MLA_Q_LORA), full2),
        pl.BlockSpec((MLA_Q_LORA, MLA_HEADS * LANE), full2),
        pl.BlockSpec((1, MLA_KV_LORA), full2),
    ]
    args = [x, mod, ng, w_raw, qnorm, wq, kvnorm]
    out_shape = [jax.ShapeDtypeStruct((nb, length, Z_W), BF16),
                 jax.ShapeDtypeStruct((S5_GROUPS, S5_ROWS, S5_TC), BF16)]
    out_specs = [pl.BlockSpec((1, TM, Z_W), lambda b, t: (b, t, 0)),
                 pl.BlockSpec((S5_GROUPS, TM_CHUNKS, S5_TC), lambda b, t: (0, b * nt + t, 0))]
    aliases = {}
    if rope:
        in_specs += [pl.BlockSpec((3, TM, LANE), lambda b, t: (0, t, 0))] * 2
        args += list(tabs)
    else:
        for k, cbuf in enumerate(caches):
            in_specs.append(pl.BlockSpec(memory_space=pl.ANY))
            args.append(cbuf)
            out_shape.append(jax.ShapeDtypeStruct(cbuf.shape, cbuf.dtype))
            out_specs.append(pl.BlockSpec((2, 1, SEQ, cbuf.shape[-1]), lambda b, t: (t, layer, 0, 0)))
            aliases[7 + k] = 2 + k
    return pl.pallas_call(
        functools.partial(_inproj_kernel, rope=rope),
        out_shape=tuple(out_shape),
        grid=(nb, nt),
        in_specs=in_specs,
        out_specs=tuple(out_specs),
        scratch_shapes=[pltpu.VMEM((2, TM, LANE), F32)],
        input_output_aliases=aliases,
        compiler_params=_cparams(("arbitrary", "arbitrary"), 56),
        name=name,
    )(*args)


def _softmax_pv(s_list, v_list, sink=None):
    m = jnp.max(s_list[0], axis=-1, keepdims=True)
    for s in s_list[1:]:
        m = jnp.maximum(m, jnp.max(s, axis=-1, keepdims=True))
    if sink is not None:
        m = jnp.maximum(m, sink)
    den = None
    o = None
    for s, v in zip(s_list, v_list):
        e = jnp.exp(s - m)
        d = jnp.sum(e, axis=-1, keepdims=True)
        pv = _dot(e.astype(BF16), v)
        den = d if den is None else den + d
        o = pv if o is None else o + pv
    if sink is not None:
        den = den + jnp.exp(sink - m)
    return o / den


def _low_lanes(shape):
    return lax.broadcasted_iota(jnp.int32, shape, 1) < HEAD_DIM


def _stack_heads(blocks):
    lo = _low_lanes(blocks[0].shape)
    zero = jnp.zeros_like(blocks[0])
    parts = []
    for b in blocks:
        b = b * jnp.asarray(HEAD_DIM ** -0.5, BF16)
        parts += [jnp.where(lo, b, zero), jnp.where(lo, zero, b)]
    return jnp.concatenate(parts, axis=0)


def _unstack_pair(o, idx, m):
    return jnp.where(_low_lanes((m, LANE)), o[2 * idx * m:(2 * idx + 1) * m], o[(2 * idx + 1) * m:(2 * idx + 2) * m])


def _sink_column(sink_ref, m):
    blk = lax.broadcasted_iota(jnp.int32, (A_HEADS * m, 1), 0) // m
    col = jnp.full((A_HEADS * m, 1), sink_ref[A_HEAD_ORDER[-1]], F32)
    for i in range(A_HEADS - 1):
        col = jnp.where(blk == i, sink_ref[A_HEAD_ORDER[i]], col)
    return col


def _mla_keys(kv, kpe_blk):
    lo = _low_lanes(kpe_blk.shape)
    return [jnp.where(lo, kv[:, h * LANE:(h + 1) * LANE], kpe_blk).astype(BF16) for h in range(MLA_HEADS)]


def _mla_heads(q_of, k_of, v_of, gate_of, out_ref):
    scale = (MLA_NOPE + MLA_ROPE) ** -0.5
    for pair in range(MLA_HEADS // 2):
        outs = [_softmax_pv([_dot_nt(q_of(h), k_of(h)) * scale], [v_of(h)]) for h in (2 * pair, 2 * pair + 1)]
        o = jnp.where(_low_lanes(outs[0].shape), pltpu.roll(outs[0], MLA_V, 1), outs[1])
        out_ref[0, :, pair * LANE:(pair + 1) * LANE] = (o * gate_of(pair)).astype(BF16)


def _ctx_attn_kernel(sink_ref, z_ref, wkv_ref, oa_ref, ob_ref, od_ref):
    m = SEQ
    gate = lambda c0: _silu(z_ref[0, :, c0:c0 + LANE].astype(F32))
    q4 = _stack_heads([z_ref[0, :, Z_AQ:Z_AQ + LANE], z_ref[0, :, Z_AQ + LANE:Z_AQ + 2 * LANE]])
    o4 = _softmax_pv([_dot_nt(q4, z_ref[0, :, Z_AK:Z_AV])], [z_ref[0, :, Z_AV:Z_DQ]], sink=_sink_column(sink_ref, m))
    for pair in range(2):
        oa_ref[0, :, pair * LANE:(pair + 1) * LANE] = (_unstack_pair(o4, pair, m)
                                                      * gate(Z_AG + pair * LANE)).astype(BF16)
    kv = _dot(z_ref[0, :, Z_CKV:Z_KPE], wkv_ref[...])
    keys = _mla_keys(kv, z_ref[0, :, Z_KPE:Z_QM].astype(F32))
    kvb = kv.astype(BF16)
    _mla_heads(lambda h: z_ref[0, :, Z_QM + LANE * h:Z_QM + LANE * (h + 1)], lambda h: keys[h],
               lambda h: kvb[:, LANE * h:LANE * (h + 1)], lambda p: gate(Z_BG + p * LANE), ob_ref)
    for pair in range(NAT_HEADS // 2):
        c = pair * LANE
        q2 = _stack_heads([z_ref[0, :, Z_DQ + c:Z_DQ + c + LANE]])
        o2 = _softmax_pv([_dot_nt(q2, z_ref[0, :, Z_DK + c:Z_DK + c + LANE])], [z_ref[0, :, Z_DV + c:Z_DV + c + LANE]])
        od_ref[0, :, c:c + LANE] = (_unstack_pair(o2, 0, m) * gate(Z_DG + c)).astype(BF16)


def _ctx_attn_call(z, sink, wkv):
    out = jax.ShapeDtypeStruct((BATCH, SEQ, BRANCH_W), BF16)
    ospec = pl.BlockSpec((1, SEQ, BRANCH_W), lambda b: (b, 0, 0))
    return pl.pallas_call(
        _ctx_attn_kernel,
        out_shape=(out, out, out),
        grid=(BATCH,),
        in_specs=[
            pl.BlockSpec(memory_space=pltpu.SMEM),
            pl.BlockSpec((1, SEQ, Z_W), lambda b: (b, 0, 0)),
            pl.BlockSpec((MLA_KV_LORA, 512), lambda b: (0, 0)),
        ],
        out_specs=(ospec, ospec, ospec),
        compiler_params=_cparams(("arbitrary",), 40),
        name="ctx_attn",
    )(sink, z, wkv)


def _win_attn_kernel(sink_ref, q_ref, kv_ref, g_ref, ck_ref, cv_ref, o_ref, ck_s, cv_s):
    nb = DEC_SEQ // Q_BLOCK
    m = Q_BLOCK

    @pl.when(pl.program_id(1) == 0)
    def _():
        ck_s[...] = ck_ref[0, 0].astype(BF16)
        cv_s[...] = cv_ref[0, 0].astype(BF16)

    sink = _sink_column(sink_ref, m)
    for qb in range(WIN_BPS):
        n = pl.program_id(1) * WIN_BPS + qb
        rows = slice(qb * m, (qb + 1) * m)
        blocks = []
        for off in (-1, 0, 1):
            start = pl.multiple_of(jnp.clip(n + off, 0, nb - 1) * m, m)
            blocks.append(kv_ref[0, pl.ds(start, m), :])
        kvw = jnp.concatenate(blocks, axis=0)
        shape = (A_HEADS * m, 3 * m)
        qpos = n * m + (lax.broadcasted_iota(jnp.int32, shape, 0) & (m - 1))
        kpos = (n - 1) * m + lax.broadcasted_iota(jnp.int32, shape, 1)
        mask = (jnp.abs(qpos - kpos) <= A_WINDOW) & (kpos >= 0) & (kpos < DEC_SEQ)
        q4 = _stack_heads([q_ref[0, rows, 0:LANE], q_ref[0, rows, LANE:2 * LANE]])
        s_win = jnp.where(mask, _dot_nt(q4, kvw[:, 0:LANE]), NEG)
        s_ctx = _dot_nt(q4, ck_s[...])
        o4 = _softmax_pv([s_win, s_ctx], [kvw[:, LANE:2 * LANE], cv_s[...]], sink=sink)
        for pair in range(2):
            g = g_ref[0, rows, pair * LANE:(pair + 1) * LANE].astype(F32)
            o_ref[0, rows, pair * LANE:(pair + 1) * LANE] = (_unstack_pair(o4, pair, m) * _silu(g)).astype(BF16)


def _win_attn_call(z, sink, cache_k, cache_v, layer):
    kvw = A_KV_HEADS * HEAD_DIM
    qrows = WIN_BPS * Q_BLOCK
    return pl.pallas_call(
        _win_attn_kernel,
        out_shape=jax.ShapeDtypeStruct((DEC_BATCH, DEC_SEQ, BRANCH_W), BF16),
        grid=(DEC_BATCH, DEC_SEQ // qrows),
        in_specs=[
            pl.BlockSpec(memory_space=pltpu.SMEM),
            pl.BlockSpec((1, qrows, 256), lambda b, n: (b, n, Z_AQ // 256)),
            pl.BlockSpec((1, DEC_SEQ, 256), lambda b, n: (b, 0, Z_AK // 256)),
            pl.BlockSpec((1, qrows, 256), lambda b, n: (b, n, Z_AG // 256)),
            pl.BlockSpec((1, 1, PAST_LEN, kvw), lambda b, n: (b, layer, 0, 0)),
            pl.BlockSpec((1, 1, PAST_LEN, kvw), lambda b, n: (b, layer, 0, 0)),
        ],
        out_specs=pl.BlockSpec((1, qrows, BRANCH_W), lambda b, n: (b, n, 0)),
        scratch_shapes=[pltpu.VMEM((PAST_LEN, kvw), BF16), pltpu.VMEM((PAST_LEN, kvw), BF16)],
        compiler_params=_cparams(("arbitrary", "arbitrary"), 40),
        name="lat_window_attn",
    )(sink, z, z, z, cache_k, cache_v)


def _nat_kernel(q_ref, k_ref, v_ref, g_ref, ck_ref, cv_ref, bias_ref, o_ref, ck_s, cv_s):
    nrows = DEC_SEQ // GRID_W
    kr = NAT_ROWS
    m = GRID_W

    @pl.when(pl.program_id(1) == 0)
    def _():
        ck_s[...] = ck_ref[0, 0].astype(BF16)
        cv_s[...] = cv_ref[0, 0].astype(BF16)

    shape = (2 * m, kr * GRID_W)
    qc = lax.broadcasted_iota(jnp.int32, shape, 0) & (m - 1)
    kc = lax.broadcasted_iota(jnp.int32, shape, 1) & (GRID_W - 1)
    rel = kc - jnp.clip(qc - NAT_COLS // 2, 0, GRID_W - NAT_COLS)
    mask = (rel >= 0) & (rel < NAT_COLS)
    for rr in range(NAT_RPS):
        r = pl.program_id(1) * NAT_RPS + rr
        rows = slice(rr * m, (rr + 1) * m)
        rs = jnp.clip(r - kr // 2, 0, nrows - kr)
        start = pl.multiple_of(rs * GRID_W, GRID_W)
        kl = k_ref[0, pl.ds(start, kr * GRID_W), :]
        vl = v_ref[0, pl.ds(start, kr * GRID_W), :]
        bidx = rs - r + (NAT_ROWS - 1)
        for pair in range(NAT_HEADS // 2):
            c = pair * LANE
            bias = jnp.concatenate(
                [jnp.concatenate([bias_ref[2 * pair + hh, bidx + 2 * j] for j in range(kr // 2)], axis=1)
                 for hh in range(2)], axis=0)
            q2 = _stack_heads([q_ref[0, rows, c:c + LANE]])
            s_lat = jnp.where(mask, _dot_nt(q2, kl[:, c:c + LANE]) + bias, NEG)
            s_ctx = _dot_nt(q2, ck_s[:, c:c + LANE])
            o2 = _softmax_pv([s_lat, s_ctx], [vl[:, c:c + LANE], cv_s[:, c:c + LANE]])
            g = g_ref[0, rows, c:c + LANE].astype(F32)
            o_ref[0, rows, c:c + LANE] = (_unstack_pair(o2, 0, m) * _silu(g)).astype(BF16)


def _nat_bias_table(rpb):
    nc = NAT_COLS - 1
    rep = lambda a, n: jnp.broadcast_to(a, a.shape[:-1] + (n,))
    ring = jnp.concatenate([rpb[..., nc:], rep(rpb[..., -1:], GRID_W - 1 - nc), rep(rpb[..., :1], GRID_W - nc),
                            rpb[..., :nc]], axis=-1)
    w = 2 * GRID_W - 1
    tq = jnp.tile(ring, (1, 1, GRID_W))[..., :GRID_W * w].reshape(NAT_HEADS, 2 * NAT_ROWS - 1, GRID_W, w)
    tq = tq[..., :GRID_W]
    nxt = jnp.concatenate([tq[:, 1:], jnp.zeros_like(tq[:, :1])], axis=1)
    return jnp.concatenate([tq, nxt], axis=-1).astype(F32)


def _nat_call(z, cache_k, cache_v, bias, layer):
    hw = NAT_HEADS * HEAD_DIM
    qrows = NAT_RPS * GRID_W
    return pl.pallas_call(
        _nat_kernel,
        out_shape=jax.ShapeDtypeStruct((DEC_BATCH, DEC_SEQ, BRANCH_W), BF16),
        grid=(DEC_BATCH, DEC_SEQ // qrows),
        in_specs=[
            pl.BlockSpec((1, qrows, 256), lambda b, r: (b, r, Z_DQ // 256)),
            pl.BlockSpec((1, DEC_SEQ, 256), lambda b, r: (b, 0, Z_DK // 256)),
            pl.BlockSpec((1, DEC_SEQ, 256), lambda b, r: (b, 0, Z_DV // 256)),
            pl.BlockSpec((1, qrows, 256), lambda b, r: (b, r, Z_DG // 256)),
            pl.BlockSpec((1, 1, PAST_LEN, hw), lambda b, r: (b, layer, 0, 0)),
            pl.BlockSpec((1, 1, PAST_LEN, hw), lambda b, r: (b, layer, 0, 0)),
            pl.BlockSpec(bias.shape, lambda b, r: (0, 0, 0, 0)),
        ],
        out_specs=pl.BlockSpec((1, qrows, BRANCH_W), lambda b, r: (b, r, 0)),
        scratch_shapes=[pltpu.VMEM((PAST_LEN, hw), BF16), pltpu.VMEM((PAST_LEN, hw), BF16)],
        compiler_params=_cparams(("arbitrary", "arbitrary"), 40),
        name="lat_nat_attn",
    )(z, z, z, z, cache_k, cache_v, bias)


def _mla_kernel(q_ref, ckv_ref, kpe_ref, g_ref, cckv_ref, ckpe_ref, wkv_ref, o_ref, k_s, v_s):
    nlat = DEC_SEQ

    @pl.when(pl.program_id(1) == 0)
    def _():
        rows = 512

        def fill(r0, ckv, kpe_blk):
            kv = _dot(ckv, wkv_ref[...])
            for h, kh in enumerate(_mla_keys(kv, kpe_blk)):
                k_s[h, r0:r0 + rows, :] = kh
                v_s[h, r0:r0 + rows, :] = kv[:, h * LANE:(h + 1) * LANE].astype(BF16)

        for c in range(nlat // rows):
            fill(c * rows, ckv_ref[0, c * rows:(c + 1) * rows, :], kpe_ref[0, c * rows:(c + 1) * rows, :].astype(F32))
        ckpe = ckpe_ref[0, 0]
        ckpe_blk = jnp.concatenate([jnp.zeros((PAST_LEN, KPE_LANE), F32), ckpe,
                                    jnp.zeros((PAST_LEN, LANE - KPE_LANE - MLA_ROPE), F32)], axis=1)
        fill(nlat, cckv_ref[0, 0].astype(BF16), ckpe_blk)

    _mla_heads(lambda h: q_ref[0, :, LANE * h:LANE * (h + 1)], lambda h: k_s[h], lambda h: v_s[h],
               lambda p: _silu(g_ref[0, :, p * LANE:(p + 1) * LANE].astype(F32)), o_ref)


def _mla_call(z, cache_ckv, cache_kpe, wkv, layer):
    nkeys = DEC_SEQ + PAST_LEN
    return pl.pallas_call(
        _mla_kernel,
        out_shape=jax.ShapeDtypeStruct((DEC_BATCH, DEC_SEQ, BRANCH_W), BF16),
        grid=(DEC_BATCH, DEC_SEQ // TQ_MLA),
        in_specs=[
            pl.BlockSpec((1, TQ_MLA, MLA_HEADS * LANE), lambda b, t: (b, t, Z_QM // (MLA_HEADS * LANE))),
            pl.BlockSpec((1, DEC_SEQ, LANE), lambda b, t: (b, 0, Z_CKV // LANE)),
            pl.BlockSpec((1, DEC_SEQ, LANE), lambda b, t: (b, 0, Z_KPE // LANE)),
            pl.BlockSpec((1, TQ_MLA, 256), lambda b, t: (b, t, Z_BG // 256)),
            pl.BlockSpec((1, 1, PAST_LEN, MLA_KV_LORA), lambda b, t: (b, layer, 0, 0)),
            pl.BlockSpec((1, 1, PAST_LEN, MLA_ROPE), lambda b, t: (b, layer, 0, 0)),
            pl.BlockSpec((MLA_KV_LORA, 512), lambda b, t: (0, 0)),
        ],
        out_specs=pl.BlockSpec((1, TQ_MLA, BRANCH_W), lambda b, t: (b, t, 0)),
        scratch_shapes=[pltpu.VMEM((MLA_HEADS, nkeys, LANE), BF16), pltpu.VMEM((MLA_HEADS, nkeys, LANE), BF16)],
        compiler_params=_cparams(("arbitrary", "arbitrary"), 56),
        name="lat_mla_attn",
    )(z, z, z, z, cache_ckv, cache_kpe, wkv)


def _s5_mats(lam_re, lam_im, log_dt, b_re, b_im, c_re, c_im):
    t = S5_T
    hp = lax.Precision.HIGHEST
    lre = jnp.minimum(lam_re, -1e-4)
    lim = lam_im
    dt = jnp.exp(log_dt)[..., None]
    er, ei = lre * dt, lim * dt
    mag = jnp.exp(er)
    are, aim = mag * jnp.cos(ei), mag * jnp.sin(ei)
    den = lre * lre + lim * lim
    qre = ((are - 1.0) * lre + aim * lim) / den
    qim = (aim * lre - (are - 1.0) * lim) / den
    bbr = qre[..., None] * b_re - qim[..., None] * b_im
    bbi = qre[..., None] * b_im + qim[..., None] * b_re
    n = jnp.arange(t + 1, dtype=F32)[:, None, None, None]
    pmag = jnp.exp(er[None] * n)
    pr, pi = pmag * jnp.cos(ei[None] * n), pmag * jnp.sin(ei[None] * n)

    cbr = c_re[..., None] * bbr[:, :, None] - c_im[..., None] * bbi[:, :, None]
    cbi = c_re[..., None] * bbi[:, :, None] + c_im[..., None] * bbr[:, :, None]
    kk = (jnp.einsum('tdgp,dgcpk->dgktc', pr[:t], cbr, precision=hp)
          - jnp.einsum('tdgp,dgcpk->dgktc', pi[:t], cbi, precision=hp))
    ring = jnp.concatenate([kk[0][:, :, 0:1] + kk[1][:, :, 0:1], kk[0][:, :, 1:], jnp.zeros_like(kk[0][:, :, 0:1]),
                            kk[1][:, :, 1:][:, :, ::-1]], axis=2).reshape(S5_GROUPS, S5_CH, 2 * S5_TC)

    def state_cols(pw_r, pw_i, d):
        sr = pw_r[..., None] * bbr[d][None] - pw_i[..., None] * bbi[d][None]
        si = pw_r[..., None] * bbi[d][None] + pw_i[..., None] * bbr[d][None]
        f = lambda a: jnp.transpose(a, (1, 0, 3, 2)).reshape(S5_GROUPS, S5_TC, S5_STATE)
        return f(sr), f(si)

    sfr, sfi = state_cols(pr[:t, 0][::-1], pi[:t, 0][::-1], 0)
    sbr, sbi = state_cols(pr[:t, 1], pi[:t, 1], 1)
    m_state = jnp.concatenate([sfr, sbr, sfi, sbi], axis=-1)

    def out_rows(pw_r, pw_i, d):
        wr = c_re[d][None] * pw_r[:, :, None, :] - c_im[d][None] * pw_i[:, :, None, :]
        wi = c_re[d][None] * pw_i[:, :, None, :] + c_im[d][None] * pw_r[:, :, None, :]
        f = lambda a: jnp.transpose(a, (1, 3, 0, 2)).reshape(S5_GROUPS, S5_STATE, S5_TC)
        return f(wr), f(-wi)

    ofr, ofi = out_rows(pr[1:t + 1, 0], pi[1:t + 1, 0], 0)
    obr, obi = out_rows(pr[1:t + 1, 1][::-1], pi[1:t + 1, 1][::-1], 1)
    m_out = jnp.concatenate([ofr, obr, ofi, obi], axis=1)

    a_t = jnp.stack([jnp.concatenate([pr[t, 0], pr[t, 1]], axis=-1),
                     jnp.concatenate([pi[t, 0], pi[t, 1]], axis=-1)], axis=1)
    return ring, m_state.astype(BF16), m_out.astype(BF16), a_t


def _s5_kernel(uc_ref, ul_ref, ring_ref, ms_ref, mo_ref, at_ref, h0_ref, yc_ref, yl_ref, hend_ref,
               s_s, hf_s, hb_s, m_s):
    p = S5_STATE
    u = jnp.concatenate([uc_ref[0], ul_ref[0]], axis=0)
    s = _dot(u, ms_ref[0])
    s_s[0] = s[:, 0:2 * p]
    s_s[1] = s[:, 2 * p:4 * p]
    are = at_ref[0, 0:1, :]
    aim = at_ref[0, 1:2, :]
    fwd_lane = lax.broadcasted_iota(jnp.int32, (1, 2 * p), 1) < p

    for k in range(S5_CH):
        rk = jnp.broadcast_to(ring_ref[0, k:k + 1, :], (S5_T, 2 * S5_TC))
        rolled = pltpu.roll(rk, 0, 1, stride=S5_CH, stride_axis=0)
        for blk in range(S5_TC // LANE):
            m_s[blk, pl.ds(k, S5_T, stride=S5_CH), :] = rolled[:, blk * LANE:(blk + 1) * LANE]

    def scan(row0, nk, nb, hre, him):
        for st in range(nk):
            rf = pl.ds(row0 + st, nb, stride=nk)
            rb = pl.ds(row0 + nk - 1 - st, nb, stride=nk)
            hf_s[0, rf, :] = hre
            hf_s[1, rf, :] = him
            hb_s[0, rb, :] = hre
            hb_s[1, rb, :] = him
            sre = jnp.where(fwd_lane, s_s[0, rf, :], s_s[0, rb, :])
            sim = jnp.where(fwd_lane, s_s[1, rf, :], s_s[1, rb, :])
            hre, him = are * hre - aim * him + sre, are * him + aim * hre + sim
        return hre, him

    zero = jnp.zeros((BATCH, 2 * p), F32)
    hre, him = scan(0, S5_NK_CTX, BATCH, zero, zero)
    hend_ref[0] = jnp.concatenate([hre, him], axis=1)
    scan(S5_ROWS, S5_NK_LAT, DEC_BATCH, h0_ref[0, :, 0:2 * p], h0_ref[0, :, 2 * p:4 * p])
    hst = jnp.concatenate([jnp.where(fwd_lane, hf_s[0], hb_s[0]), jnp.where(fwd_lane, hf_s[1], hb_s[1])], axis=1)
    m_intra = jnp.concatenate([m_s[blk] for blk in range(S5_TC // LANE)], axis=1).astype(BF16)
    y = _dot(u, m_intra) + _dot(hst.astype(BF16), mo_ref[0])
    yc_ref[0] = y[0:S5_ROWS]
    yl_ref[0] = y[S5_ROWS:2 * S5_ROWS]


def _s5_call(ut_ctx, ut_lat, ring, m_state, m_out, a_t, h0):
    g3 = lambda g: (g, 0, 0)
    yshape = jax.ShapeDtypeStruct((S5_GROUPS, S5_ROWS, S5_TC), F32)
    uspec = pl.BlockSpec((1, S5_ROWS, S5_TC), g3)
    return pl.pallas_call(
        _s5_kernel,
        out_shape=(yshape, yshape, jax.ShapeDtypeStruct((S5_GROUPS, BATCH, 4 * S5_STATE), F32)),
        grid=(S5_GROUPS,),
        in_specs=[
            uspec, uspec,
            pl.BlockSpec((1, S5_CH, 2 * S5_TC), g3),
            pl.BlockSpec((1, S5_TC, 4 * S5_STATE), g3),
            pl.BlockSpec((1, 4 * S5_STATE, S5_TC), g3),
            pl.BlockSpec((1, 2, 2 * S5_STATE), g3),
            pl.BlockSpec((1, DEC_BATCH, 4 * S5_STATE), g3),
        ],
        out_specs=(uspec, uspec, pl.BlockSpec((1, BATCH, 4 * S5_STATE), g3)),
        scratch_shapes=[pltpu.VMEM((2, 2 * S5_ROWS, LANE), F32)] * 3 + [pltpu.VMEM((S5_TC // LANE, S5_TC, LANE), F32)],
        compiler_params=_cparams(("arbitrary",), 32),
        name="s5_chunks",
    )(ut_ctx, ut_lat, ring, m_state, m_out, a_t, h0)


def _post_kernel(x_ref, mod_ref, ng_ref, ba_ref, bb_ref, bd_ref, y_ref, u_ref, cg_ref, sd_ref, wglu_ref,
                 wm_ref, wbr_ref, wo_ref, fg_ref, o_ref, ys_s, *, final):
    x = x_ref[0]
    hb = _ada_h(x, mod_ref, ng_ref).astype(BF16)
    for m in range(S5_T // 8):
        for hf in range(2):
            vs = _slot_transpose([y_ref[8 * hf + gp, :, m * LANE:(m + 1) * LANE] for gp in range(8)])
            for s in range(8):
                ys_s[hf, pl.ds(8 * m + s, TM_CHUNKS, stride=S5_T), :] = vs[s]
    yv = jnp.concatenate([ys_s[0], ys_s[1]], axis=1) + sd_ref[...] * u_ref[0].astype(F32)
    gel = 0.5 * yv * (1.0 + jnp.tanh(math.sqrt(2.0 / math.pi) * (yv + 0.044715 * (yv * yv * yv))))
    gl = _dot(gel.astype(BF16), wglu_ref[...])
    oc = gl[:, 0:BRANCH_W] * _sigmoid(gl[:, BRANCH_W:2 * BRANCH_W])
    bc = (oc * _silu(cg_ref[0].astype(F32))).astype(BF16)
    acc = None
    for k, br in enumerate((ba_ref[0], bb_ref[0], bc, bd_ref[0])):
        proj = _dot(br, wbr_ref[k])
        mg = _sigmoid(_dot(hb, wm_ref[:, k * D_MODEL:(k + 1) * D_MODEL]))
        acc = mg * proj if acc is None else acc + mg * proj
    y = _dot(acc.astype(BF16), wo_ref[...])
    out = x + mod_ref[0, :, 2 * D_MODEL:3 * D_MODEL] * y
    if final:
        out = _rms(out, fg_ref[...])
    o_ref[0] = out


def _post_call(x, mod, ng, ba, bb, bd, y_s5, z, s5d, wglu, wm, wbr, wo, fg, final, name):
    nb, length, _ = x.shape
    nt = length // TM
    tok = lambda b, t: (b, t, 0)
    full2 = lambda b, t: (0, 0)
    br_spec = pl.BlockSpec((1, TM, BRANCH_W), tok)
    return pl.pallas_call(
        functools.partial(_post_kernel, final=final),
        out_shape=jax.ShapeDtypeStruct((nb, length, D_MODEL), F32),
        grid=(nb, nt),
        in_specs=[
            pl.BlockSpec((1, TM, D_MODEL), tok),
            pl.BlockSpec((1, 1, 3 * D_MODEL), lambda b, t: (b, 0, 0)),
            pl.BlockSpec((1, D_MODEL), full2),
            br_spec, br_spec, br_spec,
            pl.BlockSpec((S5_GROUPS, TM_CHUNKS, S5_TC), lambda b, t: (0, b * nt + t, 0)),
            pl.BlockSpec((1, TM, 256), lambda b, t: (b, t, Z_CU // 256)),
            pl.BlockSpec((1, TM, 256), lambda b, t: (b, t, Z_CG // 256)),
            pl.BlockSpec((1, BRANCH_W), full2),
            pl.BlockSpec((BRANCH_W, 2 * BRANCH_W), full2),
            pl.BlockSpec((D_MODEL, N_BRANCH * D_MODEL), full2),
            pl.BlockSpec((N_BRANCH, BRANCH_W, D_MODEL), lambda b, t: (0, 0, 0)),
            pl.BlockSpec((D_MODEL, D_MODEL), full2),
            pl.BlockSpec((1, D_MODEL), full2),
        ],
        out_specs=pl.BlockSpec((1, TM, D_MODEL), tok),
        scratch_shapes=[pltpu.VMEM((2, TM, LANE), F32)],
        compiler_params=_cparams(("arbitrary", "arbitrary"), 56),
        name=name,
    )(x, mod, ng, ba, bb, bd, y_s5, z, z, s5d, wglu, wm, wbr, wo, fg)


def _rope_tables():
    t = np.arange(DEC_SEQ)
    row = (t // GRID_W).astype(np.float64)
    col = (t % GRID_W).astype(np.float64)

    def pattern(half):
        inv = ROPE_BASE ** (-np.arange(half, dtype=np.float64) / half)
        zeros = np.zeros((DEC_SEQ, half))
        cs, s_up, s_lo = [], [], []
        for pos in (row, col):
            ang = pos[:, None] * inv[None, :]
            c, s = np.cos(ang), np.sin(ang)
            cs += [c, c]
            s_up += [zeros, s]
            s_lo += [-s, zeros]
        return [np.concatenate(parts, axis=1) for parts in (cs, s_up, s_lo)]

    tab_a = np.stack([np.tile(part, (1, LANE // HEAD_DIM)) for part in pattern(HEAD_DIM // 4)])
    ident = [np.ones, np.zeros, np.zeros]
    tab_m = np.stack([np.concatenate([fill((DEC_SEQ, KPE_LANE)), part,
                                      fill((DEC_SEQ, LANE - KPE_LANE - MLA_ROPE))], axis=1)
                      for fill, part in zip(ident, pattern(MLA_ROPE // 4))])
    return jnp.asarray(tab_a, F32), jnp.asarray(tab_m, F32)


def _a_head_cols(w):
    return jnp.concatenate([w[:, h * HEAD_DIM:(h + 1) * HEAD_DIM] for h in A_HEAD_ORDER], axis=1)


def _reorder_w_in(w):
    zpad = lambda n: jnp.zeros((D_MODEL, n), w.dtype)
    return jnp.concatenate([_a_head_cols(w[:, 0:256]), w[:, 256:512], w[:, 1952:2720], w[:, 768:1152],
                            zpad(KPE_LANE), w[:, 1152:1184], zpad(LANE - KPE_LANE - MLA_ROPE),
                            w[:, 1440:1696], _a_head_cols(w[:, 512:768]), w[:, 1184:1440], w[:, 1696:1952],
                            w[:, 2720:2976]], axis=1)


def _reorder_w_q_up(w):
    w = w.reshape(MLA_Q_LORA, MLA_HEADS, MLA_NOPE + MLA_ROPE)
    w = jnp.concatenate([w, jnp.zeros((MLA_Q_LORA, MLA_HEADS, LANE - MLA_NOPE - MLA_ROPE), w.dtype)], axis=-1)
    return w.reshape(MLA_Q_LORA, MLA_HEADS * LANE)


def _reorder_w_branch(w):
    wa = w[0].reshape(A_HEADS, HEAD_DIM, D_MODEL)
    wa = jnp.concatenate([wa[h] for h in A_HEAD_ORDER], axis=0)
    return jnp.concatenate([wa[None], w[1:]], axis=0)


def kernel(x_prompt, x_sample, cache_a_k, cache_a_v, cache_mla_ckv, cache_mla_kpe, cache_na_k, cache_na_v,
           state_s5_re, state_s5_im, c, c_ctx, w_mod, b_mod, norm_g, w_in, w_merge, a_sink,
           mla_q_norm, mla_w_q_up, mla_kv_norm, mla_w_kv_up, s5_lam_re, s5_lam_im, s5_log_dt,
           s5_b_re, s5_b_im, s5_c_re, s5_c_im, s5_d, s5_w_glu, na_rpb, w_branch, w_out, final_norm_g):
    n_ctx = BATCH * SEQ
    conds = jnp.concatenate([c, c_ctx[None, :], jnp.zeros((3, D_MODEL), F32)], axis=0)
    mods = _mod_call(conds, w_mod, b_mod)
    tabs = _rope_tables()
    ck_a = cache_a_k.reshape(DEC_BATCH, DEPTH, PAST_LEN, A_KV_HEADS * HEAD_DIM)
    cv_a = cache_a_v.reshape(DEC_BATCH, DEPTH, PAST_LEN, A_KV_HEADS * HEAD_DIM)
    ck_n = cache_na_k.reshape(DEC_BATCH, DEPTH, PAST_LEN, NAT_HEADS * HEAD_DIM)
    cv_n = cache_na_v.reshape(DEC_BATCH, DEPTH, PAST_LEN, NAT_HEADS * HEAD_DIM)
    fg = final_norm_g.reshape(1, D_MODEL)
    caches = [jnp.zeros((BATCH, DEPTH, SEQ, w), F32) for w in (128, 128, MLA_KV_LORA, MLA_ROPE, 256, 256)]

    yp = x_prompt.reshape(1, n_ctx, D_MODEL)
    ys = x_sample
    s5_re, s5_im = [], []
    for i in range(DEPTH):
        mod_lat = mods[i, 0:DEC_BATCH].reshape(DEC_BATCH, 1, 3 * D_MODEL)
        mod_ctx = mods[i, DEC_BATCH:DEC_BATCH + 1].reshape(1, 1, 3 * D_MODEL)
        ng = norm_g[i].reshape(1, D_MODEL)
        w_raw = _reorder_w_in(w_in[i]).astype(BF16)
        wq = _reorder_w_q_up(mla_w_q_up[i]).astype(BF16)
        wkv = mla_w_kv_up[i].astype(BF16)
        qnorm = mla_q_norm[i].reshape(1, MLA_Q_LORA)
        kvnorm = mla_kv_norm[i].reshape(1, MLA_KV_LORA)
        wm = w_merge[i].astype(BF16)
        wbr = _reorder_w_branch(w_branch[i]).astype(BF16)
        wo = w_out[i].astype(BF16)
        wglu = s5_w_glu[i].astype(BF16)
        s5d = s5_d[i].reshape(1, BRANCH_W)

        res = _inproj_call(yp, mod_ctx, ng, w_raw, qnorm, wq, kvnorm, None, caches, i, "inproj_ctx")
        z_ctx, ut_ctx, caches = res[0], res[1], list(res[2:])
        z_lat, ut_lat = _inproj_call(ys, mod_lat, ng, w_raw, qnorm, wq, kvnorm, tabs, None, i, "inproj_lat")
        zc = z_ctx.reshape(BATCH, SEQ, Z_W)

        ring, m_state, m_out, a_t = _s5_mats(s5_lam_re[i], s5_lam_im[i], s5_log_dt[i], s5_b_re[i], s5_b_im[i],
                                             s5_c_re[i], s5_c_im[i])
        sre, sim = state_s5_re[:, i], state_s5_im[:, i]
        h0 = jnp.transpose(jnp.concatenate([sre[:, 0], sre[:, 1], sim[:, 0], sim[:, 1]], axis=-1), (1, 0, 2))
        y_ctx, y_lat, hend = _s5_call(ut_ctx, ut_lat, ring, m_state, m_out, a_t, h0)
        hend = jnp.transpose(hend.reshape(S5_GROUPS, BATCH, 2, 2, S5_STATE), (1, 2, 3, 0, 4))
        s5_re.append(hend[:, 0])
        s5_im.append(hend[:, 1])

        oa_c, ob_c, od_c = _ctx_attn_call(zc, a_sink[i], wkv)
        oa_l = _win_attn_call(z_lat, a_sink[i], ck_a, cv_a, i)
        ob_l = _mla_call(z_lat, cache_mla_ckv, cache_mla_kpe, wkv, i)
        od_l = _nat_call(z_lat, ck_n, cv_n, _nat_bias_table(na_rpb[i]), i)

        flat = lambda a: a.reshape(1, n_ctx, BRANCH_W)
        final = i == DEPTH - 1
        yp = _post_call(yp, mod_ctx, ng, flat(oa_c), flat(ob_c), flat(od_c), y_ctx, z_ctx, s5d, wglu, wm, wbr, wo,
                        fg, final, "post_ctx")
        ys = _post_call(ys, mod_lat, ng, oa_l, ob_l, od_l, y_lat, z_lat, s5d, wglu, wm, wbr, wo, fg, final,
                        "post_lat")

    ak, av, ckv, kpe, nk, nv = caches
    heads = lambda a, h: a.reshape(BATCH, DEPTH, SEQ, h, HEAD_DIM)
    return (yp.reshape(BATCH, SEQ, D_MODEL), ys, heads(ak, A_KV_HEADS), heads(av, A_KV_HEADS), ckv, kpe,
            heads(nk, NAT_HEADS), heads(nv, NAT_HEADS), jnp.stack(s5_re, axis=1), jnp.stack(s5_im, axis=1))
```

```python
import functools
import math

import numpy as np
import jax
import jax.numpy as jnp
from jax import lax
from jax.experimental import pallas as pl
from jax.experimental.pallas import tpu as pltpu

F32 = jnp.float32
BF16 = jnp.bfloat16

D_MODEL = 1024
BATCH = 32
SEQ = 256
DEPTH = 2
DEC_BATCH = 4
DEC_SEQ = 2048
PAST_LEN = 512
GRID_W = 64
HEAD_DIM = 64
BRANCH_W = 256
N_BRANCH = 4
Q_BLOCK = 128
A_HEADS = 4
A_KV_HEADS = 2
A_GROUP = A_HEADS // A_KV_HEADS
A_WINDOW = 128
MLA_HEADS = 4
MLA_Q_LORA = 256
MLA_KV_LORA = 128
MLA_NOPE = 64
MLA_ROPE = 32
MLA_V = 64
S5_CH = 16
S5_GROUPS = BRANCH_W // S5_CH
S5_STATE = 64
NAT_HEADS = 4
NAT_ROWS = 8
NAT_COLS = 16
ROPE_BASE = 10000.0
EPS = 1e-6
NEG = -1e30
LOG2E = 1.4426950408889634
Q_SCALE = HEAD_DIM ** -0.5 * LOG2E
MLA_Q_SCALE = (MLA_NOPE + MLA_ROPE) ** -0.5 * LOG2E

LANE = 128

R_AV, R_CQ, R_CKV, R_KPE, R_CU, R_G, R_W = 384, 1280, 1536, 1664, 1792, 2048, 3072
KPE_LANE = 64
A_HEAD_ORDER = (0, 2, 1, 3)
Z_AQ, Z_AK, Z_AV = 0, 256, 384
Z_DQ, Z_DK, Z_DV = 512, 768, 1024
Z_CKV, Z_KPE = 1280, 1408
Z_QM = 1536
Z_AG, Z_BG, Z_CG, Z_DG = 2048, 2304, 2560, 2816
Z_CU = 3072
Z_W = 3328

S5_T = 32
S5_TC = S5_T * S5_CH
S5_ROWS = 256
S5_NK_CTX = SEQ // S5_T
S5_NK_LAT = DEC_SEQ // S5_T

TM = 512
TM_CHUNKS = TM // S5_T
TQ_MLA = 512
NAT_RPS = 4
WIN_BPS = 2


def _cparams(sem, vmem_mb):
    return pltpu.CompilerParams(dimension_semantics=sem, vmem_limit_bytes=vmem_mb * 1024 * 1024)


def _sigmoid(x):
    return 1.0 / (1.0 + jnp.exp(-x))


def _silu(x):
    return x * _sigmoid(x)


def _rms(x, g):
    return x * lax.rsqrt(jnp.mean(x * x, axis=-1, keepdims=True) + EPS) * g


def _dot(a, b):
    return jnp.dot(a, b, preferred_element_type=F32)


def _dot_nt(a, b):
    return lax.dot_general(a, b, (((1,), (1,)), ((), ())), preferred_element_type=F32)


def _slot_transpose(vs):
    lane = lax.broadcasted_iota(jnp.int32, vs[0].shape, 1)
    vs = list(vs)
    for d in (4, 2, 1):
        keep = (lane & (S5_CH * d)) == 0
        nxt = list(vs)
        for lo in range(8):
            if lo & d:
                continue
            hi = lo + d
            nxt[lo] = jnp.where(keep, vs[lo], pltpu.roll(vs[hi], S5_CH * d, 1))
            nxt[hi] = jnp.where(keep, pltpu.roll(vs[lo], LANE - S5_CH * d, 1), vs[hi])
        vs = nxt
    return vs


def _mod_kernel(c_ref, w_ref, b_ref, o_ref):
    s = _silu(c_ref[...])
    o_ref[0] = _dot(s.astype(BF16), w_ref[0].astype(BF16)) + b_ref[0]


def _mod_call(conds, w_mod, b_mod):
    nc = 512
    return pl.pallas_call(
        _mod_kernel,
        out_shape=jax.ShapeDtypeStruct((DEPTH, 8, 3 * D_MODEL), F32),
        grid=(DEPTH, 3 * D_MODEL // nc),
        in_specs=[
            pl.BlockSpec((8, D_MODEL), lambda i, j: (0, 0)),
            pl.BlockSpec((1, D_MODEL, nc), lambda i, j: (i, 0, j)),
            pl.BlockSpec((1, 1, nc), lambda i, j: (i, 0, j)),
        ],
        out_specs=pl.BlockSpec((1, 8, nc), lambda i, j: (i, 0, j)),
        compiler_params=_cparams(("arbitrary", "arbitrary"), 32),
        name="mod_rows",
    )(conds, w_mod, b_mod.reshape(DEPTH, 1, 3 * D_MODEL))


def _ada_h(x, mod_ref, ng_ref):
    shift = mod_ref[0, :, 0:D_MODEL]
    scale = mod_ref[0, :, D_MODEL:2 * D_MODEL]
    return _rms(x, ng_ref[...]) * (1.0 + scale) + shift


def _rope_block(xs, tab_ref, shift):
    return (xs * tab_ref[0] + pltpu.roll(xs, shift, 1) * tab_ref[1]
            + pltpu.roll(xs, LANE - shift, 1) * tab_ref[2])


N_CACHE = 6


def _inproj_kernel(*refs, rope):
    x_ref, mod_ref, ng_ref, w_ref, qn_ref, wq_ref, kvn_ref = refs[:7]
    if rope:
        ta_ref, tm_ref, z_ref, ut_ref, u_s = refs[7:]
    else:
        z_ref, ut_ref, ak_ref, av_ref, ckv_ref, kpe_ref, nk_ref, nv_ref, u_s = refs[7 + N_CACHE:]
    h = _ada_h(x_ref[0], mod_ref, ng_ref)
    raw = _dot(h.astype(BF16), w_ref[...])
    for j in range(R_AV // LANE):
        blk = raw[:, j * LANE:(j + 1) * LANE]
        if j < Z_AK // LANE:
            blk = blk * Q_SCALE
        if rope:
            blk = _rope_block(blk, ta_ref, 16)
        z_ref[0, :, j * LANE:(j + 1) * LANE] = blk.astype(BF16)
    z_ref[0, :, Z_AV:Z_DQ] = raw[:, Z_AV:Z_DQ].astype(BF16)
    z_ref[0, :, Z_DQ:Z_DK] = (raw[:, Z_DQ:Z_DK] * Q_SCALE).astype(BF16)
    z_ref[0, :, Z_DK:R_CQ] = raw[:, Z_DK:R_CQ].astype(BF16)
    qn = _rms(raw[:, R_CQ:R_CKV], qn_ref[...])
    q = _dot(qn.astype(BF16), wq_ref[...]) * MLA_Q_SCALE
    kp = raw[:, R_KPE:R_CU]
    ckv = _rms(raw[:, R_CKV:R_KPE], kvn_ref[...])
    for hh in range(MLA_HEADS):
        qh = q[:, hh * LANE:(hh + 1) * LANE]
        if rope:
            qh = _rope_block(qh, tm_ref, 8)
        z_ref[0, :, Z_QM + hh * LANE:Z_QM + (hh + 1) * LANE] = qh.astype(BF16)
    z_ref[0, :, Z_CKV:Z_KPE] = ckv.astype(BF16)
    z_ref[0, :, Z_KPE:Z_QM] = (_rope_block(kp, tm_ref, 8) if rope else kp).astype(BF16)
    z_ref[0, :, Z_AG:Z_CU] = raw[:, R_G:R_W].astype(BF16)
    z_ref[0, :, Z_CU:Z_W] = raw[:, R_CU:R_G].astype(BF16)
    if not rope:
        half = TM // 2
        for e in range(2):
            rows = slice(e * half, (e + 1) * half)
            ak_ref[e, 0] = raw[rows, Z_AK:Z_AV]
            av_ref[e, 0] = raw[rows, Z_AV:Z_DQ]
            nk_ref[e, 0] = raw[rows, Z_DK:Z_DV]
            nv_ref[e, 0] = raw[rows, Z_DV:R_CQ]
            ckv_ref[e, 0] = ckv[rows]
            kpe_ref[e, 0] = kp[rows, KPE_LANE:KPE_LANE + MLA_ROPE]
    for hf in range(2):
        u_s[hf] = raw[:, R_CU + hf * LANE:R_CU + (hf + 1) * LANE]
    for m in range(S5_T // 8):
        for hf in range(2):
            outs = _slot_transpose([u_s[hf, pl.ds(8 * m + s, TM_CHUNKS, stride=S5_T), :] for s in range(8)])
            for gp in range(8):
                ut_ref[8 * hf + gp, :, m * LANE:(m + 1) * LANE] = outs[gp].astype(BF16)


def _layer_spec(shape, layer):
    zeros = (0,) * len(shape)
    return pl.BlockSpec((None,) + tuple(shape), lambda *_: (layer,) + zeros)


def _mod_spec(layer, row0):
    return pl.BlockSpec((None, 1, 1, 3 * D_MODEL), lambda b, t: (layer, row0 + b, 0, 0))


def _inproj_call(x, mods, mod_row0, p, tabs, caches, layer, name):
    nb, length, _ = x.shape
    rope = tabs is not None
    nt = length // TM
    in_specs = [
        pl.BlockSpec((1, TM, D_MODEL), lambda b, t: (b, t, 0)),
        _mod_spec(layer, mod_row0),
        _layer_spec((1, D_MODEL), layer),
        _layer_spec((D_MODEL, R_W), layer),
        _layer_spec((1, MLA_Q_LORA), layer),
        _layer_spec((MLA_Q_LORA, MLA_HEADS * LANE), layer),
        _layer_spec((1, MLA_KV_LORA), layer),
    ]
    args = [x, mods, p['ng'], p['w_raw'], p['qnorm'], p['wq'], p['kvnorm']]
    out_shape = [jax.ShapeDtypeStruct((nb, length, Z_W), BF16),
                 jax.ShapeDtypeStruct((S5_GROUPS, S5_ROWS, S5_TC), BF16)]
    out_specs = [pl.BlockSpec((1, TM, Z_W), lambda b, t: (b, t, 0)),
                 pl.BlockSpec((S5_GROUPS, TM_CHUNKS, S5_TC), lambda b, t: (0, b * nt + t, 0))]
    aliases = {}
    if rope:
        in_specs += [pl.BlockSpec((3, TM, LANE), lambda b, t: (0, t, 0))] * 2
        args += list(tabs)
    else:
        for k, cbuf in enumerate(caches):
            in_specs.append(pl.BlockSpec(memory_space=pl.ANY))
            args.append(cbuf)
            out_shape.append(jax.ShapeDtypeStruct(cbuf.shape, cbuf.dtype))
            out_specs.append(pl.BlockSpec((2, 1, SEQ, cbuf.shape[-1]), lambda b, t: (t, layer, 0, 0)))
            aliases[7 + k] = 2 + k
    return pl.pallas_call(
        functools.partial(_inproj_kernel, rope=rope),
        out_shape=tuple(out_shape),
        grid=(nb, nt),
        in_specs=in_specs,
        out_specs=tuple(out_specs),
        scratch_shapes=[pltpu.VMEM((2, TM, LANE), F32)],
        input_output_aliases=aliases,
        compiler_params=_cparams(("arbitrary", "arbitrary"), 56),
        name=name,
    )(*args)


def _softmax_pv(s_list, v_list, sink=None):
    m = jnp.max(s_list[0], axis=-1, keepdims=True)
    for s in s_list[1:]:
        m = jnp.maximum(m, jnp.max(s, axis=-1, keepdims=True))
    if sink is not None:
        m = jnp.maximum(m, sink)
    den = None
    o = None
    for s, v in zip(s_list, v_list):
        e = jnp.exp2(s - m)
        d = jnp.sum(e, axis=-1, keepdims=True)
        pv = _dot(e.astype(BF16), v)
        den = d if den is None else den + d
        o = pv if o is None else o + pv
    if sink is not None:
        den = den + jnp.exp2(sink - m)
    return o / den


def _low_lanes(shape):
    return lax.broadcasted_iota(jnp.int32, shape, 1) < HEAD_DIM


def _stack_heads(blocks):
    lo = _low_lanes(blocks[0].shape)
    zero = jnp.zeros_like(blocks[0])
    parts = []
    for b in blocks:
        parts += [jnp.where(lo, b, zero), jnp.where(lo, zero, b)]
    return jnp.concatenate(parts, axis=0)


def _unstack_pair(o, idx, m):
    return jnp.where(_low_lanes((m, LANE)), o[2 * idx * m:(2 * idx + 1) * m], o[(2 * idx + 1) * m:(2 * idx + 2) * m])


def _sink_column(sink_ref, layer, m):
    blk = lax.broadcasted_iota(jnp.int32, (A_HEADS * m, 1), 0) // m
    col = jnp.full((A_HEADS * m, 1), sink_ref[layer, A_HEAD_ORDER[-1]] * LOG2E, F32)
    for i in range(A_HEADS - 1):
        col = jnp.where(blk == i, sink_ref[layer, A_HEAD_ORDER[i]] * LOG2E, col)
    return col


def _mla_keys(kv, kpe_blk):
    lo = _low_lanes(kpe_blk.shape)
    return [jnp.where(lo, kv[:, h * LANE:(h + 1) * LANE], kpe_blk).astype(BF16) for h in range(MLA_HEADS)]


def _mla_heads(q_of, k_of, v_of, gate_of, out_ref):
    for pair in range(MLA_HEADS // 2):
        outs = [_softmax_pv([_dot_nt(q_of(h), k_of(h))], [v_of(h)]) for h in (2 * pair, 2 * pair + 1)]
        o = jnp.where(_low_lanes(outs[0].shape), pltpu.roll(outs[0], MLA_V, 1), outs[1])
        out_ref[0, :, pair * LANE:(pair + 1) * LANE] = (o * gate_of(pair)).astype(BF16)


def _ctx_attn_kernel(sink_ref, z_ref, wkv_ref, oa_ref, ob_ref, od_ref, *, layer):
    m = SEQ
    gate = lambda c0: _silu(z_ref[0, :, c0:c0 + LANE].astype(F32))
    q4 = _stack_heads([z_ref[0, :, Z_AQ:Z_AQ + LANE], z_ref[0, :, Z_AQ + LANE:Z_AQ + 2 * LANE]])
    o4 = _softmax_pv([_dot_nt(q4, z_ref[0, :, Z_AK:Z_AV])], [z_ref[0, :, Z_AV:Z_DQ]],
                     sink=_sink_column(sink_ref, layer, m))
    for pair in range(2):
        oa_ref[0, :, pair * LANE:(pair + 1) * LANE] = (_unstack_pair(o4, pair, m)
                                                      * gate(Z_AG + pair * LANE)).astype(BF16)
    kv = _dot(z_ref[0, :, Z_CKV:Z_KPE], wkv_ref[...])
    keys = _mla_keys(kv, z_ref[0, :, Z_KPE:Z_QM].astype(F32))
    kvb = kv.astype(BF16)
    _mla_heads(lambda h: z_ref[0, :, Z_QM + LANE * h:Z_QM + LANE * (h + 1)], lambda h: keys[h],
               lambda h: kvb[:, LANE * h:LANE * (h + 1)], lambda p: gate(Z_BG + p * LANE), ob_ref)
    for pair in range(NAT_HEADS // 2):
        c = pair * LANE
        q2 = _stack_heads([z_ref[0, :, Z_DQ + c:Z_DQ + c + LANE]])
        o2 = _softmax_pv([_dot_nt(q2, z_ref[0, :, Z_DK + c:Z_DK + c + LANE])], [z_ref[0, :, Z_DV + c:Z_DV + c + LANE]])
        od_ref[0, :, c:c + LANE] = (_unstack_pair(o2, 0, m) * gate(Z_DG + c)).astype(BF16)


def _ctx_attn_call(z, sink, wkv, layer):
    out = jax.ShapeDtypeStruct((BATCH, SEQ, BRANCH_W), BF16)
    ospec = pl.BlockSpec((1, SEQ, BRANCH_W), lambda b: (b, 0, 0))
    return pl.pallas_call(
        functools.partial(_ctx_attn_kernel, layer=layer),
        out_shape=(out, out, out),
        grid=(BATCH,),
        in_specs=[
            pl.BlockSpec(memory_space=pltpu.SMEM),
            pl.BlockSpec((1, SEQ, Z_W), lambda b: (b, 0, 0)),
            _layer_spec((MLA_KV_LORA, 512), layer),
        ],
        out_specs=(ospec, ospec, ospec),
        compiler_params=_cparams(("arbitrary",), 40),
        name="ctx_attn",
    )(sink, z, wkv)


def _win_attn_kernel(sink_ref, q_ref, kv_ref, g_ref, ck_ref, cv_ref, o_ref, ck_s, cv_s, *, layer):
    nb = DEC_SEQ // Q_BLOCK
    m = Q_BLOCK

    @pl.when(pl.program_id(1) == 0)
    def _():
        ck_s[...] = ck_ref[0, 0].astype(BF16)
        cv_s[...] = cv_ref[0, 0].astype(BF16)

    sink = _sink_column(sink_ref, layer, m)
    for qb in range(WIN_BPS):
        n = pl.program_id(1) * WIN_BPS + qb
        rows = slice(qb * m, (qb + 1) * m)
        blocks = []
        for off in (-1, 0, 1):
            start = pl.multiple_of(jnp.clip(n + off, 0, nb - 1) * m, m)
            blocks.append(kv_ref[0, pl.ds(start, m), :])
        kvw = jnp.concatenate(blocks, axis=0)
        shape = (A_HEADS * m, 3 * m)
        qpos = n * m + (lax.broadcasted_iota(jnp.int32, shape, 0) & (m - 1))
        kpos = (n - 1) * m + lax.broadcasted_iota(jnp.int32, shape, 1)
        mask = (jnp.abs(qpos - kpos) <= A_WINDOW) & (kpos >= 0) & (kpos < DEC_SEQ)
        q4 = _stack_heads([q_ref[0, rows, 0:LANE], q_ref[0, rows, LANE:2 * LANE]])
        s_win = jnp.where(mask, _dot_nt(q4, kvw[:, 0:LANE]), NEG)
        s_ctx = _dot_nt(q4, ck_s[...])
        o4 = _softmax_pv([s_win, s_ctx], [kvw[:, LANE:2 * LANE], cv_s[...]], sink=sink)
        for pair in range(2):
            g = g_ref[0, rows, pair * LANE:(pair + 1) * LANE].astype(F32)
            o_ref[0, rows, pair * LANE:(pair + 1) * LANE] = (_unstack_pair(o4, pair, m) * _silu(g)).astype(BF16)


def _win_attn_call(z, sink, cache_k, cache_v, layer):
    kvw = A_KV_HEADS * HEAD_DIM
    qrows = WIN_BPS * Q_BLOCK
    return pl.pallas_call(
        functools.partial(_win_attn_kernel, layer=layer),
        out_shape=jax.ShapeDtypeStruct((DEC_BATCH, DEC_SEQ, BRANCH_W), BF16),
        grid=(DEC_BATCH, DEC_SEQ // qrows),
        in_specs=[
            pl.BlockSpec(memory_space=pltpu.SMEM),
            pl.BlockSpec((1, qrows, 256), lambda b, n: (b, n, Z_AQ // 256)),
            pl.BlockSpec((1, DEC_SEQ, 256), lambda b, n: (b, 0, Z_AK // 256)),
            pl.BlockSpec((1, qrows, 256), lambda b, n: (b, n, Z_AG // 256)),
            pl.BlockSpec((1, 1, PAST_LEN, kvw), lambda b, n: (b, layer, 0, 0)),
            pl.BlockSpec((1, 1, PAST_LEN, kvw), lambda b, n: (b, layer, 0, 0)),
        ],
        out_specs=pl.BlockSpec((1, qrows, BRANCH_W), lambda b, n: (b, n, 0)),
        scratch_shapes=[pltpu.VMEM((PAST_LEN, kvw), BF16), pltpu.VMEM((PAST_LEN, kvw), BF16)],
        compiler_params=_cparams(("arbitrary", "arbitrary"), 40),
        name="lat_window_attn",
    )(sink, z, z, z, cache_k, cache_v)


def _nat_kernel(q_ref, k_ref, v_ref, g_ref, ck_ref, cv_ref, bias_ref, o_ref, ck_s, cv_s):
    nrows = DEC_SEQ // GRID_W
    kr = NAT_ROWS
    m = GRID_W

    @pl.when(pl.program_id(1) == 0)
    def _():
        ck_s[...] = ck_ref[0, 0].astype(BF16)
        cv_s[...] = cv_ref[0, 0].astype(BF16)

    shape = (2 * m, kr * GRID_W)
    qc = lax.broadcasted_iota(jnp.int32, shape, 0) & (m - 1)
    kc = lax.broadcasted_iota(jnp.int32, shape, 1) & (GRID_W - 1)
    rel = kc - jnp.clip(qc - NAT_COLS // 2, 0, GRID_W - NAT_COLS)
    mask = (rel >= 0) & (rel < NAT_COLS)
    for rr in range(NAT_RPS):
        r = pl.program_id(1) * NAT_RPS + rr
        rows = slice(rr * m, (rr + 1) * m)
        rs = jnp.clip(r - kr // 2, 0, nrows - kr)
        start = pl.multiple_of(rs * GRID_W, GRID_W)
        kl = k_ref[0, pl.ds(start, kr * GRID_W), :]
        vl = v_ref[0, pl.ds(start, kr * GRID_W), :]
        bidx = rs - r + (NAT_ROWS - 1)
        for pair in range(NAT_HEADS // 2):
            c = pair * LANE
            bias = jnp.concatenate(
                [jnp.concatenate([bias_ref[2 * pair + hh, bidx + 2 * j] for j in range(kr // 2)], axis=1)
                 for hh in range(2)], axis=0)
            q2 = _stack_heads([q_ref[0, rows, c:c + LANE]])
            s_lat = jnp.where(mask, _dot_nt(q2, kl[:, c:c + LANE]) + bias, NEG)
            s_ctx = _dot_nt(q2, ck_s[:, c:c + LANE])
            o2 = _softmax_pv([s_lat, s_ctx], [vl[:, c:c + LANE], cv_s[:, c:c + LANE]])
            g = g_ref[0, rows, c:c + LANE].astype(F32)
            o_ref[0, rows, c:c + LANE] = (_unstack_pair(o2, 0, m) * _silu(g)).astype(BF16)


def _nat_bias_table(rpb):
    nc = NAT_COLS - 1
    rep = lambda a, n: jnp.broadcast_to(a, a.shape[:-1] + (n,))
    ring = jnp.concatenate([rpb[..., nc:], rep(rpb[..., -1:], GRID_W - 1 - nc), rep(rpb[..., :1], GRID_W - nc),
                            rpb[..., :nc]], axis=-1)
    w = 2 * GRID_W - 1
    tq = jnp.tile(ring, (1, 1, GRID_W))[..., :GRID_W * w].reshape(NAT_HEADS, 2 * NAT_ROWS - 1, GRID_W, w)
    tq = tq[..., :GRID_W]
    nxt = jnp.concatenate([tq[:, 1:], jnp.zeros_like(tq[:, :1])], axis=1)
    return jnp.concatenate([tq, nxt], axis=-1).astype(F32)


def _nat_call(z, cache_k, cache_v, bias, layer):
    hw = NAT_HEADS * HEAD_DIM
    qrows = NAT_RPS * GRID_W
    return pl.pallas_call(
        _nat_kernel,
        out_shape=jax.ShapeDtypeStruct((DEC_BATCH, DEC_SEQ, BRANCH_W), BF16),
        grid=(DEC_BATCH, DEC_SEQ // qrows),
        in_specs=[
            pl.BlockSpec((1, qrows, 256), lambda b, r: (b, r, Z_DQ // 256)),
            pl.BlockSpec((1, DEC_SEQ, 256), lambda b, r: (b, 0, Z_DK // 256)),
            pl.BlockSpec((1, DEC_SEQ, 256), lambda b, r: (b, 0, Z_DV // 256)),
            pl.BlockSpec((1, qrows, 256), lambda b, r: (b, r, Z_DG // 256)),
            pl.BlockSpec((1, 1, PAST_LEN, hw), lambda b, r: (b, layer, 0, 0)),
            pl.BlockSpec((1, 1, PAST_LEN, hw), lambda b, r: (b, layer, 0, 0)),
            _layer_spec(bias.shape[1:], layer),
        ],
        out_specs=pl.BlockSpec((1, qrows, BRANCH_W), lambda b, r: (b, r, 0)),
        scratch_shapes=[pltpu.VMEM((PAST_LEN, hw), BF16), pltpu.VMEM((PAST_LEN, hw), BF16)],
        compiler_params=_cparams(("arbitrary", "arbitrary"), 40),
        name="lat_nat_attn",
    )(z, z, z, z, cache_k, cache_v, bias)


def _mla_kernel(q_ref, ckv_ref, kpe_ref, g_ref, cckv_ref, ckpe_ref, wkv_ref, o_ref, k_s, v_s):
    nlat = DEC_SEQ

    @pl.when(pl.program_id(1) == 0)
    def _():
        rows = 512

        def fill(r0, ckv, kpe_blk):
            kv = _dot(ckv, wkv_ref[...])
            for h, kh in enumerate(_mla_keys(kv, kpe_blk)):
                k_s[h, r0:r0 + rows, :] = kh
                v_s[h, r0:r0 + rows, :] = kv[:, h * LANE:(h + 1) * LANE].astype(BF16)

        for c in range(nlat // rows):
            fill(c * rows, ckv_ref[0, c * rows:(c + 1) * rows, :], kpe_ref[0, c * rows:(c + 1) * rows, :].astype(F32))
        ckpe = ckpe_ref[0, 0]
        ckpe_blk = jnp.concatenate([jnp.zeros((PAST_LEN, KPE_LANE), F32), ckpe,
                                    jnp.zeros((PAST_LEN, LANE - KPE_LANE - MLA_ROPE), F32)], axis=1)
        fill(nlat, cckv_ref[0, 0].astype(BF16), ckpe_blk)

    _mla_heads(lambda h: q_ref[0, :, LANE * h:LANE * (h + 1)], lambda h: k_s[h], lambda h: v_s[h],
               lambda p: _silu(g_ref[0, :, p * LANE:(p + 1) * LANE].astype(F32)), o_ref)


def _mla_call(z, cache_ckv, cache_kpe, wkv, layer):
    nkeys = DEC_SEQ + PAST_LEN
    return pl.pallas_call(
        _mla_kernel,
        out_shape=jax.ShapeDtypeStruct((DEC_BATCH, DEC_SEQ, BRANCH_W), BF16),
        grid=(DEC_BATCH, DEC_SEQ // TQ_MLA),
        in_specs=[
            pl.BlockSpec((1, TQ_MLA, MLA_HEADS * LANE), lambda b, t: (b, t, Z_QM // (MLA_HEADS * LANE))),
            pl.BlockSpec((1, DEC_SEQ, LANE), lambda b, t: (b, 0, Z_CKV // LANE)),
            pl.BlockSpec((1, DEC_SEQ, LANE), lambda b, t: (b, 0, Z_KPE // LANE)),
            pl.BlockSpec((1, TQ_MLA, 256), lambda b, t: (b, t, Z_BG // 256)),
            pl.BlockSpec((1, 1, PAST_LEN, MLA_KV_LORA), lambda b, t: (b, layer, 0, 0)),
            pl.BlockSpec((1, 1, PAST_LEN, MLA_ROPE), lambda b, t: (b, layer, 0, 0)),
            _layer_spec((MLA_KV_LORA, 512), layer),
        ],
        out_specs=pl.BlockSpec((1, TQ_MLA, BRANCH_W), lambda b, t: (b, t, 0)),
        scratch_shapes=[pltpu.VMEM((MLA_HEADS, nkeys, LANE), BF16), pltpu.VMEM((MLA_HEADS, nkeys, LANE), BF16)],
        compiler_params=_cparams(("arbitrary", "arbitrary"), 56),
        name="lat_mla_attn",
    )(z, z, z, z, cache_ckv, cache_kpe, wkv)


def _s5_mats(lam_re, lam_im, log_dt, b_re, b_im, c_re, c_im):
    t = S5_T
    hp = lax.Precision.HIGHEST
    lre = jnp.minimum(lam_re, -1e-4)
    lim = lam_im
    dt = jnp.exp(log_dt)[..., None]
    er, ei = lre * dt, lim * dt
    mag = jnp.exp(er)
    are, aim = mag * jnp.cos(ei), mag * jnp.sin(ei)
    den = lre * lre + lim * lim
    qre = ((are - 1.0) * lre + aim * lim) / den
    qim = (aim * lre - (are - 1.0) * lim) / den
    bbr = qre[..., None] * b_re - qim[..., None] * b_im
    bbi = qre[..., None] * b_im + qim[..., None] * b_re
    n = jnp.arange(t + 1, dtype=F32)[:, None, None, None]
    pmag = jnp.exp(er[None] * n)
    pr, pi = pmag * jnp.cos(ei[None] * n), pmag * jnp.sin(ei[None] * n)

    cbr = c_re[..., None] * bbr[:, :, None] - c_im[..., None] * bbi[:, :, None]
    cbi = c_re[..., None] * bbi[:, :, None] + c_im[..., None] * bbr[:, :, None]
    kk = (jnp.einsum('tdgp,dgcpk->dgktc', pr[:t], cbr, precision=hp)
          - jnp.einsum('tdgp,dgcpk->dgktc', pi[:t], cbi, precision=hp))
    ring = jnp.concatenate([kk[0][:, :, 0:1] + kk[1][:, :, 0:1], kk[0][:, :, 1:], jnp.zeros_like(kk[0][:, :, 0:1]),
                            kk[1][:, :, 1:][:, :, ::-1]], axis=2).reshape(S5_GROUPS, S5_CH, 2 * S5_TC)

    def state_cols(pw_r, pw_i, d):
        sr = pw_r[..., None] * bbr[d][None] - pw_i[..., None] * bbi[d][None]
        si = pw_r[..., None] * bbi[d][None] + pw_i[..., None] * bbr[d][None]
        f = lambda a: jnp.transpose(a, (1, 0, 3, 2)).reshape(S5_GROUPS, S5_TC, S5_STATE)
        return f(sr), f(si)

    sfr, sfi = state_cols(pr[:t, 0][::-1], pi[:t, 0][::-1], 0)
    sbr, sbi = state_cols(pr[:t, 1], pi[:t, 1], 1)
    m_state = jnp.concatenate([sfr, sbr, sfi, sbi], axis=-1)

    def out_rows(pw_r, pw_i, d):
        wr = c_re[d][None] * pw_r[:, :, None, :] - c_im[d][None] * pw_i[:, :, None, :]
        wi = c_re[d][None] * pw_i[:, :, None, :] + c_im[d][None] * pw_r[:, :, None, :]
        f = lambda a: jnp.transpose(a, (1, 3, 0, 2)).reshape(S5_GROUPS, S5_STATE, S5_TC)
        return f(wr), f(-wi)

    ofr, ofi = out_rows(pr[1:t + 1, 0], pi[1:t + 1, 0], 0)
    obr, obi = out_rows(pr[1:t + 1, 1][::-1], pi[1:t + 1, 1][::-1], 1)
    m_out = jnp.concatenate([ofr, obr, ofi, obi], axis=1)

    a_t = jnp.stack([jnp.concatenate([pr[t, 0], pr[t, 1]], axis=-1),
                     jnp.concatenate([pi[t, 0], pi[t, 1]], axis=-1)], axis=1)
    return ring, m_state.astype(BF16), m_out.astype(BF16), a_t


def _s5_kernel(uc_ref, ul_ref, ring_ref, ms_ref, mo_ref, at_ref, h0_ref, yc_ref, yl_ref, hend_ref,
               s_s, hf_s, hb_s, m_s):
    p = S5_STATE
    u = jnp.concatenate([uc_ref[0], ul_ref[0]], axis=0)
    s = _dot(u, ms_ref[0])
    s_s[0] = s[:, 0:2 * p]
    s_s[1] = s[:, 2 * p:4 * p]
    are = at_ref[0, 0:1, :]
    aim = at_ref[0, 1:2, :]
    fwd_lane = lax.broadcasted_iota(jnp.int32, (1, 2 * p), 1) < p

    for k in range(S5_CH):
        rk = jnp.broadcast_to(ring_ref[0, k:k + 1, :], (S5_T, 2 * S5_TC))
        rolled = pltpu.roll(rk, 0, 1, stride=S5_CH, stride_axis=0)
        for blk in range(S5_TC // LANE):
            m_s[blk, pl.ds(k, S5_T, stride=S5_CH), :] = rolled[:, blk * LANE:(blk + 1) * LANE]

    def scan(row0, nk, nb, hre, him):
        for st in range(nk):
            rf = pl.ds(row0 + st, nb, stride=nk)
            rb = pl.ds(row0 + nk - 1 - st, nb, stride=nk)
            hf_s[0, rf, :] = hre
            hf_s[1, rf, :] = him
            hb_s[0, rb, :] = hre
            hb_s[1, rb, :] = him
            sre = jnp.where(fwd_lane, s_s[0, rf, :], s_s[0, rb, :])
            sim = jnp.where(fwd_lane, s_s[1, rf, :], s_s[1, rb, :])
            hre, him = are * hre - aim * him + sre, are * him + aim * hre + sim
        return hre, him

    zero = jnp.zeros((BATCH, 2 * p), F32)
    hre, him = scan(0, S5_NK_CTX, BATCH, zero, zero)
    hend_ref[0] = jnp.concatenate([hre, him], axis=1)
    scan(S5_ROWS, S5_NK_LAT, DEC_BATCH, h0_ref[0, :, 0:2 * p], h0_ref[0, :, 2 * p:4 * p])
    hst = jnp.concatenate([jnp.where(fwd_lane, hf_s[0], hb_s[0]), jnp.where(fwd_lane, hf_s[1], hb_s[1])], axis=1)
    m_intra = jnp.concatenate([m_s[blk] for blk in range(S5_TC // LANE)], axis=1).astype(BF16)
    y = _dot(u, m_intra) + _dot(hst.astype(BF16), mo_ref[0])
    yc_ref[0] = y[0:S5_ROWS]
    yl_ref[0] = y[S5_ROWS:2 * S5_ROWS]


def _s5_call(ut_ctx, ut_lat, ring, m_state, m_out, a_t, h0, layer):
    g3 = lambda g: (g, 0, 0)
    lg = lambda shape: pl.BlockSpec((None, 1) + shape, lambda g: (layer, g, 0, 0))
    yshape = jax.ShapeDtypeStruct((S5_GROUPS, S5_ROWS, S5_TC), F32)
    uspec = pl.BlockSpec((1, S5_ROWS, S5_TC), g3)
    return pl.pallas_call(
        _s5_kernel,
        out_shape=(yshape, yshape, jax.ShapeDtypeStruct((S5_GROUPS, BATCH, 4 * S5_STATE), F32)),
        grid=(S5_GROUPS,),
        in_specs=[
            uspec, uspec,
            lg((S5_CH, 2 * S5_TC)),
            lg((S5_TC, 4 * S5_STATE)),
            lg((4 * S5_STATE, S5_TC)),
            lg((2, 2 * S5_STATE)),
            lg((DEC_BATCH, 4 * S5_STATE)),
        ],
        out_specs=(uspec, uspec, pl.BlockSpec((1, BATCH, 4 * S5_STATE), g3)),
        scratch_shapes=[pltpu.VMEM((2, 2 * S5_ROWS, LANE), F32)] * 3 + [pltpu.VMEM((S5_TC // LANE, S5_TC, LANE), F32)],
        compiler_params=_cparams(("arbitrary",), 32),
        name="s5_chunks",
    )(ut_ctx, ut_lat, ring, m_state, m_out, a_t, h0)


def _post_kernel(x_ref, mod_ref, ng_ref, ba_ref, bb_ref, bd_ref, y_ref, u_ref, cg_ref, sd_ref, wglu_ref,
                 wm_ref, wbr_ref, wo_ref, fg_ref, o_ref, ys_s, *, final):
    x = x_ref[0]
    hb = _ada_h(x, mod_ref, ng_ref).astype(BF16)
    for m in range(S5_T // 8):
        for hf in range(2):
            vs = _slot_transpose([y_ref[8 * hf + gp, :, m * LANE:(m + 1) * LANE] for gp in range(8)])
            for s in range(8):
                ys_s[hf, pl.ds(8 * m + s, TM_CHUNKS, stride=S5_T), :] = vs[s]
    yv = jnp.concatenate([ys_s[0], ys_s[1]], axis=1) + sd_ref[...] * u_ref[0].astype(F32)
    gel = 0.5 * yv * (1.0 + jnp.tanh(math.sqrt(2.0 / math.pi) * (yv + 0.044715 * (yv * yv * yv))))
    gl = _dot(gel.astype(BF16), wglu_ref[...])
    oc = gl[:, 0:BRANCH_W] * _sigmoid(gl[:, BRANCH_W:2 * BRANCH_W])
    bc = (oc * _silu(cg_ref[0].astype(F32))).astype(BF16)
    acc = None
    for k, br in enumerate((ba_ref[0], bb_ref[0], bc, bd_ref[0])):
        proj = _dot(br, wbr_ref[k])
        mg = _sigmoid(_dot(hb, wm_ref[:, k * D_MODEL:(k + 1) * D_MODEL]))
        acc = mg * proj if acc is None else acc + mg * proj
    y = _dot(acc.astype(BF16), wo_ref[...])
    out = x + mod_ref[0, :, 2 * D_MODEL:3 * D_MODEL] * y
    if final:
        out = _rms(out, fg_ref[...])
    o_ref[0] = out


def _post_call(x, mods, mod_row0, p, ba, bb, bd, y_s5, z, fg, final, layer, name):
    nb, length, _ = x.shape
    nt = length // TM
    tok = lambda b, t: (b, t, 0)
    br_spec = pl.BlockSpec((1, TM, BRANCH_W), tok)
    return pl.pallas_call(
        functools.partial(_post_kernel, final=final),
        out_shape=jax.ShapeDtypeStruct((nb, length, D_MODEL), F32),
        grid=(nb, nt),
        in_specs=[
            pl.BlockSpec((1, TM, D_MODEL), tok),
            _mod_spec(layer, mod_row0),
            _layer_spec((1, D_MODEL), layer),
            br_spec, br_spec, br_spec,
            pl.BlockSpec((S5_GROUPS, TM_CHUNKS, S5_TC), lambda b, t: (0, b * nt + t, 0)),
            pl.BlockSpec((1, TM, 256), lambda b, t: (b, t, Z_CU // 256)),
            pl.BlockSpec((1, TM, 256), lambda b, t: (b, t, Z_CG // 256)),
            _layer_spec((1, BRANCH_W), layer),
            _layer_spec((BRANCH_W, 2 * BRANCH_W), layer),
            _layer_spec((D_MODEL, N_BRANCH * D_MODEL), layer),
            _layer_spec((N_BRANCH, BRANCH_W, D_MODEL), layer),
            _layer_spec((D_MODEL, D_MODEL), layer),
            pl.BlockSpec((1, D_MODEL), lambda b, t: (0, 0)),
        ],
        out_specs=pl.BlockSpec((1, TM, D_MODEL), tok),
        scratch_shapes=[pltpu.VMEM((2, TM, LANE), F32)],
        compiler_params=_cparams(("arbitrary", "arbitrary"), 56),
        name=name,
    )(x, mods, p['ng'], ba, bb, bd, y_s5, z, z, p['s5d'], p['wglu'], p['wm'], p['wbr'], p['wo'], fg)


def _rope_tables():
    t = np.arange(DEC_SEQ)
    row = (t // GRID_W).astype(np.float64)
    col = (t % GRID_W).astype(np.float64)

    def pattern(half):
        inv = ROPE_BASE ** (-np.arange(half, dtype=np.float64) / half)
        zeros = np.zeros((DEC_SEQ, half))
        cs, s_up, s_lo = [], [], []
        for pos in (row, col):
            ang = pos[:, None] * inv[None, :]
            c, s = np.cos(ang), np.sin(ang)
            cs += [c, c]
            s_up += [zeros, s]
            s_lo += [-s, zeros]
        return [np.concatenate(parts, axis=1) for parts in (cs, s_up, s_lo)]

    tab_a = np.stack([np.tile(part, (1, LANE // HEAD_DIM)) for part in pattern(HEAD_DIM // 4)])
    ident = [np.ones, np.zeros, np.zeros]
    tab_m = np.stack([np.concatenate([fill((DEC_SEQ, KPE_LANE)), part,
                                      fill((DEC_SEQ, LANE - KPE_LANE - MLA_ROPE))], axis=1)
                      for fill, part in zip(ident, pattern(MLA_ROPE // 4))])
    return jnp.asarray(tab_a, F32), jnp.asarray(tab_m, F32)


def _a_head_cols(w):
    return jnp.concatenate([w[:, h * HEAD_DIM:(h + 1) * HEAD_DIM] for h in A_HEAD_ORDER], axis=1)


def _reorder_w_in(w):
    zpad = lambda n: jnp.zeros((D_MODEL, n), w.dtype)
    return jnp.concatenate([_a_head_cols(w[:, 0:256]), w[:, 256:512], w[:, 1952:2720], w[:, 768:1152],
                            zpad(KPE_LANE), w[:, 1152:1184], zpad(LANE - KPE_LANE - MLA_ROPE),
                            w[:, 1440:1696], _a_head_cols(w[:, 512:768]), w[:, 1184:1440], w[:, 1696:1952],
                            w[:, 2720:2976]], axis=1)


def _reorder_w_q_up(w):
    w = w.reshape(MLA_Q_LORA, MLA_HEADS, MLA_NOPE + MLA_ROPE)
    w = jnp.concatenate([w, jnp.zeros((MLA_Q_LORA, MLA_HEADS, LANE - MLA_NOPE - MLA_ROPE), w.dtype)], axis=-1)
    return w.reshape(MLA_Q_LORA, MLA_HEADS * LANE)


def _reorder_w_branch(w):
    wa = w[0].reshape(A_HEADS, HEAD_DIM, D_MODEL)
    wa = jnp.concatenate([wa[h] for h in A_HEAD_ORDER], axis=0)
    return jnp.concatenate([wa[None], w[1:]], axis=0)


def kernel(x_prompt, x_sample, cache_a_k, cache_a_v, cache_mla_ckv, cache_mla_kpe, cache_na_k, cache_na_v,
           state_s5_re, state_s5_im, c, c_ctx, w_mod, b_mod, norm_g, w_in, w_merge, a_sink,
           mla_q_norm, mla_w_q_up, mla_kv_norm, mla_w_kv_up, s5_lam_re, s5_lam_im, s5_log_dt,
           s5_b_re, s5_b_im, s5_c_re, s5_c_im, s5_d, s5_w_glu, na_rpb, w_branch, w_out, final_norm_g):
    n_ctx = BATCH * SEQ
    conds = jnp.concatenate([c, c_ctx[None, :], jnp.zeros((3, D_MODEL), F32)], axis=0)
    mods = _mod_call(conds, w_mod, b_mod).reshape(DEPTH, 8, 1, 3 * D_MODEL)
    tabs = _rope_tables()
    ck_a = cache_a_k.reshape(DEC_BATCH, DEPTH, PAST_LEN, A_KV_HEADS * HEAD_DIM)
    cv_a = cache_a_v.reshape(DEC_BATCH, DEPTH, PAST_LEN, A_KV_HEADS * HEAD_DIM)
    ck_n = cache_na_k.reshape(DEC_BATCH, DEPTH, PAST_LEN, NAT_HEADS * HEAD_DIM)
    cv_n = cache_na_v.reshape(DEC_BATCH, DEPTH, PAST_LEN, NAT_HEADS * HEAD_DIM)
    fg = final_norm_g.reshape(1, D_MODEL)
    caches = [jnp.zeros((BATCH, DEPTH, SEQ, w), F32) for w in (128, 128, MLA_KV_LORA, MLA_ROPE, 256, 256)]

    p = dict(
        ng=norm_g.reshape(DEPTH, 1, D_MODEL),
        w_raw=jax.vmap(_reorder_w_in)(w_in).astype(BF16),
        qnorm=mla_q_norm.reshape(DEPTH, 1, MLA_Q_LORA),
        wq=jax.vmap(_reorder_w_q_up)(mla_w_q_up).astype(BF16),
        kvnorm=mla_kv_norm.reshape(DEPTH, 1, MLA_KV_LORA),
        wkv=mla_w_kv_up.astype(BF16),
        wm=w_merge.astype(BF16),
        wbr=jax.vmap(_reorder_w_branch)(w_branch).astype(BF16),
        wo=w_out.astype(BF16),
        wglu=s5_w_glu.astype(BF16),
        s5d=s5_d.reshape(DEPTH, 1, BRANCH_W),
    )
    ring, m_state, m_out, a_t = jax.vmap(_s5_mats)(s5_lam_re, s5_lam_im, s5_log_dt, s5_b_re, s5_b_im, s5_c_re, s5_c_im)
    nat_bias = jax.vmap(_nat_bias_table)(na_rpb * LOG2E)
    h0 = jnp.concatenate([state_s5_re[:, :, 0], state_s5_re[:, :, 1], state_s5_im[:, :, 0], state_s5_im[:, :, 1]],
                         axis=-1)
    h0 = jnp.transpose(h0, (1, 2, 0, 3))

    yp = x_prompt.reshape(1, n_ctx, D_MODEL)
    ys = x_sample
    hends = []
    for i in range(DEPTH):
        res = _inproj_call(yp, mods, DEC_BATCH, p, None, caches, i, "inproj_ctx")
        z_ctx, ut_ctx, caches = res[0], res[1], list(res[2:])
        z_lat, ut_lat = _inproj_call(ys, mods, 0, p, tabs, None, i, "inproj_lat")
        zc = z_ctx.reshape(BATCH, SEQ, Z_W)

        y_ctx, y_lat, hend = _s5_call(ut_ctx, ut_lat, ring, m_state, m_out, a_t, h0, i)
        hends.append(hend)

        oa_c, ob_c, od_c = _ctx_attn_call(zc, a_sink, p['wkv'], i)
        oa_l = _win_attn_call(z_lat, a_sink, ck_a, cv_a, i)
        ob_l = _mla_call(z_lat, cache_mla_ckv, cache_mla_kpe, p['wkv'], i)
        od_l = _nat_call(z_lat, ck_n, cv_n, nat_bias, i)

        flat = lambda a: a.reshape(1, n_ctx, BRANCH_W)
        final = i == DEPTH - 1
        yp = _post_call(yp, mods, DEC_BATCH, p, flat(oa_c), flat(ob_c), flat(od_c), y_ctx, z_ctx, fg, final, i,
                        "post_ctx")
        ys = _post_call(ys, mods, 0, p, oa_l, ob_l, od_l, y_lat, z_lat, fg, final, i, "post_lat")

    hend = jnp.stack(hends).reshape(DEPTH, S5_GROUPS, BATCH, 2, 2, S5_STATE)
    hend = jnp.transpose(hend, (3, 2, 0, 4, 1, 5))
    ak, av, ckv, kpe, nk, nv = caches
    heads = lambda a, h: a.reshape(BATCH, DEPTH, SEQ, h, HEAD_DIM)
    return (yp.reshape(BATCH, SEQ, D_MODEL), ys, heads(ak, A_KV_HEADS), heads(av, A_KV_HEADS), ckv, kpe,
            heads(nk, NAT_HEADS), heads(nv, NAT_HEADS), hend[0], hend[1])
```

```python
import functools
import math

import numpy as np
import jax
import jax.numpy as jnp
from jax import lax
from jax.experimental import pallas as pl
from jax.experimental.pallas import tpu as pltpu

F32 = jnp.float32
BF16 = jnp.bfloat16

D_MODEL = 1024
BATCH = 32
SEQ = 256
DEPTH = 2
DEC_BATCH = 4
DEC_SEQ = 2048
PAST_LEN = 512
GRID_W = 64
HEAD_DIM = 64
BRANCH_W = 256
N_BRANCH = 4
Q_BLOCK = 128
A_HEADS = 4
A_KV_HEADS = 2
A_GROUP = A_HEADS // A_KV_HEADS
A_WINDOW = 128
MLA_HEADS = 4
MLA_Q_LORA = 256
MLA_KV_LORA = 128
MLA_NOPE = 64
MLA_ROPE = 32
MLA_V = 64
S5_CH = 16
S5_GROUPS = BRANCH_W // S5_CH
S5_STATE = 64
NAT_HEADS = 4
NAT_ROWS = 8
NAT_COLS = 16
ROPE_BASE = 10000.0
EPS = 1e-6
NEG = -1e30
LOG2E = 1.4426950408889634
Q_SCALE = HEAD_DIM ** -0.5 * LOG2E
MLA_Q_SCALE = (MLA_NOPE + MLA_ROPE) ** -0.5 * LOG2E

LANE = 128

R_AV, R_CQ, R_CKV, R_KPE, R_CU, R_G, R_W = 384, 1280, 1536, 1664, 1792, 2048, 3072
KPE_LANE = 64
A_HEAD_ORDER = (0, 2, 1, 3)
Z_AQ, Z_AK, Z_AV = 0, 256, 384
Z_DQ, Z_DK, Z_DV = 512, 768, 1024
Z_CKV, Z_KPE = 1280, 1408
Z_QM = 1536
Z_AG, Z_BG, Z_CG, Z_DG = 2048, 2304, 2560, 2816
Z_CU = 3072
Z_W = 3328

S5_T = 32
S5_TC = S5_T * S5_CH
S5_ROWS = 256
S5_NK_CTX = SEQ // S5_T
S5_NK_LAT = DEC_SEQ // S5_T

TM = 512
TM_CHUNKS = TM // S5_T
TQ_MLA = 1024
NAT_RPS = 4
NAT_UNION = NAT_ROWS + NAT_RPS
NAT_BIAS_PAD = NAT_RPS
WIN_BPS = 2


def _cparams(sem, vmem_mb):
    return pltpu.CompilerParams(dimension_semantics=sem, vmem_limit_bytes=vmem_mb * 1024 * 1024)


def _sigmoid(x):
    return 1.0 / (1.0 + jnp.exp(-x))


def _silu(x):
    return x * _sigmoid(x)


def _rms(x, g):
    return x * lax.rsqrt(jnp.mean(x * x, axis=-1, keepdims=True) + EPS) * g


def _dot(a, b):
    return jnp.dot(a, b, preferred_element_type=F32)


def _dot_nt(a, b):
    return lax.dot_general(a, b, (((1,), (1,)), ((), ())), preferred_element_type=F32)


def _slot_transpose(vs):
    lane = lax.broadcasted_iota(jnp.int32, vs[0].shape, 1)
    vs = list(vs)
    for d in (4, 2, 1):
        keep = (lane & (S5_CH * d)) == 0
        nxt = list(vs)
        for lo in range(8):
            if lo & d:
                continue
            hi = lo + d
            nxt[lo] = jnp.where(keep, vs[lo], pltpu.roll(vs[hi], S5_CH * d, 1))
            nxt[hi] = jnp.where(keep, pltpu.roll(vs[lo], LANE - S5_CH * d, 1), vs[hi])
        vs = nxt
    return vs


def _mod_kernel(c_ref, w_ref, b_ref, o_ref):
    s = _silu(c_ref[...])
    o_ref[0] = _dot(s.astype(BF16), w_ref[0].astype(BF16)) + b_ref[0]


def _mod_call(conds, w_mod, b_mod):
    nc = 512
    return pl.pallas_call(
        _mod_kernel,
        out_shape=jax.ShapeDtypeStruct((DEPTH, 8, 3 * D_MODEL), F32),
        grid=(DEPTH, 3 * D_MODEL // nc),
        in_specs=[
            pl.BlockSpec((8, D_MODEL), lambda i, j: (0, 0)),
            pl.BlockSpec((1, D_MODEL, nc), lambda i, j: (i, 0, j)),
            pl.BlockSpec((1, 1, nc), lambda i, j: (i, 0, j)),
        ],
        out_specs=pl.BlockSpec((1, 8, nc), lambda i, j: (i, 0, j)),
        compiler_params=_cparams(("arbitrary", "arbitrary"), 32),
        name="mod_rows",
    )(conds, w_mod, b_mod.reshape(DEPTH, 1, 3 * D_MODEL))


def _ada_h(x, mod_ref, ng_ref):
    shift = mod_ref[0, :, 0:D_MODEL]
    scale = mod_ref[0, :, D_MODEL:2 * D_MODEL]
    return _rms(x, ng_ref[...]) * (1.0 + scale) + shift


def _rope_block(xs, tab_ref, shift):
    return (xs * tab_ref[0] + pltpu.roll(xs, shift, 1) * tab_ref[1]
            + pltpu.roll(xs, LANE - shift, 1) * tab_ref[2])


N_CACHE = 6


def _inproj_kernel(*refs, rope):
    x_ref, mod_ref, ng_ref, w_ref, qn_ref, wq_ref, kvn_ref = refs[:7]
    if rope:
        ta_ref, tm_ref, z_ref, ut_ref, u_s = refs[7:]
    else:
        z_ref, ut_ref, ak_ref, av_ref, ckv_ref, kpe_ref, nk_ref, nv_ref, u_s = refs[7 + N_CACHE:]
    h = _ada_h(x_ref[0], mod_ref, ng_ref)
    raw = _dot(h.astype(BF16), w_ref[...])
    for j in range(R_AV // LANE):
        blk = raw[:, j * LANE:(j + 1) * LANE]
        if j < Z_AK // LANE:
            blk = blk * Q_SCALE
        if rope:
            blk = _rope_block(blk, ta_ref, 16)
        z_ref[0, :, j * LANE:(j + 1) * LANE] = blk.astype(BF16)
    z_ref[0, :, Z_AV:Z_DQ] = raw[:, Z_AV:Z_DQ].astype(BF16)
    z_ref[0, :, Z_DQ:Z_DK] = (raw[:, Z_DQ:Z_DK] * Q_SCALE).astype(BF16)
    z_ref[0, :, Z_DK:R_CQ] = raw[:, Z_DK:R_CQ].astype(BF16)
    qn = _rms(raw[:, R_CQ:R_CKV], qn_ref[...])
    q = _dot(qn.astype(BF16), wq_ref[...]) * MLA_Q_SCALE
    kp = raw[:, R_KPE:R_CU]
    ckv = _rms(raw[:, R_CKV:R_KPE], kvn_ref[...])
    for hh in range(MLA_HEADS):
        qh = q[:, hh * LANE:(hh + 1) * LANE]
        if rope:
            qh = _rope_block(qh, tm_ref, 8)
        z_ref[0, :, Z_QM + hh * LANE:Z_QM + (hh + 1) * LANE] = qh.astype(BF16)
    z_ref[0, :, Z_CKV:Z_KPE] = ckv.astype(BF16)
    z_ref[0, :, Z_KPE:Z_QM] = (_rope_block(kp, tm_ref, 8) if rope else kp).astype(BF16)
    z_ref[0, :, Z_AG:Z_CU] = raw[:, R_G:R_W].astype(BF16)
    z_ref[0, :, Z_CU:Z_W] = raw[:, R_CU:R_G].astype(BF16)
    if not rope:
        half = TM // 2
        for e in range(2):
            rows = slice(e * half, (e + 1) * half)
            ak_ref[e, 0] = raw[rows, Z_AK:Z_AV]
            av_ref[e, 0] = raw[rows, Z_AV:Z_DQ]
            nk_ref[e, 0] = raw[rows, Z_DK:Z_DV]
            nv_ref[e, 0] = raw[rows, Z_DV:R_CQ]
            ckv_ref[e, 0] = ckv[rows]
            kpe_ref[e, 0] = kp[rows, KPE_LANE:KPE_LANE + MLA_ROPE]
    for hf in range(2):
        u_s[hf] = raw[:, R_CU + hf * LANE:R_CU + (hf + 1) * LANE]
    for m in range(S5_T // 8):
        for hf in range(2):
            outs = _slot_transpose([u_s[hf, pl.ds(8 * m + s, TM_CHUNKS, stride=S5_T), :] for s in range(8)])
            for gp in range(8):
                ut_ref[8 * hf + gp, :, m * LANE:(m + 1) * LANE] = outs[gp].astype(BF16)


def _layer_spec(shape, layer):
    zeros = (0,) * len(shape)
    return pl.BlockSpec((None,) + tuple(shape), lambda *_: (layer,) + zeros)


def _mod_spec(layer, row0):
    return pl.BlockSpec((None, 1, 1, 3 * D_MODEL), lambda b, t: (layer, row0 + b, 0, 0))


def _inproj_call(x, mods, mod_row0, p, tabs, caches, layer, name):
    nb, length, _ = x.shape
    rope = tabs is not None
    nt = length // TM
    in_specs = [
        pl.BlockSpec((1, TM, D_MODEL), lambda b, t: (b, t, 0)),
        _mod_spec(layer, mod_row0),
        _layer_spec((1, D_MODEL), layer),
        _layer_spec((D_MODEL, R_W), layer),
        _layer_spec((1, MLA_Q_LORA), layer),
        _layer_spec((MLA_Q_LORA, MLA_HEADS * LANE), layer),
        _layer_spec((1, MLA_KV_LORA), layer),
    ]
    args = [x, mods, p['ng'], p['w_raw'], p['qnorm'], p['wq'], p['kvnorm']]
    out_shape = [jax.ShapeDtypeStruct((nb, length, Z_W), BF16),
                 jax.ShapeDtypeStruct((S5_GROUPS, S5_ROWS, S5_TC), BF16)]
    out_specs = [pl.BlockSpec((1, TM, Z_W), lambda b, t: (b, t, 0)),
                 pl.BlockSpec((S5_GROUPS, TM_CHUNKS, S5_TC), lambda b, t: (0, b * nt + t, 0))]
    aliases = {}
    if rope:
        in_specs += [pl.BlockSpec((3, TM, LANE), lambda b, t: (0, t, 0))] * 2
        args += list(tabs)
    else:
        for k, cbuf in enumerate(caches):
            in_specs.append(pl.BlockSpec(memory_space=pl.ANY))
            args.append(cbuf)
            out_shape.append(jax.ShapeDtypeStruct(cbuf.shape, cbuf.dtype))
            out_specs.append(pl.BlockSpec((2, 1, SEQ, cbuf.shape[-1]), lambda b, t: (t, layer, 0, 0)))
            aliases[7 + k] = 2 + k
    return pl.pallas_call(
        functools.partial(_inproj_kernel, rope=rope),
        out_shape=tuple(out_shape),
        grid=(nb, nt),
        in_specs=in_specs,
        out_specs=tuple(out_specs),
        scratch_shapes=[pltpu.VMEM((2, TM, LANE), F32)],
        input_output_aliases=aliases,
        compiler_params=_cparams(("arbitrary", "arbitrary"), 56),
        name=name,
    )(*args)


def _softmax_pv(s_list, v_list, sink=None):
    m = jnp.max(s_list[0], axis=-1, keepdims=True)
    for s in s_list[1:]:
        m = jnp.maximum(m, jnp.max(s, axis=-1, keepdims=True))
    if sink is not None:
        m = jnp.maximum(m, sink)
    den = None
    o = None
    for s, v in zip(s_list, v_list):
        e = jnp.exp2(s - m)
        d = jnp.sum(e, axis=-1, keepdims=True)
        pv = _dot(e.astype(BF16), v)
        den = d if den is None else den + d
        o = pv if o is None else o + pv
    if sink is not None:
        den = den + jnp.exp2(sink - m)
    return o / den


def _low_lanes(shape):
    return lax.broadcasted_iota(jnp.int32, shape, 1) < HEAD_DIM


def _stack_heads(blocks):
    lo = _low_lanes(blocks[0].shape)
    zero = jnp.zeros_like(blocks[0])
    parts = []
    for b in blocks:
        parts += [jnp.where(lo, b, zero), jnp.where(lo, zero, b)]
    return jnp.concatenate(parts, axis=0)


def _unstack_pair(o, idx, m):
    return jnp.where(_low_lanes((m, LANE)), o[2 * idx * m:(2 * idx + 1) * m], o[(2 * idx + 1) * m:(2 * idx + 2) * m])


def _sink_column(sink_ref, layer, m):
    blk = lax.broadcasted_iota(jnp.int32, (A_HEADS * m, 1), 0) // m
    col = jnp.full((A_HEADS * m, 1), sink_ref[layer, A_HEAD_ORDER[-1]] * LOG2E, F32)
    for i in range(A_HEADS - 1):
        col = jnp.where(blk == i, sink_ref[layer, A_HEAD_ORDER[i]] * LOG2E, col)
    return col


def _mla_keys(kv, kpe_blk):
    lo = _low_lanes(kpe_blk.shape)
    return [jnp.where(lo, kv[:, h * LANE:(h + 1) * LANE], kpe_blk).astype(BF16) for h in range(MLA_HEADS)]


def _mla_heads(q_of, k_of, v_of, gate_of, out_ref):
    for pair in range(MLA_HEADS // 2):
        outs = [_softmax_pv([_dot_nt(q_of(h), k_of(h))], [v_of(h)]) for h in (2 * pair, 2 * pair + 1)]
        o = jnp.where(_low_lanes(outs[0].shape), pltpu.roll(outs[0], MLA_V, 1), outs[1])
        out_ref[0, :, pair * LANE:(pair + 1) * LANE] = (o * gate_of(pair)).astype(BF16)


def _ctx_attn_kernel(sink_ref, z_ref, wkv_ref, oa_ref, ob_ref, od_ref, *, layer):
    m = SEQ
    gate = lambda c0: _silu(z_ref[0, :, c0:c0 + LANE].astype(F32))
    q4 = _stack_heads([z_ref[0, :, Z_AQ:Z_AQ + LANE], z_ref[0, :, Z_AQ + LANE:Z_AQ + 2 * LANE]])
    o4 = _softmax_pv([_dot_nt(q4, z_ref[0, :, Z_AK:Z_AV])], [z_ref[0, :, Z_AV:Z_DQ]],
                     sink=_sink_column(sink_ref, layer, m))
    for pair in range(2):
        oa_ref[0, :, pair * LANE:(pair + 1) * LANE] = (_unstack_pair(o4, pair, m)
                                                      * gate(Z_AG + pair * LANE)).astype(BF16)
    kv = _dot(z_ref[0, :, Z_CKV:Z_KPE], wkv_ref[...])
    keys = _mla_keys(kv, z_ref[0, :, Z_KPE:Z_QM].astype(F32))
    kvb = kv.astype(BF16)
    _mla_heads(lambda h: z_ref[0, :, Z_QM + LANE * h:Z_QM + LANE * (h + 1)], lambda h: keys[h],
               lambda h: kvb[:, LANE * h:LANE * (h + 1)], lambda p: gate(Z_BG + p * LANE), ob_ref)
    for pair in range(NAT_HEADS // 2):
        c = pair * LANE
        q2 = _stack_heads([z_ref[0, :, Z_DQ + c:Z_DQ + c + LANE]])
        o2 = _softmax_pv([_dot_nt(q2, z_ref[0, :, Z_DK + c:Z_DK + c + LANE])], [z_ref[0, :, Z_DV + c:Z_DV + c + LANE]])
        od_ref[0, :, c:c + LANE] = (_unstack_pair(o2, 0, m) * gate(Z_DG + c)).astype(BF16)


def _ctx_attn_call(z, sink, wkv, layer):
    out = jax.ShapeDtypeStruct((BATCH, SEQ, BRANCH_W), BF16)
    ospec = pl.BlockSpec((1, SEQ, BRANCH_W), lambda b: (b, 0, 0))
    return pl.pallas_call(
        functools.partial(_ctx_attn_kernel, layer=layer),
        out_shape=(out, out, out),
        grid=(BATCH,),
        in_specs=[
            pl.BlockSpec(memory_space=pltpu.SMEM),
            pl.BlockSpec((1, SEQ, Z_W), lambda b: (b, 0, 0)),
            _layer_spec((MLA_KV_LORA, 512), layer),
        ],
        out_specs=(ospec, ospec, ospec),
        compiler_params=_cparams(("arbitrary",), 40),
        name="ctx_attn",
    )(sink, z, wkv)


def _win_attn_kernel(sink_ref, q_ref, kv_ref, g_ref, ck_ref, cv_ref, o_ref, ck_s, cv_s, *, layer):
    nb = DEC_SEQ // Q_BLOCK
    m = Q_BLOCK

    @pl.when(pl.program_id(1) == 0)
    def _():
        ck_s[...] = ck_ref[0, 0].astype(BF16)
        cv_s[...] = cv_ref[0, 0].astype(BF16)

    sink = _sink_column(sink_ref, layer, m)
    for qb in range(WIN_BPS):
        n = pl.program_id(1) * WIN_BPS + qb
        rows = slice(qb * m, (qb + 1) * m)
        blocks = []
        for off in (-1, 0, 1):
            start = pl.multiple_of(jnp.clip(n + off, 0, nb - 1) * m, m)
            blocks.append(kv_ref[0, pl.ds(start, m), :])
        kvw = jnp.concatenate(blocks, axis=0)
        shape = (A_HEADS * m, 3 * m)
        qpos = n * m + (lax.broadcasted_iota(jnp.int32, shape, 0) & (m - 1))
        kpos = (n - 1) * m + lax.broadcasted_iota(jnp.int32, shape, 1)
        mask = (jnp.abs(qpos - kpos) <= A_WINDOW) & (kpos >= 0) & (kpos < DEC_SEQ)
        q4 = _stack_heads([q_ref[0, rows, 0:LANE], q_ref[0, rows, LANE:2 * LANE]])
        s_win = jnp.where(mask, _dot_nt(q4, kvw[:, 0:LANE]), NEG)
        s_ctx = _dot_nt(q4, ck_s[...])
        o4 = _softmax_pv([s_win, s_ctx], [kvw[:, LANE:2 * LANE], cv_s[...]], sink=sink)
        for pair in range(2):
            g = g_ref[0, rows, pair * LANE:(pair + 1) * LANE].astype(F32)
            o_ref[0, rows, pair * LANE:(pair + 1) * LANE] = (_unstack_pair(o4, pair, m) * _silu(g)).astype(BF16)


def _win_attn_call(z, sink, cache_k, cache_v, layer):
    kvw = A_KV_HEADS * HEAD_DIM
    qrows = WIN_BPS * Q_BLOCK
    return pl.pallas_call(
        functools.partial(_win_attn_kernel, layer=layer),
        out_shape=jax.ShapeDtypeStruct((DEC_BATCH, DEC_SEQ, BRANCH_W), BF16),
        grid=(DEC_BATCH, DEC_SEQ // qrows),
        in_specs=[
            pl.BlockSpec(memory_space=pltpu.SMEM),
            pl.BlockSpec((1, qrows, 256), lambda b, n: (b, n, Z_AQ // 256)),
            pl.BlockSpec((1, DEC_SEQ, 256), lambda b, n: (b, 0, Z_AK // 256)),
            pl.BlockSpec((1, qrows, 256), lambda b, n: (b, n, Z_AG // 256)),
            pl.BlockSpec((1, 1, PAST_LEN, kvw), lambda b, n: (b, layer, 0, 0)),
            pl.BlockSpec((1, 1, PAST_LEN, kvw), lambda b, n: (b, layer, 0, 0)),
        ],
        out_specs=pl.BlockSpec((1, qrows, BRANCH_W), lambda b, n: (b, n, 0)),
        scratch_shapes=[pltpu.VMEM((PAST_LEN, kvw), BF16), pltpu.VMEM((PAST_LEN, kvw), BF16)],
        compiler_params=_cparams(("arbitrary", "arbitrary"), 40),
        name="lat_window_attn",
    )(sink, z, z, z, cache_k, cache_v)


def _nat_kernel(q_ref, k_ref, v_ref, g_ref, ck_ref, cv_ref, bias_ref, o_ref, ck_s, cv_s):
    nrows = DEC_SEQ // GRID_W
    kr = NAT_ROWS
    m = NAT_RPS * GRID_W
    step = pl.program_id(1)

    @pl.when(step == 0)
    def _():
        ck_s[...] = ck_ref[0, 0].astype(BF16)
        cv_s[...] = cv_ref[0, 0].astype(BF16)

    ws = jnp.clip(step * NAT_RPS - kr // 2, 0, nrows - NAT_UNION)
    start = pl.multiple_of(ws * GRID_W, GRID_W)
    kl = k_ref[0, pl.ds(start, NAT_UNION * GRID_W), :]
    vl = v_ref[0, pl.ds(start, NAT_UNION * GRID_W), :]
    key_row = lax.broadcasted_iota(jnp.int32, (1, LANE), 1) // GRID_W
    for pair in range(NAT_HEADS // 2):
        c = pair * LANE
        blocks = []
        for hh in range(2):
            for rr in range(NAT_RPS):
                r = step * NAT_RPS + rr
                lo = jnp.clip(r - kr // 2, 0, nrows - kr) - ws
                ro = ws - r + (NAT_ROWS - 1) + NAT_BIAS_PAD
                row = []
                for t in range(NAT_UNION // 2):
                    kj = key_row + 2 * t
                    row.append(jnp.where((kj >= lo) & (kj < lo + kr), bias_ref[2 * pair + hh, ro + 2 * t], NEG))
                blocks.append(jnp.concatenate(row, axis=1))
        bias = jnp.concatenate(blocks, axis=0)
        q2 = _stack_heads([q_ref[0, :, c:c + LANE]])
        s_lat = _dot_nt(q2, kl[:, c:c + LANE]) + bias
        s_ctx = _dot_nt(q2, ck_s[:, c:c + LANE])
        o2 = _softmax_pv([s_lat, s_ctx], [vl[:, c:c + LANE], cv_s[:, c:c + LANE]])
        g = g_ref[0, :, c:c + LANE].astype(F32)
        o_ref[0, :, c:c + LANE] = (_unstack_pair(o2, 0, m) * _silu(g)).astype(BF16)


def _nat_bias_table(rpb):
    nc = NAT_COLS - 1
    rep = lambda a, n: jnp.broadcast_to(a, a.shape[:-1] + (n,))
    ring = jnp.concatenate([rpb[..., nc:], rep(rpb[..., -1:], GRID_W - 1 - nc), rep(rpb[..., :1], GRID_W - nc),
                            rpb[..., :nc]], axis=-1)
    w = 2 * GRID_W - 1
    tq = jnp.tile(ring, (1, 1, GRID_W))[..., :GRID_W * w].reshape(NAT_HEADS, 2 * NAT_ROWS - 1, GRID_W, w)
    cidx = np.arange(GRID_W)
    rel = cidx[None, :] - np.clip(cidx - NAT_COLS // 2, 0, GRID_W - NAT_COLS)[:, None]
    tq = jnp.where(jnp.asarray((rel >= 0) & (rel < NAT_COLS)), tq[..., :GRID_W], NEG)
    neg = lambda n: jnp.full((NAT_HEADS, n, GRID_W, GRID_W), NEG, F32)
    tq = jnp.concatenate([neg(NAT_BIAS_PAD), tq, neg(NAT_BIAS_PAD + 1)], axis=1)
    return jnp.concatenate([tq[:, :-1], tq[:, 1:]], axis=-1).astype(F32)


def _nat_call(z, cache_k, cache_v, bias, layer):
    hw = NAT_HEADS * HEAD_DIM
    qrows = NAT_RPS * GRID_W
    return pl.pallas_call(
        _nat_kernel,
        out_shape=jax.ShapeDtypeStruct((DEC_BATCH, DEC_SEQ, BRANCH_W), BF16),
        grid=(DEC_BATCH, DEC_SEQ // qrows),
        in_specs=[
            pl.BlockSpec((1, qrows, 256), lambda b, r: (b, r, Z_DQ // 256)),
            pl.BlockSpec((1, DEC_SEQ, 256), lambda b, r: (b, 0, Z_DK // 256)),
            pl.BlockSpec((1, DEC_SEQ, 256), lambda b, r: (b, 0, Z_DV // 256)),
            pl.BlockSpec((1, qrows, 256), lambda b, r: (b, r, Z_DG // 256)),
            pl.BlockSpec((1, 1, PAST_LEN, hw), lambda b, r: (b, layer, 0, 0)),
            pl.BlockSpec((1, 1, PAST_LEN, hw), lambda b, r: (b, layer, 0, 0)),
            _layer_spec(bias.shape[1:], layer),
        ],
        out_specs=pl.BlockSpec((1, qrows, BRANCH_W), lambda b, r: (b, r, 0)),
        scratch_shapes=[pltpu.VMEM((PAST_LEN, hw), BF16), pltpu.VMEM((PAST_LEN, hw), BF16)],
        compiler_params=_cparams(("arbitrary", "arbitrary"), 40),
        name="lat_nat_attn",
    )(z, z, z, z, cache_k, cache_v, bias)


def _mla_kernel(q_ref, ckv_ref, kpe_ref, g_ref, cckv_ref, ckpe_ref, wkv_ref, o_ref, k_s, v_s):
    nlat = DEC_SEQ

    @pl.when(pl.program_id(1) == 0)
    def _():
        rows = 512

        def fill(r0, ckv, kpe_blk):
            kv = _dot(ckv, wkv_ref[...])
            for h, kh in enumerate(_mla_keys(kv, kpe_blk)):
                k_s[h, r0:r0 + rows, :] = kh
                v_s[h, r0:r0 + rows, :] = kv[:, h * LANE:(h + 1) * LANE].astype(BF16)

        for c in range(nlat // rows):
            fill(c * rows, ckv_ref[0, c * rows:(c + 1) * rows, :], kpe_ref[0, c * rows:(c + 1) * rows, :].astype(F32))
        ckpe = ckpe_ref[0, 0]
        ckpe_blk = jnp.concatenate([jnp.zeros((PAST_LEN, KPE_LANE), F32), ckpe,
                                    jnp.zeros((PAST_LEN, LANE - KPE_LANE - MLA_ROPE), F32)], axis=1)
        fill(nlat, cckv_ref[0, 0].astype(BF16), ckpe_blk)

    _mla_heads(lambda h: q_ref[0, :, LANE * h:LANE * (h + 1)], lambda h: k_s[h], lambda h: v_s[h],
               lambda p: _silu(g_ref[0, :, p * LANE:(p + 1) * LANE].astype(F32)), o_ref)


def _mla_call(z, cache_ckv, cache_kpe, wkv, layer):
    nkeys = DEC_SEQ + PAST_LEN
    return pl.pallas_call(
        _mla_kernel,
        out_shape=jax.ShapeDtypeStruct((DEC_BATCH, DEC_SEQ, BRANCH_W), BF16),
        grid=(DEC_BATCH, DEC_SEQ // TQ_MLA),
        in_specs=[
            pl.BlockSpec((1, TQ_MLA, MLA_HEADS * LANE), lambda b, t: (b, t, Z_QM // (MLA_HEADS * LANE))),
            pl.BlockSpec((1, DEC_SEQ, LANE), lambda b, t: (b, 0, Z_CKV // LANE)),
            pl.BlockSpec((1, DEC_SEQ, LANE), lambda b, t: (b, 0, Z_KPE // LANE)),
            pl.BlockSpec((1, TQ_MLA, 256), lambda b, t: (b, t, Z_BG // 256)),
            pl.BlockSpec((1, 1, PAST_LEN, MLA_KV_LORA), lambda b, t: (b, layer, 0, 0)),
            pl.BlockSpec((1, 1, PAST_LEN, MLA_ROPE), lambda b, t: (b, layer, 0, 0)),
            _layer_spec((MLA_KV_LORA, 512), layer),
        ],
        out_specs=pl.BlockSpec((1, TQ_MLA, BRANCH_W), lambda b, t: (b, t, 0)),
        scratch_shapes=[pltpu.VMEM((MLA_HEADS, nkeys, LANE), BF16), pltpu.VMEM((MLA_HEADS, nkeys, LANE), BF16)],
        compiler_params=_cparams(("arbitrary", "arbitrary"), 56),
        name="lat_mla_attn",
    )(z, z, z, z, cache_ckv, cache_kpe, wkv)


def _s5_mats(lam_re, lam_im, log_dt, b_re, b_im, c_re, c_im):
    t = S5_T
    hp = lax.Precision.HIGHEST
    lre = jnp.minimum(lam_re, -1e-4)
    lim = lam_im
    dt = jnp.exp(log_dt)[..., None]
    er, ei = lre * dt, lim * dt
    mag = jnp.exp(er)
    are, aim = mag * jnp.cos(ei), mag * jnp.sin(ei)
    den = lre * lre + lim * lim
    qre = ((are - 1.0) * lre + aim * lim) / den
    qim = (aim * lre - (are - 1.0) * lim) / den
    bbr = qre[..., None] * b_re - qim[..., None] * b_im
    bbi = qre[..., None] * b_im + qim[..., None] * b_re
    n = jnp.arange(t + 1, dtype=F32)[:, None, None, None]
    pmag = jnp.exp(er[None] * n)
    pr, pi = pmag * jnp.cos(ei[None] * n), pmag * jnp.sin(ei[None] * n)

    cbr = c_re[..., None] * bbr[:, :, None] - c_im[..., None] * bbi[:, :, None]
    cbi = c_re[..., None] * bbi[:, :, None] + c_im[..., None] * bbr[:, :, None]
    kk = (jnp.einsum('tdgp,dgcpk->dgktc', pr[:t], cbr, precision=hp)
          - jnp.einsum('tdgp,dgcpk->dgktc', pi[:t], cbi, precision=hp))
    ring = jnp.concatenate([kk[0][:, :, 0:1] + kk[1][:, :, 0:1], kk[0][:, :, 1:], jnp.zeros_like(kk[0][:, :, 0:1]),
                            kk[1][:, :, 1:][:, :, ::-1]], axis=2).reshape(S5_GROUPS, S5_CH, 2 * S5_TC)

    def state_cols(pw_r, pw_i, d):
        sr = pw_r[..., None] * bbr[d][None] - pw_i[..., None] * bbi[d][None]
        si = pw_r[..., None] * bbi[d][None] + pw_i[..., None] * bbr[d][None]
        f = lambda a: jnp.transpose(a, (1, 0, 3, 2)).reshape(S5_GROUPS, S5_TC, S5_STATE)
        return f(sr), f(si)

    sfr, sfi = state_cols(pr[:t, 0][::-1], pi[:t, 0][::-1], 0)
    sbr, sbi = state_cols(pr[:t, 1], pi[:t, 1], 1)
    m_state = jnp.concatenate([sfr, sbr, sfi, sbi], axis=-1)

    def out_rows(pw_r, pw_i, d):
        wr = c_re[d][None] * pw_r[:, :, None, :] - c_im[d][None] * pw_i[:, :, None, :]
        wi = c_re[d][None] * pw_i[:, :, None, :] + c_im[d][None] * pw_r[:, :, None, :]
        f = lambda a: jnp.transpose(a, (1, 3, 0, 2)).reshape(S5_GROUPS, S5_STATE, S5_TC)
        return f(wr), f(-wi)

    ofr, ofi = out_rows(pr[1:t + 1, 0], pi[1:t + 1, 0], 0)
    obr, obi = out_rows(pr[1:t + 1, 1][::-1], pi[1:t + 1, 1][::-1], 1)
    m_out = jnp.concatenate([ofr, obr, ofi, obi], axis=1)

    a_t = jnp.stack([jnp.concatenate([pr[t, 0], pr[t, 1]], axis=-1),
                     jnp.concatenate([pi[t, 0], pi[t, 1]], axis=-1)], axis=1)
    return ring, m_state.astype(BF16), m_out.astype(BF16), a_t


def _s5_kernel(uc_ref, ul_ref, ring_ref, ms_ref, mo_ref, at_ref, h0_ref, yc_ref, yl_ref, hend_ref,
               s_s, hf_s, hb_s, m_s):
    p = S5_STATE
    u = jnp.concatenate([uc_ref[0], ul_ref[0]], axis=0)
    s = _dot(u, ms_ref[0])
    s_s[0] = s[:, 0:2 * p]
    s_s[1] = s[:, 2 * p:4 * p]
    are = at_ref[0, 0:1, :]
    aim = at_ref[0, 1:2, :]
    fwd_lane = lax.broadcasted_iota(jnp.int32, (1, 2 * p), 1) < p

    for k in range(S5_CH):
        rk = jnp.broadcast_to(ring_ref[0, k:k + 1, :], (S5_T, 2 * S5_TC))
        rolled = pltpu.roll(rk, 0, 1, stride=S5_CH, stride_axis=0)
        for blk in range(S5_TC // LANE):
            m_s[blk, pl.ds(k, S5_T, stride=S5_CH), :] = rolled[:, blk * LANE:(blk + 1) * LANE]

    def scan(row0, nk, nb, hre, him):
        for st in range(nk):
            rf = pl.ds(row0 + st, nb, stride=nk)
            rb = pl.ds(row0 + nk - 1 - st, nb, stride=nk)
            hf_s[0, rf, :] = hre
            hf_s[1, rf, :] = him
            hb_s[0, rb, :] = hre
            hb_s[1, rb, :] = him
            sre = jnp.where(fwd_lane, s_s[0, rf, :], s_s[0, rb, :])
            sim = jnp.where(fwd_lane, s_s[1, rf, :], s_s[1, rb, :])
            hre, him = are * hre - aim * him + sre, are * him + aim * hre + sim
        return hre, him

    zero = jnp.zeros((BATCH, 2 * p), F32)
    hre, him = scan(0, S5_NK_CTX, BATCH, zero, zero)
    hend_ref[0] = jnp.concatenate([hre, him], axis=1)
    scan(S5_ROWS, S5_NK_LAT, DEC_BATCH, h0_ref[0, :, 0:2 * p], h0_ref[0, :, 2 * p:4 * p])
    hst = jnp.concatenate([jnp.where(fwd_lane, hf_s[0], hb_s[0]), jnp.where(fwd_lane, hf_s[1], hb_s[1])], axis=1)
    m_intra = jnp.concatenate([m_s[blk] for blk in range(S5_TC // LANE)], axis=1).astype(BF16)
    y = _dot(u, m_intra) + _dot(hst.astype(BF16), mo_ref[0])
    yc_ref[0] = y[0:S5_ROWS]
    yl_ref[0] = y[S5_ROWS:2 * S5_ROWS]


def _s5_call(ut_ctx, ut_lat, ring, m_state, m_out, a_t, h0, layer):
    g3 = lambda g: (g, 0, 0)
    lg = lambda shape: pl.BlockSpec((None, 1) + shape, lambda g: (layer, g, 0, 0))
    yshape = jax.ShapeDtypeStruct((S5_GROUPS, S5_ROWS, S5_TC), F32)
    uspec = pl.BlockSpec((1, S5_ROWS, S5_TC), g3)
    return pl.pallas_call(
        _s5_kernel,
        out_shape=(yshape, yshape, jax.ShapeDtypeStruct((S5_GROUPS, BATCH, 4 * S5_STATE), F32)),
        grid=(S5_GROUPS,),
        in_specs=[
            uspec, uspec,
            lg((S5_CH, 2 * S5_TC)),
            lg((S5_TC, 4 * S5_STATE)),
            lg((4 * S5_STATE, S5_TC)),
            lg((2, 2 * S5_STATE)),
            lg((DEC_BATCH, 4 * S5_STATE)),
        ],
        out_specs=(uspec, uspec, pl.BlockSpec((1, BATCH, 4 * S5_STATE), g3)),
        scratch_shapes=[pltpu.VMEM((2, 2 * S5_ROWS, LANE), F32)] * 3 + [pltpu.VMEM((S5_TC // LANE, S5_TC, LANE), F32)],
        compiler_params=_cparams(("arbitrary",), 32),
        name="s5_chunks",
    )(ut_ctx, ut_lat, ring, m_state, m_out, a_t, h0)


def _post_kernel(x_ref, mod_ref, ng_ref, ba_ref, bb_ref, bd_ref, y_ref, u_ref, cg_ref, sd_ref, wglu_ref,
                 wm_ref, wbr_ref, wo_ref, fg_ref, o_ref, ys_s, *, final):
    x = x_ref[0]
    hb = _ada_h(x, mod_ref, ng_ref).astype(BF16)
    for m in range(S5_T // 8):
        for hf in range(2):
            vs = _slot_transpose([y_ref[8 * hf + gp, :, m * LANE:(m + 1) * LANE] for gp in range(8)])
            for s in range(8):
                ys_s[hf, pl.ds(8 * m + s, TM_CHUNKS, stride=S5_T), :] = vs[s]
    yv = jnp.concatenate([ys_s[0], ys_s[1]], axis=1) + sd_ref[...] * u_ref[0].astype(F32)
    gel = 0.5 * yv * (1.0 + jnp.tanh(math.sqrt(2.0 / math.pi) * (yv + 0.044715 * (yv * yv * yv))))
    gl = _dot(gel.astype(BF16), wglu_ref[...])
    oc = gl[:, 0:BRANCH_W] * _sigmoid(gl[:, BRANCH_W:2 * BRANCH_W])
    bc = (oc * _silu(cg_ref[0].astype(F32))).astype(BF16)
    acc = None
    for k, br in enumerate((ba_ref[0], bb_ref[0], bc, bd_ref[0])):
        proj = _dot(br, wbr_ref[k])
        mg = _sigmoid(_dot(hb, wm_ref[:, k * D_MODEL:(k + 1) * D_MODEL]))
        acc = mg * proj if acc is None else acc + mg * proj
    y = _dot(acc.astype(BF16), wo_ref[...])
    out = x + mod_ref[0, :, 2 * D_MODEL:3 * D_MODEL] * y
    if final:
        out = _rms(out, fg_ref[...])
    o_ref[0] = out


def _post_call(x, mods, mod_row0, p, ba, bb, bd, y_s5, z, fg, final, layer, name):
    nb, length, _ = x.shape
    nt = length // TM
    tok = lambda b, t: (b, t, 0)
    br_spec = pl.BlockSpec((1, TM, BRANCH_W), tok)
    return pl.pallas_call(
        functools.partial(_post_kernel, final=final),
        out_shape=jax.ShapeDtypeStruct((nb, length, D_MODEL), F32),
        grid=(nb, nt),
        in_specs=[
            pl.BlockSpec((1, TM, D_MODEL), tok),
            _mod_spec(layer, mod_row0),
            _layer_spec((1, D_MODEL), layer),
            br_spec, br_spec, br_spec,
            pl.BlockSpec((S5_GROUPS, TM_CHUNKS, S5_TC), lambda b, t: (0, b * nt + t, 0)),
            pl.BlockSpec((1, TM, 256), lambda b, t: (b, t, Z_CU // 256)),
            pl.BlockSpec((1, TM, 256), lambda b, t: (b, t, Z_CG // 256)),
            _layer_spec((1, BRANCH_W), layer),
            _layer_spec((BRANCH_W, 2 * BRANCH_W), layer),
            _layer_spec((D_MODEL, N_BRANCH * D_MODEL), layer),
            _layer_spec((N_BRANCH, BRANCH_W, D_MODEL), layer),
            _layer_spec((D_MODEL, D_MODEL), layer),
            pl.BlockSpec((1, D_MODEL), lambda b, t: (0, 0)),
        ],
        out_specs=pl.BlockSpec((1, TM, D_MODEL), tok),
        scratch_shapes=[pltpu.VMEM((2, TM, LANE), F32)],
        compiler_params=_cparams(("arbitrary", "arbitrary"), 56),
        name=name,
    )(x, mods, p['ng'], ba, bb, bd, y_s5, z, z, p['s5d'], p['wglu'], p['wm'], p['wbr'], p['wo'], fg)


def _rope_tables():
    t = np.arange(DEC_SEQ)
    row = (t // GRID_W).astype(np.float64)
    col = (t % GRID_W).astype(np.float64)

    def pattern(half):
        inv = ROPE_BASE ** (-np.arange(half, dtype=np.float64) / half)
        zeros = np.zeros((DEC_SEQ, half))
        cs, s_up, s_lo = [], [], []
        for pos in (row, col):
            ang = pos[:, None] * inv[None, :]
            c, s = np.cos(ang), np.sin(ang)
            cs += [c, c]
            s_up += [zeros, s]
            s_lo += [-s, zeros]
        return [np.concatenate(parts, axis=1) for parts in (cs, s_up, s_lo)]

    tab_a = np.stack([np.tile(part, (1, LANE // HEAD_DIM)) for part in pattern(HEAD_DIM // 4)])
    ident = [np.ones, np.zeros, np.zeros]
    tab_m = np.stack([np.concatenate([fill((DEC_SEQ, KPE_LANE)), part,
                                      fill((DEC_SEQ, LANE - KPE_LANE - MLA_ROPE))], axis=1)
                      for fill, part in zip(ident, pattern(MLA_ROPE // 4))])
    return jnp.asarray(tab_a, F32), jnp.asarray(tab_m, F32)


_W_IN_MOVES = tuple(
    [(HEAD_DIM * i, HEAD_DIM * h, HEAD_DIM) for i, h in enumerate(A_HEAD_ORDER)]
    + [(256, 256, 256), (512, 1952, 768), (R_CQ, 768, 384), (R_KPE + KPE_LANE, 1152, MLA_ROPE), (R_CU, 1440, 256)]
    + [(R_G + HEAD_DIM * i, 512 + HEAD_DIM * h, HEAD_DIM) for i, h in enumerate(A_HEAD_ORDER)]
    + [(R_G + 256, 1184, 256), (R_G + 512, 1696, 256), (R_G + 768, 2720, 256)])
W_IN_ROWS = 256


def _w_in_kernel(w_ref, o_ref):
    o_ref[0, :, R_KPE:R_CU] = jnp.zeros((W_IN_ROWS, LANE), BF16)
    for dst, src, width in _W_IN_MOVES:
        o_ref[0, :, dst:dst + width] = w_ref[0, :, src:src + width].astype(BF16)


def _w_in_call(w_in):
    return pl.pallas_call(
        _w_in_kernel,
        out_shape=jax.ShapeDtypeStruct((DEPTH, D_MODEL, R_W), BF16),
        grid=(DEPTH, D_MODEL // W_IN_ROWS),
        in_specs=[pl.BlockSpec((1, W_IN_ROWS, w_in.shape[-1]), lambda i, r: (i, r, 0))],
        out_specs=pl.BlockSpec((1, W_IN_ROWS, R_W), lambda i, r: (i, r, 0)),
        compiler_params=_cparams(("arbitrary", "arbitrary"), 32),
        name="w_in_layout",
    )(w_in)


def _cast_kernel(w_ref, o_ref):
    o_ref[...] = w_ref[...].astype(BF16)


def _cast_call(w):
    _, rows, cols = w.shape
    spec = pl.BlockSpec((1, W_IN_ROWS, cols), lambda i, r: (i, r, 0))
    return pl.pallas_call(
        _cast_kernel,
        out_shape=jax.ShapeDtypeStruct(w.shape, BF16),
        grid=(DEPTH, rows // W_IN_ROWS),
        in_specs=[spec],
        out_specs=spec,
        compiler_params=_cparams(("arbitrary", "arbitrary"), 32),
        name="w_cast",
    )(w)


def _reorder_w_q_up(w):
    w = w.reshape(MLA_Q_LORA, MLA_HEADS, MLA_NOPE + MLA_ROPE)
    w = jnp.concatenate([w, jnp.zeros((MLA_Q_LORA, MLA_HEADS, LANE - MLA_NOPE - MLA_ROPE), w.dtype)], axis=-1)
    return w.reshape(MLA_Q_LORA, MLA_HEADS * LANE)


def _reorder_w_branch(w):
    wa = w[0].reshape(A_HEADS, HEAD_DIM, D_MODEL)
    wa = jnp.concatenate([wa[h] for h in A_HEAD_ORDER], axis=0)
    return jnp.concatenate([wa[None], w[1:]], axis=0)


def kernel(x_prompt, x_sample, cache_a_k, cache_a_v, cache_mla_ckv, cache_mla_kpe, cache_na_k, cache_na_v,
           state_s5_re, state_s5_im, c, c_ctx, w_mod, b_mod, norm_g, w_in, w_merge, a_sink,
           mla_q_norm, mla_w_q_up, mla_kv_norm, mla_w_kv_up, s5_lam_re, s5_lam_im, s5_log_dt,
           s5_b_re, s5_b_im, s5_c_re, s5_c_im, s5_d, s5_w_glu, na_rpb, w_branch, w_out, final_norm_g):
    n_ctx = BATCH * SEQ
    conds = jnp.concatenate([c, c_ctx[None, :], jnp.zeros((3, D_MODEL), F32)], axis=0)
    mods = _mod_call(conds, w_mod, b_mod).reshape(DEPTH, 8, 1, 3 * D_MODEL)
    tabs = _rope_tables()
    ck_a = cache_a_k.reshape(DEC_BATCH, DEPTH, PAST_LEN, A_KV_HEADS * HEAD_DIM)
    cv_a = cache_a_v.reshape(DEC_BATCH, DEPTH, PAST_LEN, A_KV_HEADS * HEAD_DIM)
    ck_n = cache_na_k.reshape(DEC_BATCH, DEPTH, PAST_LEN, NAT_HEADS * HEAD_DIM)
    cv_n = cache_na_v.reshape(DEC_BATCH, DEPTH, PAST_LEN, NAT_HEADS * HEAD_DIM)
    fg = final_norm_g.reshape(1, D_MODEL)
    caches = [jnp.zeros((BATCH, DEPTH, SEQ, w), F32) for w in (128, 128, MLA_KV_LORA, MLA_ROPE, 256, 256)]

    p = dict(
        ng=norm_g.reshape(DEPTH, 1, D_MODEL),
        w_raw=_w_in_call(w_in),
        qnorm=mla_q_norm.reshape(DEPTH, 1, MLA_Q_LORA),
        wq=jax.vmap(_reorder_w_q_up)(mla_w_q_up).astype(BF16),
        kvnorm=mla_kv_norm.reshape(DEPTH, 1, MLA_KV_LORA),
        wkv=mla_w_kv_up.astype(BF16),
        wm=_cast_call(w_merge),
        wbr=jax.vmap(_reorder_w_branch)(w_branch).astype(BF16),
        wo=_cast_call(w_out),
        wglu=s5_w_glu.astype(BF16),
        s5d=s5_d.reshape(DEPTH, 1, BRANCH_W),
    )
    ring, m_state, m_out, a_t = jax.vmap(_s5_mats)(s5_lam_re, s5_lam_im, s5_log_dt, s5_b_re, s5_b_im, s5_c_re, s5_c_im)
    nat_bias = jax.vmap(_nat_bias_table)(na_rpb * LOG2E)
    h0 = jnp.concatenate([state_s5_re[:, :, 0], state_s5_re[:, :, 1], state_s5_im[:, :, 0], state_s5_im[:, :, 1]],
                         axis=-1)
    h0 = jnp.transpose(h0, (1, 2, 0, 3))

    yp = x_prompt.reshape(1, n_ctx, D_MODEL)
    ys = x_sample
    hends = []
    for i in range(DEPTH):
        res = _inproj_call(yp, mods, DEC_BATCH, p, None, caches, i, "inproj_ctx")
        z_ctx, ut_ctx, caches = res[0], res[1], list(res[2:])
        z_lat, ut_lat = _inproj_call(ys, mods, 0, p, tabs, None, i, "inproj_lat")
        zc = z_ctx.reshape(BATCH, SEQ, Z_W)

        y_ctx, y_lat, hend = _s5_call(ut_ctx, ut_lat, ring, m_state, m_out, a_t, h0, i)
        hends.append(hend)

        oa_c, ob_c, od_c = _ctx_attn_call(zc, a_sink, p['wkv'], i)
        oa_l = _win_attn_call(z_lat, a_sink, ck_a, cv_a, i)
        ob_l = _mla_call(z_lat, cache_mla_ckv, cache_mla_kpe, p['wkv'], i)
        od_l = _nat_call(z_lat, ck_n, cv_n, nat_bias, i)

        flat = lambda a: a.reshape(1, n_ctx, BRANCH_W)
        final = i == DEPTH - 1
        yp = _post_call(yp, mods, DEC_BATCH, p, flat(oa_c), flat(ob_c), flat(od_c), y_ctx, z_ctx, fg, final, i,
                        "post_ctx")
        ys = _post_call(ys, mods, 0, p, oa_l, ob_l, od_l, y_lat, z_lat, fg, final, i, "post_lat")

    hend = jnp.stack(hends).reshape(DEPTH, S5_GROUPS, BATCH, 2, 2, S5_STATE)
    hend = jnp.transpose(hend, (3, 2, 0, 4, 1, 5))
    ak, av, ckv, kpe, nk, nv = caches
    heads = lambda a, h: a.reshape(BATCH, DEPTH, SEQ, h, HEAD_DIM)
    return (yp.reshape(BATCH, SEQ, D_MODEL), ys, heads(ak, A_KV_HEADS), heads(av, A_KV_HEADS), ckv, kpe,
            heads(nk, NAT_HEADS), heads(nv, NAT_HEADS), hend[0], hend[1])
```

```python
import functools
import math

import numpy as np
import jax
import jax.numpy as jnp
from jax import lax
from jax.experimental import pallas as pl
from jax.experimental.pallas import tpu as pltpu

F32 = jnp.float32
BF16 = jnp.bfloat16

D_MODEL = 1024
BATCH = 32
SEQ = 256
DEPTH = 2
DEC_BATCH = 4
DEC_SEQ = 2048
PAST_LEN = 512
GRID_W = 64
HEAD_DIM = 64
BRANCH_W = 256
N_BRANCH = 4
Q_BLOCK = 128
A_HEADS = 4
A_KV_HEADS = 2
A_GROUP = A_HEADS // A_KV_HEADS
A_WINDOW = 128
MLA_HEADS = 4
MLA_Q_LORA = 256
MLA_KV_LORA = 128
MLA_NOPE = 64
MLA_ROPE = 32
MLA_V = 64
S5_CH = 16
S5_GROUPS = BRANCH_W // S5_CH
S5_STATE = 64
NAT_HEADS = 4
NAT_ROWS = 8
NAT_COLS = 16
ROPE_BASE = 10000.0
EPS = 1e-6
NEG = -1e30
LOG2E = 1.4426950408889634
Q_SCALE = HEAD_DIM ** -0.5 * LOG2E
MLA_Q_SCALE = (MLA_NOPE + MLA_ROPE) ** -0.5 * LOG2E

LANE = 128

R_AV, R_CQ, R_CKV, R_KPE, R_CU, R_G, R_W = 384, 1280, 1536, 1664, 1792, 2048, 3072
KPE_LANE = 64
A_HEAD_ORDER = (0, 2, 1, 3)
Z_AQ, Z_AK, Z_AV = 0, 256, 384
Z_DQ, Z_DK, Z_DV = 512, 768, 1024
Z_CKV, Z_KPE = 1280, 1408
Z_QM = 1536
Z_AG, Z_BG, Z_CG, Z_DG = 2048, 2304, 2560, 2816
Z_CU = 3072
Z_W = 3328

S5_T = 32
S5_TC = S5_T * S5_CH
S5_ROWS = 256
S5_NK_CTX = SEQ // S5_T
S5_NK_LAT = DEC_SEQ // S5_T

TM = 512
TM_CHUNKS = TM // S5_T
TQ_MLA = 512
NAT_RPS = 4
NAT_UNION = NAT_ROWS + NAT_RPS
NAT_BIAS_PAD = NAT_RPS
NAT_BIAS_ROWS = 2 * NAT_ROWS - 1 + 2 * NAT_BIAS_PAD
WIN_BPS = 2


def _cparams(sem, vmem_mb):
    return pltpu.CompilerParams(dimension_semantics=sem, vmem_limit_bytes=vmem_mb * 1024 * 1024)


def _sigmoid(x):
    return 1.0 / (1.0 + jnp.exp(-x))


def _silu(x):
    return x * _sigmoid(x)


def _rms(x, g):
    return x * lax.rsqrt(jnp.mean(x * x, axis=-1, keepdims=True) + EPS) * g


def _dot(a, b):
    return jnp.dot(a, b, preferred_element_type=F32)


def _dot_nt(a, b):
    return lax.dot_general(a, b, (((1,), (1,)), ((), ())), preferred_element_type=F32)


def _slot_transpose(vs):
    lane = lax.broadcasted_iota(jnp.int32, vs[0].shape, 1)
    vs = list(vs)
    for d in (4, 2, 1):
        keep = (lane & (S5_CH * d)) == 0
        nxt = list(vs)
        for lo in range(8):
            if lo & d:
                continue
            hi = lo + d
            nxt[lo] = jnp.where(keep, vs[lo], pltpu.roll(vs[hi], S5_CH * d, 1))
            nxt[hi] = jnp.where(keep, pltpu.roll(vs[lo], LANE - S5_CH * d, 1), vs[hi])
        vs = nxt
    return vs


def _mod_kernel(c_ref, w_ref, b_ref, o_ref):
    s = _silu(c_ref[...])
    o_ref[0] = _dot(s.astype(BF16), w_ref[0].astype(BF16)) + b_ref[0]


def _mod_call(conds, w_mod, b_mod):
    nc = 512
    return pl.pallas_call(
        _mod_kernel,
        out_shape=jax.ShapeDtypeStruct((DEPTH, 8, 3 * D_MODEL), F32),
        grid=(DEPTH, 3 * D_MODEL // nc),
        in_specs=[
            pl.BlockSpec((8, D_MODEL), lambda i, j: (0, 0)),
            pl.BlockSpec((1, D_MODEL, nc), lambda i, j: (i, 0, j)),
            pl.BlockSpec((1, 1, nc), lambda i, j: (i, 0, j)),
        ],
        out_specs=pl.BlockSpec((1, 8, nc), lambda i, j: (i, 0, j)),
        compiler_params=_cparams(("arbitrary", "arbitrary"), 32),
        name="mod_rows",
    )(conds, w_mod, b_mod.reshape(DEPTH, 1, 3 * D_MODEL))


def _ada_h(x, mod_ref, ng_ref):
    shift = mod_ref[0, :, 0:D_MODEL]
    scale = mod_ref[0, :, D_MODEL:2 * D_MODEL]
    return _rms(x, ng_ref[...]) * (1.0 + scale) + shift


def _rope_block(xs, tab_ref, shift):
    return (xs * tab_ref[0] + pltpu.roll(xs, shift, 1) * tab_ref[1]
            + pltpu.roll(xs, LANE - shift, 1) * tab_ref[2])


N_CACHE = 6


def _inproj_kernel(*refs, rope):
    x_ref, mod_ref, ng_ref, w_ref, qn_ref, wq_ref, kvn_ref = refs[:7]
    if rope:
        ta_ref, tm_ref, z_ref, ut_ref, u_s = refs[7:]
    else:
        z_ref, ut_ref, ak_ref, av_ref, ckv_ref, kpe_ref, nk_ref, nv_ref, u_s = refs[7 + N_CACHE:]
    h = _ada_h(x_ref[0], mod_ref, ng_ref)
    raw = _dot(h.astype(BF16), w_ref[...])
    for j in range(R_AV // LANE):
        blk = raw[:, j * LANE:(j + 1) * LANE]
        if j < Z_AK // LANE:
            blk = blk * Q_SCALE
        if rope:
            blk = _rope_block(blk, ta_ref, 16)
        z_ref[0, :, j * LANE:(j + 1) * LANE] = blk.astype(BF16)
    z_ref[0, :, Z_AV:Z_DQ] = raw[:, Z_AV:Z_DQ].astype(BF16)
    z_ref[0, :, Z_DQ:Z_DK] = (raw[:, Z_DQ:Z_DK] * Q_SCALE).astype(BF16)
    z_ref[0, :, Z_DK:R_CQ] = raw[:, Z_DK:R_CQ].astype(BF16)
    qn = _rms(raw[:, R_CQ:R_CKV], qn_ref[...])
    q = _dot(qn.astype(BF16), wq_ref[...]) * MLA_Q_SCALE
    kp = raw[:, R_KPE:R_CU]
    ckv = _rms(raw[:, R_CKV:R_KPE], kvn_ref[...])
    for hh in range(MLA_HEADS):
        qh = q[:, hh * LANE:(hh + 1) * LANE]
        if rope:
            qh = _rope_block(qh, tm_ref, 8)
        z_ref[0, :, Z_QM + hh * LANE:Z_QM + (hh + 1) * LANE] = qh.astype(BF16)
    z_ref[0, :, Z_CKV:Z_KPE] = ckv.astype(BF16)
    z_ref[0, :, Z_KPE:Z_QM] = (_rope_block(kp, tm_ref, 8) if rope else kp).astype(BF16)
    z_ref[0, :, Z_AG:Z_CU] = raw[:, R_G:R_W].astype(BF16)
    z_ref[0, :, Z_CU:Z_W] = raw[:, R_CU:R_G].astype(BF16)
    if not rope:
        half = TM // 2
        for e in range(2):
            rows = slice(e * half, (e + 1) * half)
            ak_ref[e, 0] = raw[rows, Z_AK:Z_AV]
            av_ref[e, 0] = raw[rows, Z_AV:Z_DQ]
            nk_ref[e, 0] = raw[rows, Z_DK:Z_DV]
            nv_ref[e, 0] = raw[rows, Z_DV:R_CQ]
            ckv_ref[e, 0] = ckv[rows]
            kpe_ref[e, 0] = kp[rows, KPE_LANE:KPE_LANE + MLA_ROPE]
    for hf in range(2):
        u_s[hf] = raw[:, R_CU + hf * LANE:R_CU + (hf + 1) * LANE]
    for m in range(S5_T // 8):
        for hf in range(2):
            outs = _slot_transpose([u_s[hf, pl.ds(8 * m + s, TM_CHUNKS, stride=S5_T), :] for s in range(8)])
            for gp in range(8):
                ut_ref[8 * hf + gp, :, m * LANE:(m + 1) * LANE] = outs[gp].astype(BF16)


def _layer_spec(shape, layer):
    zeros = (0,) * len(shape)
    return pl.BlockSpec((None,) + tuple(shape), lambda *_: (layer,) + zeros)


def _mod_spec(layer, row0):
    return pl.BlockSpec((None, 1, 1, 3 * D_MODEL), lambda b, t: (layer, row0 + b, 0, 0))


def _inproj_call(x, mods, mod_row0, p, tabs, caches, layer, name):
    nb, length, _ = x.shape
    rope = tabs is not None
    nt = length // TM
    in_specs = [
        pl.BlockSpec((1, TM, D_MODEL), lambda b, t: (b, t, 0)),
        _mod_spec(layer, mod_row0),
        _layer_spec((1, D_MODEL), layer),
        _layer_spec((D_MODEL, R_W), layer),
        _layer_spec((1, MLA_Q_LORA), layer),
        _layer_spec((MLA_Q_LORA, MLA_HEADS * LANE), layer),
        _layer_spec((1, MLA_KV_LORA), layer),
    ]
    args = [x, mods, p['ng'], p['w_raw'], p['qnorm'], p['wq'], p['kvnorm']]
    out_shape = [jax.ShapeDtypeStruct((nb, length, Z_W), BF16),
                 jax.ShapeDtypeStruct((S5_GROUPS, S5_ROWS, S5_TC), BF16)]
    out_specs = [pl.BlockSpec((1, TM, Z_W), lambda b, t: (b, t, 0)),
                 pl.BlockSpec((S5_GROUPS, TM_CHUNKS, S5_TC), lambda b, t: (0, b * nt + t, 0))]
    aliases = {}
    if rope:
        in_specs += [pl.BlockSpec((3, TM, LANE), lambda b, t: (0, t, 0))] * 2
        args += list(tabs)
    else:
        for k, cbuf in enumerate(caches):
            in_specs.append(pl.BlockSpec(memory_space=pl.ANY))
            args.append(cbuf)
            out_shape.append(jax.ShapeDtypeStruct(cbuf.shape, cbuf.dtype))
            out_specs.append(pl.BlockSpec((2, 1, SEQ, cbuf.shape[-1]), lambda b, t: (t, layer, 0, 0)))
            aliases[7 + k] = 2 + k
    return pl.pallas_call(
        functools.partial(_inproj_kernel, rope=rope),
        out_shape=tuple(out_shape),
        grid=(nb, nt),
        in_specs=in_specs,
        out_specs=tuple(out_specs),
        scratch_shapes=[pltpu.VMEM((2, TM, LANE), F32)],
        input_output_aliases=aliases,
        compiler_params=_cparams(("arbitrary", "arbitrary"), 56),
        name=name,
    )(*args)


def _softmax_pv(s_list, v_list, sink=None):
    m = jnp.max(s_list[0], axis=-1, keepdims=True)
    for s in s_list[1:]:
        m = jnp.maximum(m, jnp.max(s, axis=-1, keepdims=True))
    if sink is not None:
        m = jnp.maximum(m, sink)
    den = None
    o = None
    for s, v in zip(s_list, v_list):
        e = jnp.exp2(s - m)
        d = jnp.sum(e, axis=-1, keepdims=True)
        pv = _dot(e.astype(BF16), v)
        den = d if den is None else den + d
        o = pv if o is None else o + pv
    if sink is not None:
        den = den + jnp.exp2(sink - m)
    return o / den


def _low_lanes(shape):
    return lax.broadcasted_iota(jnp.int32, shape, 1) < HEAD_DIM


def _stack_heads(blocks):
    lo = _low_lanes(blocks[0].shape)
    zero = jnp.zeros_like(blocks[0])
    parts = []
    for b in blocks:
        parts += [jnp.where(lo, b, zero), jnp.where(lo, zero, b)]
    return jnp.concatenate(parts, axis=0)


def _unstack_pair(o, idx, m):
    return jnp.where(_low_lanes((m, LANE)), o[2 * idx * m:(2 * idx + 1) * m], o[(2 * idx + 1) * m:(2 * idx + 2) * m])


def _sink_column(sink_ref, layer, m):
    blk = lax.broadcasted_iota(jnp.int32, (A_HEADS * m, 1), 0) // m
    col = jnp.full((A_HEADS * m, 1), sink_ref[layer, A_HEAD_ORDER[-1]] * LOG2E, F32)
    for i in range(A_HEADS - 1):
        col = jnp.where(blk == i, sink_ref[layer, A_HEAD_ORDER[i]] * LOG2E, col)
    return col


def _mla_keys(kv, kpe_blk):
    lo = _low_lanes(kpe_blk.shape)
    return [jnp.where(lo, kv[:, h * LANE:(h + 1) * LANE], kpe_blk).astype(BF16) for h in range(MLA_HEADS)]


def _mla_heads(q_of, k_of, v_of, gate_of, out_ref):
    for pair in range(MLA_HEADS // 2):
        outs = [_softmax_pv([_dot_nt(q_of(h), k_of(h))], [v_of(h)]) for h in (2 * pair, 2 * pair + 1)]
        o = jnp.where(_low_lanes(outs[0].shape), pltpu.roll(outs[0], MLA_V, 1), outs[1])
        out_ref[0, :, pair * LANE:(pair + 1) * LANE] = (o * gate_of(pair)).astype(BF16)


def _ctx_attn_kernel(sink_ref, z_ref, wkv_ref, oa_ref, ob_ref, od_ref, *, layer):
    m = SEQ
    gate = lambda c0: _silu(z_ref[0, :, c0:c0 + LANE].astype(F32))
    q4 = _stack_heads([z_ref[0, :, Z_AQ:Z_AQ + LANE], z_ref[0, :, Z_AQ + LANE:Z_AQ + 2 * LANE]])
    o4 = _softmax_pv([_dot_nt(q4, z_ref[0, :, Z_AK:Z_AV])], [z_ref[0, :, Z_AV:Z_DQ]],
                     sink=_sink_column(sink_ref, layer, m))
    for pair in range(2):
        oa_ref[0, :, pair * LANE:(pair + 1) * LANE] = (_unstack_pair(o4, pair, m)
                                                      * gate(Z_AG + pair * LANE)).astype(BF16)
    kv = _dot(z_ref[0, :, Z_CKV:Z_KPE], wkv_ref[...])
    keys = _mla_keys(kv, z_ref[0, :, Z_KPE:Z_QM].astype(F32))
    kvb = kv.astype(BF16)
    _mla_heads(lambda h: z_ref[0, :, Z_QM + LANE * h:Z_QM + LANE * (h + 1)], lambda h: keys[h],
               lambda h: kvb[:, LANE * h:LANE * (h + 1)], lambda p: gate(Z_BG + p * LANE), ob_ref)
    for pair in range(NAT_HEADS // 2):
        c = pair * LANE
        q2 = _stack_heads([z_ref[0, :, Z_DQ + c:Z_DQ + c + LANE]])
        o2 = _softmax_pv([_dot_nt(q2, z_ref[0, :, Z_DK + c:Z_DK + c + LANE])], [z_ref[0, :, Z_DV + c:Z_DV + c + LANE]])
        od_ref[0, :, c:c + LANE] = (_unstack_pair(o2, 0, m) * gate(Z_DG + c)).astype(BF16)


def _ctx_attn_call(z, sink, wkv, layer):
    out = jax.ShapeDtypeStruct((BATCH, SEQ, BRANCH_W), BF16)
    ospec = pl.BlockSpec((1, SEQ, BRANCH_W), lambda b: (b, 0, 0))
    return pl.pallas_call(
        functools.partial(_ctx_attn_kernel, layer=layer),
        out_shape=(out, out, out),
        grid=(BATCH,),
        in_specs=[
            pl.BlockSpec(memory_space=pltpu.SMEM),
            pl.BlockSpec((1, SEQ, Z_W), lambda b: (b, 0, 0)),
            _layer_spec((MLA_KV_LORA, 512), layer),
        ],
        out_specs=(ospec, ospec, ospec),
        compiler_params=_cparams(("arbitrary",), 40),
        name="ctx_attn",
    )(sink, z, wkv)


def _win_attn_kernel(sink_ref, q_ref, kv_ref, g_ref, ck_ref, cv_ref, o_ref, ck_s, cv_s, *, layer):
    nb = DEC_SEQ // Q_BLOCK
    m = Q_BLOCK

    @pl.when(pl.program_id(1) == 0)
    def _():
        ck_s[...] = ck_ref[0, 0].astype(BF16)
        cv_s[...] = cv_ref[0, 0].astype(BF16)

    sink = _sink_column(sink_ref, layer, m)
    for qb in range(WIN_BPS):
        n = pl.program_id(1) * WIN_BPS + qb
        rows = slice(qb * m, (qb + 1) * m)
        blocks = []
        for off in (-1, 0, 1):
            start = pl.multiple_of(jnp.clip(n + off, 0, nb - 1) * m, m)
            blocks.append(kv_ref[0, pl.ds(start, m), :])
        kvw = jnp.concatenate(blocks, axis=0)
        shape = (A_HEADS * m, 3 * m)
        qpos = n * m + (lax.broadcasted_iota(jnp.int32, shape, 0) & (m - 1))
        kpos = (n - 1) * m + lax.broadcasted_iota(jnp.int32, shape, 1)
        mask = (jnp.abs(qpos - kpos) <= A_WINDOW) & (kpos >= 0) & (kpos < DEC_SEQ)
        q4 = _stack_heads([q_ref[0, rows, 0:LANE], q_ref[0, rows, LANE:2 * LANE]])
        s_win = jnp.where(mask, _dot_nt(q4, kvw[:, 0:LANE]), NEG)
        s_ctx = _dot_nt(q4, ck_s[...])
        o4 = _softmax_pv([s_win, s_ctx], [kvw[:, LANE:2 * LANE], cv_s[...]], sink=sink)
        for pair in range(2):
            g = g_ref[0, rows, pair * LANE:(pair + 1) * LANE].astype(F32)
            o_ref[0, rows, pair * LANE:(pair + 1) * LANE] = (_unstack_pair(o4, pair, m) * _silu(g)).astype(BF16)


def _win_attn_call(z, sink, cache_k, cache_v, layer):
    kvw = A_KV_HEADS * HEAD_DIM
    qrows = WIN_BPS * Q_BLOCK
    return pl.pallas_call(
        functools.partial(_win_attn_kernel, layer=layer),
        out_shape=jax.ShapeDtypeStruct((DEC_BATCH, DEC_SEQ, BRANCH_W), BF16),
        grid=(DEC_BATCH, DEC_SEQ // qrows),
        in_specs=[
            pl.BlockSpec(memory_space=pltpu.SMEM),
            pl.BlockSpec((1, qrows, 256), lambda b, n: (b, n, Z_AQ // 256)),
            pl.BlockSpec((1, DEC_SEQ, 256), lambda b, n: (b, 0, Z_AK // 256)),
            pl.BlockSpec((1, qrows, 256), lambda b, n: (b, n, Z_AG // 256)),
            pl.BlockSpec((1, 1, PAST_LEN, kvw), lambda b, n: (b, layer, 0, 0)),
            pl.BlockSpec((1, 1, PAST_LEN, kvw), lambda b, n: (b, layer, 0, 0)),
        ],
        out_specs=pl.BlockSpec((1, qrows, BRANCH_W), lambda b, n: (b, n, 0)),
        scratch_shapes=[pltpu.VMEM((PAST_LEN, kvw), BF16), pltpu.VMEM((PAST_LEN, kvw), BF16)],
        compiler_params=_cparams(("arbitrary", "arbitrary"), 40),
        name="lat_window_attn",
    )(sink, z, z, z, cache_k, cache_v)


def _nat_kernel(q_ref, k_ref, v_ref, g_ref, ck_ref, cv_ref, ring_ref, o_ref, ck_s, cv_s, bias_s):
    nrows = DEC_SEQ // GRID_W
    kr = NAT_ROWS
    m = NAT_RPS * GRID_W
    step = pl.program_id(1)

    @pl.when((pl.program_id(0) == 0) & (step == 0))
    def _():
        shape = (GRID_W, LANE)
        lane = lax.broadcasted_iota(jnp.int32, shape, 1)
        qcol = lax.broadcasted_iota(jnp.int32, shape, 0)
        rel = (lane & (GRID_W - 1)) - jnp.clip(qcol - NAT_COLS // 2, 0, GRID_W - NAT_COLS)
        col_ok = (rel >= 0) & (rel < NAT_COLS)
        for h in range(NAT_HEADS):
            for i in range(NAT_BIAS_ROWS):
                a = pltpu.roll(jnp.broadcast_to(ring_ref[h, i:i + 1, :], shape), 0, 1, stride=1, stride_axis=0)
                b = pltpu.roll(jnp.broadcast_to(ring_ref[h, i + 1:i + 2, :], shape), GRID_W, 1, stride=1,
                               stride_axis=0)
                bias_s[h, i] = jnp.where(col_ok, jnp.where(lane < GRID_W, a, b), NEG)

    @pl.when(step == 0)
    def _():
        ck_s[...] = ck_ref[0, 0].astype(BF16)
        cv_s[...] = cv_ref[0, 0].astype(BF16)

    ws = jnp.clip(step * NAT_RPS - kr // 2, 0, nrows - NAT_UNION)
    start = pl.multiple_of(ws * GRID_W, GRID_W)
    kl = k_ref[0, pl.ds(start, NAT_UNION * GRID_W), :]
    vl = v_ref[0, pl.ds(start, NAT_UNION * GRID_W), :]
    key_row = lax.broadcasted_iota(jnp.int32, (1, LANE), 1) // GRID_W
    for pair in range(NAT_HEADS // 2):
        c = pair * LANE
        blocks = []
        for hh in range(2):
            for rr in range(NAT_RPS):
                r = step * NAT_RPS + rr
                lo = jnp.clip(r - kr // 2, 0, nrows - kr) - ws
                ro = ws - r + (NAT_ROWS - 1) + NAT_BIAS_PAD
                row = []
                for t in range(NAT_UNION // 2):
                    kj = key_row + 2 * t
                    row.append(jnp.where((kj >= lo) & (kj < lo + kr), bias_s[2 * pair + hh, ro + 2 * t], NEG))
                blocks.append(jnp.concatenate(row, axis=1))
        bias = jnp.concatenate(blocks, axis=0)
        q2 = _stack_heads([q_ref[0, :, c:c + LANE]])
        s_lat = _dot_nt(q2, kl[:, c:c + LANE]) + bias
        s_ctx = _dot_nt(q2, ck_s[:, c:c + LANE])
        o2 = _softmax_pv([s_lat, s_ctx], [vl[:, c:c + LANE], cv_s[:, c:c + LANE]])
        g = g_ref[0, :, c:c + LANE].astype(F32)
        o_ref[0, :, c:c + LANE] = (_unstack_pair(o2, 0, m) * _silu(g)).astype(BF16)


def _nat_bias_rings(rpb):
    nc = NAT_COLS - 1
    rep = lambda a, n: jnp.broadcast_to(a, a.shape[:-1] + (n,))
    ring = jnp.concatenate([rpb[..., nc:], rep(rpb[..., -1:], GRID_W - 1 - nc), rep(rpb[..., :1], GRID_W - nc),
                            rpb[..., :nc]], axis=-1)
    neg = lambda n: jnp.full(ring.shape[:2] + (n, 2 * GRID_W), NEG, F32)
    return jnp.concatenate([neg(NAT_BIAS_PAD), ring, neg(NAT_BIAS_PAD + 1)], axis=2)


def _nat_call(z, cache_k, cache_v, bias, layer):
    hw = NAT_HEADS * HEAD_DIM
    qrows = NAT_RPS * GRID_W
    return pl.pallas_call(
        _nat_kernel,
        out_shape=jax.ShapeDtypeStruct((DEC_BATCH, DEC_SEQ, BRANCH_W), BF16),
        grid=(DEC_BATCH, DEC_SEQ // qrows),
        in_specs=[
            pl.BlockSpec((1, qrows, 256), lambda b, r: (b, r, Z_DQ // 256)),
            pl.BlockSpec((1, DEC_SEQ, 256), lambda b, r: (b, 0, Z_DK // 256)),
            pl.BlockSpec((1, DEC_SEQ, 256), lambda b, r: (b, 0, Z_DV // 256)),
            pl.BlockSpec((1, qrows, 256), lambda b, r: (b, r, Z_DG // 256)),
            pl.BlockSpec((1, 1, PAST_LEN, hw), lambda b, r: (b, layer, 0, 0)),
            pl.BlockSpec((1, 1, PAST_LEN, hw), lambda b, r: (b, layer, 0, 0)),
            _layer_spec(bias.shape[1:], layer),
        ],
        out_specs=pl.BlockSpec((1, qrows, BRANCH_W), lambda b, r: (b, r, 0)),
        scratch_shapes=[pltpu.VMEM((PAST_LEN, hw), BF16), pltpu.VMEM((PAST_LEN, hw), BF16),
                        pltpu.VMEM((NAT_HEADS, NAT_BIAS_ROWS, GRID_W, LANE), F32)],
        compiler_params=_cparams(("arbitrary", "arbitrary"), 40),
        name="lat_nat_attn",
    )(z, z, z, z, cache_k, cache_v, bias)


def _mla_kernel(q_ref, ckv_ref, kpe_ref, g_ref, cckv_ref, ckpe_ref, wkv_ref, o_ref, k_s, v_s):
    nlat = DEC_SEQ

    @pl.when(pl.program_id(1) == 0)
    def _():
        rows = 512

        def fill(r0, ckv, kpe_blk):
            kv = _dot(ckv, wkv_ref[...])
            for h, kh in enumerate(_mla_keys(kv, kpe_blk)):
                k_s[h, r0:r0 + rows, :] = kh
                v_s[h, r0:r0 + rows, :] = kv[:, h * LANE:(h + 1) * LANE].astype(BF16)

        for c in range(nlat // rows):
            fill(c * rows, ckv_ref[0, c * rows:(c + 1) * rows, :], kpe_ref[0, c * rows:(c + 1) * rows, :].astype(F32))
        ckpe = ckpe_ref[0, 0]
        ckpe_blk = jnp.concatenate([jnp.zeros((PAST_LEN, KPE_LANE), F32), ckpe,
                                    jnp.zeros((PAST_LEN, LANE - KPE_LANE - MLA_ROPE), F32)], axis=1)
        fill(nlat, cckv_ref[0, 0].astype(BF16), ckpe_blk)

    _mla_heads(lambda h: q_ref[0, :, LANE * h:LANE * (h + 1)], lambda h: k_s[h], lambda h: v_s[h],
               lambda p: _silu(g_ref[0, :, p * LANE:(p + 1) * LANE].astype(F32)), o_ref)


def _mla_call(z, cache_ckv, cache_kpe, wkv, layer):
    nkeys = DEC_SEQ + PAST_LEN
    return pl.pallas_call(
        _mla_kernel,
        out_shape=jax.ShapeDtypeStruct((DEC_BATCH, DEC_SEQ, BRANCH_W), BF16),
        grid=(DEC_BATCH, DEC_SEQ // TQ_MLA),
        in_specs=[
            pl.BlockSpec((1, TQ_MLA, MLA_HEADS * LANE), lambda b, t: (b, t, Z_QM // (MLA_HEADS * LANE))),
            pl.BlockSpec((1, DEC_SEQ, LANE), lambda b, t: (b, 0, Z_CKV // LANE)),
            pl.BlockSpec((1, DEC_SEQ, LANE), lambda b, t: (b, 0, Z_KPE // LANE)),
            pl.BlockSpec((1, TQ_MLA, 256), lambda b, t: (b, t, Z_BG // 256)),
            pl.BlockSpec((1, 1, PAST_LEN, MLA_KV_LORA), lambda b, t: (b, layer, 0, 0)),
            pl.BlockSpec((1, 1, PAST_LEN, MLA_ROPE), lambda b, t: (b, layer, 0, 0)),
            _layer_spec((MLA_KV_LORA, 512), layer),
        ],
        out_specs=pl.BlockSpec((1, TQ_MLA, BRANCH_W), lambda b, t: (b, t, 0)),
        scratch_shapes=[pltpu.VMEM((MLA_HEADS, nkeys, LANE), BF16), pltpu.VMEM((MLA_HEADS, nkeys, LANE), BF16)],
        compiler_params=_cparams(("arbitrary", "arbitrary"), 56),
        name="lat_mla_attn",
    )(z, z, z, z, cache_ckv, cache_kpe, wkv)


def _s5_mats(lam_re, lam_im, log_dt, b_re, b_im, c_re, c_im):
    t = S5_T
    hp = lax.Precision.HIGHEST
    lre = jnp.minimum(lam_re, -1e-4)
    lim = lam_im
    dt = jnp.exp(log_dt)[..., None]
    er, ei = lre * dt, lim * dt
    mag = jnp.exp(er)
    are, aim = mag * jnp.cos(ei), mag * jnp.sin(ei)
    den = lre * lre + lim * lim
    qre = ((are - 1.0) * lre + aim * lim) / den
    qim = (aim * lre - (are - 1.0) * lim) / den
    bbr = qre[..., None] * b_re - qim[..., None] * b_im
    bbi = qre[..., None] * b_im + qim[..., None] * b_re
    n = jnp.arange(t + 1, dtype=F32)[:, None, None, None]
    pmag = jnp.exp(er[None] * n)
    pr, pi = pmag * jnp.cos(ei[None] * n), pmag * jnp.sin(ei[None] * n)

    def c_times_pow(d, pw_r, pw_i):
        rep = lambda a: jnp.repeat(jnp.transpose(a, (1, 2, 0)), S5_CH, axis=-1)
        til = lambda a: jnp.tile(jnp.transpose(a, (0, 2, 1)), (1, 1, t))
        cr, ci, ar, ai = til(c_re[d]), til(c_im[d]), rep(pw_r), rep(pw_i)
        return cr * ar - ci * ai, cr * ai + ci * ar

    def lag_kernels(d, wr, wi):
        return (jnp.einsum('gpk,gpx->gkx', bbr[d], wr, precision=hp)
                - jnp.einsum('gpk,gpx->gkx', bbi[d], wi, precision=hp))

    wf0 = c_times_pow(0, pr[:t, 0], pi[:t, 0])
    wf1 = c_times_pow(0, pr[1:, 0], pi[1:, 0])
    wb = c_times_pow(1, pr[1:, 1][::-1], pi[1:, 1][::-1])
    kb0 = (jnp.einsum('gpk,gcp->gkc', bbr[1], c_re[1], precision=hp)
           - jnp.einsum('gpk,gcp->gkc', bbi[1], c_im[1], precision=hp))
    kf = lag_kernels(0, *wf0) + jnp.pad(kb0, ((0, 0), (0, 0), (0, S5_TC - S5_CH)))
    kb = lag_kernels(1, *wb) * jnp.asarray(np.arange(S5_TC) >= S5_CH, F32)
    ring = jnp.concatenate([kf, kb], axis=-1)

    def state_cols(pw_r, pw_i, d):
        rep = lambda a: jnp.repeat(jnp.transpose(a, (1, 0, 2)), S5_CH, axis=1)
        til = lambda a: jnp.tile(jnp.transpose(a, (0, 2, 1)), (1, t, 1))
        ar, ai, br, bi = rep(pw_r), rep(pw_i), til(bbr[d]), til(bbi[d])
        return ar * br - ai * bi, ar * bi + ai * br

    sfr, sfi = state_cols(pr[:t, 0][::-1], pi[:t, 0][::-1], 0)
    sbr, sbi = state_cols(pr[:t, 1], pi[:t, 1], 1)
    m_state = jnp.concatenate([sfr, sbr, sfi, sbi], axis=-1)
    m_out = jnp.concatenate([wf1[0], wb[0], -wf1[1], -wb[1]], axis=1)

    a_t = jnp.stack([jnp.concatenate([pr[t, 0], pr[t, 1]], axis=-1),
                     jnp.concatenate([pi[t, 0], pi[t, 1]], axis=-1)], axis=1)
    return ring, m_state.astype(BF16), m_out.astype(BF16), a_t


def _s5_kernel(uc_ref, ul_ref, ring_ref, ms_ref, mo_ref, at_ref, h0_ref, yc_ref, yl_ref, hend_ref,
               s_s, hf_s, hb_s, m_s):
    p = S5_STATE
    u = jnp.concatenate([uc_ref[0], ul_ref[0]], axis=0)
    s = _dot(u, ms_ref[0])
    s_s[0] = s[:, 0:2 * p]
    s_s[1] = s[:, 2 * p:4 * p]
    are = at_ref[0, 0:1, :]
    aim = at_ref[0, 1:2, :]
    fwd_lane = lax.broadcasted_iota(jnp.int32, (1, 2 * p), 1) < p

    for k in range(S5_CH):
        rk = jnp.broadcast_to(ring_ref[0, k:k + 1, :], (S5_T, 2 * S5_TC))
        rolled = pltpu.roll(rk, 0, 1, stride=S5_CH, stride_axis=0)
        for blk in range(S5_TC // LANE):
            m_s[blk, pl.ds(k, S5_T, stride=S5_CH), :] = rolled[:, blk * LANE:(blk + 1) * LANE]

    def scan(row0, nk, nb, hre, him):
        for st in range(nk):
            rf = pl.ds(row0 + st, nb, stride=nk)
            rb = pl.ds(row0 + nk - 1 - st, nb, stride=nk)
            hf_s[0, rf, :] = hre
            hf_s[1, rf, :] = him
            hb_s[0, rb, :] = hre
            hb_s[1, rb, :] = him
            sre = jnp.where(fwd_lane, s_s[0, rf, :], s_s[0, rb, :])
            sim = jnp.where(fwd_lane, s_s[1, rf, :], s_s[1, rb, :])
            hre, him = are * hre - aim * him + sre, are * him + aim * hre + sim
        return hre, him

    zero = jnp.zeros((BATCH, 2 * p), F32)
    hre, him = scan(0, S5_NK_CTX, BATCH, zero, zero)
    hend_ref[0] = jnp.concatenate([hre, him], axis=1)
    scan(S5_ROWS, S5_NK_LAT, DEC_BATCH, h0_ref[0, :, 0:2 * p], h0_ref[0, :, 2 * p:4 * p])
    hst = jnp.concatenate([jnp.where(fwd_lane, hf_s[0], hb_s[0]), jnp.where(fwd_lane, hf_s[1], hb_s[1])], axis=1)
    m_intra = jnp.concatenate([m_s[blk] for blk in range(S5_TC // LANE)], axis=1).astype(BF16)
    y = _dot(u, m_intra) + _dot(hst.astype(BF16), mo_ref[0])
    yc_ref[0] = y[0:S5_ROWS]
    yl_ref[0] = y[S5_ROWS:2 * S5_ROWS]


def _s5_call(ut_ctx, ut_lat, ring, m_state, m_out, a_t, h0, layer):
    g3 = lambda g: (g, 0, 0)
    lg = lambda shape: pl.BlockSpec((None, 1) + shape, lambda g: (layer, g, 0, 0))
    yshape = jax.ShapeDtypeStruct((S5_GROUPS, S5_ROWS, S5_TC), F32)
    uspec = pl.BlockSpec((1, S5_ROWS, S5_TC), g3)
    return pl.pallas_call(
        _s5_kernel,
        out_shape=(yshape, yshape, jax.ShapeDtypeStruct((S5_GROUPS, BATCH, 4 * S5_STATE), F32)),
        grid=(S5_GROUPS,),
        in_specs=[
            uspec, uspec,
            lg((S5_CH, 2 * S5_TC)),
            lg((S5_TC, 4 * S5_STATE)),
            lg((4 * S5_STATE, S5_TC)),
            lg((2, 2 * S5_STATE)),
            lg((DEC_BATCH, 4 * S5_STATE)),
        ],
        out_specs=(uspec, uspec, pl.BlockSpec((1, BATCH, 4 * S5_STATE), g3)),
        scratch_shapes=[pltpu.VMEM((2, 2 * S5_ROWS, LANE), F32)] * 3 + [pltpu.VMEM((S5_TC // LANE, S5_TC, LANE), F32)],
        compiler_params=_cparams(("arbitrary",), 32),
        name="s5_chunks",
    )(ut_ctx, ut_lat, ring, m_state, m_out, a_t, h0)


def _post_kernel(x_ref, mod_ref, ng_ref, ba_ref, bb_ref, bd_ref, y_ref, u_ref, cg_ref, sd_ref, wglu_ref,
                 wm_ref, wbr_ref, wo_ref, fg_ref, o_ref, ys_s, *, final):
    x = x_ref[0]
    hb = _ada_h(x, mod_ref, ng_ref).astype(BF16)
    for m in range(S5_T // 8):
        for hf in range(2):
            vs = _slot_transpose([y_ref[8 * hf + gp, :, m * LANE:(m + 1) * LANE] for gp in range(8)])
            for s in range(8):
                ys_s[hf, pl.ds(8 * m + s, TM_CHUNKS, stride=S5_T), :] = vs[s]
    yv = jnp.concatenate([ys_s[0], ys_s[1]], axis=1) + sd_ref[...] * u_ref[0].astype(F32)
    gel = 0.5 * yv * (1.0 + jnp.tanh(math.sqrt(2.0 / math.pi) * (yv + 0.044715 * (yv * yv * yv))))
    gl = _dot(gel.astype(BF16), wglu_ref[...])
    oc = gl[:, 0:BRANCH_W] * _sigmoid(gl[:, BRANCH_W:2 * BRANCH_W])
    bc = (oc * _silu(cg_ref[0].astype(F32))).astype(BF16)
    acc = None
    for k, br in enumerate((ba_ref[0], bb_ref[0], bc, bd_ref[0])):
        proj = _dot(br, wbr_ref[k])
        mg = _sigmoid(_dot(hb, wm_ref[:, k * D_MODEL:(k + 1) * D_MODEL]))
        acc = mg * proj if acc is None else acc + mg * proj
    y = _dot(acc.astype(BF16), wo_ref[...])
    out = x + mod_ref[0, :, 2 * D_MODEL:3 * D_MODEL] * y
    if final:
        out = _rms(out, fg_ref[...])
    o_ref[0] = out


def _post_call(x, mods, mod_row0, p, ba, bb, bd, y_s5, z, fg, final, layer, name):
    nb, length, _ = x.shape
    nt = length // TM
    tok = lambda b, t: (b, t, 0)
    br_spec = pl.BlockSpec((1, TM, BRANCH_W), tok)
    return pl.pallas_call(
        functools.partial(_post_kernel, final=final),
        out_shape=jax.ShapeDtypeStruct((nb, length, D_MODEL), F32),
        grid=(nb, nt),
        in_specs=[
            pl.BlockSpec((1, TM, D_MODEL), tok),
            _mod_spec(layer, mod_row0),
            _layer_spec((1, D_MODEL), layer),
            br_spec, br_spec, br_spec,
            pl.BlockSpec((S5_GROUPS, TM_CHUNKS, S5_TC), lambda b, t: (0, b * nt + t, 0)),
            pl.BlockSpec((1, TM, 256), lambda b, t: (b, t, Z_CU // 256)),
            pl.BlockSpec((1, TM, 256), lambda b, t: (b, t, Z_CG // 256)),
            _layer_spec((1, BRANCH_W), layer),
            _layer_spec((BRANCH_W, 2 * BRANCH_W), layer),
            _layer_spec((D_MODEL, N_BRANCH * D_MODEL), layer),
            _layer_spec((N_BRANCH, BRANCH_W, D_MODEL), layer),
            _layer_spec((D_MODEL, D_MODEL), layer),
            pl.BlockSpec((1, D_MODEL), lambda b, t: (0, 0)),
        ],
        out_specs=pl.BlockSpec((1, TM, D_MODEL), tok),
        scratch_shapes=[pltpu.VMEM((2, TM, LANE), F32)],
        compiler_params=_cparams(("arbitrary", "arbitrary"), 56),
        name=name,
    )(x, mods, p['ng'], ba, bb, bd, y_s5, z, z, p['s5d'], p['wglu'], p['wm'], p['wbr'], p['wo'], fg)


def _rope_tables():
    t = np.arange(DEC_SEQ)
    row = (t // GRID_W).astype(np.float64)
    col = (t % GRID_W).astype(np.float64)

    def pattern(half):
        inv = ROPE_BASE ** (-np.arange(half, dtype=np.float64) / half)
        zeros = np.zeros((DEC_SEQ, half))
        cs, s_up, s_lo = [], [], []
        for pos in (row, col):
            ang = pos[:, None] * inv[None, :]
            c, s = np.cos(ang), np.sin(ang)
            cs += [c, c]
            s_up += [zeros, s]
            s_lo += [-s, zeros]
        return [np.concatenate(parts, axis=1) for parts in (cs, s_up, s_lo)]

    tab_a = np.stack([np.tile(part, (1, LANE // HEAD_DIM)) for part in pattern(HEAD_DIM // 4)])
    ident = [np.ones, np.zeros, np.zeros]
    tab_m = np.stack([np.concatenate([fill((DEC_SEQ, KPE_LANE)), part,
                                      fill((DEC_SEQ, LANE - KPE_LANE - MLA_ROPE))], axis=1)
                      for fill, part in zip(ident, pattern(MLA_ROPE // 4))])
    return jnp.asarray(tab_a, F32), jnp.asarray(tab_m, F32)


_W_IN_MOVES = tuple(
    [(HEAD_DIM * i, HEAD_DIM * h, HEAD_DIM) for i, h in enumerate(A_HEAD_ORDER)]
    + [(256, 256, 256), (512, 1952, 768), (R_CQ, 768, 384), (R_KPE + KPE_LANE, 1152, MLA_ROPE), (R_CU, 1440, 256)]
    + [(R_G + HEAD_DIM * i, 512 + HEAD_DIM * h, HEAD_DIM) for i, h in enumerate(A_HEAD_ORDER)]
    + [(R_G + 256, 1184, 256), (R_G + 512, 1696, 256), (R_G + 768, 2720, 256)])
W_IN_ROWS = 256


def _w_in_kernel(w_ref, o_ref):
    o_ref[0, :, R_KPE:R_CU] = jnp.zeros((W_IN_ROWS, LANE), BF16)
    for dst, src, width in _W_IN_MOVES:
        o_ref[0, :, dst:dst + width] = w_ref[0, :, src:src + width].astype(BF16)


def _w_in_call(w_in):
    return pl.pallas_call(
        _w_in_kernel,
        out_shape=jax.ShapeDtypeStruct((DEPTH, D_MODEL, R_W), BF16),
        grid=(DEPTH, D_MODEL // W_IN_ROWS),
        in_specs=[pl.BlockSpec((1, W_IN_ROWS, w_in.shape[-1]), lambda i, r: (i, r, 0))],
        out_specs=pl.BlockSpec((1, W_IN_ROWS, R_W), lambda i, r: (i, r, 0)),
        compiler_params=_cparams(("arbitrary", "arbitrary"), 32),
        name="w_in_layout",
    )(w_in)


def _cast_kernel(w_ref, o_ref):
    o_ref[...] = w_ref[...].astype(BF16)


def _cast_call(w):
    _, rows, cols = w.shape
    spec = pl.BlockSpec((1, W_IN_ROWS, cols), lambda i, r: (i, r, 0))
    return pl.pallas_call(
        _cast_kernel,
        out_shape=jax.ShapeDtypeStruct(w.shape, BF16),
        grid=(DEPTH, rows // W_IN_ROWS),
        in_specs=[spec],
        out_specs=spec,
        compiler_params=_cparams(("arbitrary", "arbitrary"), 32),
        name="w_cast",
    )(w)


def _reorder_w_q_up(w):
    w = w.reshape(MLA_Q_LORA, MLA_HEADS, MLA_NOPE + MLA_ROPE)
    w = jnp.concatenate([w, jnp.zeros((MLA_Q_LORA, MLA_HEADS, LANE - MLA_NOPE - MLA_ROPE), w.dtype)], axis=-1)
    return w.reshape(MLA_Q_LORA, MLA_HEADS * LANE)


def _reorder_w_branch(w):
    wa = w[0].reshape(A_HEADS, HEAD_DIM, D_MODEL)
    wa = jnp.concatenate([wa[h] for h in A_HEAD_ORDER], axis=0)
    return jnp.concatenate([wa[None], w[1:]], axis=0)


def kernel(x_prompt, x_sample, cache_a_k, cache_a_v, cache_mla_ckv, cache_mla_kpe, cache_na_k, cache_na_v,
           state_s5_re, state_s5_im, c, c_ctx, w_mod, b_mod, norm_g, w_in, w_merge, a_sink,
           mla_q_norm, mla_w_q_up, mla_kv_norm, mla_w_kv_up, s5_lam_re, s5_lam_im, s5_log_dt,
           s5_b_re, s5_b_im, s5_c_re, s5_c_im, s5_d, s5_w_glu, na_rpb, w_branch, w_out, final_norm_g):
    n_ctx = BATCH * SEQ
    conds = jnp.concatenate([c, c_ctx[None, :], jnp.zeros((3, D_MODEL), F32)], axis=0)
    mods = _mod_call(conds, w_mod, b_mod).reshape(DEPTH, 8, 1, 3 * D_MODEL)
    tabs = _rope_tables()
    ck_a = cache_a_k.reshape(DEC_BATCH, DEPTH, PAST_LEN, A_KV_HEADS * HEAD_DIM)
    cv_a = cache_a_v.reshape(DEC_BATCH, DEPTH, PAST_LEN, A_KV_HEADS * HEAD_DIM)
    ck_n = cache_na_k.reshape(DEC_BATCH, DEPTH, PAST_LEN, NAT_HEADS * HEAD_DIM)
    cv_n = cache_na_v.reshape(DEC_BATCH, DEPTH, PAST_LEN, NAT_HEADS * HEAD_DIM)
    fg = final_norm_g.reshape(1, D_MODEL)
    caches = [jnp.zeros((BATCH, DEPTH, SEQ, w), F32) for w in (128, 128, MLA_KV_LORA, MLA_ROPE, 256, 256)]

    p = dict(
        ng=norm_g.reshape(DEPTH, 1, D_MODEL),
        w_raw=_w_in_call(w_in),
        qnorm=mla_q_norm.reshape(DEPTH, 1, MLA_Q_LORA),
        wq=jax.vmap(_reorder_w_q_up)(mla_w_q_up).astype(BF16),
        kvnorm=mla_kv_norm.reshape(DEPTH, 1, MLA_KV_LORA),
        wkv=mla_w_kv_up.astype(BF16),
        wm=_cast_call(w_merge),
        wbr=jax.vmap(_reorder_w_branch)(w_branch).astype(BF16),
        wo=_cast_call(w_out),
        wglu=s5_w_glu.astype(BF16),
        s5d=s5_d.reshape(DEPTH, 1, BRANCH_W),
    )
    ring, m_state, m_out, a_t = jax.vmap(_s5_mats)(s5_lam_re, s5_lam_im, s5_log_dt, s5_b_re, s5_b_im, s5_c_re, s5_c_im)
    nat_bias = _nat_bias_rings(na_rpb * LOG2E)
    h0 = jnp.concatenate([state_s5_re[:, :, 0], state_s5_re[:, :, 1], state_s5_im[:, :, 0], state_s5_im[:, :, 1]],
                         axis=-1)
    h0 = jnp.transpose(h0, (1, 2, 0, 3))

    yp = x_prompt.reshape(1, n_ctx, D_MODEL)
    ys = x_sample
    hends = []
    for i in range(DEPTH):
        res = _inproj_call(yp, mods, DEC_BATCH, p, None, caches, i, "inproj_ctx")
        z_ctx, ut_ctx, caches = res[0], res[1], list(res[2:])
        z_lat, ut_lat = _inproj_call(ys, mods, 0, p, tabs, None, i, "inproj_lat")
        zc = z_ctx.reshape(BATCH, SEQ, Z_W)

        y_ctx, y_lat, hend = _s5_call(ut_ctx, ut_lat, ring, m_state, m_out, a_t, h0, i)
        hends.append(hend)

        oa_c, ob_c, od_c = _ctx_attn_call(zc, a_sink, p['wkv'], i)
        oa_l = _win_attn_call(z_lat, a_sink, ck_a, cv_a, i)
        ob_l = _mla_call(z_lat, cache_mla_ckv, cache_mla_kpe, p['wkv'], i)
        od_l = _nat_call(z_lat, ck_n, cv_n, nat_bias, i)

        flat = lambda a: a.reshape(1, n_ctx, BRANCH_W)
        final = i == DEPTH - 1
        yp = _post_call(yp, mods, DEC_BATCH, p, flat(oa_c), flat(ob_c), flat(od_c), y_ctx, z_ctx, fg, final, i,
                        "post_ctx")
        ys = _post_call(ys, mods, 0, p, oa_l, ob_l, od_l, y_lat, z_lat, fg, final, i, "post_lat")

    hend = jnp.stack(hends).reshape(DEPTH, S5_GROUPS, BATCH, 2, 2, S5_STATE)
    hend = jnp.transpose(hend, (3, 2, 0, 4, 1, 5))
    ak, av, ckv, kpe, nk, nv = caches
    heads = lambda a, h: a.reshape(BATCH, DEPTH, SEQ, h, HEAD_DIM)
    return (yp.reshape(BATCH, SEQ, D_MODEL), ys, heads(ak, A_KV_HEADS), heads(av, A_KV_HEADS), ckv, kpe,
            heads(nk, NAT_HEADS), heads(nv, NAT_HEADS), hend[0], hend[1])
```

```python
import functools
import math
from typing import Callable, NamedTuple

import numpy as np
import jax
import jax.numpy as jnp
from jax import lax
from jax.experimental import pallas as pl
from jax.experimental.pallas import tpu as pltpu

F32 = jnp.float32
BF16 = jnp.bfloat16

D_MODEL = 1024
BATCH = 32
SEQ = 256
DEPTH = 2
DEC_BATCH = 4
DEC_SEQ = 2048
PAST_LEN = 512
GRID_W = 64
HEAD_DIM = 64
BRANCH_W = 256
N_BRANCH = 4
Q_BLOCK = 128
A_HEADS = 4
A_KV_HEADS = 2
A_GROUP = A_HEADS // A_KV_HEADS
A_WINDOW = 128
MLA_HEADS = 4
MLA_Q_LORA = 256
MLA_KV_LORA = 128
MLA_NOPE = 64
MLA_ROPE = 32
MLA_V = 64
S5_CH = 16
S5_GROUPS = BRANCH_W // S5_CH
S5_STATE = 64
NAT_HEADS = 4
NAT_ROWS = 8
NAT_COLS = 16
ROPE_BASE = 10000.0
EPS = 1e-6
NEG = -1e30
LOG2E = 1.4426950408889634
Q_SCALE = HEAD_DIM ** -0.5 * LOG2E
MLA_Q_SCALE = (MLA_NOPE + MLA_ROPE) ** -0.5 * LOG2E

LANE = 128

R_AV, R_CQ, R_CKV, R_KPE, R_CU, R_G, R_W = 384, 1280, 1536, 1664, 1792, 2048, 3072
KPE_LANE = 64
A_HEAD_ORDER = (0, 2, 1, 3)
Z_AQ, Z_AK, Z_AV = 0, 256, 384
Z_DQ, Z_DK, Z_DV = 512, 768, 1024
Z_CKV, Z_KPE = 1280, 1408
Z_QM = 1536
Z_AG, Z_BG, Z_CG, Z_DG = 2048, 2304, 2560, 2816
Z_CU = 3072
Z_W = 3328

S5_T = 32
S5_TC = S5_T * S5_CH
S5_ROWS = 256
S5_NK_CTX = SEQ // S5_T
S5_NK_LAT = DEC_SEQ // S5_T

TM = 512
TM_CHUNKS = TM // S5_T
TQ_MLA = 512
MLA_TILES = DEC_SEQ // TQ_MLA
POST_STEPS = BATCH * SEQ // TM
CTX_BPS = 2
NAT_RPS = 4
NAT_UNION = NAT_ROWS + NAT_RPS
NAT_BIAS_PAD = NAT_RPS
NAT_BIAS_ROWS = 2 * NAT_ROWS - 1 + 2 * NAT_BIAS_PAD
WIN_BPS = 2


def _cparams(sem, vmem_mb):
    return pltpu.CompilerParams(dimension_semantics=sem, vmem_limit_bytes=vmem_mb * 1024 * 1024)


def _sigmoid(x):
    return 1.0 / (1.0 + jnp.exp(-x))


def _silu(x):
    return x * _sigmoid(x)


def _rms(x, g):
    return x * lax.rsqrt(jnp.mean(x * x, axis=-1, keepdims=True) + EPS) * g


def _dot(a, b):
    return jnp.dot(a, b, preferred_element_type=F32)


def _dot_nt(a, b):
    return lax.dot_general(a, b, (((1,), (1,)), ((), ())), preferred_element_type=F32)


def _slot_transpose(vs):
    lane = lax.broadcasted_iota(jnp.int32, vs[0].shape, 1)
    vs = list(vs)
    for d in (4, 2, 1):
        keep = (lane & (S5_CH * d)) == 0
        nxt = list(vs)
        for lo in range(8):
            if lo & d:
                continue
            hi = lo + d
            nxt[lo] = jnp.where(keep, vs[lo], pltpu.roll(vs[hi], S5_CH * d, 1))
            nxt[hi] = jnp.where(keep, pltpu.roll(vs[lo], LANE - S5_CH * d, 1), vs[hi])
        vs = nxt
    return vs


def _mod_kernel(c_ref, w_ref, b_ref, o_ref):
    s = _silu(c_ref[...])
    o_ref[0] = _dot(s.astype(BF16), w_ref[0].astype(BF16)) + b_ref[0]


def _mod_call(conds, w_mod, b_mod):
    nc = 512
    return pl.pallas_call(
        _mod_kernel,
        out_shape=jax.ShapeDtypeStruct((DEPTH, 8, 3 * D_MODEL), F32),
        grid=(DEPTH, 3 * D_MODEL // nc),
        in_specs=[
            pl.BlockSpec((8, D_MODEL), lambda i, j: (0, 0)),
            pl.BlockSpec((1, D_MODEL, nc), lambda i, j: (i, 0, j)),
            pl.BlockSpec((1, 1, nc), lambda i, j: (i, 0, j)),
        ],
        out_specs=pl.BlockSpec((1, 8, nc), lambda i, j: (i, 0, j)),
        compiler_params=_cparams(("arbitrary", "arbitrary"), 32),
        name="mod_rows",
    )(conds, w_mod, b_mod.reshape(DEPTH, 1, 3 * D_MODEL))


def _ada_h(x, mod_ref, ng_ref):
    shift = mod_ref[0, :, 0:D_MODEL]
    scale = mod_ref[0, :, D_MODEL:2 * D_MODEL]
    return _rms(x, ng_ref[...]) * (1.0 + scale) + shift


def _rope_block(xs, tab_ref, shift):
    return (xs * tab_ref[0] + pltpu.roll(xs, shift, 1) * tab_ref[1]
            + pltpu.roll(xs, LANE - shift, 1) * tab_ref[2])


N_CACHE = 6


def _inproj_kernel(*refs, rope, rider):
    x_ref, mod_ref, ng_ref, w_ref, qn_ref, wq_ref, kvn_ref = refs[:7]
    if rope:
        ta_ref, tm_ref = refs[7:9]
        n_in, n_out = len(rider.in_specs), len(rider.out_specs)
        z_ref, ut_ref = refs[9 + n_in:11 + n_in]
        u_s = refs[11 + n_in + n_out]
        rider.body(*refs[9:9 + n_in], *refs[11 + n_in:11 + n_in + n_out], *refs[12 + n_in + n_out:])
    else:
        z_ref, ut_ref, ak_ref, av_ref, ckv_ref, kpe_ref, nk_ref, nv_ref, u_s = refs[7 + N_CACHE:]
    h = _ada_h(x_ref[0], mod_ref, ng_ref)
    raw = _dot(h.astype(BF16), w_ref[...])
    for j in range(R_AV // LANE):
        blk = raw[:, j * LANE:(j + 1) * LANE]
        if j < Z_AK // LANE:
            blk = blk * Q_SCALE
        if rope:
            blk = _rope_block(blk, ta_ref, 16)
        z_ref[0, :, j * LANE:(j + 1) * LANE] = blk.astype(BF16)
    z_ref[0, :, Z_AV:Z_DQ] = raw[:, Z_AV:Z_DQ].astype(BF16)
    z_ref[0, :, Z_DQ:Z_DK] = (raw[:, Z_DQ:Z_DK] * Q_SCALE).astype(BF16)
    z_ref[0, :, Z_DK:R_CQ] = raw[:, Z_DK:R_CQ].astype(BF16)
    qn = _rms(raw[:, R_CQ:R_CKV], qn_ref[...])
    q = _dot(qn.astype(BF16), wq_ref[...]) * MLA_Q_SCALE
    kp = raw[:, R_KPE:R_CU]
    ckv = _rms(raw[:, R_CKV:R_KPE], kvn_ref[...])
    for hh in range(MLA_HEADS):
        qh = q[:, hh * LANE:(hh + 1) * LANE]
        if rope:
            qh = _rope_block(qh, tm_ref, 8)
        z_ref[0, :, Z_QM + hh * LANE:Z_QM + (hh + 1) * LANE] = qh.astype(BF16)
    z_ref[0, :, Z_CKV:Z_KPE] = ckv.astype(BF16)
    z_ref[0, :, Z_KPE:Z_QM] = (_rope_block(kp, tm_ref, 8) if rope else kp).astype(BF16)
    z_ref[0, :, Z_AG:Z_CU] = raw[:, R_G:R_W].astype(BF16)
    z_ref[0, :, Z_CU:Z_W] = raw[:, R_CU:R_G].astype(BF16)
    if not rope:
        half = TM // 2
        for e in range(2):
            rows = slice(e * half, (e + 1) * half)
            ak_ref[e, 0] = raw[rows, Z_AK:Z_AV]
            av_ref[e, 0] = raw[rows, Z_AV:Z_DQ]
            nk_ref[e, 0] = raw[rows, Z_DK:Z_DV]
            nv_ref[e, 0] = raw[rows, Z_DV:R_CQ]
            ckv_ref[e, 0] = ckv[rows]
            kpe_ref[e, 0] = kp[rows, KPE_LANE:KPE_LANE + MLA_ROPE]
    for hf in range(2):
        u_s[hf] = raw[:, R_CU + hf * LANE:R_CU + (hf + 1) * LANE]
    for m in range(S5_T // 8):
        for hf in range(2):
            outs = _slot_transpose([u_s[hf, pl.ds(8 * m + s, TM_CHUNKS, stride=S5_T), :] for s in range(8)])
            for gp in range(8):
                ut_ref[8 * hf + gp, :, m * LANE:(m + 1) * LANE] = outs[gp].astype(BF16)


def _layer_spec(shape, layer):
    zeros = (0,) * len(shape)
    return pl.BlockSpec((None,) + tuple(shape), lambda *_: (layer,) + zeros)


def _mod_spec(layer, row0):
    return pl.BlockSpec((None, 1, 1, 3 * D_MODEL), lambda b, t: (layer, row0 + b, 0, 0))


def _inproj_call(x, mods, mod_row0, p, tabs, caches, layer, name, rider=None):
    nb, length, _ = x.shape
    rope = tabs is not None
    nt = length // TM
    in_specs = [
        pl.BlockSpec((1, TM, D_MODEL), lambda b, t: (b, t, 0)),
        _mod_spec(layer, mod_row0),
        _layer_spec((1, D_MODEL), layer),
        _layer_spec((D_MODEL, R_W), layer),
        _layer_spec((1, MLA_Q_LORA), layer),
        _layer_spec((MLA_Q_LORA, MLA_HEADS * LANE), layer),
        _layer_spec((1, MLA_KV_LORA), layer),
    ]
    args = [x, mods, p['ng'], p['w_raw'], p['qnorm'], p['wq'], p['kvnorm']]
    out_shape = [jax.ShapeDtypeStruct((nb, length, Z_W), BF16),
                 jax.ShapeDtypeStruct((S5_GROUPS, S5_ROWS, S5_TC), BF16)]
    out_specs = [pl.BlockSpec((1, TM, Z_W), lambda b, t: (b, t, 0)),
                 pl.BlockSpec((S5_GROUPS, TM_CHUNKS, S5_TC), lambda b, t: (0, b * nt + t, 0))]
    aliases = {}
    scratch = [pltpu.VMEM((2, TM, LANE), F32)]
    if rope:
        assert nb * nt == POST_STEPS
        in_specs += [pl.BlockSpec((3, TM, LANE), lambda b, t: (0, t, 0))] * 2 + rider.in_specs
        args += list(tabs) + rider.args
        out_shape += rider.out_shape
        out_specs += rider.out_specs
        scratch += rider.scratch
    else:
        for k, cbuf in enumerate(caches):
            in_specs.append(pl.BlockSpec(memory_space=pl.ANY))
            args.append(cbuf)
            out_shape.append(jax.ShapeDtypeStruct(cbuf.shape, cbuf.dtype))
            out_specs.append(pl.BlockSpec((2, 1, SEQ, cbuf.shape[-1]), lambda b, t: (t, layer, 0, 0)))
            aliases[7 + k] = 2 + k
    return pl.pallas_call(
        functools.partial(_inproj_kernel, rope=rope, rider=rider),
        out_shape=tuple(out_shape),
        grid=(nb, nt),
        in_specs=in_specs,
        out_specs=tuple(out_specs),
        scratch_shapes=scratch,
        input_output_aliases=aliases,
        compiler_params=_cparams(("arbitrary", "arbitrary"), 56),
        name=name,
    )(*args)


def _softmax_pv(s_list, v_list, sink=None):
    m = jnp.max(s_list[0], axis=-1, keepdims=True)
    for s in s_list[1:]:
        m = jnp.maximum(m, jnp.max(s, axis=-1, keepdims=True))
    if sink is not None:
        m = jnp.maximum(m, sink)
    den = None
    o = None
    for s, v in zip(s_list, v_list):
        e = jnp.exp2(s - m)
        d = jnp.sum(e, axis=-1, keepdims=True)
        pv = _dot(e.astype(BF16), v)
        den = d if den is None else den + d
        o = pv if o is None else o + pv
    if sink is not None:
        den = den + jnp.exp2(sink - m)
    return o / den


def _low_lanes(shape):
    return lax.broadcasted_iota(jnp.int32, shape, 1) < HEAD_DIM


def _stack_heads(blocks):
    lo = _low_lanes(blocks[0].shape)
    zero = jnp.zeros_like(blocks[0])
    parts = []
    for b in blocks:
        parts += [jnp.where(lo, b, zero), jnp.where(lo, zero, b)]
    return jnp.concatenate(parts, axis=0)


def _unstack_pair(o, idx, m):
    return jnp.where(_low_lanes((m, LANE)), o[2 * idx * m:(2 * idx + 1) * m], o[(2 * idx + 1) * m:(2 * idx + 2) * m])


def _sink_column(sink_ref, layer, m):
    blk = lax.broadcasted_iota(jnp.int32, (A_HEADS * m, 1), 0) // m
    col = jnp.full((A_HEADS * m, 1), sink_ref[layer, A_HEAD_ORDER[-1]] * LOG2E, F32)
    for i in range(A_HEADS - 1):
        col = jnp.where(blk == i, sink_ref[layer, A_HEAD_ORDER[i]] * LOG2E, col)
    return col


def _mla_keys(kv, kpe_blk):
    lo = _low_lanes(kpe_blk.shape)
    return [jnp.where(lo, kv[:, h * LANE:(h + 1) * LANE], kpe_blk).astype(BF16) for h in range(MLA_HEADS)]


def _mla_heads(q_of, k_of, v_of, gate_of, store):
    for pair in range(MLA_HEADS // 2):
        outs = [_softmax_pv([_dot_nt(q_of(h), k_of(h))], [v_of(h)]) for h in (2 * pair, 2 * pair + 1)]
        o = jnp.where(_low_lanes(outs[0].shape), pltpu.roll(outs[0], MLA_V, 1), outs[1])
        store(pair, (o * gate_of(pair)).astype(BF16))


def _ctx_attn_body(sink_ref, z_ref, wkv_ref, oa_ref, ob_ref, od_ref, e, layer):
    m = SEQ
    gate = lambda c0: _silu(z_ref[e, :, c0:c0 + LANE].astype(F32))

    def store(ref, pair, val):
        ref[e, :, pair * LANE:(pair + 1) * LANE] = val

    q4 = _stack_heads([z_ref[e, :, Z_AQ:Z_AQ + LANE], z_ref[e, :, Z_AQ + LANE:Z_AQ + 2 * LANE]])
    o4 = _softmax_pv([_dot_nt(q4, z_ref[e, :, Z_AK:Z_AV])], [z_ref[e, :, Z_AV:Z_DQ]],
                     sink=_sink_column(sink_ref, layer, m))
    for pair in range(2):
        store(oa_ref, pair, (_unstack_pair(o4, pair, m) * gate(Z_AG + pair * LANE)).astype(BF16))
    kv = _dot(z_ref[e, :, Z_CKV:Z_KPE], wkv_ref[...])
    keys = _mla_keys(kv, z_ref[e, :, Z_KPE:Z_QM].astype(F32))
    kvb = kv.astype(BF16)
    _mla_heads(lambda h: z_ref[e, :, Z_QM + LANE * h:Z_QM + LANE * (h + 1)], lambda h: keys[h],
               lambda h: kvb[:, LANE * h:LANE * (h + 1)], lambda p: gate(Z_BG + p * LANE),
               functools.partial(store, ob_ref))
    for pair in range(NAT_HEADS // 2):
        c = pair * LANE
        q2 = _stack_heads([z_ref[e, :, Z_DQ + c:Z_DQ + c + LANE]])
        o2 = _softmax_pv([_dot_nt(q2, z_ref[e, :, Z_DK + c:Z_DK + c + LANE])], [z_ref[e, :, Z_DV + c:Z_DV + c + LANE]])
        store(od_ref, pair, (_unstack_pair(o2, 0, m) * gate(Z_DG + c)).astype(BF16))


def _ctx_attn_rider(z, sink, wkv, layer, nt):
    def body(sink_ref, z_ref, wkv_ref, oa_ref, ob_ref, od_ref):
        for e in range(CTX_BPS):
            _ctx_attn_body(sink_ref, z_ref, wkv_ref, oa_ref, ob_ref, od_ref, e, layer)

    step = lambda b, t: (b * nt + t, 0, 0)
    return _Rider(
        body=body,
        in_specs=[pl.BlockSpec(memory_space=pltpu.SMEM), pl.BlockSpec((CTX_BPS, SEQ, Z_W), step),
                  _layer_spec((MLA_KV_LORA, 512), layer)],
        args=[sink, z, wkv],
        out_shape=[jax.ShapeDtypeStruct((BATCH, SEQ, BRANCH_W), BF16)] * 3,
        out_specs=[pl.BlockSpec((CTX_BPS, SEQ, BRANCH_W), step)] * 3,
        scratch=[])


def _win_attn_kernel(sink_ref, q_ref, kv_ref, g_ref, ck_ref, cv_ref, o_ref, ck_s, cv_s, *, layer):
    nb = DEC_SEQ // Q_BLOCK
    m = Q_BLOCK

    @pl.when(pl.program_id(1) == 0)
    def _():
        ck_s[...] = ck_ref[0, 0].astype(BF16)
        cv_s[...] = cv_ref[0, 0].astype(BF16)

    sink = _sink_column(sink_ref, layer, m)
    for qb in range(WIN_BPS):
        n = pl.program_id(1) * WIN_BPS + qb
        rows = slice(qb * m, (qb + 1) * m)
        blocks = []
        for off in (-1, 0, 1):
            start = pl.multiple_of(jnp.clip(n + off, 0, nb - 1) * m, m)
            blocks.append(kv_ref[0, pl.ds(start, m), :])
        kvw = jnp.concatenate(blocks, axis=0)
        shape = (A_HEADS * m, 3 * m)
        qpos = n * m + (lax.broadcasted_iota(jnp.int32, shape, 0) & (m - 1))
        kpos = (n - 1) * m + lax.broadcasted_iota(jnp.int32, shape, 1)
        mask = (jnp.abs(qpos - kpos) <= A_WINDOW) & (kpos >= 0) & (kpos < DEC_SEQ)
        q4 = _stack_heads([q_ref[0, rows, 0:LANE], q_ref[0, rows, LANE:2 * LANE]])
        s_win = jnp.where(mask, _dot_nt(q4, kvw[:, 0:LANE]), NEG)
        s_ctx = _dot_nt(q4, ck_s[...])
        o4 = _softmax_pv([s_win, s_ctx], [kvw[:, LANE:2 * LANE], cv_s[...]], sink=sink)
        for pair in range(2):
            g = g_ref[0, rows, pair * LANE:(pair + 1) * LANE].astype(F32)
            o_ref[0, rows, pair * LANE:(pair + 1) * LANE] = (_unstack_pair(o4, pair, m) * _silu(g)).astype(BF16)


def _win_attn_call(z, sink, cache_k, cache_v, layer):
    kvw = A_KV_HEADS * HEAD_DIM
    qrows = WIN_BPS * Q_BLOCK
    return pl.pallas_call(
        functools.partial(_win_attn_kernel, layer=layer),
        out_shape=jax.ShapeDtypeStruct((DEC_BATCH, DEC_SEQ, BRANCH_W), BF16),
        grid=(DEC_BATCH, DEC_SEQ // qrows),
        in_specs=[
            pl.BlockSpec(memory_space=pltpu.SMEM),
            pl.BlockSpec((1, qrows, 256), lambda b, n: (b, n, Z_AQ // 256)),
            pl.BlockSpec((1, DEC_SEQ, 256), lambda b, n: (b, 0, Z_AK // 256)),
            pl.BlockSpec((1, qrows, 256), lambda b, n: (b, n, Z_AG // 256)),
            pl.BlockSpec((1, 1, PAST_LEN, kvw), lambda b, n: (b, layer, 0, 0)),
            pl.BlockSpec((1, 1, PAST_LEN, kvw), lambda b, n: (b, layer, 0, 0)),
        ],
        out_specs=pl.BlockSpec((1, qrows, BRANCH_W), lambda b, n: (b, n, 0)),
        scratch_shapes=[pltpu.VMEM((PAST_LEN, kvw), BF16), pltpu.VMEM((PAST_LEN, kvw), BF16)],
        compiler_params=_cparams(("arbitrary", "arbitrary"), 40),
        name="lat_window_attn",
    )(sink, z, z, z, cache_k, cache_v)


def _nat_kernel(q_ref, k_ref, v_ref, g_ref, ck_ref, cv_ref, ring_ref, o_ref, ck_s, cv_s, bias_s):
    nrows = DEC_SEQ // GRID_W
    kr = NAT_ROWS
    m = NAT_RPS * GRID_W
    step = pl.program_id(1)

    @pl.when((pl.program_id(0) == 0) & (step == 0))
    def _():
        shape = (GRID_W, LANE)
        lane = lax.broadcasted_iota(jnp.int32, shape, 1)
        qcol = lax.broadcasted_iota(jnp.int32, shape, 0)
        rel = (lane & (GRID_W - 1)) - jnp.clip(qcol - NAT_COLS // 2, 0, GRID_W - NAT_COLS)
        col_ok = (rel >= 0) & (rel < NAT_COLS)
        for h in range(NAT_HEADS):
            for i in range(NAT_BIAS_ROWS):
                a = pltpu.roll(jnp.broadcast_to(ring_ref[h, i:i + 1, :], shape), 0, 1, stride=1, stride_axis=0)
                b = pltpu.roll(jnp.broadcast_to(ring_ref[h, i + 1:i + 2, :], shape), GRID_W, 1, stride=1,
                               stride_axis=0)
                bias_s[h, i] = jnp.where(col_ok, jnp.where(lane < GRID_W, a, b), NEG)

    @pl.when(step == 0)
    def _():
        ck_s[...] = ck_ref[0, 0].astype(BF16)
        cv_s[...] = cv_ref[0, 0].astype(BF16)

    ws = jnp.clip(step * NAT_RPS - kr // 2, 0, nrows - NAT_UNION)
    start = pl.multiple_of(ws * GRID_W, GRID_W)
    kl = k_ref[0, pl.ds(start, NAT_UNION * GRID_W), :]
    vl = v_ref[0, pl.ds(start, NAT_UNION * GRID_W), :]
    key_row = lax.broadcasted_iota(jnp.int32, (1, LANE), 1) // GRID_W
    for pair in range(NAT_HEADS // 2):
        c = pair * LANE
        blocks = []
        for hh in range(2):
            for rr in range(NAT_RPS):
                r = step * NAT_RPS + rr
                lo = jnp.clip(r - kr // 2, 0, nrows - kr) - ws
                ro = ws - r + (NAT_ROWS - 1) + NAT_BIAS_PAD
                row = []
                for t in range(NAT_UNION // 2):
                    kj = key_row + 2 * t
                    row.append(jnp.where((kj >= lo) & (kj < lo + kr), bias_s[2 * pair + hh, ro + 2 * t], NEG))
                blocks.append(jnp.concatenate(row, axis=1))
        bias = jnp.concatenate(blocks, axis=0)
        q2 = _stack_heads([q_ref[0, :, c:c + LANE]])
        s_lat = _dot_nt(q2, kl[:, c:c + LANE]) + bias
        s_ctx = _dot_nt(q2, ck_s[:, c:c + LANE])
        o2 = _softmax_pv([s_lat, s_ctx], [vl[:, c:c + LANE], cv_s[:, c:c + LANE]])
        g = g_ref[0, :, c:c + LANE].astype(F32)
        o_ref[0, :, c:c + LANE] = (_unstack_pair(o2, 0, m) * _silu(g)).astype(BF16)


def _nat_bias_rings(rpb):
    nc = NAT_COLS - 1
    rep = lambda a, n: jnp.broadcast_to(a, a.shape[:-1] + (n,))
    ring = jnp.concatenate([rpb[..., nc:], rep(rpb[..., -1:], GRID_W - 1 - nc), rep(rpb[..., :1], GRID_W - nc),
                            rpb[..., :nc]], axis=-1)
    neg = lambda n: jnp.full(ring.shape[:2] + (n, 2 * GRID_W), NEG, F32)
    return jnp.concatenate([neg(NAT_BIAS_PAD), ring, neg(NAT_BIAS_PAD + 1)], axis=2)


def _nat_call(z, cache_k, cache_v, bias, layer):
    hw = NAT_HEADS * HEAD_DIM
    qrows = NAT_RPS * GRID_W
    return pl.pallas_call(
        _nat_kernel,
        out_shape=jax.ShapeDtypeStruct((DEC_BATCH, DEC_SEQ, BRANCH_W), BF16),
        grid=(DEC_BATCH, DEC_SEQ // qrows),
        in_specs=[
            pl.BlockSpec((1, qrows, 256), lambda b, r: (b, r, Z_DQ // 256)),
            pl.BlockSpec((1, DEC_SEQ, 256), lambda b, r: (b, 0, Z_DK // 256)),
            pl.BlockSpec((1, DEC_SEQ, 256), lambda b, r: (b, 0, Z_DV // 256)),
            pl.BlockSpec((1, qrows, 256), lambda b, r: (b, r, Z_DG // 256)),
            pl.BlockSpec((1, 1, PAST_LEN, hw), lambda b, r: (b, layer, 0, 0)),
            pl.BlockSpec((1, 1, PAST_LEN, hw), lambda b, r: (b, layer, 0, 0)),
            _layer_spec(bias.shape[1:], layer),
        ],
        out_specs=pl.BlockSpec((1, qrows, BRANCH_W), lambda b, r: (b, r, 0)),
        scratch_shapes=[pltpu.VMEM((PAST_LEN, hw), BF16), pltpu.VMEM((PAST_LEN, hw), BF16),
                        pltpu.VMEM((NAT_HEADS, NAT_BIAS_ROWS, GRID_W, LANE), F32)],
        compiler_params=_cparams(("arbitrary", "arbitrary"), 40),
        name="lat_nat_attn",
    )(z, z, z, z, cache_k, cache_v, bias)


def _mla_body(q_ref, ckv_ref, kpe_ref, g_ref, cckv_ref, ckpe_ref, wkv_ref, o_ref, k_s, v_s):
    nlat = DEC_SEQ
    step = pl.program_id(0) * pl.num_programs(1) + pl.program_id(1)

    @pl.when(step % MLA_TILES == 0)
    def _():
        rows = 512

        def fill(r0, ckv, kpe_blk):
            kv = _dot(ckv, wkv_ref[...])
            for h, kh in enumerate(_mla_keys(kv, kpe_blk)):
                k_s[h, r0:r0 + rows, :] = kh
                v_s[h, r0:r0 + rows, :] = kv[:, h * LANE:(h + 1) * LANE].astype(BF16)

        for c in range(nlat // rows):
            fill(c * rows, ckv_ref[0, c * rows:(c + 1) * rows, :], kpe_ref[0, c * rows:(c + 1) * rows, :].astype(F32))
        ckpe = ckpe_ref[0, 0]
        ckpe_blk = jnp.concatenate([jnp.zeros((PAST_LEN, KPE_LANE), F32), ckpe,
                                    jnp.zeros((PAST_LEN, LANE - KPE_LANE - MLA_ROPE), F32)], axis=1)
        fill(nlat, cckv_ref[0, 0].astype(BF16), ckpe_blk)

    def store(pair, val):
        o_ref[0, :, pair * LANE:(pair + 1) * LANE] = val

    _mla_heads(lambda h: q_ref[0, :, LANE * h:LANE * (h + 1)], lambda h: k_s[h], lambda h: v_s[h],
               lambda p: _silu(g_ref[0, :, p * LANE:(p + 1) * LANE].astype(F32)), store)


def _mla_rider(z, cache_ckv, cache_kpe, wkv, layer, nt):
    nkeys = DEC_SEQ + PAST_LEN
    assert DEC_BATCH * MLA_TILES == POST_STEPS
    bt = lambda b, t: ((b * nt + t) // MLA_TILES, (b * nt + t) % MLA_TILES)
    tile = lambda col: (lambda b, t: bt(b, t) + (col,))
    whole = lambda col: (lambda b, t: (bt(b, t)[0], 0, col))
    cache = lambda b, t: (bt(b, t)[0], layer, 0, 0)
    return _Rider(
        body=_mla_body,
        in_specs=[
            pl.BlockSpec((1, TQ_MLA, MLA_HEADS * LANE), tile(Z_QM // (MLA_HEADS * LANE))),
            pl.BlockSpec((1, DEC_SEQ, LANE), whole(Z_CKV // LANE)),
            pl.BlockSpec((1, DEC_SEQ, LANE), whole(Z_KPE // LANE)),
            pl.BlockSpec((1, TQ_MLA, 256), tile(Z_BG // 256)),
            pl.BlockSpec((1, 1, PAST_LEN, MLA_KV_LORA), cache),
            pl.BlockSpec((1, 1, PAST_LEN, MLA_ROPE), cache),
            _layer_spec((MLA_KV_LORA, 512), layer),
        ],
        args=[z, z, z, z, cache_ckv, cache_kpe, wkv],
        out_shape=[jax.ShapeDtypeStruct((DEC_BATCH, DEC_SEQ, BRANCH_W), BF16)],
        out_specs=[pl.BlockSpec((1, TQ_MLA, BRANCH_W), tile(0))],
        scratch=[pltpu.VMEM((MLA_HEADS, nkeys, LANE), BF16), pltpu.VMEM((MLA_HEADS, nkeys, LANE), BF16)])


def _s5_mats(lam_re, lam_im, log_dt, b_re, b_im, c_re, c_im):
    t = S5_T
    hp = lax.Precision.HIGHEST
    lre = jnp.minimum(lam_re, -1e-4)
    lim = lam_im
    dt = jnp.exp(log_dt)[..., None]
    er, ei = lre * dt, lim * dt
    mag = jnp.exp(er)
    are, aim = mag * jnp.cos(ei), mag * jnp.sin(ei)
    den = lre * lre + lim * lim
    qre = ((are - 1.0) * lre + aim * lim) / den
    qim = (aim * lre - (are - 1.0) * lim) / den
    bbr = qre[..., None] * b_re - qim[..., None] * b_im
    bbi = qre[..., None] * b_im + qim[..., None] * b_re
    n = jnp.arange(t + 1, dtype=F32)[:, None, None, None]
    pmag = jnp.exp(er[None] * n)
    pr, pi = pmag * jnp.cos(ei[None] * n), pmag * jnp.sin(ei[None] * n)

    def c_times_pow(d, pw_r, pw_i):
        rep = lambda a: jnp.repeat(jnp.transpose(a, (1, 2, 0)), S5_CH, axis=-1)
        til = lambda a: jnp.tile(jnp.transpose(a, (0, 2, 1)), (1, 1, t))
        cr, ci, ar, ai = til(c_re[d]), til(c_im[d]), rep(pw_r), rep(pw_i)
        return cr * ar - ci * ai, cr * ai + ci * ar

    def lag_kernels(d, wr, wi):
        return (jnp.einsum('gpk,gpx->gkx', bbr[d], wr, precision=hp)
                - jnp.einsum('gpk,gpx->gkx', bbi[d], wi, precision=hp))

    wf0 = c_times_pow(0, pr[:t, 0], pi[:t, 0])
    wf1 = c_times_pow(0, pr[1:, 0], pi[1:, 0])
    wb = c_times_pow(1, pr[1:, 1][::-1], pi[1:, 1][::-1])
    kb0 = (jnp.einsum('gpk,gcp->gkc', bbr[1], c_re[1], precision=hp)
           - jnp.einsum('gpk,gcp->gkc', bbi[1], c_im[1], precision=hp))
    kf = lag_kernels(0, *wf0) + jnp.pad(kb0, ((0, 0), (0, 0), (0, S5_TC - S5_CH)))
    kb = lag_kernels(1, *wb) * jnp.asarray(np.arange(S5_TC) >= S5_CH, F32)
    ring = jnp.concatenate([kf, kb], axis=-1)

    def state_cols(pw_r, pw_i, d):
        rep = lambda a: jnp.repeat(jnp.transpose(a, (1, 0, 2)), S5_CH, axis=1)
        til = lambda a: jnp.tile(jnp.transpose(a, (0, 2, 1)), (1, t, 1))
        ar, ai, br, bi = rep(pw_r), rep(pw_i), til(bbr[d]), til(bbi[d])
        return ar * br - ai * bi, ar * bi + ai * br

    sfr, sfi = state_cols(pr[:t, 0][::-1], pi[:t, 0][::-1], 0)
    sbr, sbi = state_cols(pr[:t, 1], pi[:t, 1], 1)
    m_state = jnp.concatenate([sfr, sbr, sfi, sbi], axis=-1)
    m_out = jnp.concatenate([wf1[0], wb[0], -wf1[1], -wb[1]], axis=1)

    a_t = jnp.stack([jnp.concatenate([pr[t, 0], pr[t, 1]], axis=-1),
                     jnp.concatenate([pi[t, 0], pi[t, 1]], axis=-1)], axis=1)
    return ring, m_state.astype(BF16), m_out.astype(BF16), a_t


def _s5_kernel(uc_ref, ul_ref, ring_ref, ms_ref, mo_ref, at_ref, h0_ref, yc_ref, yl_ref, hend_ref,
               s_s, hf_s, hb_s, m_s):
    p = S5_STATE
    u = jnp.concatenate([uc_ref[0], ul_ref[0]], axis=0)
    s = _dot(u, ms_ref[0])
    s_s[0] = s[:, 0:2 * p]
    s_s[1] = s[:, 2 * p:4 * p]
    are = at_ref[0, 0:1, :]
    aim = at_ref[0, 1:2, :]
    fwd_lane = lax.broadcasted_iota(jnp.int32, (1, 2 * p), 1) < p

    for k in range(S5_CH):
        rk = jnp.broadcast_to(ring_ref[0, k:k + 1, :], (S5_T, 2 * S5_TC))
        rolled = pltpu.roll(rk, 0, 1, stride=S5_CH, stride_axis=0)
        for blk in range(S5_TC // LANE):
            m_s[blk, pl.ds(k, S5_T, stride=S5_CH), :] = rolled[:, blk * LANE:(blk + 1) * LANE]

    def scan(row0, nk, nb, hre, him):
        for st in range(nk):
            rf = pl.ds(row0 + st, nb, stride=nk)
            rb = pl.ds(row0 + nk - 1 - st, nb, stride=nk)
            hf_s[0, rf, :] = hre
            hf_s[1, rf, :] = him
            hb_s[0, rb, :] = hre
            hb_s[1, rb, :] = him
            sre = jnp.where(fwd_lane, s_s[0, rf, :], s_s[0, rb, :])
            sim = jnp.where(fwd_lane, s_s[1, rf, :], s_s[1, rb, :])
            hre, him = are * hre - aim * him + sre, are * him + aim * hre + sim
        return hre, him

    zero = jnp.zeros((BATCH, 2 * p), F32)
    hre, him = scan(0, S5_NK_CTX, BATCH, zero, zero)
    hend_ref[0] = jnp.concatenate([hre, him], axis=1)
    scan(S5_ROWS, S5_NK_LAT, DEC_BATCH, h0_ref[0, :, 0:2 * p], h0_ref[0, :, 2 * p:4 * p])
    hst = jnp.concatenate([jnp.where(fwd_lane, hf_s[0], hb_s[0]), jnp.where(fwd_lane, hf_s[1], hb_s[1])], axis=1)
    m_intra = jnp.concatenate([m_s[blk] for blk in range(S5_TC // LANE)], axis=1).astype(BF16)
    y = _dot(u, m_intra) + _dot(hst.astype(BF16), mo_ref[0])
    yc_ref[0] = y[0:S5_ROWS]
    yl_ref[0] = y[S5_ROWS:2 * S5_ROWS]


def _s5_call(ut_ctx, ut_lat, ring, m_state, m_out, a_t, h0, layer):
    g3 = lambda g: (g, 0, 0)
    lg = lambda shape: pl.BlockSpec((None, 1) + shape, lambda g: (layer, g, 0, 0))
    yshape = jax.ShapeDtypeStruct((S5_GROUPS, S5_ROWS, S5_TC), F32)
    uspec = pl.BlockSpec((1, S5_ROWS, S5_TC), g3)
    return pl.pallas_call(
        _s5_kernel,
        out_shape=(yshape, yshape, jax.ShapeDtypeStruct((S5_GROUPS, BATCH, 4 * S5_STATE), F32)),
        grid=(S5_GROUPS,),
        in_specs=[
            uspec, uspec,
            lg((S5_CH, 2 * S5_TC)),
            lg((S5_TC, 4 * S5_STATE)),
            lg((4 * S5_STATE, S5_TC)),
            lg((2, 2 * S5_STATE)),
            lg((DEC_BATCH, 4 * S5_STATE)),
        ],
        out_specs=(uspec, uspec, pl.BlockSpec((1, BATCH, 4 * S5_STATE), g3)),
        scratch_shapes=[pltpu.VMEM((2, 2 * S5_ROWS, LANE), F32)] * 3 + [pltpu.VMEM((S5_TC // LANE, S5_TC, LANE), F32)],
        compiler_params=_cparams(("arbitrary",), 32),
        name="s5_chunks",
    )(ut_ctx, ut_lat, ring, m_state, m_out, a_t, h0)


N_POST_IN = 15


class _Rider(NamedTuple):
    body: Callable
    in_specs: list
    args: list
    out_shape: list
    out_specs: list
    scratch: list


NO_RIDER = _Rider(body=lambda: None, in_specs=[], args=[], out_shape=[], out_specs=[], scratch=[])


def _post_kernel(*refs, final, rider):
    (x_ref, mod_ref, ng_ref, ba_ref, bb_ref, bd_ref, y_ref, u_ref, cg_ref, sd_ref, wglu_ref,
     wm_ref, wbr_ref, wo_ref, fg_ref) = refs[:N_POST_IN]
    n_in, n_out = len(rider.in_specs), len(rider.out_specs)
    o_ref = refs[N_POST_IN + n_in]
    ys_s = refs[N_POST_IN + n_in + 1 + n_out]
    rider.body(*refs[N_POST_IN:N_POST_IN + n_in], *refs[N_POST_IN + n_in + 1:N_POST_IN + n_in + 1 + n_out],
               *refs[N_POST_IN + n_in + 2 + n_out:])
    x = x_ref[0]
    hb = _ada_h(x, mod_ref, ng_ref).astype(BF16)
    for m in range(S5_T // 8):
        for hf in range(2):
            vs = _slot_transpose([y_ref[8 * hf + gp, :, m * LANE:(m + 1) * LANE] for gp in range(8)])
            for s in range(8):
                ys_s[hf, pl.ds(8 * m + s, TM_CHUNKS, stride=S5_T), :] = vs[s]
    yv = jnp.concatenate([ys_s[0], ys_s[1]], axis=1) + sd_ref[...] * u_ref[0].astype(F32)
    gel = 0.5 * yv * (1.0 + jnp.tanh(math.sqrt(2.0 / math.pi) * (yv + 0.044715 * (yv * yv * yv))))
    gl = _dot(gel.astype(BF16), wglu_ref[...])
    oc = gl[:, 0:BRANCH_W] * _sigmoid(gl[:, BRANCH_W:2 * BRANCH_W])
    bc = (oc * _silu(cg_ref[0].astype(F32))).astype(BF16)
    acc = None
    for k, br in enumerate((ba_ref[0], bb_ref[0], bc, bd_ref[0])):
        proj = _dot(br, wbr_ref[k])
        mg = _sigmoid(_dot(hb, wm_ref[:, k * D_MODEL:(k + 1) * D_MODEL]))
        acc = mg * proj if acc is None else acc + mg * proj
    y = _dot(acc.astype(BF16), wo_ref[...])
    out = x + mod_ref[0, :, 2 * D_MODEL:3 * D_MODEL] * y
    if final:
        out = _rms(out, fg_ref[...])
    o_ref[0] = out


def _post_call(x, mods, mod_row0, p, ba, bb, bd, y_s5, z, fg, final, layer, name, rider):
    nb, length, _ = x.shape
    nt = length // TM
    assert nb * nt == POST_STEPS
    tok = lambda b, t: (b, t, 0)
    br_spec = pl.BlockSpec((1, TM, BRANCH_W), tok)
    return pl.pallas_call(
        functools.partial(_post_kernel, final=final, rider=rider),
        out_shape=tuple([jax.ShapeDtypeStruct((nb, length, D_MODEL), F32)] + rider.out_shape),
        grid=(nb, nt),
        in_specs=[
            pl.BlockSpec((1, TM, D_MODEL), tok),
            _mod_spec(layer, mod_row0),
            _layer_spec((1, D_MODEL), layer),
            br_spec, br_spec, br_spec,
            pl.BlockSpec((S5_GROUPS, TM_CHUNKS, S5_TC), lambda b, t: (0, b * nt + t, 0)),
            pl.BlockSpec((1, TM, 256), lambda b, t: (b, t, Z_CU // 256)),
            pl.BlockSpec((1, TM, 256), lambda b, t: (b, t, Z_CG // 256)),
            _layer_spec((1, BRANCH_W), layer),
            _layer_spec((BRANCH_W, 2 * BRANCH_W), layer),
            _layer_spec((D_MODEL, N_BRANCH * D_MODEL), layer),
            _layer_spec((N_BRANCH, BRANCH_W, D_MODEL), layer),
            _layer_spec((D_MODEL, D_MODEL), layer),
            pl.BlockSpec((1, D_MODEL), lambda b, t: (0, 0)),
        ] + rider.in_specs,
        out_specs=tuple([pl.BlockSpec((1, TM, D_MODEL), tok)] + rider.out_specs),
        scratch_shapes=[pltpu.VMEM((2, TM, LANE), F32)] + rider.scratch,
        compiler_params=_cparams(("arbitrary", "arbitrary"), 56),
        name=name,
    )(x, mods, p['ng'], ba, bb, bd, y_s5, z, z, p['s5d'], p['wglu'], p['wm'], p['wbr'], p['wo'], fg, *rider.args)


def _rope_tables():
    t = np.arange(DEC_SEQ)
    row = (t // GRID_W).astype(np.float64)
    col = (t % GRID_W).astype(np.float64)

    def pattern(half):
        inv = ROPE_BASE ** (-np.arange(half, dtype=np.float64) / half)
        zeros = np.zeros((DEC_SEQ, half))
        cs, s_up, s_lo = [], [], []
        for pos in (row, col):
            ang = pos[:, None] * inv[None, :]
            c, s = np.cos(ang), np.sin(ang)
            cs += [c, c]
            s_up += [zeros, s]
            s_lo += [-s, zeros]
        return [np.concatenate(parts, axis=1) for parts in (cs, s_up, s_lo)]

    tab_a = np.stack([np.tile(part, (1, LANE // HEAD_DIM)) for part in pattern(HEAD_DIM // 4)])
    ident = [np.ones, np.zeros, np.zeros]
    tab_m = np.stack([np.concatenate([fill((DEC_SEQ, KPE_LANE)), part,
                                      fill((DEC_SEQ, LANE - KPE_LANE - MLA_ROPE))], axis=1)
                      for fill, part in zip(ident, pattern(MLA_ROPE // 4))])
    return jnp.asarray(tab_a, F32), jnp.asarray(tab_m, F32)


_W_IN_MOVES = tuple(
    [(HEAD_DIM * i, HEAD_DIM * h, HEAD_DIM) for i, h in enumerate(A_HEAD_ORDER)]
    + [(256, 256, 256), (512, 1952, 768), (R_CQ, 768, 384), (R_KPE + KPE_LANE, 1152, MLA_ROPE), (R_CU, 1440, 256)]
    + [(R_G + HEAD_DIM * i, 512 + HEAD_DIM * h, HEAD_DIM) for i, h in enumerate(A_HEAD_ORDER)]
    + [(R_G + 256, 1184, 256), (R_G + 512, 1696, 256), (R_G + 768, 2720, 256)])
W_IN_ROWS = 256


def _w_in_kernel(w_ref, o_ref):
    o_ref[0, :, R_KPE:R_CU] = jnp.zeros((W_IN_ROWS, LANE), BF16)
    for dst, src, width in _W_IN_MOVES:
        o_ref[0, :, dst:dst + width] = w_ref[0, :, src:src + width].astype(BF16)


def _w_in_call(w_in):
    return pl.pallas_call(
        _w_in_kernel,
        out_shape=jax.ShapeDtypeStruct((DEPTH, D_MODEL, R_W), BF16),
        grid=(DEPTH, D_MODEL // W_IN_ROWS),
        in_specs=[pl.BlockSpec((1, W_IN_ROWS, w_in.shape[-1]), lambda i, r: (i, r, 0))],
        out_specs=pl.BlockSpec((1, W_IN_ROWS, R_W), lambda i, r: (i, r, 0)),
        compiler_params=_cparams(("arbitrary", "arbitrary"), 32),
        name="w_in_layout",
    )(w_in)


def _cast_kernel(w_ref, o_ref):
    o_ref[...] = w_ref[...].astype(BF16)


def _cast_call(w):
    _, rows, cols = w.shape
    spec = pl.BlockSpec((1, W_IN_ROWS, cols), lambda i, r: (i, r, 0))
    return pl.pallas_call(
        _cast_kernel,
        out_shape=jax.ShapeDtypeStruct(w.shape, BF16),
        grid=(DEPTH, rows // W_IN_ROWS),
        in_specs=[spec],
        out_specs=spec,
        compiler_params=_cparams(("arbitrary", "arbitrary"), 32),
        name="w_cast",
    )(w)


def _reorder_w_q_up(w):
    w = w.reshape(MLA_Q_LORA, MLA_HEADS, MLA_NOPE + MLA_ROPE)
    w = jnp.concatenate([w, jnp.zeros((MLA_Q_LORA, MLA_HEADS, LANE - MLA_NOPE - MLA_ROPE), w.dtype)], axis=-1)
    return w.reshape(MLA_Q_LORA, MLA_HEADS * LANE)


def _reorder_w_branch(w):
    wa = w[0].reshape(A_HEADS, HEAD_DIM, D_MODEL)
    wa = jnp.concatenate([wa[h] for h in A_HEAD_ORDER], axis=0)
    return jnp.concatenate([wa[None], w[1:]], axis=0)


def kernel(x_prompt, x_sample, cache_a_k, cache_a_v, cache_mla_ckv, cache_mla_kpe, cache_na_k, cache_na_v,
           state_s5_re, state_s5_im, c, c_ctx, w_mod, b_mod, norm_g, w_in, w_merge, a_sink,
           mla_q_norm, mla_w_q_up, mla_kv_norm, mla_w_kv_up, s5_lam_re, s5_lam_im, s5_log_dt,
           s5_b_re, s5_b_im, s5_c_re, s5_c_im, s5_d, s5_w_glu, na_rpb, w_branch, w_out, final_norm_g):
    n_ctx = BATCH * SEQ
    conds = jnp.concatenate([c, c_ctx[None, :], jnp.zeros((3, D_MODEL), F32)], axis=0)
    mods = _mod_call(conds, w_mod, b_mod).reshape(DEPTH, 8, 1, 3 * D_MODEL)
    tabs = _rope_tables()
    ck_a = cache_a_k.reshape(DEC_BATCH, DEPTH, PAST_LEN, A_KV_HEADS * HEAD_DIM)
    cv_a = cache_a_v.reshape(DEC_BATCH, DEPTH, PAST_LEN, A_KV_HEADS * HEAD_DIM)
    ck_n = cache_na_k.reshape(DEC_BATCH, DEPTH, PAST_LEN, NAT_HEADS * HEAD_DIM)
    cv_n = cache_na_v.reshape(DEC_BATCH, DEPTH, PAST_LEN, NAT_HEADS * HEAD_DIM)
    fg = final_norm_g.reshape(1, D_MODEL)
    caches = [jnp.zeros((BATCH, DEPTH, SEQ, w), F32) for w in (128, 128, MLA_KV_LORA, MLA_ROPE, 256, 256)]

    p = dict(
        ng=norm_g.reshape(DEPTH, 1, D_MODEL),
        w_raw=_w_in_call(w_in),
        qnorm=mla_q_norm.reshape(DEPTH, 1, MLA_Q_LORA),
        wq=jax.vmap(_reorder_w_q_up)(mla_w_q_up).astype(BF16),
        kvnorm=mla_kv_norm.reshape(DEPTH, 1, MLA_KV_LORA),
        wkv=mla_w_kv_up.astype(BF16),
        wm=_cast_call(w_merge),
        wbr=jax.vmap(_reorder_w_branch)(w_branch).astype(BF16),
        wo=_cast_call(w_out),
        wglu=s5_w_glu.astype(BF16),
        s5d=s5_d.reshape(DEPTH, 1, BRANCH_W),
    )
    ring, m_state, m_out, a_t = jax.vmap(_s5_mats)(s5_lam_re, s5_lam_im, s5_log_dt, s5_b_re, s5_b_im, s5_c_re, s5_c_im)
    nat_bias = _nat_bias_rings(na_rpb * LOG2E)
    h0 = jnp.concatenate([state_s5_re[:, :, 0], state_s5_re[:, :, 1], state_s5_im[:, :, 0], state_s5_im[:, :, 1]],
                         axis=-1)
    h0 = jnp.transpose(h0, (1, 2, 0, 3))

    yp = x_prompt.reshape(1, n_ctx, D_MODEL)
    ys = x_sample
    hends = []
    for i in range(DEPTH):
        res = _inproj_call(yp, mods, DEC_BATCH, p, None, caches, i, "inproj_ctx")
        z_ctx, ut_ctx, caches = res[0], res[1], list(res[2:])
        zc = z_ctx.reshape(BATCH, SEQ, Z_W)
        z_lat, ut_lat, oa_c, ob_c, od_c = _inproj_call(ys, mods, 0, p, tabs, None, i, "inproj_lat_ctx_attn",
                                                       _ctx_attn_rider(zc, a_sink, p['wkv'], i, DEC_SEQ // TM))

        y_ctx, y_lat, hend = _s5_call(ut_ctx, ut_lat, ring, m_state, m_out, a_t, h0, i)
        hends.append(hend)

        oa_l = _win_attn_call(z_lat, a_sink, ck_a, cv_a, i)
        od_l = _nat_call(z_lat, ck_n, cv_n, nat_bias, i)

        flat = lambda a: a.reshape(1, n_ctx, BRANCH_W)
        final = i == DEPTH - 1
        yp, ob_l = _post_call(yp, mods, DEC_BATCH, p, flat(oa_c), flat(ob_c), flat(od_c), y_ctx, z_ctx, fg, final, i,
                              "post_ctx_lat_mla", _mla_rider(z_lat, cache_mla_ckv, cache_mla_kpe, p['wkv'], i, POST_STEPS))
        ys, = _post_call(ys, mods, 0, p, oa_l, ob_l, od_l, y_lat, z_lat, fg, final, i, "post_lat", NO_RIDER)

    hend = jnp.stack(hends).reshape(DEPTH, S5_GROUPS, BATCH, 2, 2, S5_STATE)
    hend = jnp.transpose(hend, (3, 2, 0, 4, 1, 5))
    ak, av, ckv, kpe, nk, nv = caches
    heads = lambda a, h: a.reshape(BATCH, DEPTH, SEQ, h, HEAD_DIM)
    return (yp.reshape(BATCH, SEQ, D_MODEL), ys, heads(ak, A_KV_HEADS), heads(av, A_KV_HEADS), ckv, kpe,
            heads(nk, NAT_HEADS), heads(nv, NAT_HEADS), hend[0], hend[1])
```

```python
import functools
import math
from typing import Callable, NamedTuple

import numpy as np
import jax
import jax.numpy as jnp
from jax import lax
from jax.experimental import pallas as pl
from jax.experimental.pallas import tpu as pltpu

F32 = jnp.float32
BF16 = jnp.bfloat16

D_MODEL = 1024
BATCH = 32
SEQ = 256
DEPTH = 2
DEC_BATCH = 4
DEC_SEQ = 2048
PAST_LEN = 512
GRID_W = 64
HEAD_DIM = 64
BRANCH_W = 256
N_BRANCH = 4
Q_BLOCK = 128
A_HEADS = 4
A_KV_HEADS = 2
A_GROUP = A_HEADS // A_KV_HEADS
A_WINDOW = 128
MLA_HEADS = 4
MLA_Q_LORA = 256
MLA_KV_LORA = 128
MLA_NOPE = 64
MLA_ROPE = 32
MLA_V = 64
S5_CH = 16
S5_GROUPS = BRANCH_W // S5_CH
S5_STATE = 64
NAT_HEADS = 4
NAT_ROWS = 8
NAT_COLS = 16
ROPE_BASE = 10000.0
EPS = 1e-6
NEG = -1e30
LOG2E = 1.4426950408889634
Q_SCALE = HEAD_DIM ** -0.5 * LOG2E
MLA_Q_SCALE = (MLA_NOPE + MLA_ROPE) ** -0.5 * LOG2E

LANE = 128

R_AV, R_CQ, R_CKV, R_KPE, R_CU, R_G, R_W = 384, 1280, 1536, 1664, 1792, 2048, 3072
KPE_LANE = 64
A_HEAD_ORDER = (0, 2, 1, 3)
Z_AQ, Z_AK, Z_AV = 0, 256, 384
Z_DQ, Z_DK, Z_DV = 512, 768, 1024
Z_CKV, Z_KPE = 1280, 1408
Z_QM = 1536
Z_AG, Z_BG, Z_CG, Z_DG = 2048, 2304, 2560, 2816
Z_CU = 3072
Z_W = 3328

S5_T = 32
S5_TC = S5_T * S5_CH
S5_ROWS = 256
S5_NK_CTX = SEQ // S5_T
S5_NK_LAT = DEC_SEQ // S5_T

TM = 512
TM_CHUNKS = TM // S5_T
TQ_MLA = 512
MLA_TILES = DEC_SEQ // TQ_MLA
POST_STEPS = BATCH * SEQ // TM
CTX_BPS = 2
NAT_RPS = 4
NAT_UNION = NAT_ROWS + NAT_RPS
NAT_BIAS_PAD = NAT_RPS
NAT_BIAS_ROWS = 2 * NAT_ROWS - 1 + 2 * NAT_BIAS_PAD
WIN_BPS = 4


def _cparams(sem, vmem_mb):
    return pltpu.CompilerParams(dimension_semantics=sem, vmem_limit_bytes=vmem_mb * 1024 * 1024)


def _sigmoid(x):
    return 1.0 / (1.0 + jnp.exp(-x))


def _silu(x):
    return x * _sigmoid(x)


def _rms(x, g):
    return x * lax.rsqrt(jnp.mean(x * x, axis=-1, keepdims=True) + EPS) * g


def _dot(a, b):
    return jnp.dot(a, b, preferred_element_type=F32)


def _dot_nt(a, b):
    return lax.dot_general(a, b, (((1,), (1,)), ((), ())), preferred_element_type=F32)


def _slot_transpose(vs):
    lane = lax.broadcasted_iota(jnp.int32, vs[0].shape, 1)
    vs = list(vs)
    for d in (4, 2, 1):
        keep = (lane & (S5_CH * d)) == 0
        nxt = list(vs)
        for lo in range(8):
            if lo & d:
                continue
            hi = lo + d
            nxt[lo] = jnp.where(keep, vs[lo], pltpu.roll(vs[hi], S5_CH * d, 1))
            nxt[hi] = jnp.where(keep, pltpu.roll(vs[lo], LANE - S5_CH * d, 1), vs[hi])
        vs = nxt
    return vs


def _mod_kernel(c_ref, w_ref, b_ref, o_ref):
    s = _silu(c_ref[...])
    o_ref[0] = _dot(s.astype(BF16), w_ref[0].astype(BF16)) + b_ref[0]


def _mod_call(conds, w_mod, b_mod):
    nc = 512
    return pl.pallas_call(
        _mod_kernel,
        out_shape=jax.ShapeDtypeStruct((DEPTH, 8, 3 * D_MODEL), F32),
        grid=(DEPTH, 3 * D_MODEL // nc),
        in_specs=[
            pl.BlockSpec((8, D_MODEL), lambda i, j: (0, 0)),
            pl.BlockSpec((1, D_MODEL, nc), lambda i, j: (i, 0, j)),
            pl.BlockSpec((1, 1, nc), lambda i, j: (i, 0, j)),
        ],
        out_specs=pl.BlockSpec((1, 8, nc), lambda i, j: (i, 0, j)),
        compiler_params=_cparams(("arbitrary", "arbitrary"), 32),
        name="mod_rows",
    )(conds, w_mod, b_mod.reshape(DEPTH, 1, 3 * D_MODEL))


def _ada_h(x, mod_ref, ng_ref):
    shift = mod_ref[0, :, 0:D_MODEL]
    scale = mod_ref[0, :, D_MODEL:2 * D_MODEL]
    return _rms(x, ng_ref[...]) * (1.0 + scale) + shift


def _rope_block(xs, tab_ref, shift):
    return (xs * tab_ref[0] + pltpu.roll(xs, shift, 1) * tab_ref[1]
            + pltpu.roll(xs, LANE - shift, 1) * tab_ref[2])


N_CACHE = 6


def _inproj_kernel(*refs, rope, rider):
    x_ref, mod_ref, ng_ref, w_ref, qn_ref, wq_ref, kvn_ref = refs[:7]
    if rope:
        ta_ref, tm_ref = refs[7:9]
        n_in, n_out = len(rider.in_specs), len(rider.out_specs)
        z_ref, ut_ref = refs[9 + n_in:11 + n_in]
        u_s = refs[11 + n_in + n_out]
        rider.body(*refs[9:9 + n_in], *refs[11 + n_in:11 + n_in + n_out], *refs[12 + n_in + n_out:])
    else:
        z_ref, ut_ref, ak_ref, av_ref, ckv_ref, kpe_ref, nk_ref, nv_ref, u_s = refs[7 + N_CACHE:]
    h = _ada_h(x_ref[0], mod_ref, ng_ref)
    raw = _dot(h.astype(BF16), w_ref[...])
    for j in range(R_AV // LANE):
        blk = raw[:, j * LANE:(j + 1) * LANE]
        if j < Z_AK // LANE:
            blk = blk * Q_SCALE
        if rope:
            blk = _rope_block(blk, ta_ref, 16)
        z_ref[0, :, j * LANE:(j + 1) * LANE] = blk.astype(BF16)
    z_ref[0, :, Z_AV:Z_DQ] = raw[:, Z_AV:Z_DQ].astype(BF16)
    z_ref[0, :, Z_DQ:Z_DK] = (raw[:, Z_DQ:Z_DK] * Q_SCALE).astype(BF16)
    z_ref[0, :, Z_DK:R_CQ] = raw[:, Z_DK:R_CQ].astype(BF16)
    qn = _rms(raw[:, R_CQ:R_CKV], qn_ref[...])
    q = _dot(qn.astype(BF16), wq_ref[...]) * MLA_Q_SCALE
    kp = raw[:, R_KPE:R_CU]
    ckv = _rms(raw[:, R_CKV:R_KPE], kvn_ref[...])
    for hh in range(MLA_HEADS):
        qh = q[:, hh * LANE:(hh + 1) * LANE]
        if rope:
            qh = _rope_block(qh, tm_ref, 8)
        z_ref[0, :, Z_QM + hh * LANE:Z_QM + (hh + 1) * LANE] = qh.astype(BF16)
    z_ref[0, :, Z_CKV:Z_KPE] = ckv.astype(BF16)
    z_ref[0, :, Z_KPE:Z_QM] = (_rope_block(kp, tm_ref, 8) if rope else kp).astype(BF16)
    z_ref[0, :, Z_AG:Z_CU] = raw[:, R_G:R_W].astype(BF16)
    z_ref[0, :, Z_CU:Z_W] = raw[:, R_CU:R_G].astype(BF16)
    if not rope:
        half = TM // 2
        for e in range(2):
            rows = slice(e * half, (e + 1) * half)
            ak_ref[e, 0] = raw[rows, Z_AK:Z_AV]
            av_ref[e, 0] = raw[rows, Z_AV:Z_DQ]
            nk_ref[e, 0] = raw[rows, Z_DK:Z_DV]
            nv_ref[e, 0] = raw[rows, Z_DV:R_CQ]
            ckv_ref[e, 0] = ckv[rows]
            kpe_ref[e, 0] = kp[rows, KPE_LANE:KPE_LANE + MLA_ROPE]
    for hf in range(2):
        u_s[hf] = raw[:, R_CU + hf * LANE:R_CU + (hf + 1) * LANE]
    for m in range(S5_T // 8):
        for hf in range(2):
            outs = _slot_transpose([u_s[hf, pl.ds(8 * m + s, TM_CHUNKS, stride=S5_T), :] for s in range(8)])
            for gp in range(8):
                ut_ref[8 * hf + gp, :, m * LANE:(m + 1) * LANE] = outs[gp].astype(BF16)


def _layer_spec(shape, layer):
    zeros = (0,) * len(shape)
    return pl.BlockSpec((None,) + tuple(shape), lambda *_: (layer,) + zeros)


def _mod_spec(layer, row0):
    return pl.BlockSpec((None, 1, 1, 3 * D_MODEL), lambda b, t: (layer, row0 + b, 0, 0))


def _inproj_call(x, mods, mod_row0, p, tabs, caches, layer, name, rider=None):
    nb, length, _ = x.shape
    rope = tabs is not None
    nt = length // TM
    in_specs = [
        pl.BlockSpec((1, TM, D_MODEL), lambda b, t: (b, t, 0)),
        _mod_spec(layer, mod_row0),
        _layer_spec((1, D_MODEL), layer),
        _layer_spec((D_MODEL, R_W), layer),
        _layer_spec((1, MLA_Q_LORA), layer),
        _layer_spec((MLA_Q_LORA, MLA_HEADS * LANE), layer),
        _layer_spec((1, MLA_KV_LORA), layer),
    ]
    args = [x, mods, p['ng'], p['w_raw'], p['qnorm'], p['wq'], p['kvnorm']]
    out_shape = [jax.ShapeDtypeStruct((nb, length, Z_W), BF16),
                 jax.ShapeDtypeStruct((S5_GROUPS, S5_ROWS, S5_TC), BF16)]
    out_specs = [pl.BlockSpec((1, TM, Z_W), lambda b, t: (b, t, 0)),
                 pl.BlockSpec((S5_GROUPS, TM_CHUNKS, S5_TC), lambda b, t: (0, b * nt + t, 0))]
    aliases = {}
    scratch = [pltpu.VMEM((2, TM, LANE), F32)]
    if rope:
        assert nb * nt == POST_STEPS
        in_specs += [pl.BlockSpec((3, TM, LANE), lambda b, t: (0, t, 0))] * 2 + rider.in_specs
        args += list(tabs) + rider.args
        out_shape += rider.out_shape
        out_specs += rider.out_specs
        scratch += rider.scratch
    else:
        for k, cbuf in enumerate(caches):
            in_specs.append(pl.BlockSpec(memory_space=pl.ANY))
            args.append(cbuf)
            out_shape.append(jax.ShapeDtypeStruct(cbuf.shape, cbuf.dtype))
            out_specs.append(pl.BlockSpec((2, 1, SEQ, cbuf.shape[-1]), lambda b, t: (t, layer, 0, 0)))
            aliases[7 + k] = 2 + k
    return pl.pallas_call(
        functools.partial(_inproj_kernel, rope=rope, rider=rider),
        out_shape=tuple(out_shape),
        grid=(nb, nt),
        in_specs=in_specs,
        out_specs=tuple(out_specs),
        scratch_shapes=scratch,
        input_output_aliases=aliases,
        compiler_params=_cparams(("arbitrary", "arbitrary"), 56),
        name=name,
    )(*args)


def _softmax_pv(s_list, v_list, sink=None):
    m = jnp.max(s_list[0], axis=-1, keepdims=True)
    for s in s_list[1:]:
        m = jnp.maximum(m, jnp.max(s, axis=-1, keepdims=True))
    if sink is not None:
        m = jnp.maximum(m, sink)
    den = None
    o = None
    for s, v in zip(s_list, v_list):
        e = jnp.exp2(s - m)
        d = jnp.sum(e, axis=-1, keepdims=True)
        pv = _dot(e.astype(BF16), v)
        den = d if den is None else den + d
        o = pv if o is None else o + pv
    if sink is not None:
        den = den + jnp.exp2(sink - m)
    return o / den


def _low_lanes(shape):
    return lax.broadcasted_iota(jnp.int32, shape, 1) < HEAD_DIM


def _stack_heads(blocks):
    lo = _low_lanes(blocks[0].shape)
    zero = jnp.zeros_like(blocks[0])
    parts = []
    for b in blocks:
        parts += [jnp.where(lo, b, zero), jnp.where(lo, zero, b)]
    return jnp.concatenate(parts, axis=0)


def _unstack_pair(o, idx, m):
    return jnp.where(_low_lanes((m, LANE)), o[2 * idx * m:(2 * idx + 1) * m], o[(2 * idx + 1) * m:(2 * idx + 2) * m])


def _sink_column(sink_ref, layer, m):
    blk = lax.broadcasted_iota(jnp.int32, (A_HEADS * m, 1), 0) // m
    col = jnp.full((A_HEADS * m, 1), sink_ref[layer, A_HEAD_ORDER[-1]] * LOG2E, F32)
    for i in range(A_HEADS - 1):
        col = jnp.where(blk == i, sink_ref[layer, A_HEAD_ORDER[i]] * LOG2E, col)
    return col


def _mla_keys(kv, kpe_blk):
    lo = _low_lanes(kpe_blk.shape)
    return [jnp.where(lo, kv[:, h * LANE:(h + 1) * LANE], kpe_blk).astype(BF16) for h in range(MLA_HEADS)]


def _mla_heads(q_of, k_of, v_of, gate_of, store):
    for pair in range(MLA_HEADS // 2):
        outs = [_softmax_pv([_dot_nt(q_of(h), k_of(h))], [v_of(h)]) for h in (2 * pair, 2 * pair + 1)]
        o = jnp.where(_low_lanes(outs[0].shape), pltpu.roll(outs[0], MLA_V, 1), outs[1])
        store(pair, (o * gate_of(pair)).astype(BF16))


def _ctx_attn_body(sink_ref, z_ref, wkv_ref, oa_ref, ob_ref, od_ref, e, layer):
    m = SEQ
    gate = lambda c0: _silu(z_ref[e, :, c0:c0 + LANE].astype(F32))

    def store(ref, pair, val):
        ref[e, :, pair * LANE:(pair + 1) * LANE] = val

    q4 = _stack_heads([z_ref[e, :, Z_AQ:Z_AQ + LANE], z_ref[e, :, Z_AQ + LANE:Z_AQ + 2 * LANE]])
    o4 = _softmax_pv([_dot_nt(q4, z_ref[e, :, Z_AK:Z_AV])], [z_ref[e, :, Z_AV:Z_DQ]],
                     sink=_sink_column(sink_ref, layer, m))
    for pair in range(2):
        store(oa_ref, pair, (_unstack_pair(o4, pair, m) * gate(Z_AG + pair * LANE)).astype(BF16))
    kv = _dot(z_ref[e, :, Z_CKV:Z_KPE], wkv_ref[...])
    keys = _mla_keys(kv, z_ref[e, :, Z_KPE:Z_QM].astype(F32))
    kvb = kv.astype(BF16)
    _mla_heads(lambda h: z_ref[e, :, Z_QM + LANE * h:Z_QM + LANE * (h + 1)], lambda h: keys[h],
               lambda h: kvb[:, LANE * h:LANE * (h + 1)], lambda p: gate(Z_BG + p * LANE),
               functools.partial(store, ob_ref))
    for pair in range(NAT_HEADS // 2):
        c = pair * LANE
        q2 = _stack_heads([z_ref[e, :, Z_DQ + c:Z_DQ + c + LANE]])
        o2 = _softmax_pv([_dot_nt(q2, z_ref[e, :, Z_DK + c:Z_DK + c + LANE])], [z_ref[e, :, Z_DV + c:Z_DV + c + LANE]])
        store(od_ref, pair, (_unstack_pair(o2, 0, m) * gate(Z_DG + c)).astype(BF16))


def _ctx_attn_rider(z, sink, wkv, layer, nt):
    def body(sink_ref, z_ref, wkv_ref, oa_ref, ob_ref, od_ref):
        for e in range(CTX_BPS):
            _ctx_attn_body(sink_ref, z_ref, wkv_ref, oa_ref, ob_ref, od_ref, e, layer)

    step = lambda b, t: (b * nt + t, 0, 0)
    return _Rider(
        body=body,
        in_specs=[pl.BlockSpec(memory_space=pltpu.SMEM), pl.BlockSpec((CTX_BPS, SEQ, Z_W), step),
                  _layer_spec((MLA_KV_LORA, 512), layer)],
        args=[sink, z, wkv],
        out_shape=[jax.ShapeDtypeStruct((BATCH, SEQ, BRANCH_W), BF16)] * 3,
        out_specs=[pl.BlockSpec((CTX_BPS, SEQ, BRANCH_W), step)] * 3,
        scratch=[])


def _win_attn_kernel(sink_ref, q_ref, kv_ref, g_ref, ck_ref, cv_ref, o_ref, ck_s, cv_s, *, layer):
    nb = DEC_SEQ // Q_BLOCK
    m = Q_BLOCK

    @pl.when(pl.program_id(1) == 0)
    def _():
        ck_s[...] = ck_ref[0, 0].astype(BF16)
        cv_s[...] = cv_ref[0, 0].astype(BF16)

    sink = _sink_column(sink_ref, layer, m)
    for qb in range(WIN_BPS):
        n = pl.program_id(1) * WIN_BPS + qb
        rows = slice(qb * m, (qb + 1) * m)
        blocks = []
        for off in (-1, 0, 1):
            start = pl.multiple_of(jnp.clip(n + off, 0, nb - 1) * m, m)
            blocks.append(kv_ref[0, pl.ds(start, m), :])
        kvw = jnp.concatenate(blocks, axis=0)
        shape = (A_HEADS * m, 3 * m)
        qpos = n * m + (lax.broadcasted_iota(jnp.int32, shape, 0) & (m - 1))
        kpos = (n - 1) * m + lax.broadcasted_iota(jnp.int32, shape, 1)
        mask = (jnp.abs(qpos - kpos) <= A_WINDOW) & (kpos >= 0) & (kpos < DEC_SEQ)
        q4 = _stack_heads([q_ref[0, rows, 0:LANE], q_ref[0, rows, LANE:2 * LANE]])
        s_win = jnp.where(mask, _dot_nt(q4, kvw[:, 0:LANE]), NEG)
        s_ctx = _dot_nt(q4, ck_s[...])
        o4 = _softmax_pv([s_win, s_ctx], [kvw[:, LANE:2 * LANE], cv_s[...]], sink=sink)
        for pair in range(2):
            g = g_ref[0, rows, pair * LANE:(pair + 1) * LANE].astype(F32)
            o_ref[0, rows, pair * LANE:(pair + 1) * LANE] = (_unstack_pair(o4, pair, m) * _silu(g)).astype(BF16)


def _win_attn_call(z, sink, cache_k, cache_v, layer):
    kvw = A_KV_HEADS * HEAD_DIM
    qrows = WIN_BPS * Q_BLOCK
    return pl.pallas_call(
        functools.partial(_win_attn_kernel, layer=layer),
        out_shape=jax.ShapeDtypeStruct((DEC_BATCH, DEC_SEQ, BRANCH_W), BF16),
        grid=(DEC_BATCH, DEC_SEQ // qrows),
        in_specs=[
            pl.BlockSpec(memory_space=pltpu.SMEM),
            pl.BlockSpec((1, qrows, 256), lambda b, n: (b, n, Z_AQ // 256)),
            pl.BlockSpec((1, DEC_SEQ, 256), lambda b, n: (b, 0, Z_AK // 256)),
            pl.BlockSpec((1, qrows, 256), lambda b, n: (b, n, Z_AG // 256)),
            pl.BlockSpec((1, 1, PAST_LEN, kvw), lambda b, n: (b, layer, 0, 0)),
            pl.BlockSpec((1, 1, PAST_LEN, kvw), lambda b, n: (b, layer, 0, 0)),
        ],
        out_specs=pl.BlockSpec((1, qrows, BRANCH_W), lambda b, n: (b, n, 0)),
        scratch_shapes=[pltpu.VMEM((PAST_LEN, kvw), BF16), pltpu.VMEM((PAST_LEN, kvw), BF16)],
        compiler_params=_cparams(("arbitrary", "arbitrary"), 40),
        name="lat_window_attn",
    )(sink, z, z, z, cache_k, cache_v)


def _nat_kernel(q_ref, k_ref, v_ref, g_ref, ck_ref, cv_ref, ring_ref, o_ref, ck_s, cv_s, bias_s):
    nrows = DEC_SEQ // GRID_W
    kr = NAT_ROWS
    m = NAT_RPS * GRID_W
    step = pl.program_id(1)

    @pl.when((pl.program_id(0) == 0) & (step == 0))
    def _():
        shape = (GRID_W, LANE)
        lane = lax.broadcasted_iota(jnp.int32, shape, 1)
        qcol = lax.broadcasted_iota(jnp.int32, shape, 0)
        rel = (lane & (GRID_W - 1)) - jnp.clip(qcol - NAT_COLS // 2, 0, GRID_W - NAT_COLS)
        col_ok = (rel >= 0) & (rel < NAT_COLS)
        for h in range(NAT_HEADS):
            for i in range(NAT_BIAS_ROWS):
                a = pltpu.roll(jnp.broadcast_to(ring_ref[h, i:i + 1, :], shape), 0, 1, stride=1, stride_axis=0)
                b = pltpu.roll(jnp.broadcast_to(ring_ref[h, i + 1:i + 2, :], shape), GRID_W, 1, stride=1,
                               stride_axis=0)
                bias_s[h, i] = jnp.where(col_ok, jnp.where(lane < GRID_W, a, b), NEG)

    @pl.when(step == 0)
    def _():
        ck_s[...] = ck_ref[0, 0].astype(BF16)
        cv_s[...] = cv_ref[0, 0].astype(BF16)

    ws = jnp.clip(step * NAT_RPS - kr // 2, 0, nrows - NAT_UNION)
    start = pl.multiple_of(ws * GRID_W, GRID_W)
    kl = k_ref[0, pl.ds(start, NAT_UNION * GRID_W), :]
    vl = v_ref[0, pl.ds(start, NAT_UNION * GRID_W), :]
    key_row = lax.broadcasted_iota(jnp.int32, (1, LANE), 1) // GRID_W
    for pair in range(NAT_HEADS // 2):
        c = pair * LANE
        blocks = []
        for hh in range(2):
            for rr in range(NAT_RPS):
                r = step * NAT_RPS + rr
                lo = jnp.clip(r - kr // 2, 0, nrows - kr) - ws
                ro = ws - r + (NAT_ROWS - 1) + NAT_BIAS_PAD
                row = []
                for t in range(NAT_UNION // 2):
                    kj = key_row + 2 * t
                    row.append(jnp.where((kj >= lo) & (kj < lo + kr), bias_s[2 * pair + hh, ro + 2 * t], NEG))
                blocks.append(jnp.concatenate(row, axis=1))
        bias = jnp.concatenate(blocks, axis=0)
        q2 = _stack_heads([q_ref[0, :, c:c + LANE]])
        s_lat = _dot_nt(q2, kl[:, c:c + LANE]) + bias
        s_ctx = _dot_nt(q2, ck_s[:, c:c + LANE])
        o2 = _softmax_pv([s_lat, s_ctx], [vl[:, c:c + LANE], cv_s[:, c:c + LANE]])
        g = g_ref[0, :, c:c + LANE].astype(F32)
        o_ref[0, :, c:c + LANE] = (_unstack_pair(o2, 0, m) * _silu(g)).astype(BF16)


def _nat_bias_rings(rpb):
    nc = NAT_COLS - 1
    rep = lambda a, n: jnp.broadcast_to(a, a.shape[:-1] + (n,))
    ring = jnp.concatenate([rpb[..., nc:], rep(rpb[..., -1:], GRID_W - 1 - nc), rep(rpb[..., :1], GRID_W - nc),
                            rpb[..., :nc]], axis=-1)
    neg = lambda n: jnp.full(ring.shape[:2] + (n, 2 * GRID_W), NEG, F32)
    return jnp.concatenate([neg(NAT_BIAS_PAD), ring, neg(NAT_BIAS_PAD + 1)], axis=2)


def _nat_call(z, cache_k, cache_v, bias, layer):
    hw = NAT_HEADS * HEAD_DIM
    qrows = NAT_RPS * GRID_W
    return pl.pallas_call(
        _nat_kernel,
        out_shape=jax.ShapeDtypeStruct((DEC_BATCH, DEC_SEQ, BRANCH_W), BF16),
        grid=(DEC_BATCH, DEC_SEQ // qrows),
        in_specs=[
            pl.BlockSpec((1, qrows, 256), lambda b, r: (b, r, Z_DQ // 256)),
            pl.BlockSpec((1, DEC_SEQ, 256), lambda b, r: (b, 0, Z_DK // 256)),
            pl.BlockSpec((1, DEC_SEQ, 256), lambda b, r: (b, 0, Z_DV // 256)),
            pl.BlockSpec((1, qrows, 256), lambda b, r: (b, r, Z_DG // 256)),
            pl.BlockSpec((1, 1, PAST_LEN, hw), lambda b, r: (b, layer, 0, 0)),
            pl.BlockSpec((1, 1, PAST_LEN, hw), lambda b, r: (b, layer, 0, 0)),
            _layer_spec(bias.shape[1:], layer),
        ],
        out_specs=pl.BlockSpec((1, qrows, BRANCH_W), lambda b, r: (b, r, 0)),
        scratch_shapes=[pltpu.VMEM((PAST_LEN, hw), BF16), pltpu.VMEM((PAST_LEN, hw), BF16),
                        pltpu.VMEM((NAT_HEADS, NAT_BIAS_ROWS, GRID_W, LANE), F32)],
        compiler_params=_cparams(("arbitrary", "arbitrary"), 40),
        name="lat_nat_attn",
    )(z, z, z, z, cache_k, cache_v, bias)


def _mla_body(q_ref, ckv_ref, kpe_ref, g_ref, cckv_ref, ckpe_ref, wkv_ref, o_ref, k_s, v_s):
    nlat = DEC_SEQ
    step = pl.program_id(0) * pl.num_programs(1) + pl.program_id(1)

    @pl.when(step % MLA_TILES == 0)
    def _():
        rows = 512

        def fill(r0, ckv, kpe_blk):
            kv = _dot(ckv, wkv_ref[...])
            for h, kh in enumerate(_mla_keys(kv, kpe_blk)):
                k_s[h, r0:r0 + rows, :] = kh
                v_s[h, r0:r0 + rows, :] = kv[:, h * LANE:(h + 1) * LANE].astype(BF16)

        for c in range(nlat // rows):
            fill(c * rows, ckv_ref[0, c * rows:(c + 1) * rows, :], kpe_ref[0, c * rows:(c + 1) * rows, :].astype(F32))
        ckpe = ckpe_ref[0, 0]
        ckpe_blk = jnp.concatenate([jnp.zeros((PAST_LEN, KPE_LANE), F32), ckpe,
                                    jnp.zeros((PAST_LEN, LANE - KPE_LANE - MLA_ROPE), F32)], axis=1)
        fill(nlat, cckv_ref[0, 0].astype(BF16), ckpe_blk)

    def store(pair, val):
        o_ref[0, :, pair * LANE:(pair + 1) * LANE] = val

    _mla_heads(lambda h: q_ref[0, :, LANE * h:LANE * (h + 1)], lambda h: k_s[h], lambda h: v_s[h],
               lambda p: _silu(g_ref[0, :, p * LANE:(p + 1) * LANE].astype(F32)), store)


def _mla_rider(z, cache_ckv, cache_kpe, wkv, layer, nt):
    nkeys = DEC_SEQ + PAST_LEN
    assert DEC_BATCH * MLA_TILES == POST_STEPS
    bt = lambda b, t: ((b * nt + t) // MLA_TILES, (b * nt + t) % MLA_TILES)
    tile = lambda col: (lambda b, t: bt(b, t) + (col,))
    whole = lambda col: (lambda b, t: (bt(b, t)[0], 0, col))
    cache = lambda b, t: (bt(b, t)[0], layer, 0, 0)
    return _Rider(
        body=_mla_body,
        in_specs=[
            pl.BlockSpec((1, TQ_MLA, MLA_HEADS * LANE), tile(Z_QM // (MLA_HEADS * LANE))),
            pl.BlockSpec((1, DEC_SEQ, LANE), whole(Z_CKV // LANE)),
            pl.BlockSpec((1, DEC_SEQ, LANE), whole(Z_KPE // LANE)),
            pl.BlockSpec((1, TQ_MLA, 256), tile(Z_BG // 256)),
            pl.BlockSpec((1, 1, PAST_LEN, MLA_KV_LORA), cache),
            pl.BlockSpec((1, 1, PAST_LEN, MLA_ROPE), cache),
            _layer_spec((MLA_KV_LORA, 512), layer),
        ],
        args=[z, z, z, z, cache_ckv, cache_kpe, wkv],
        out_shape=[jax.ShapeDtypeStruct((DEC_BATCH, DEC_SEQ, BRANCH_W), BF16)],
        out_specs=[pl.BlockSpec((1, TQ_MLA, BRANCH_W), tile(0))],
        scratch=[pltpu.VMEM((MLA_HEADS, nkeys, LANE), BF16), pltpu.VMEM((MLA_HEADS, nkeys, LANE), BF16)])


def _s5_mats(lam_re, lam_im, log_dt, b_re, b_im, c_re, c_im):
    t = S5_T
    hp = lax.Precision.HIGHEST
    lre = jnp.minimum(lam_re, -1e-4)
    lim = lam_im
    dt = jnp.exp(log_dt)[..., None]
    er, ei = lre * dt, lim * dt
    mag = jnp.exp(er)
    are, aim = mag * jnp.cos(ei), mag * jnp.sin(ei)
    den = lre * lre + lim * lim
    qre = ((are - 1.0) * lre + aim * lim) / den
    qim = (aim * lre - (are - 1.0) * lim) / den
    bbr = qre[..., None] * b_re - qim[..., None] * b_im
    bbi = qre[..., None] * b_im + qim[..., None] * b_re
    n = jnp.arange(t + 1, dtype=F32)[:, None, None, None]
    pmag = jnp.exp(er[None] * n)
    pr, pi = pmag * jnp.cos(ei[None] * n), pmag * jnp.sin(ei[None] * n)

    def c_times_pow(d, pw_r, pw_i):
        rep = lambda a: jnp.repeat(jnp.transpose(a, (1, 2, 0)), S5_CH, axis=-1)
        til = lambda a: jnp.tile(jnp.transpose(a, (0, 2, 1)), (1, 1, t))
        cr, ci, ar, ai = til(c_re[d]), til(c_im[d]), rep(pw_r), rep(pw_i)
        return cr * ar - ci * ai, cr * ai + ci * ar

    def lag_kernels(d, wr, wi):
        return (jnp.einsum('gpk,gpx->gkx', bbr[d], wr, precision=hp)
                - jnp.einsum('gpk,gpx->gkx', bbi[d], wi, precision=hp))

    wf0 = c_times_pow(0, pr[:t, 0], pi[:t, 0])
    wf1 = c_times_pow(0, pr[1:, 0], pi[1:, 0])
    wb = c_times_pow(1, pr[1:, 1][::-1], pi[1:, 1][::-1])
    kb0 = (jnp.einsum('gpk,gcp->gkc', bbr[1], c_re[1], precision=hp)
           - jnp.einsum('gpk,gcp->gkc', bbi[1], c_im[1], precision=hp))
    kf = lag_kernels(0, *wf0) + jnp.pad(kb0, ((0, 0), (0, 0), (0, S5_TC - S5_CH)))
    kb = lag_kernels(1, *wb) * jnp.asarray(np.arange(S5_TC) >= S5_CH, F32)
    ring = jnp.concatenate([kf, kb], axis=-1)

    def state_cols(pw_r, pw_i, d):
        rep = lambda a: jnp.repeat(jnp.transpose(a, (1, 0, 2)), S5_CH, axis=1)
        til = lambda a: jnp.tile(jnp.transpose(a, (0, 2, 1)), (1, t, 1))
        ar, ai, br, bi = rep(pw_r), rep(pw_i), til(bbr[d]), til(bbi[d])
        return ar * br - ai * bi, ar * bi + ai * br

    sfr, sfi = state_cols(pr[:t, 0][::-1], pi[:t, 0][::-1], 0)
    sbr, sbi = state_cols(pr[:t, 1], pi[:t, 1], 1)
    m_state = jnp.concatenate([sfr, sbr, sfi, sbi], axis=-1)
    m_out = jnp.concatenate([wf1[0], wb[0], -wf1[1], -wb[1]], axis=1)

    a_t = jnp.stack([jnp.concatenate([pr[t, 0], pr[t, 1]], axis=-1),
                     jnp.concatenate([pi[t, 0], pi[t, 1]], axis=-1)], axis=1)
    return ring, m_state.astype(BF16), m_out.astype(BF16), a_t


def _s5_kernel(uc_ref, ul_ref, ring_ref, ms_ref, mo_ref, at_ref, h0_ref, yc_ref, yl_ref, hend_ref,
               s_s, hf_s, hb_s, m_s):
    p = S5_STATE
    u = jnp.concatenate([uc_ref[0], ul_ref[0]], axis=0)
    s = _dot(u, ms_ref[0])
    s_s[0] = s[:, 0:2 * p]
    s_s[1] = s[:, 2 * p:4 * p]
    are = at_ref[0, 0:1, :]
    aim = at_ref[0, 1:2, :]
    fwd_lane = lax.broadcasted_iota(jnp.int32, (1, 2 * p), 1) < p

    for k in range(S5_CH):
        rk = jnp.broadcast_to(ring_ref[0, k:k + 1, :], (S5_T, 2 * S5_TC))
        rolled = pltpu.roll(rk, 0, 1, stride=S5_CH, stride_axis=0)
        for blk in range(S5_TC // LANE):
            m_s[blk, pl.ds(k, S5_T, stride=S5_CH), :] = rolled[:, blk * LANE:(blk + 1) * LANE]

    def scan(row0, nk, nb, hre, him):
        for st in range(nk):
            rf = pl.ds(row0 + st, nb, stride=nk)
            rb = pl.ds(row0 + nk - 1 - st, nb, stride=nk)
            hf_s[0, rf, :] = hre
            hf_s[1, rf, :] = him
            hb_s[0, rb, :] = hre
            hb_s[1, rb, :] = him
            sre = jnp.where(fwd_lane, s_s[0, rf, :], s_s[0, rb, :])
            sim = jnp.where(fwd_lane, s_s[1, rf, :], s_s[1, rb, :])
            hre, him = are * hre - aim * him + sre, are * him + aim * hre + sim
        return hre, him

    zero = jnp.zeros((BATCH, 2 * p), F32)
    hre, him = scan(0, S5_NK_CTX, BATCH, zero, zero)
    hend_ref[0] = jnp.concatenate([hre, him], axis=1)
    scan(S5_ROWS, S5_NK_LAT, DEC_BATCH, h0_ref[0, :, 0:2 * p], h0_ref[0, :, 2 * p:4 * p])
    hst = jnp.concatenate([jnp.where(fwd_lane, hf_s[0], hb_s[0]), jnp.where(fwd_lane, hf_s[1], hb_s[1])], axis=1)
    m_intra = jnp.concatenate([m_s[blk] for blk in range(S5_TC // LANE)], axis=1).astype(BF16)
    y = _dot(u, m_intra) + _dot(hst.astype(BF16), mo_ref[0])
    yc_ref[0] = y[0:S5_ROWS]
    yl_ref[0] = y[S5_ROWS:2 * S5_ROWS]


def _s5_call(ut_ctx, ut_lat, ring, m_state, m_out, a_t, h0, layer):
    g3 = lambda g: (g, 0, 0)
    lg = lambda shape: pl.BlockSpec((None, 1) + shape, lambda g: (layer, g, 0, 0))
    yshape = jax.ShapeDtypeStruct((S5_GROUPS, S5_ROWS, S5_TC), F32)
    uspec = pl.BlockSpec((1, S5_ROWS, S5_TC), g3)
    return pl.pallas_call(
        _s5_kernel,
        out_shape=(yshape, yshape, jax.ShapeDtypeStruct((S5_GROUPS, BATCH, 4 * S5_STATE), F32)),
        grid=(S5_GROUPS,),
        in_specs=[
            uspec, uspec,
            lg((S5_CH, 2 * S5_TC)),
            lg((S5_TC, 4 * S5_STATE)),
            lg((4 * S5_STATE, S5_TC)),
            lg((2, 2 * S5_STATE)),
            lg((DEC_BATCH, 4 * S5_STATE)),
        ],
        out_specs=(uspec, uspec, pl.BlockSpec((1, BATCH, 4 * S5_STATE), g3)),
        scratch_shapes=[pltpu.VMEM((2, 2 * S5_ROWS, LANE), F32)] * 3 + [pltpu.VMEM((S5_TC // LANE, S5_TC, LANE), F32)],
        compiler_params=_cparams(("arbitrary",), 32),
        name="s5_chunks",
    )(ut_ctx, ut_lat, ring, m_state, m_out, a_t, h0)


N_POST_IN = 15


class _Rider(NamedTuple):
    body: Callable
    in_specs: list
    args: list
    out_shape: list
    out_specs: list
    scratch: list


NO_RIDER = _Rider(body=lambda: None, in_specs=[], args=[], out_shape=[], out_specs=[], scratch=[])


def _post_kernel(*refs, final, rider):
    (x_ref, mod_ref, ng_ref, ba_ref, bb_ref, bd_ref, y_ref, u_ref, cg_ref, sd_ref, wglu_ref,
     wm_ref, wbr_ref, wo_ref, fg_ref) = refs[:N_POST_IN]
    n_in, n_out = len(rider.in_specs), len(rider.out_specs)
    o_ref = refs[N_POST_IN + n_in]
    ys_s = refs[N_POST_IN + n_in + 1 + n_out]
    rider.body(*refs[N_POST_IN:N_POST_IN + n_in], *refs[N_POST_IN + n_in + 1:N_POST_IN + n_in + 1 + n_out],
               *refs[N_POST_IN + n_in + 2 + n_out:])
    x = x_ref[0]
    hb = _ada_h(x, mod_ref, ng_ref).astype(BF16)
    for m in range(S5_T // 8):
        for hf in range(2):
            vs = _slot_transpose([y_ref[8 * hf + gp, :, m * LANE:(m + 1) * LANE] for gp in range(8)])
            for s in range(8):
                ys_s[hf, pl.ds(8 * m + s, TM_CHUNKS, stride=S5_T), :] = vs[s]
    yv = jnp.concatenate([ys_s[0], ys_s[1]], axis=1) + sd_ref[...] * u_ref[0].astype(F32)
    gel = 0.5 * yv * (1.0 + jnp.tanh(math.sqrt(2.0 / math.pi) * (yv + 0.044715 * (yv * yv * yv))))
    gl = _dot(gel.astype(BF16), wglu_ref[...])
    oc = gl[:, 0:BRANCH_W] * _sigmoid(gl[:, BRANCH_W:2 * BRANCH_W])
    bc = (oc * _silu(cg_ref[0].astype(F32))).astype(BF16)
    acc = None
    for k, br in enumerate((ba_ref[0], bb_ref[0], bc, bd_ref[0])):
        proj = _dot(br, wbr_ref[k])
        mg = _sigmoid(_dot(hb, wm_ref[:, k * D_MODEL:(k + 1) * D_MODEL]))
        acc = mg * proj if acc is None else acc + mg * proj
    y = _dot(acc.astype(BF16), wo_ref[...])
    out = x + mod_ref[0, :, 2 * D_MODEL:3 * D_MODEL] * y
    if final:
        out = _rms(out, fg_ref[...])
    o_ref[0] = out


def _post_call(x, mods, mod_row0, p, ba, bb, bd, y_s5, z, fg, final, layer, name, rider):
    nb, length, _ = x.shape
    nt = length // TM
    assert nb * nt == POST_STEPS
    tok = lambda b, t: (b, t, 0)
    br_spec = pl.BlockSpec((1, TM, BRANCH_W), tok)
    return pl.pallas_call(
        functools.partial(_post_kernel, final=final, rider=rider),
        out_shape=tuple([jax.ShapeDtypeStruct((nb, length, D_MODEL), F32)] + rider.out_shape),
        grid=(nb, nt),
        in_specs=[
            pl.BlockSpec((1, TM, D_MODEL), tok),
            _mod_spec(layer, mod_row0),
            _layer_spec((1, D_MODEL), layer),
            br_spec, br_spec, br_spec,
            pl.BlockSpec((S5_GROUPS, TM_CHUNKS, S5_TC), lambda b, t: (0, b * nt + t, 0)),
            pl.BlockSpec((1, TM, 256), lambda b, t: (b, t, Z_CU // 256)),
            pl.BlockSpec((1, TM, 256), lambda b, t: (b, t, Z_CG // 256)),
            _layer_spec((1, BRANCH_W), layer),
            _layer_spec((BRANCH_W, 2 * BRANCH_W), layer),
            _layer_spec((D_MODEL, N_BRANCH * D_MODEL), layer),
            _layer_spec((N_BRANCH, BRANCH_W, D_MODEL), layer),
            _layer_spec((D_MODEL, D_MODEL), layer),
            pl.BlockSpec((1, D_MODEL), lambda b, t: (0, 0)),
        ] + rider.in_specs,
        out_specs=tuple([pl.BlockSpec((1, TM, D_MODEL), tok)] + rider.out_specs),
        scratch_shapes=[pltpu.VMEM((2, TM, LANE), F32)] + rider.scratch,
        compiler_params=_cparams(("arbitrary", "arbitrary"), 56),
        name=name,
    )(x, mods, p['ng'], ba, bb, bd, y_s5, z, z, p['s5d'], p['wglu'], p['wm'], p['wbr'], p['wo'], fg, *rider.args)


def _rope_tables():
    t = np.arange(DEC_SEQ)
    row = (t // GRID_W).astype(np.float64)
    col = (t % GRID_W).astype(np.float64)

    def pattern(half):
        inv = ROPE_BASE ** (-np.arange(half, dtype=np.float64) / half)
        zeros = np.zeros((DEC_SEQ, half))
        cs, s_up, s_lo = [], [], []
        for pos in (row, col):
            ang = pos[:, None] * inv[None, :]
            c, s = np.cos(ang), np.sin(ang)
            cs += [c, c]
            s_up += [zeros, s]
            s_lo += [-s, zeros]
        return [np.concatenate(parts, axis=1) for parts in (cs, s_up, s_lo)]

    tab_a = np.stack([np.tile(part, (1, LANE // HEAD_DIM)) for part in pattern(HEAD_DIM // 4)])
    ident = [np.ones, np.zeros, np.zeros]
    tab_m = np.stack([np.concatenate([fill((DEC_SEQ, KPE_LANE)), part,
                                      fill((DEC_SEQ, LANE - KPE_LANE - MLA_ROPE))], axis=1)
                      for fill, part in zip(ident, pattern(MLA_ROPE // 4))])
    return jnp.asarray(tab_a, F32), jnp.asarray(tab_m, F32)


_W_IN_MOVES = tuple(
    [(HEAD_DIM * i, HEAD_DIM * h, HEAD_DIM) for i, h in enumerate(A_HEAD_ORDER)]
    + [(256, 256, 256), (512, 1952, 768), (R_CQ, 768, 384), (R_KPE + KPE_LANE, 1152, MLA_ROPE), (R_CU, 1440, 256)]
    + [(R_G + HEAD_DIM * i, 512 + HEAD_DIM * h, HEAD_DIM) for i, h in enumerate(A_HEAD_ORDER)]
    + [(R_G + 256, 1184, 256), (R_G + 512, 1696, 256), (R_G + 768, 2720, 256)])
W_IN_ROWS = 256


def _w_in_kernel(w_ref, o_ref):
    o_ref[0, :, R_KPE:R_CU] = jnp.zeros((W_IN_ROWS, LANE), BF16)
    for dst, src, width in _W_IN_MOVES:
        o_ref[0, :, dst:dst + width] = w_ref[0, :, src:src + width].astype(BF16)


def _w_in_call(w_in):
    return pl.pallas_call(
        _w_in_kernel,
        out_shape=jax.ShapeDtypeStruct((DEPTH, D_MODEL, R_W), BF16),
        grid=(DEPTH, D_MODEL // W_IN_ROWS),
        in_specs=[pl.BlockSpec((1, W_IN_ROWS, w_in.shape[-1]), lambda i, r: (i, r, 0))],
        out_specs=pl.BlockSpec((1, W_IN_ROWS, R_W), lambda i, r: (i, r, 0)),
        compiler_params=_cparams(("arbitrary", "arbitrary"), 32),
        name="w_in_layout",
    )(w_in)


def _cast_kernel(w_ref, o_ref):
    o_ref[...] = w_ref[...].astype(BF16)


def _cast_call(w):
    _, rows, cols = w.shape
    spec = pl.BlockSpec((1, W_IN_ROWS, cols), lambda i, r: (i, r, 0))
    return pl.pallas_call(
        _cast_kernel,
        out_shape=jax.ShapeDtypeStruct(w.shape, BF16),
        grid=(DEPTH, rows // W_IN_ROWS),
        in_specs=[spec],
        out_specs=spec,
        compiler_params=_cparams(("arbitrary", "arbitrary"), 32),
        name="w_cast",
    )(w)


def _reorder_w_q_up(w):
    w = w.reshape(MLA_Q_LORA, MLA_HEADS, MLA_NOPE + MLA_ROPE)
    w = jnp.concatenate([w, jnp.zeros((MLA_Q_LORA, MLA_HEADS, LANE - MLA_NOPE - MLA_ROPE), w.dtype)], axis=-1)
    return w.reshape(MLA_Q_LORA, MLA_HEADS * LANE)


def _reorder_w_branch(w):
    wa = w[0].reshape(A_HEADS, HEAD_DIM, D_MODEL)
    wa = jnp.concatenate([wa[h] for h in A_HEAD_ORDER], axis=0)
    return jnp.concatenate([wa[None], w[1:]], axis=0)


def kernel(x_prompt, x_sample, cache_a_k, cache_a_v, cache_mla_ckv, cache_mla_kpe, cache_na_k, cache_na_v,
           state_s5_re, state_s5_im, c, c_ctx, w_mod, b_mod, norm_g, w_in, w_merge, a_sink,
           mla_q_norm, mla_w_q_up, mla_kv_norm, mla_w_kv_up, s5_lam_re, s5_lam_im, s5_log_dt,
           s5_b_re, s5_b_im, s5_c_re, s5_c_im, s5_d, s5_w_glu, na_rpb, w_branch, w_out, final_norm_g):
    n_ctx = BATCH * SEQ
    conds = jnp.concatenate([c, c_ctx[None, :], jnp.zeros((3, D_MODEL), F32)], axis=0)
    mods = _mod_call(conds, w_mod, b_mod).reshape(DEPTH, 8, 1, 3 * D_MODEL)
    tabs = _rope_tables()
    ck_a = cache_a_k.reshape(DEC_BATCH, DEPTH, PAST_LEN, A_KV_HEADS * HEAD_DIM)
    cv_a = cache_a_v.reshape(DEC_BATCH, DEPTH, PAST_LEN, A_KV_HEADS * HEAD_DIM)
    ck_n = cache_na_k.reshape(DEC_BATCH, DEPTH, PAST_LEN, NAT_HEADS * HEAD_DIM)
    cv_n = cache_na_v.reshape(DEC_BATCH, DEPTH, PAST_LEN, NAT_HEADS * HEAD_DIM)
    fg = final_norm_g.reshape(1, D_MODEL)
    caches = [jnp.zeros((BATCH, DEPTH, SEQ, w), F32) for w in (128, 128, MLA_KV_LORA, MLA_ROPE, 256, 256)]

    p = dict(
        ng=norm_g.reshape(DEPTH, 1, D_MODEL),
        w_raw=_w_in_call(w_in),
        qnorm=mla_q_norm.reshape(DEPTH, 1, MLA_Q_LORA),
        wq=jax.vmap(_reorder_w_q_up)(mla_w_q_up).astype(BF16),
        kvnorm=mla_kv_norm.reshape(DEPTH, 1, MLA_KV_LORA),
        wkv=mla_w_kv_up.astype(BF16),
        wm=_cast_call(w_merge),
        wbr=jax.vmap(_reorder_w_branch)(w_branch).astype(BF16),
        wo=_cast_call(w_out),
        wglu=s5_w_glu.astype(BF16),
        s5d=s5_d.reshape(DEPTH, 1, BRANCH_W),
    )
    ring, m_state, m_out, a_t = jax.vmap(_s5_mats)(s5_lam_re, s5_lam_im, s5_log_dt, s5_b_re, s5_b_im, s5_c_re, s5_c_im)
    nat_bias = _nat_bias_rings(na_rpb * LOG2E)
    h0 = jnp.concatenate([state_s5_re[:, :, 0], state_s5_re[:, :, 1], state_s5_im[:, :, 0], state_s5_im[:, :, 1]],
                         axis=-1)
    h0 = jnp.transpose(h0, (1, 2, 0, 3))

    yp = x_prompt.reshape(1, n_ctx, D_MODEL)
    ys = x_sample
    hends = []
    for i in range(DEPTH):
        res = _inproj_call(yp, mods, DEC_BATCH, p, None, caches, i, "inproj_ctx")
        z_ctx, ut_ctx, caches = res[0], res[1], list(res[2:])
        zc = z_ctx.reshape(BATCH, SEQ, Z_W)
        z_lat, ut_lat, oa_c, ob_c, od_c = _inproj_call(ys, mods, 0, p, tabs, None, i, "inproj_lat_ctx_attn",
                                                       _ctx_attn_rider(zc, a_sink, p['wkv'], i, DEC_SEQ // TM))

        y_ctx, y_lat, hend = _s5_call(ut_ctx, ut_lat, ring, m_state, m_out, a_t, h0, i)
        hends.append(hend)

        oa_l = _win_attn_call(z_lat, a_sink, ck_a, cv_a, i)
        od_l = _nat_call(z_lat, ck_n, cv_n, nat_bias, i)

        flat = lambda a: a.reshape(1, n_ctx, BRANCH_W)
        final = i == DEPTH - 1
        yp, ob_l = _post_call(yp, mods, DEC_BATCH, p, flat(oa_c), flat(ob_c), flat(od_c), y_ctx, z_ctx, fg, final, i,
                              "post_ctx_lat_mla", _mla_rider(z_lat, cache_mla_ckv, cache_mla_kpe, p['wkv'], i, POST_STEPS))
        ys, = _post_call(ys, mods, 0, p, oa_l, ob_l, od_l, y_lat, z_lat, fg, final, i, "post_lat", NO_RIDER)

    hend = jnp.stack(hends).reshape(DEPTH, S5_GROUPS, BATCH, 2, 2, S5_STATE)
    hend = jnp.transpose(hend, (3, 2, 0, 4, 1, 5))
    ak, av, ckv, kpe, nk, nv = caches
    heads = lambda a, h: a.reshape(BATCH, DEPTH, SEQ, h, HEAD_DIM)
    return (yp.reshape(BATCH, SEQ, D_MODEL), ys, heads(ak, A_KV_HEADS), heads(av, A_KV_HEADS), ckv, kpe,
            heads(nk, NAT_HEADS), heads(nv, NAT_HEADS), hend[0], hend[1])
```

```python
import functools
import math
from typing import Callable, NamedTuple

import numpy as np
import jax
import jax.numpy as jnp
from jax import lax
from jax.experimental import pallas as pl
from jax.experimental.pallas import tpu as pltpu

F32 = jnp.float32
BF16 = jnp.bfloat16

D_MODEL = 1024
BATCH = 32
SEQ = 256
DEPTH = 2
DEC_BATCH = 4
DEC_SEQ = 2048
PAST_LEN = 512
GRID_W = 64
HEAD_DIM = 64
BRANCH_W = 256
N_BRANCH = 4
Q_BLOCK = 128
A_HEADS = 4
A_KV_HEADS = 2
A_GROUP = A_HEADS // A_KV_HEADS
A_WINDOW = 128
MLA_HEADS = 4
MLA_Q_LORA = 256
MLA_KV_LORA = 128
MLA_NOPE = 64
MLA_ROPE = 32
MLA_V = 64
S5_CH = 16
S5_GROUPS = BRANCH_W // S5_CH
S5_STATE = 64
NAT_HEADS = 4
NAT_ROWS = 8
NAT_COLS = 16
ROPE_BASE = 10000.0
EPS = 1e-6
NEG = -1e30
LOG2E = 1.4426950408889634
Q_SCALE = HEAD_DIM ** -0.5 * LOG2E
MLA_Q_SCALE = (MLA_NOPE + MLA_ROPE) ** -0.5 * LOG2E

LANE = 128

R_AV, R_CQ, R_CKV, R_KPE, R_CU, R_G, R_W = 384, 1280, 1536, 1664, 1792, 2048, 3072
KPE_LANE = 64
A_HEAD_ORDER = (0, 2, 1, 3)
Z_AQ, Z_AK, Z_AV = 0, 256, 384
Z_DQ, Z_DK, Z_DV = 512, 768, 1024
Z_CKV, Z_KPE = 1280, 1408
Z_QM = 1536
Z_AG, Z_BG, Z_CG, Z_DG = 2048, 2304, 2560, 2816
Z_CU = 3072
Z_W = 3328

S5_T = 32
S5_TC = S5_T * S5_CH
S5_ROWS = 256
S5_NK_CTX = SEQ // S5_T
S5_NK_LAT = DEC_SEQ // S5_T

TM = 512
TM_CHUNKS = TM // S5_T
TQ_MLA = 512
MLA_TILES = DEC_SEQ // TQ_MLA
POST_STEPS = BATCH * SEQ // TM
CTX_BPS = 2
NAT_RPS = 4
NAT_UNION = NAT_ROWS + NAT_RPS
NAT_BIAS_PAD = NAT_RPS
NAT_BIAS_ROWS = 2 * NAT_ROWS - 1 + 2 * NAT_BIAS_PAD


def _cparams(sem, vmem_mb):
    return pltpu.CompilerParams(dimension_semantics=sem, vmem_limit_bytes=vmem_mb * 1024 * 1024)


def _sigmoid(x):
    return 1.0 / (1.0 + jnp.exp(-x))


def _silu(x):
    return x * _sigmoid(x)


def _rms(x, g):
    return x * lax.rsqrt(jnp.mean(x * x, axis=-1, keepdims=True) + EPS) * g


def _dot(a, b):
    return jnp.dot(a, b, preferred_element_type=F32)


def _dot_nt(a, b):
    return lax.dot_general(a, b, (((1,), (1,)), ((), ())), preferred_element_type=F32)


def _slot_transpose(vs):
    lane = lax.broadcasted_iota(jnp.int32, vs[0].shape, 1)
    vs = list(vs)
    for d in (4, 2, 1):
        keep = (lane & (S5_CH * d)) == 0
        nxt = list(vs)
        for lo in range(8):
            if lo & d:
                continue
            hi = lo + d
            nxt[lo] = jnp.where(keep, vs[lo], pltpu.roll(vs[hi], S5_CH * d, 1))
            nxt[hi] = jnp.where(keep, pltpu.roll(vs[lo], LANE - S5_CH * d, 1), vs[hi])
        vs = nxt
    return vs


def _mod_kernel(c_ref, w_ref, b_ref, o_ref):
    s = _silu(c_ref[...])
    o_ref[0] = _dot(s.astype(BF16), w_ref[0].astype(BF16)) + b_ref[0]


def _mod_call(conds, w_mod, b_mod):
    nc = 512
    return pl.pallas_call(
        _mod_kernel,
        out_shape=jax.ShapeDtypeStruct((DEPTH, 8, 3 * D_MODEL), F32),
        grid=(DEPTH, 3 * D_MODEL // nc),
        in_specs=[
            pl.BlockSpec((8, D_MODEL), lambda i, j: (0, 0)),
            pl.BlockSpec((1, D_MODEL, nc), lambda i, j: (i, 0, j)),
            pl.BlockSpec((1, 1, nc), lambda i, j: (i, 0, j)),
        ],
        out_specs=pl.BlockSpec((1, 8, nc), lambda i, j: (i, 0, j)),
        compiler_params=_cparams(("arbitrary", "arbitrary"), 32),
        name="mod_rows",
    )(conds, w_mod, b_mod.reshape(DEPTH, 1, 3 * D_MODEL))


def _ada_h(x, mod_ref, ng_ref):
    shift = mod_ref[0, :, 0:D_MODEL]
    scale = mod_ref[0, :, D_MODEL:2 * D_MODEL]
    return _rms(x, ng_ref[...]) * (1.0 + scale) + shift


def _rope_block(xs, tab_ref, shift):
    return (xs * tab_ref[0] + pltpu.roll(xs, shift, 1) * tab_ref[1]
            + pltpu.roll(xs, LANE - shift, 1) * tab_ref[2])


N_CACHE = 6


def _inproj_kernel(*refs, rope, rider):
    x_ref, mod_ref, ng_ref, w_ref, qn_ref, wq_ref, kvn_ref = refs[:7]
    if rope:
        ta_ref, tm_ref = refs[7:9]
        n_in, n_out = len(rider.in_specs), len(rider.out_specs)
        z_ref, ut_ref = refs[9 + n_in:11 + n_in]
        u_s = refs[11 + n_in + n_out]
        rider.body(*refs[9:9 + n_in], *refs[11 + n_in:11 + n_in + n_out], *refs[12 + n_in + n_out:])
    else:
        z_ref, ut_ref, ak_ref, av_ref, ckv_ref, kpe_ref, nk_ref, nv_ref, u_s = refs[7 + N_CACHE:]
    h = _ada_h(x_ref[0], mod_ref, ng_ref)
    raw = _dot(h.astype(BF16), w_ref[...])
    for j in range(R_AV // LANE):
        blk = raw[:, j * LANE:(j + 1) * LANE]
        if j < Z_AK // LANE:
            blk = blk * Q_SCALE
        if rope:
            blk = _rope_block(blk, ta_ref, 16)
        z_ref[0, :, j * LANE:(j + 1) * LANE] = blk.astype(BF16)
    z_ref[0, :, Z_AV:Z_DQ] = raw[:, Z_AV:Z_DQ].astype(BF16)
    z_ref[0, :, Z_DQ:Z_DK] = (raw[:, Z_DQ:Z_DK] * Q_SCALE).astype(BF16)
    z_ref[0, :, Z_DK:R_CQ] = raw[:, Z_DK:R_CQ].astype(BF16)
    qn = _rms(raw[:, R_CQ:R_CKV], qn_ref[...])
    q = _dot(qn.astype(BF16), wq_ref[...]) * MLA_Q_SCALE
    kp = raw[:, R_KPE:R_CU]
    ckv = _rms(raw[:, R_CKV:R_KPE], kvn_ref[...])
    for hh in range(MLA_HEADS):
        qh = q[:, hh * LANE:(hh + 1) * LANE]
        if rope:
            qh = _rope_block(qh, tm_ref, 8)
        z_ref[0, :, Z_QM + hh * LANE:Z_QM + (hh + 1) * LANE] = qh.astype(BF16)
    z_ref[0, :, Z_CKV:Z_KPE] = ckv.astype(BF16)
    z_ref[0, :, Z_KPE:Z_QM] = (_rope_block(kp, tm_ref, 8) if rope else kp).astype(BF16)
    z_ref[0, :, Z_AG:Z_CU] = raw[:, R_G:R_W].astype(BF16)
    z_ref[0, :, Z_CU:Z_W] = raw[:, R_CU:R_G].astype(BF16)
    if not rope:
        half = TM // 2
        for e in range(2):
            rows = slice(e * half, (e + 1) * half)
            ak_ref[e, 0] = raw[rows, Z_AK:Z_AV]
            av_ref[e, 0] = raw[rows, Z_AV:Z_DQ]
            nk_ref[e, 0] = raw[rows, Z_DK:Z_DV]
            nv_ref[e, 0] = raw[rows, Z_DV:R_CQ]
            ckv_ref[e, 0] = ckv[rows]
            kpe_ref[e, 0] = kp[rows, KPE_LANE:KPE_LANE + MLA_ROPE]
    for hf in range(2):
        u_s[hf] = raw[:, R_CU + hf * LANE:R_CU + (hf + 1) * LANE]
    for m in range(S5_T // 8):
        for hf in range(2):
            outs = _slot_transpose([u_s[hf, pl.ds(8 * m + s, TM_CHUNKS, stride=S5_T), :] for s in range(8)])
            for gp in range(8):
                ut_ref[8 * hf + gp, :, m * LANE:(m + 1) * LANE] = outs[gp].astype(BF16)


def _layer_spec(shape, layer):
    zeros = (0,) * len(shape)
    return pl.BlockSpec((None,) + tuple(shape), lambda *_: (layer,) + zeros)


def _mod_spec(layer, row0):
    return pl.BlockSpec((None, 1, 1, 3 * D_MODEL), lambda b, t: (layer, row0 + b, 0, 0))


def _inproj_call(x, mods, mod_row0, p, tabs, caches, layer, name, rider=None):
    nb, length, _ = x.shape
    rope = tabs is not None
    nt = length // TM
    in_specs = [
        pl.BlockSpec((1, TM, D_MODEL), lambda b, t: (b, t, 0)),
        _mod_spec(layer, mod_row0),
        _layer_spec((1, D_MODEL), layer),
        _layer_spec((D_MODEL, R_W), layer),
        _layer_spec((1, MLA_Q_LORA), layer),
        _layer_spec((MLA_Q_LORA, MLA_HEADS * LANE), layer),
        _layer_spec((1, MLA_KV_LORA), layer),
    ]
    args = [x, mods, p['ng'], p['w_raw'], p['qnorm'], p['wq'], p['kvnorm']]
    out_shape = [jax.ShapeDtypeStruct((nb, length, Z_W), BF16),
                 jax.ShapeDtypeStruct((S5_GROUPS, S5_ROWS, S5_TC), BF16)]
    out_specs = [pl.BlockSpec((1, TM, Z_W), lambda b, t: (b, t, 0)),
                 pl.BlockSpec((S5_GROUPS, TM_CHUNKS, S5_TC), lambda b, t: (0, b * nt + t, 0))]
    aliases = {}
    scratch = [pltpu.VMEM((2, TM, LANE), F32)]
    if rope:
        assert nb * nt == POST_STEPS
        in_specs += [pl.BlockSpec((3, TM, LANE), lambda b, t: (0, t, 0))] * 2 + rider.in_specs
        args += list(tabs) + rider.args
        out_shape += rider.out_shape
        out_specs += rider.out_specs
        scratch += rider.scratch
    else:
        for k, cbuf in enumerate(caches):
            in_specs.append(pl.BlockSpec(memory_space=pl.ANY))
            args.append(cbuf)
            out_shape.append(jax.ShapeDtypeStruct(cbuf.shape, cbuf.dtype))
            out_specs.append(pl.BlockSpec((2, 1, SEQ, cbuf.shape[-1]), lambda b, t: (t, layer, 0, 0)))
            aliases[7 + k] = 2 + k
    return pl.pallas_call(
        functools.partial(_inproj_kernel, rope=rope, rider=rider),
        out_shape=tuple(out_shape),
        grid=(nb, nt),
        in_specs=in_specs,
        out_specs=tuple(out_specs),
        scratch_shapes=scratch,
        input_output_aliases=aliases,
        compiler_params=_cparams(("arbitrary", "arbitrary"), 56),
        name=name,
    )(*args)


def _softmax_pv(s_list, v_list, sink=None):
    m = jnp.max(s_list[0], axis=-1, keepdims=True)
    for s in s_list[1:]:
        m = jnp.maximum(m, jnp.max(s, axis=-1, keepdims=True))
    if sink is not None:
        m = jnp.maximum(m, sink)
    den = None
    o = None
    for s, v in zip(s_list, v_list):
        e = jnp.exp2(s - m)
        d = jnp.sum(e, axis=-1, keepdims=True)
        pv = _dot(e.astype(BF16), v)
        den = d if den is None else den + d
        o = pv if o is None else o + pv
    if sink is not None:
        den = den + jnp.exp2(sink - m)
    return o / den


def _low_lanes(shape):
    return lax.broadcasted_iota(jnp.int32, shape, 1) < HEAD_DIM


def _stack_heads(blocks):
    lo = _low_lanes(blocks[0].shape)
    zero = jnp.zeros_like(blocks[0])
    parts = []
    for b in blocks:
        parts += [jnp.where(lo, b, zero), jnp.where(lo, zero, b)]
    return jnp.concatenate(parts, axis=0)


def _unstack_pair(o, idx, m):
    return jnp.where(_low_lanes((m, LANE)), o[2 * idx * m:(2 * idx + 1) * m], o[(2 * idx + 1) * m:(2 * idx + 2) * m])


def _sink_column(sink_ref, layer, m):
    blk = lax.broadcasted_iota(jnp.int32, (A_HEADS * m, 1), 0) // m
    col = jnp.full((A_HEADS * m, 1), sink_ref[layer, A_HEAD_ORDER[-1]] * LOG2E, F32)
    for i in range(A_HEADS - 1):
        col = jnp.where(blk == i, sink_ref[layer, A_HEAD_ORDER[i]] * LOG2E, col)
    return col


def _mla_keys(kv, kpe_blk):
    lo = _low_lanes(kpe_blk.shape)
    return [jnp.where(lo, kv[:, h * LANE:(h + 1) * LANE], kpe_blk).astype(BF16) for h in range(MLA_HEADS)]


def _mla_heads(q_of, k_of, v_of, gate_of, store):
    for pair in range(MLA_HEADS // 2):
        outs = [_softmax_pv([_dot_nt(q_of(h), k_of(h))], [v_of(h)]) for h in (2 * pair, 2 * pair + 1)]
        o = jnp.where(_low_lanes(outs[0].shape), pltpu.roll(outs[0], MLA_V, 1), outs[1])
        store(pair, (o * gate_of(pair)).astype(BF16))


def _ctx_attn_body(sink_ref, z_ref, wkv_ref, oa_ref, ob_ref, od_ref, e, layer):
    m = SEQ
    gate = lambda c0: _silu(z_ref[e, :, c0:c0 + LANE].astype(F32))

    def store(ref, pair, val):
        ref[e, :, pair * LANE:(pair + 1) * LANE] = val

    q4 = _stack_heads([z_ref[e, :, Z_AQ:Z_AQ + LANE], z_ref[e, :, Z_AQ + LANE:Z_AQ + 2 * LANE]])
    o4 = _softmax_pv([_dot_nt(q4, z_ref[e, :, Z_AK:Z_AV])], [z_ref[e, :, Z_AV:Z_DQ]],
                     sink=_sink_column(sink_ref, layer, m))
    for pair in range(2):
        store(oa_ref, pair, (_unstack_pair(o4, pair, m) * gate(Z_AG + pair * LANE)).astype(BF16))
    kv = _dot(z_ref[e, :, Z_CKV:Z_KPE], wkv_ref[...])
    keys = _mla_keys(kv, z_ref[e, :, Z_KPE:Z_QM].astype(F32))
    kvb = kv.astype(BF16)
    _mla_heads(lambda h: z_ref[e, :, Z_QM + LANE * h:Z_QM + LANE * (h + 1)], lambda h: keys[h],
               lambda h: kvb[:, LANE * h:LANE * (h + 1)], lambda p: gate(Z_BG + p * LANE),
               functools.partial(store, ob_ref))
    for pair in range(NAT_HEADS // 2):
        c = pair * LANE
        q2 = _stack_heads([z_ref[e, :, Z_DQ + c:Z_DQ + c + LANE]])
        o2 = _softmax_pv([_dot_nt(q2, z_ref[e, :, Z_DK + c:Z_DK + c + LANE])], [z_ref[e, :, Z_DV + c:Z_DV + c + LANE]])
        store(od_ref, pair, (_unstack_pair(o2, 0, m) * gate(Z_DG + c)).astype(BF16))


def _ctx_attn_rider(z, sink, wkv, layer, nt):
    def body(sink_ref, z_ref, wkv_ref, oa_ref, ob_ref, od_ref):
        for e in range(CTX_BPS):
            _ctx_attn_body(sink_ref, z_ref, wkv_ref, oa_ref, ob_ref, od_ref, e, layer)

    step = lambda b, t: (b * nt + t, 0, 0)
    return _Rider(
        body=body,
        in_specs=[pl.BlockSpec(memory_space=pltpu.SMEM), pl.BlockSpec((CTX_BPS, SEQ, Z_W), step),
                  _layer_spec((MLA_KV_LORA, 512), layer)],
        args=[sink, z, wkv],
        out_shape=[jax.ShapeDtypeStruct((BATCH, SEQ, BRANCH_W), BF16)] * 3,
        out_specs=[pl.BlockSpec((CTX_BPS, SEQ, BRANCH_W), step)] * 3,
        scratch=[])


def _win_attn_rows(sink_ref, q_ref, kv_ref, g_ref, ck_ref, cv_ref, ck_s, cv_s, dst_ref, layer):
    nb = DEC_SEQ // Q_BLOCK
    m = Q_BLOCK
    tile = pl.program_id(1)

    @pl.when(tile == 0)
    def _():
        ck_s[...] = ck_ref[0, 0].astype(BF16)
        cv_s[...] = cv_ref[0, 0].astype(BF16)

    sink = _sink_column(sink_ref, layer, m)
    for qb in range(TM // m):
        n = tile * (TM // m) + qb
        rows = slice(qb * m, (qb + 1) * m)
        blocks = []
        for off in (-1, 0, 1):
            start = pl.multiple_of(jnp.clip(n + off, 0, nb - 1) * m, m)
            blocks.append(kv_ref[0, pl.ds(start, m), :])
        kvw = jnp.concatenate(blocks, axis=0)
        shape = (A_HEADS * m, 3 * m)
        qpos = n * m + (lax.broadcasted_iota(jnp.int32, shape, 0) & (m - 1))
        kpos = (n - 1) * m + lax.broadcasted_iota(jnp.int32, shape, 1)
        mask = (jnp.abs(qpos - kpos) <= A_WINDOW) & (kpos >= 0) & (kpos < DEC_SEQ)
        q4 = _stack_heads([q_ref[0, rows, 0:LANE], q_ref[0, rows, LANE:2 * LANE]])
        s_win = jnp.where(mask, _dot_nt(q4, kvw[:, 0:LANE]), NEG)
        s_ctx = _dot_nt(q4, ck_s[...])
        o4 = _softmax_pv([s_win, s_ctx], [kvw[:, LANE:2 * LANE], cv_s[...]], sink=sink)
        for pair in range(2):
            g = g_ref[0, rows, pair * LANE:(pair + 1) * LANE].astype(F32)
            dst_ref[rows, pair * LANE:(pair + 1) * LANE] = (_unstack_pair(o4, pair, m) * _silu(g)).astype(BF16)


def _nat_rows(q_ref, k_ref, v_ref, g_ref, ck_ref, cv_ref, ring_ref, ck_s, cv_s, bias_s, dst_ref):
    nrows = DEC_SEQ // GRID_W
    kr = NAT_ROWS
    m = NAT_RPS * GRID_W
    step = pl.program_id(1)

    @pl.when((pl.program_id(0) == 0) & (step == 0))
    def _():
        shape = (GRID_W, LANE)
        lane = lax.broadcasted_iota(jnp.int32, shape, 1)
        qcol = lax.broadcasted_iota(jnp.int32, shape, 0)
        rel = (lane & (GRID_W - 1)) - jnp.clip(qcol - NAT_COLS // 2, 0, GRID_W - NAT_COLS)
        col_ok = (rel >= 0) & (rel < NAT_COLS)
        for h in range(NAT_HEADS):
            for i in range(NAT_BIAS_ROWS):
                a = pltpu.roll(jnp.broadcast_to(ring_ref[h, i:i + 1, :], shape), 0, 1, stride=1, stride_axis=0)
                b = pltpu.roll(jnp.broadcast_to(ring_ref[h, i + 1:i + 2, :], shape), GRID_W, 1, stride=1,
                               stride_axis=0)
                bias_s[h, i] = jnp.where(col_ok, jnp.where(lane < GRID_W, a, b), NEG)

    @pl.when(step == 0)
    def _():
        ck_s[...] = ck_ref[0, 0].astype(BF16)
        cv_s[...] = cv_ref[0, 0].astype(BF16)

    key_row = lax.broadcasted_iota(jnp.int32, (1, LANE), 1) // GRID_W
    for grp in range(TM // m):
        row0 = (step * (TM // m) + grp) * NAT_RPS
        rows = slice(grp * m, (grp + 1) * m)
        ws = jnp.clip(row0 - kr // 2, 0, nrows - NAT_UNION)
        start = pl.multiple_of(ws * GRID_W, GRID_W)
        kl = k_ref[0, pl.ds(start, NAT_UNION * GRID_W), :]
        vl = v_ref[0, pl.ds(start, NAT_UNION * GRID_W), :]
        for pair in range(NAT_HEADS // 2):
            c = pair * LANE
            blocks = []
            for hh in range(2):
                for rr in range(NAT_RPS):
                    r = row0 + rr
                    lo = jnp.clip(r - kr // 2, 0, nrows - kr) - ws
                    ro = ws - r + (NAT_ROWS - 1) + NAT_BIAS_PAD
                    row = []
                    for t in range(NAT_UNION // 2):
                        kj = key_row + 2 * t
                        row.append(jnp.where((kj >= lo) & (kj < lo + kr), bias_s[2 * pair + hh, ro + 2 * t], NEG))
                    blocks.append(jnp.concatenate(row, axis=1))
            bias = jnp.concatenate(blocks, axis=0)
            q2 = _stack_heads([q_ref[0, rows, c:c + LANE]])
            s_lat = _dot_nt(q2, kl[:, c:c + LANE]) + bias
            s_ctx = _dot_nt(q2, ck_s[:, c:c + LANE])
            o2 = _softmax_pv([s_lat, s_ctx], [vl[:, c:c + LANE], cv_s[:, c:c + LANE]])
            g = g_ref[0, rows, c:c + LANE].astype(F32)
            dst_ref[rows, c:c + LANE] = (_unstack_pair(o2, 0, m) * _silu(g)).astype(BF16)


def _nat_bias_rings(rpb):
    nc = NAT_COLS - 1
    rep = lambda a, n: jnp.broadcast_to(a, a.shape[:-1] + (n,))
    ring = jnp.concatenate([rpb[..., nc:], rep(rpb[..., -1:], GRID_W - 1 - nc), rep(rpb[..., :1], GRID_W - nc),
                            rpb[..., :nc]], axis=-1)
    neg = lambda n: jnp.full(ring.shape[:2] + (n, 2 * GRID_W), NEG, F32)
    return jnp.concatenate([neg(NAT_BIAS_PAD), ring, neg(NAT_BIAS_PAD + 1)], axis=2)


def _lat_attn_rider(z, sink, ck_a, cv_a, ck_n, cv_n, rings, layer):
    kvw = A_KV_HEADS * HEAD_DIM
    hw = NAT_HEADS * HEAD_DIM

    def body(sink_ref, qa_ref, kva_ref, ga_ref, cka_ref, cva_ref, qd_ref, kd_ref, vd_ref, gd_ref, ckd_ref, cvd_ref,
             ring_ref, cka_s, cva_s, ckd_s, cvd_s, bias_s, ba_s, bd_s):
        _win_attn_rows(sink_ref, qa_ref, kva_ref, ga_ref, cka_ref, cva_ref, cka_s, cva_s, ba_s, layer)
        _nat_rows(qd_ref, kd_ref, vd_ref, gd_ref, ckd_ref, cvd_ref, ring_ref, ckd_s, cvd_s, bias_s, bd_s)

    tile = lambda col: pl.BlockSpec((1, TM, 256), lambda b, t: (b, t, col))
    whole = lambda col: pl.BlockSpec((1, DEC_SEQ, 256), lambda b, t: (b, 0, col))
    cache = lambda w: pl.BlockSpec((1, 1, PAST_LEN, w), lambda b, t: (b, layer, 0, 0))
    return _Rider(
        body=body,
        in_specs=[pl.BlockSpec(memory_space=pltpu.SMEM), tile(Z_AQ // 256), whole(Z_AK // 256), tile(Z_AG // 256),
                  cache(kvw), cache(kvw), tile(Z_DQ // 256), whole(Z_DK // 256), whole(Z_DV // 256),
                  tile(Z_DG // 256), cache(hw), cache(hw), _layer_spec(rings.shape[1:], layer)],
        args=[sink, z, z, z, ck_a, cv_a, z, z, z, z, ck_n, cv_n, rings],
        out_shape=[], out_specs=[],
        scratch=[pltpu.VMEM((PAST_LEN, kvw), BF16), pltpu.VMEM((PAST_LEN, kvw), BF16),
                 pltpu.VMEM((PAST_LEN, hw), BF16), pltpu.VMEM((PAST_LEN, hw), BF16),
                 pltpu.VMEM((NAT_HEADS, NAT_BIAS_ROWS, GRID_W, LANE), F32),
                 pltpu.VMEM((TM, BRANCH_W), BF16), pltpu.VMEM((TM, BRANCH_W), BF16)],
        branches=True)


def _mla_body(q_ref, ckv_ref, kpe_ref, g_ref, cckv_ref, ckpe_ref, wkv_ref, o_ref, k_s, v_s):
    nlat = DEC_SEQ
    step = pl.program_id(0) * pl.num_programs(1) + pl.program_id(1)

    @pl.when(step % MLA_TILES == 0)
    def _():
        rows = 512

        def fill(r0, ckv, kpe_blk):
            kv = _dot(ckv, wkv_ref[...])
            for h, kh in enumerate(_mla_keys(kv, kpe_blk)):
                k_s[h, r0:r0 + rows, :] = kh
                v_s[h, r0:r0 + rows, :] = kv[:, h * LANE:(h + 1) * LANE].astype(BF16)

        for c in range(nlat // rows):
            fill(c * rows, ckv_ref[0, c * rows:(c + 1) * rows, :], kpe_ref[0, c * rows:(c + 1) * rows, :].astype(F32))
        ckpe = ckpe_ref[0, 0]
        ckpe_blk = jnp.concatenate([jnp.zeros((PAST_LEN, KPE_LANE), F32), ckpe,
                                    jnp.zeros((PAST_LEN, LANE - KPE_LANE - MLA_ROPE), F32)], axis=1)
        fill(nlat, cckv_ref[0, 0].astype(BF16), ckpe_blk)

    def store(pair, val):
        o_ref[0, :, pair * LANE:(pair + 1) * LANE] = val

    _mla_heads(lambda h: q_ref[0, :, LANE * h:LANE * (h + 1)], lambda h: k_s[h], lambda h: v_s[h],
               lambda p: _silu(g_ref[0, :, p * LANE:(p + 1) * LANE].astype(F32)), store)


def _mla_rider(z, cache_ckv, cache_kpe, wkv, layer, nt):
    nkeys = DEC_SEQ + PAST_LEN
    assert DEC_BATCH * MLA_TILES == POST_STEPS
    bt = lambda b, t: ((b * nt + t) // MLA_TILES, (b * nt + t) % MLA_TILES)
    tile = lambda col: (lambda b, t: bt(b, t) + (col,))
    whole = lambda col: (lambda b, t: (bt(b, t)[0], 0, col))
    cache = lambda b, t: (bt(b, t)[0], layer, 0, 0)
    return _Rider(
        body=_mla_body,
        in_specs=[
            pl.BlockSpec((1, TQ_MLA, MLA_HEADS * LANE), tile(Z_QM // (MLA_HEADS * LANE))),
            pl.BlockSpec((1, DEC_SEQ, LANE), whole(Z_CKV // LANE)),
            pl.BlockSpec((1, DEC_SEQ, LANE), whole(Z_KPE // LANE)),
            pl.BlockSpec((1, TQ_MLA, 256), tile(Z_BG // 256)),
            pl.BlockSpec((1, 1, PAST_LEN, MLA_KV_LORA), cache),
            pl.BlockSpec((1, 1, PAST_LEN, MLA_ROPE), cache),
            _layer_spec((MLA_KV_LORA, 512), layer),
        ],
        args=[z, z, z, z, cache_ckv, cache_kpe, wkv],
        out_shape=[jax.ShapeDtypeStruct((DEC_BATCH, DEC_SEQ, BRANCH_W), BF16)],
        out_specs=[pl.BlockSpec((1, TQ_MLA, BRANCH_W), tile(0))],
        scratch=[pltpu.VMEM((MLA_HEADS, nkeys, LANE), BF16), pltpu.VMEM((MLA_HEADS, nkeys, LANE), BF16)])


def _s5_mats(lam_re, lam_im, log_dt, b_re, b_im, c_re, c_im):
    t = S5_T
    hp = lax.Precision.HIGHEST
    lre = jnp.minimum(lam_re, -1e-4)
    lim = lam_im
    dt = jnp.exp(log_dt)[..., None]
    er, ei = lre * dt, lim * dt
    mag = jnp.exp(er)
    are, aim = mag * jnp.cos(ei), mag * jnp.sin(ei)
    den = lre * lre + lim * lim
    qre = ((are - 1.0) * lre + aim * lim) / den
    qim = (aim * lre - (are - 1.0) * lim) / den
    bbr = qre[..., None] * b_re - qim[..., None] * b_im
    bbi = qre[..., None] * b_im + qim[..., None] * b_re
    n = jnp.arange(t + 1, dtype=F32)[:, None, None, None]
    pmag = jnp.exp(er[None] * n)
    pr, pi = pmag * jnp.cos(ei[None] * n), pmag * jnp.sin(ei[None] * n)

    def c_times_pow(d, pw_r, pw_i):
        rep = lambda a: jnp.repeat(jnp.transpose(a, (1, 2, 0)), S5_CH, axis=-1)
        til = lambda a: jnp.tile(jnp.transpose(a, (0, 2, 1)), (1, 1, t))
        cr, ci, ar, ai = til(c_re[d]), til(c_im[d]), rep(pw_r), rep(pw_i)
        return cr * ar - ci * ai, cr * ai + ci * ar

    def lag_kernels(d, wr, wi):
        return (jnp.einsum('gpk,gpx->gkx', bbr[d], wr, precision=hp)
                - jnp.einsum('gpk,gpx->gkx', bbi[d], wi, precision=hp))

    wf0 = c_times_pow(0, pr[:t, 0], pi[:t, 0])
    wf1 = c_times_pow(0, pr[1:, 0], pi[1:, 0])
    wb = c_times_pow(1, pr[1:, 1][::-1], pi[1:, 1][::-1])
    kb0 = (jnp.einsum('gpk,gcp->gkc', bbr[1], c_re[1], precision=hp)
           - jnp.einsum('gpk,gcp->gkc', bbi[1], c_im[1], precision=hp))
    kf = lag_kernels(0, *wf0) + jnp.pad(kb0, ((0, 0), (0, 0), (0, S5_TC - S5_CH)))
    kb = lag_kernels(1, *wb) * jnp.asarray(np.arange(S5_TC) >= S5_CH, F32)
    ring = jnp.concatenate([kf, kb], axis=-1)

    def state_cols(pw_r, pw_i, d):
        rep = lambda a: jnp.repeat(jnp.transpose(a, (1, 0, 2)), S5_CH, axis=1)
        til = lambda a: jnp.tile(jnp.transpose(a, (0, 2, 1)), (1, t, 1))
        ar, ai, br, bi = rep(pw_r), rep(pw_i), til(bbr[d]), til(bbi[d])
        return ar * br - ai * bi, ar * bi + ai * br

    sfr, sfi = state_cols(pr[:t, 0][::-1], pi[:t, 0][::-1], 0)
    sbr, sbi = state_cols(pr[:t, 1], pi[:t, 1], 1)
    m_state = jnp.concatenate([sfr, sbr, sfi, sbi], axis=-1)
    m_out = jnp.concatenate([wf1[0], wb[0], -wf1[1], -wb[1]], axis=1)

    a_t = jnp.stack([jnp.concatenate([pr[t, 0], pr[t, 1]], axis=-1),
                     jnp.concatenate([pi[t, 0], pi[t, 1]], axis=-1)], axis=1)
    return ring, m_state.astype(BF16), m_out.astype(BF16), a_t


def _s5_kernel(uc_ref, ul_ref, ring_ref, ms_ref, mo_ref, at_ref, h0_ref, yc_ref, yl_ref, hend_ref,
               s_s, hf_s, hb_s, m_s):
    p = S5_STATE
    u = jnp.concatenate([uc_ref[0], ul_ref[0]], axis=0)
    s = _dot(u, ms_ref[0])
    s_s[0] = s[:, 0:2 * p]
    s_s[1] = s[:, 2 * p:4 * p]
    are = at_ref[0, 0:1, :]
    aim = at_ref[0, 1:2, :]
    fwd_lane = lax.broadcasted_iota(jnp.int32, (1, 2 * p), 1) < p

    for k in range(S5_CH):
        rk = jnp.broadcast_to(ring_ref[0, k:k + 1, :], (S5_T, 2 * S5_TC))
        rolled = pltpu.roll(rk, 0, 1, stride=S5_CH, stride_axis=0)
        for blk in range(S5_TC // LANE):
            m_s[blk, pl.ds(k, S5_T, stride=S5_CH), :] = rolled[:, blk * LANE:(blk + 1) * LANE]

    def scan(row0, nk, nb, hre, him):
        for st in range(nk):
            rf = pl.ds(row0 + st, nb, stride=nk)
            rb = pl.ds(row0 + nk - 1 - st, nb, stride=nk)
            hf_s[0, rf, :] = hre
            hf_s[1, rf, :] = him
            hb_s[0, rb, :] = hre
            hb_s[1, rb, :] = him
            sre = jnp.where(fwd_lane, s_s[0, rf, :], s_s[0, rb, :])
            sim = jnp.where(fwd_lane, s_s[1, rf, :], s_s[1, rb, :])
            hre, him = are * hre - aim * him + sre, are * him + aim * hre + sim
        return hre, him

    zero = jnp.zeros((BATCH, 2 * p), F32)
    hre, him = scan(0, S5_NK_CTX, BATCH, zero, zero)
    hend_ref[0] = jnp.concatenate([hre, him], axis=1)
    scan(S5_ROWS, S5_NK_LAT, DEC_BATCH, h0_ref[0, :, 0:2 * p], h0_ref[0, :, 2 * p:4 * p])
    hst = jnp.concatenate([jnp.where(fwd_lane, hf_s[0], hb_s[0]), jnp.where(fwd_lane, hf_s[1], hb_s[1])], axis=1)
    m_intra = jnp.concatenate([m_s[blk] for blk in range(S5_TC // LANE)], axis=1).astype(BF16)
    y = _dot(u, m_intra) + _dot(hst.astype(BF16), mo_ref[0])
    yc_ref[0] = y[0:S5_ROWS]
    yl_ref[0] = y[S5_ROWS:2 * S5_ROWS]


def _s5_call(ut_ctx, ut_lat, ring, m_state, m_out, a_t, h0, layer):
    g3 = lambda g: (g, 0, 0)
    lg = lambda shape: pl.BlockSpec((None, 1) + shape, lambda g: (layer, g, 0, 0))
    yshape = jax.ShapeDtypeStruct((S5_GROUPS, S5_ROWS, S5_TC), F32)
    uspec = pl.BlockSpec((1, S5_ROWS, S5_TC), g3)
    return pl.pallas_call(
        _s5_kernel,
        out_shape=(yshape, yshape, jax.ShapeDtypeStruct((S5_GROUPS, BATCH, 4 * S5_STATE), F32)),
        grid=(S5_GROUPS,),
        in_specs=[
            uspec, uspec,
            lg((S5_CH, 2 * S5_TC)),
            lg((S5_TC, 4 * S5_STATE)),
            lg((4 * S5_STATE, S5_TC)),
            lg((2, 2 * S5_STATE)),
            lg((DEC_BATCH, 4 * S5_STATE)),
        ],
        out_specs=(uspec, uspec, pl.BlockSpec((1, BATCH, 4 * S5_STATE), g3)),
        scratch_shapes=[pltpu.VMEM((2, 2 * S5_ROWS, LANE), F32)] * 3 + [pltpu.VMEM((S5_TC // LANE, S5_TC, LANE), F32)],
        compiler_params=_cparams(("arbitrary",), 32),
        name="s5_chunks",
    )(ut_ctx, ut_lat, ring, m_state, m_out, a_t, h0)


_POST_IN = ('x', 'mod', 'ng', 'ba', 'bb', 'bd', 'y', 'u', 'cg', 'sd', 'wglu', 'wm', 'wbr', 'wo', 'fg')


class _Rider(NamedTuple):
    body: Callable
    in_specs: list
    args: list
    out_shape: list
    out_specs: list
    scratch: list
    branches: bool = False


def _post_kernel(*refs, final, rider):
    names = [n for n in _POST_IN if not (rider.branches and n in ('ba', 'bd'))]
    r = dict(zip(names, refs))
    n0, n_in, n_out = len(names), len(rider.in_specs), len(rider.out_specs)
    o_ref = refs[n0 + n_in]
    ys_s = refs[n0 + n_in + 1 + n_out]
    rider_scratch = refs[n0 + n_in + 2 + n_out:]
    rider.body(*refs[n0:n0 + n_in], *refs[n0 + n_in + 1:n0 + n_in + 1 + n_out], *rider_scratch)
    ba, bd = (rider_scratch[-2][...], rider_scratch[-1][...]) if rider.branches else (r['ba'][0], r['bd'][0])
    x_ref, mod_ref, ng_ref, y_ref = r['x'], r['mod'], r['ng'], r['y']
    x = x_ref[0]
    hb = _ada_h(x, mod_ref, ng_ref).astype(BF16)
    for m in range(S5_T // 8):
        for hf in range(2):
            vs = _slot_transpose([y_ref[8 * hf + gp, :, m * LANE:(m + 1) * LANE] for gp in range(8)])
            for s in range(8):
                ys_s[hf, pl.ds(8 * m + s, TM_CHUNKS, stride=S5_T), :] = vs[s]
    yv = jnp.concatenate([ys_s[0], ys_s[1]], axis=1) + r['sd'][...] * r['u'][0].astype(F32)
    gel = 0.5 * yv * (1.0 + jnp.tanh(math.sqrt(2.0 / math.pi) * (yv + 0.044715 * (yv * yv * yv))))
    gl = _dot(gel.astype(BF16), r['wglu'][...])
    oc = gl[:, 0:BRANCH_W] * _sigmoid(gl[:, BRANCH_W:2 * BRANCH_W])
    bc = (oc * _silu(r['cg'][0].astype(F32))).astype(BF16)
    acc = None
    for k, br in enumerate((ba, r['bb'][0], bc, bd)):
        proj = _dot(br, r['wbr'][k])
        mg = _sigmoid(_dot(hb, r['wm'][:, k * D_MODEL:(k + 1) * D_MODEL]))
        acc = mg * proj if acc is None else acc + mg * proj
    y = _dot(acc.astype(BF16), r['wo'][...])
    out = x + mod_ref[0, :, 2 * D_MODEL:3 * D_MODEL] * y
    if final:
        out = _rms(out, r['fg'][...])
    o_ref[0] = out


def _post_call(x, mods, mod_row0, p, ba, bb, bd, y_s5, z, fg, final, layer, name, rider):
    nb, length, _ = x.shape
    nt = length // TM
    assert nb * nt == POST_STEPS
    tok = lambda b, t: (b, t, 0)
    br_spec = pl.BlockSpec((1, TM, BRANCH_W), tok)
    branch_args = [bb] if rider.branches else [ba, bb, bd]
    return pl.pallas_call(
        functools.partial(_post_kernel, final=final, rider=rider),
        out_shape=tuple([jax.ShapeDtypeStruct((nb, length, D_MODEL), F32)] + rider.out_shape),
        grid=(nb, nt),
        in_specs=[
            pl.BlockSpec((1, TM, D_MODEL), tok),
            _mod_spec(layer, mod_row0),
            _layer_spec((1, D_MODEL), layer),
        ] + [br_spec] * len(branch_args) + [
            pl.BlockSpec((S5_GROUPS, TM_CHUNKS, S5_TC), lambda b, t: (0, b * nt + t, 0)),
            pl.BlockSpec((1, TM, 256), lambda b, t: (b, t, Z_CU // 256)),
            pl.BlockSpec((1, TM, 256), lambda b, t: (b, t, Z_CG // 256)),
            _layer_spec((1, BRANCH_W), layer),
            _layer_spec((BRANCH_W, 2 * BRANCH_W), layer),
            _layer_spec((D_MODEL, N_BRANCH * D_MODEL), layer),
            _layer_spec((N_BRANCH, BRANCH_W, D_MODEL), layer),
            _layer_spec((D_MODEL, D_MODEL), layer),
            pl.BlockSpec((1, D_MODEL), lambda b, t: (0, 0)),
        ] + rider.in_specs,
        out_specs=tuple([pl.BlockSpec((1, TM, D_MODEL), tok)] + rider.out_specs),
        scratch_shapes=[pltpu.VMEM((2, TM, LANE), F32)] + rider.scratch,
        compiler_params=_cparams(("arbitrary", "arbitrary"), 56),
        name=name,
    )(x, mods, p['ng'], *branch_args, y_s5, z, z, p['s5d'], p['wglu'], p['wm'], p['wbr'], p['wo'], fg, *rider.args)


def _rope_tables():
    t = np.arange(DEC_SEQ)
    row = (t // GRID_W).astype(np.float64)
    col = (t % GRID_W).astype(np.float64)

    def pattern(half):
        inv = ROPE_BASE ** (-np.arange(half, dtype=np.float64) / half)
        zeros = np.zeros((DEC_SEQ, half))
        cs, s_up, s_lo = [], [], []
        for pos in (row, col):
            ang = pos[:, None] * inv[None, :]
            c, s = np.cos(ang), np.sin(ang)
            cs += [c, c]
            s_up += [zeros, s]
            s_lo += [-s, zeros]
        return [np.concatenate(parts, axis=1) for parts in (cs, s_up, s_lo)]

    tab_a = np.stack([np.tile(part, (1, LANE // HEAD_DIM)) for part in pattern(HEAD_DIM // 4)])
    ident = [np.ones, np.zeros, np.zeros]
    tab_m = np.stack([np.concatenate([fill((DEC_SEQ, KPE_LANE)), part,
                                      fill((DEC_SEQ, LANE - KPE_LANE - MLA_ROPE))], axis=1)
                      for fill, part in zip(ident, pattern(MLA_ROPE // 4))])
    return jnp.asarray(tab_a, F32), jnp.asarray(tab_m, F32)


_W_IN_MOVES = tuple(
    [(HEAD_DIM * i, HEAD_DIM * h, HEAD_DIM) for i, h in enumerate(A_HEAD_ORDER)]
    + [(256, 256, 256), (512, 1952, 768), (R_CQ, 768, 384), (R_KPE + KPE_LANE, 1152, MLA_ROPE), (R_CU, 1440, 256)]
    + [(R_G + HEAD_DIM * i, 512 + HEAD_DIM * h, HEAD_DIM) for i, h in enumerate(A_HEAD_ORDER)]
    + [(R_G + 256, 1184, 256), (R_G + 512, 1696, 256), (R_G + 768, 2720, 256)])
W_IN_ROWS = 256


def _w_in_kernel(w_ref, o_ref):
    o_ref[0, :, R_KPE:R_CU] = jnp.zeros((W_IN_ROWS, LANE), BF16)
    for dst, src, width in _W_IN_MOVES:
        o_ref[0, :, dst:dst + width] = w_ref[0, :, src:src + width].astype(BF16)


def _w_in_call(w_in):
    return pl.pallas_call(
        _w_in_kernel,
        out_shape=jax.ShapeDtypeStruct((DEPTH, D_MODEL, R_W), BF16),
        grid=(DEPTH, D_MODEL // W_IN_ROWS),
        in_specs=[pl.BlockSpec((1, W_IN_ROWS, w_in.shape[-1]), lambda i, r: (i, r, 0))],
        out_specs=pl.BlockSpec((1, W_IN_ROWS, R_W), lambda i, r: (i, r, 0)),
        compiler_params=_cparams(("arbitrary", "arbitrary"), 32),
        name="w_in_layout",
    )(w_in)


def _cast_kernel(w_ref, o_ref):
    o_ref[...] = w_ref[...].astype(BF16)


def _cast_call(w):
    _, rows, cols = w.shape
    spec = pl.BlockSpec((1, W_IN_ROWS, cols), lambda i, r: (i, r, 0))
    return pl.pallas_call(
        _cast_kernel,
        out_shape=jax.ShapeDtypeStruct(w.shape, BF16),
        grid=(DEPTH, rows // W_IN_ROWS),
        in_specs=[spec],
        out_specs=spec,
        compiler_params=_cparams(("arbitrary", "arbitrary"), 32),
        name="w_cast",
    )(w)


def _reorder_w_q_up(w):
    w = w.reshape(MLA_Q_LORA, MLA_HEADS, MLA_NOPE + MLA_ROPE)
    w = jnp.concatenate([w, jnp.zeros((MLA_Q_LORA, MLA_HEADS, LANE - MLA_NOPE - MLA_ROPE), w.dtype)], axis=-1)
    return w.reshape(MLA_Q_LORA, MLA_HEADS * LANE)


def _reorder_w_branch(w):
    wa = w[0].reshape(A_HEADS, HEAD_DIM, D_MODEL)
    wa = jnp.concatenate([wa[h] for h in A_HEAD_ORDER], axis=0)
    return jnp.concatenate([wa[None], w[1:]], axis=0)


def kernel(x_prompt, x_sample, cache_a_k, cache_a_v, cache_mla_ckv, cache_mla_kpe, cache_na_k, cache_na_v,
           state_s5_re, state_s5_im, c, c_ctx, w_mod, b_mod, norm_g, w_in, w_merge, a_sink,
           mla_q_norm, mla_w_q_up, mla_kv_norm, mla_w_kv_up, s5_lam_re, s5_lam_im, s5_log_dt,
           s5_b_re, s5_b_im, s5_c_re, s5_c_im, s5_d, s5_w_glu, na_rpb, w_branch, w_out, final_norm_g):
    n_ctx = BATCH * SEQ
    conds = jnp.concatenate([c, c_ctx[None, :], jnp.zeros((3, D_MODEL), F32)], axis=0)
    mods = _mod_call(conds, w_mod, b_mod).reshape(DEPTH, 8, 1, 3 * D_MODEL)
    tabs = _rope_tables()
    ck_a = cache_a_k.reshape(DEC_BATCH, DEPTH, PAST_LEN, A_KV_HEADS * HEAD_DIM)
    cv_a = cache_a_v.reshape(DEC_BATCH, DEPTH, PAST_LEN, A_KV_HEADS * HEAD_DIM)
    ck_n = cache_na_k.reshape(DEC_BATCH, DEPTH, PAST_LEN, NAT_HEADS * HEAD_DIM)
    cv_n = cache_na_v.reshape(DEC_BATCH, DEPTH, PAST_LEN, NAT_HEADS * HEAD_DIM)
    fg = final_norm_g.reshape(1, D_MODEL)
    caches = [jnp.zeros((BATCH, DEPTH, SEQ, w), F32) for w in (128, 128, MLA_KV_LORA, MLA_ROPE, 256, 256)]

    p = dict(
        ng=norm_g.reshape(DEPTH, 1, D_MODEL),
        w_raw=_w_in_call(w_in),
        qnorm=mla_q_norm.reshape(DEPTH, 1, MLA_Q_LORA),
        wq=jax.vmap(_reorder_w_q_up)(mla_w_q_up).astype(BF16),
        kvnorm=mla_kv_norm.reshape(DEPTH, 1, MLA_KV_LORA),
        wkv=mla_w_kv_up.astype(BF16),
        wm=_cast_call(w_merge),
        wbr=jax.vmap(_reorder_w_branch)(w_branch).astype(BF16),
        wo=_cast_call(w_out),
        wglu=s5_w_glu.astype(BF16),
        s5d=s5_d.reshape(DEPTH, 1, BRANCH_W),
    )
    ring, m_state, m_out, a_t = jax.vmap(_s5_mats)(s5_lam_re, s5_lam_im, s5_log_dt, s5_b_re, s5_b_im, s5_c_re, s5_c_im)
    nat_bias = _nat_bias_rings(na_rpb * LOG2E)
    h0 = jnp.concatenate([state_s5_re[:, :, 0], state_s5_re[:, :, 1], state_s5_im[:, :, 0], state_s5_im[:, :, 1]],
                         axis=-1)
    h0 = jnp.transpose(h0, (1, 2, 0, 3))

    yp = x_prompt.reshape(1, n_ctx, D_MODEL)
    ys = x_sample
    hends = []
    for i in range(DEPTH):
        res = _inproj_call(yp, mods, DEC_BATCH, p, None, caches, i, "inproj_ctx")
        z_ctx, ut_ctx, caches = res[0], res[1], list(res[2:])
        zc = z_ctx.reshape(BATCH, SEQ, Z_W)
        z_lat, ut_lat, oa_c, ob_c, od_c = _inproj_call(ys, mods, 0, p, tabs, None, i, "inproj_lat_ctx_attn",
                                                       _ctx_attn_rider(zc, a_sink, p['wkv'], i, DEC_SEQ // TM))

        y_ctx, y_lat, hend = _s5_call(ut_ctx, ut_lat, ring, m_state, m_out, a_t, h0, i)
        hends.append(hend)

        flat = lambda a: a.reshape(1, n_ctx, BRANCH_W)
        final = i == DEPTH - 1
        yp, ob_l = _post_call(yp, mods, DEC_BATCH, p, flat(oa_c), flat(ob_c), flat(od_c), y_ctx, z_ctx, fg, final, i,
                              "post_ctx_lat_mla", _mla_rider(z_lat, cache_mla_ckv, cache_mla_kpe, p['wkv'], i, POST_STEPS))
        ys, = _post_call(ys, mods, 0, p, None, ob_l, None, y_lat, z_lat, fg, final, i, "post_lat_win_nat",
                         _lat_attn_rider(z_lat, a_sink, ck_a, cv_a, ck_n, cv_n, nat_bias, i))

    hend = jnp.stack(hends).reshape(DEPTH, S5_GROUPS, BATCH, 2, 2, S5_STATE)
    hend = jnp.transpose(hend, (3, 2, 0, 4, 1, 5))
    ak, av, ckv, kpe, nk, nv = caches
    heads = lambda a, h: a.reshape(BATCH, DEPTH, SEQ, h, HEAD_DIM)
    return (yp.reshape(BATCH, SEQ, D_MODEL), ys, heads(ak, A_KV_HEADS), heads(av, A_KV_HEADS), ckv, kpe,
            heads(nk, NAT_HEADS), heads(nv, NAT_HEADS), hend[0], hend[1])
```

```python
import functools
import math
from typing import Callable, NamedTuple

import numpy as np
import jax
import jax.numpy as jnp
from jax import lax
from jax.experimental import pallas as pl
from jax.experimental.pallas import tpu as pltpu

F32 = jnp.float32
BF16 = jnp.bfloat16

D_MODEL = 1024
BATCH = 32
SEQ = 256
DEPTH = 2
DEC_BATCH = 4
DEC_SEQ = 2048
PAST_LEN = 512
GRID_W = 64
HEAD_DIM = 64
BRANCH_W = 256
N_BRANCH = 4
Q_BLOCK = 128
A_HEADS = 4
A_KV_HEADS = 2
A_GROUP = A_HEADS // A_KV_HEADS
A_WINDOW = 128
MLA_HEADS = 4
MLA_Q_LORA = 256
MLA_KV_LORA = 128
MLA_NOPE = 64
MLA_ROPE = 32
MLA_V = 64
S5_CH = 16
S5_GROUPS = BRANCH_W // S5_CH
S5_STATE = 64
NAT_HEADS = 4
NAT_ROWS = 8
NAT_COLS = 16
ROPE_BASE = 10000.0
EPS = 1e-6
NEG = -1e30
LOG2E = 1.4426950408889634
Q_SCALE = HEAD_DIM ** -0.5 * LOG2E
MLA_Q_SCALE = (MLA_NOPE + MLA_ROPE) ** -0.5 * LOG2E

LANE = 128

R_AV, R_CQ, R_CKV, R_KPE, R_CU, R_G, R_W = 384, 1280, 1536, 1664, 1792, 2048, 3072
KPE_LANE = 64
A_HEAD_ORDER = (0, 2, 1, 3)
Z_AQ, Z_AK, Z_AV = 0, 256, 384
Z_DQ, Z_DK, Z_DV = 512, 768, 1024
Z_CKV, Z_KPE = 1280, 1408
Z_QM = 1536
Z_AG, Z_BG, Z_CG, Z_DG = 2048, 2304, 2560, 2816
Z_CU = 3072
Z_W = 3328

S5_T = 32
S5_TC = S5_T * S5_CH
S5_ROWS = 256
S5_NK_CTX = SEQ // S5_T
S5_NK_LAT = DEC_SEQ // S5_T

TM = 512
TM_CHUNKS = TM // S5_T
TQ_MLA = 512
MLA_TILES = DEC_SEQ // TQ_MLA
POST_STEPS = BATCH * SEQ // TM
CTX_BPS = 2
NAT_RPS = 4
NAT_UNION = NAT_ROWS + NAT_RPS
NAT_BIAS_PAD = NAT_RPS
NAT_BIAS_ROWS = 2 * NAT_ROWS - 1 + 2 * NAT_BIAS_PAD


def _cparams(sem, vmem_mb):
    return pltpu.CompilerParams(dimension_semantics=sem, vmem_limit_bytes=vmem_mb * 1024 * 1024)


def _sigmoid(x):
    return 1.0 / (1.0 + jnp.exp(-x))


def _silu(x):
    return x * _sigmoid(x)


def _rms(x, g):
    return x * lax.rsqrt(jnp.mean(x * x, axis=-1, keepdims=True) + EPS) * g


def _dot(a, b):
    return jnp.dot(a, b, preferred_element_type=F32)


def _dot_nt(a, b):
    return lax.dot_general(a, b, (((1,), (1,)), ((), ())), preferred_element_type=F32)


def _slot_transpose(vs):
    lane = lax.broadcasted_iota(jnp.int32, vs[0].shape, 1)
    vs = list(vs)
    for d in (4, 2, 1):
        keep = (lane & (S5_CH * d)) == 0
        nxt = list(vs)
        for lo in range(8):
            if lo & d:
                continue
            hi = lo + d
            nxt[lo] = jnp.where(keep, vs[lo], pltpu.roll(vs[hi], S5_CH * d, 1))
            nxt[hi] = jnp.where(keep, pltpu.roll(vs[lo], LANE - S5_CH * d, 1), vs[hi])
        vs = nxt
    return vs


def _mod_kernel(c_ref, w_ref, b_ref, o_ref):
    s = _silu(c_ref[...])
    o_ref[0] = _dot(s.astype(BF16), w_ref[0].astype(BF16)) + b_ref[0]


def _mod_call(conds, w_mod, b_mod):
    nc = 512
    return pl.pallas_call(
        _mod_kernel,
        out_shape=jax.ShapeDtypeStruct((DEPTH, 8, 3 * D_MODEL), F32),
        grid=(DEPTH, 3 * D_MODEL // nc),
        in_specs=[
            pl.BlockSpec((8, D_MODEL), lambda i, j: (0, 0)),
            pl.BlockSpec((1, D_MODEL, nc), lambda i, j: (i, 0, j)),
            pl.BlockSpec((1, 1, nc), lambda i, j: (i, 0, j)),
        ],
        out_specs=pl.BlockSpec((1, 8, nc), lambda i, j: (i, 0, j)),
        compiler_params=_cparams(("arbitrary", "arbitrary"), 32),
        name="mod_rows",
    )(conds, w_mod, b_mod.reshape(DEPTH, 1, 3 * D_MODEL))


def _ada_h(x, mod_ref, ng_ref):
    shift = mod_ref[0, :, 0:D_MODEL]
    scale = mod_ref[0, :, D_MODEL:2 * D_MODEL]
    return _rms(x, ng_ref[...]) * (1.0 + scale) + shift


def _rope_block(xs, tab_ref, shift):
    return (xs * tab_ref[0] + pltpu.roll(xs, shift, 1) * tab_ref[1]
            + pltpu.roll(xs, LANE - shift, 1) * tab_ref[2])


CACHE_WIDTHS = (128, 128, MLA_KV_LORA, MLA_ROPE, 256, 256)
N_CACHE = len(CACHE_WIDTHS)


def _inproj_kernel(*refs, rope, rider, first):
    x_ref, mod_ref, ng_ref, w_ref, qn_ref, wq_ref, kvn_ref = refs[:7]
    if rope:
        ta_ref, tm_ref = refs[7:9]
        n_in, n_out = len(rider.in_specs), len(rider.out_specs)
        z_ref, ut_ref = refs[9 + n_in:11 + n_in]
        u_s = refs[11 + n_in + n_out]
        rider.body(*refs[9:9 + n_in], *refs[11 + n_in:11 + n_in + n_out], *refs[12 + n_in + n_out:])
    else:
        z_ref, ut_ref, ak_ref, av_ref, ckv_ref, kpe_ref, nk_ref, nv_ref, u_s = refs[7 + (0 if first else N_CACHE):]
    h = _ada_h(x_ref[0], mod_ref, ng_ref)
    raw = _dot(h.astype(BF16), w_ref[...])
    for j in range(R_AV // LANE):
        blk = raw[:, j * LANE:(j + 1) * LANE]
        if j < Z_AK // LANE:
            blk = blk * Q_SCALE
        if rope:
            blk = _rope_block(blk, ta_ref, 16)
        z_ref[0, :, j * LANE:(j + 1) * LANE] = blk.astype(BF16)
    z_ref[0, :, Z_AV:Z_DQ] = raw[:, Z_AV:Z_DQ].astype(BF16)
    z_ref[0, :, Z_DQ:Z_DK] = (raw[:, Z_DQ:Z_DK] * Q_SCALE).astype(BF16)
    z_ref[0, :, Z_DK:R_CQ] = raw[:, Z_DK:R_CQ].astype(BF16)
    qn = _rms(raw[:, R_CQ:R_CKV], qn_ref[...])
    q = _dot(qn.astype(BF16), wq_ref[...]) * MLA_Q_SCALE
    kp = raw[:, R_KPE:R_CU]
    ckv = _rms(raw[:, R_CKV:R_KPE], kvn_ref[...])
    for hh in range(MLA_HEADS):
        qh = q[:, hh * LANE:(hh + 1) * LANE]
        if rope:
            qh = _rope_block(qh, tm_ref, 8)
        z_ref[0, :, Z_QM + hh * LANE:Z_QM + (hh + 1) * LANE] = qh.astype(BF16)
    z_ref[0, :, Z_CKV:Z_KPE] = ckv.astype(BF16)
    z_ref[0, :, Z_KPE:Z_QM] = (_rope_block(kp, tm_ref, 8) if rope else kp).astype(BF16)
    z_ref[0, :, Z_AG:Z_CU] = raw[:, R_G:R_W].astype(BF16)
    z_ref[0, :, Z_CU:Z_W] = raw[:, R_CU:R_G].astype(BF16)
    if not rope:
        half = TM // 2
        for e in range(2):
            rows = slice(e * half, (e + 1) * half)
            ak_ref[e, 0] = raw[rows, Z_AK:Z_AV]
            av_ref[e, 0] = raw[rows, Z_AV:Z_DQ]
            nk_ref[e, 0] = raw[rows, Z_DK:Z_DV]
            nv_ref[e, 0] = raw[rows, Z_DV:R_CQ]
            ckv_ref[e, 0] = ckv[rows]
            kpe_ref[e, 0] = kp[rows, KPE_LANE:KPE_LANE + MLA_ROPE]
            if first:
                for cref in (ak_ref, av_ref, nk_ref, nv_ref, ckv_ref, kpe_ref):
                    cref[e, 1:DEPTH] = jnp.zeros((DEPTH - 1,) + cref.shape[2:], F32)
    for hf in range(2):
        u_s[hf] = raw[:, R_CU + hf * LANE:R_CU + (hf + 1) * LANE]
    for m in range(S5_T // 8):
        for hf in range(2):
            outs = _slot_transpose([u_s[hf, pl.ds(8 * m + s, TM_CHUNKS, stride=S5_T), :] for s in range(8)])
            for gp in range(8):
                ut_ref[8 * hf + gp, :, m * LANE:(m + 1) * LANE] = outs[gp].astype(BF16)


def _layer_spec(shape, layer):
    zeros = (0,) * len(shape)
    return pl.BlockSpec((None,) + tuple(shape), lambda *_: (layer,) + zeros)


def _mod_spec(layer, row0):
    return pl.BlockSpec((None, 1, 1, 3 * D_MODEL), lambda b, t: (layer, row0 + b, 0, 0))


def _inproj_call(x, mods, mod_row0, p, tabs, caches, layer, name, rider=None):
    nb, length, _ = x.shape
    rope = tabs is not None
    nt = length // TM
    in_specs = [
        pl.BlockSpec((1, TM, D_MODEL), lambda b, t: (b, t, 0)),
        _mod_spec(layer, mod_row0),
        _layer_spec((1, D_MODEL), layer),
        _layer_spec((D_MODEL, R_W), layer),
        _layer_spec((1, MLA_Q_LORA), layer),
        _layer_spec((MLA_Q_LORA, MLA_HEADS * LANE), layer),
        _layer_spec((1, MLA_KV_LORA), layer),
    ]
    args = [x, mods, p['ng'], p['w_raw'], p['qnorm'], p['wq'], p['kvnorm']]
    out_shape = [jax.ShapeDtypeStruct((nb, length, Z_W), BF16),
                 jax.ShapeDtypeStruct((S5_GROUPS, S5_ROWS, S5_TC), BF16)]
    out_specs = [pl.BlockSpec((1, TM, Z_W), lambda b, t: (b, t, 0)),
                 pl.BlockSpec((S5_GROUPS, TM_CHUNKS, S5_TC), lambda b, t: (0, b * nt + t, 0))]
    aliases = {}
    scratch = [pltpu.VMEM((2, TM, LANE), F32)]
    if rope:
        assert nb * nt == POST_STEPS
        in_specs += [pl.BlockSpec((3, TM, LANE), lambda b, t: (0, t, 0))] * 2 + rider.in_specs
        args += list(tabs) + rider.args
        out_shape += rider.out_shape
        out_specs += rider.out_specs
        scratch += rider.scratch
    else:
        for k, width in enumerate(CACHE_WIDTHS):
            out_shape.append(jax.ShapeDtypeStruct((BATCH, DEPTH, SEQ, width), F32))
            if caches is None:
                out_specs.append(pl.BlockSpec((2, DEPTH, SEQ, width), lambda b, t: (t, 0, 0, 0)))
            else:
                in_specs.append(pl.BlockSpec(memory_space=pl.ANY))
                args.append(caches[k])
                out_specs.append(pl.BlockSpec((2, 1, SEQ, width), lambda b, t: (t, layer, 0, 0)))
                aliases[7 + k] = 2 + k
    return pl.pallas_call(
        functools.partial(_inproj_kernel, rope=rope, rider=rider, first=not rope and caches is None),
        out_shape=tuple(out_shape),
        grid=(nb, nt),
        in_specs=in_specs,
        out_specs=tuple(out_specs),
        scratch_shapes=scratch,
        input_output_aliases=aliases,
        compiler_params=_cparams(("arbitrary", "arbitrary"), 56),
        name=name,
    )(*args)


def _softmax_pv(s_list, v_list, sink=None):
    m = jnp.max(s_list[0], axis=-1, keepdims=True)
    for s in s_list[1:]:
        m = jnp.maximum(m, jnp.max(s, axis=-1, keepdims=True))
    if sink is not None:
        m = jnp.maximum(m, sink)
    den = None
    o = None
    for s, v in zip(s_list, v_list):
        e = jnp.exp2(s - m)
        d = jnp.sum(e, axis=-1, keepdims=True)
        pv = _dot(e.astype(BF16), v)
        den = d if den is None else den + d
        o = pv if o is None else o + pv
    if sink is not None:
        den = den + jnp.exp2(sink - m)
    return o / den


def _low_lanes(shape):
    return lax.broadcasted_iota(jnp.int32, shape, 1) < HEAD_DIM


def _stack_heads(blocks):
    lo = _low_lanes(blocks[0].shape)
    zero = jnp.zeros_like(blocks[0])
    parts = []
    for b in blocks:
        parts += [jnp.where(lo, b, zero), jnp.where(lo, zero, b)]
    return jnp.concatenate(parts, axis=0)


def _unstack_pair(o, idx, m):
    return jnp.where(_low_lanes((m, LANE)), o[2 * idx * m:(2 * idx + 1) * m], o[(2 * idx + 1) * m:(2 * idx + 2) * m])


def _sink_column(sink_ref, layer, m):
    blk = lax.broadcasted_iota(jnp.int32, (A_HEADS * m, 1), 0) // m
    col = jnp.full((A_HEADS * m, 1), sink_ref[layer, A_HEAD_ORDER[-1]] * LOG2E, F32)
    for i in range(A_HEADS - 1):
        col = jnp.where(blk == i, sink_ref[layer, A_HEAD_ORDER[i]] * LOG2E, col)
    return col


def _mla_keys(kv, kpe_blk):
    lo = _low_lanes(kpe_blk.shape)
    return [jnp.where(lo, kv[:, h * LANE:(h + 1) * LANE], kpe_blk).astype(BF16) for h in range(MLA_HEADS)]


def _mla_heads(q_of, k_of, v_of, gate_of, store):
    for pair in range(MLA_HEADS // 2):
        outs = [_softmax_pv([_dot_nt(q_of(h), k_of(h))], [v_of(h)]) for h in (2 * pair, 2 * pair + 1)]
        o = jnp.where(_low_lanes(outs[0].shape), pltpu.roll(outs[0], MLA_V, 1), outs[1])
        store(pair, (o * gate_of(pair)).astype(BF16))


def _ctx_attn_body(sink_ref, z_ref, wkv_ref, oa_ref, ob_ref, od_ref, e, layer):
    m = SEQ
    gate = lambda c0: _silu(z_ref[e, :, c0:c0 + LANE].astype(F32))

    def store(ref, pair, val):
        ref[e, :, pair * LANE:(pair + 1) * LANE] = val

    q4 = _stack_heads([z_ref[e, :, Z_AQ:Z_AQ + LANE], z_ref[e, :, Z_AQ + LANE:Z_AQ + 2 * LANE]])
    o4 = _softmax_pv([_dot_nt(q4, z_ref[e, :, Z_AK:Z_AV])], [z_ref[e, :, Z_AV:Z_DQ]],
                     sink=_sink_column(sink_ref, layer, m))
    for pair in range(2):
        store(oa_ref, pair, (_unstack_pair(o4, pair, m) * gate(Z_AG + pair * LANE)).astype(BF16))
    kv = _dot(z_ref[e, :, Z_CKV:Z_KPE], wkv_ref[...])
    keys = _mla_keys(kv, z_ref[e, :, Z_KPE:Z_QM].astype(F32))
    kvb = kv.astype(BF16)
    _mla_heads(lambda h: z_ref[e, :, Z_QM + LANE * h:Z_QM + LANE * (h + 1)], lambda h: keys[h],
               lambda h: kvb[:, LANE * h:LANE * (h + 1)], lambda p: gate(Z_BG + p * LANE),
               functools.partial(store, ob_ref))
    for pair in range(NAT_HEADS // 2):
        c = pair * LANE
        q2 = _stack_heads([z_ref[e, :, Z_DQ + c:Z_DQ + c + LANE]])
        o2 = _softmax_pv([_dot_nt(q2, z_ref[e, :, Z_DK + c:Z_DK + c + LANE])], [z_ref[e, :, Z_DV + c:Z_DV + c + LANE]])
        store(od_ref, pair, (_unstack_pair(o2, 0, m) * gate(Z_DG + c)).astype(BF16))


def _ctx_attn_rider(z, sink, wkv, layer, nt):
    def body(sink_ref, z_ref, wkv_ref, oa_ref, ob_ref, od_ref):
        for e in range(CTX_BPS):
            _ctx_attn_body(sink_ref, z_ref, wkv_ref, oa_ref, ob_ref, od_ref, e, layer)

    step = lambda b, t: (b * nt + t, 0, 0)
    return _Rider(
        body=body,
        in_specs=[pl.BlockSpec(memory_space=pltpu.SMEM), pl.BlockSpec((CTX_BPS, SEQ, Z_W), step),
                  _layer_spec((MLA_KV_LORA, 512), layer)],
        args=[sink, z, wkv],
        out_shape=[jax.ShapeDtypeStruct((BATCH, SEQ, BRANCH_W), BF16)] * 3,
        out_specs=[pl.BlockSpec((CTX_BPS, SEQ, BRANCH_W), step)] * 3,
        scratch=[])


def _win_attn_rows(sink_ref, q_ref, kv_ref, g_ref, ck_ref, cv_ref, ck_s, cv_s, dst_ref, layer):
    nb = DEC_SEQ // Q_BLOCK
    m = Q_BLOCK
    tile = pl.program_id(1)

    @pl.when(tile == 0)
    def _():
        ck_s[...] = ck_ref[0, 0].astype(BF16)
        cv_s[...] = cv_ref[0, 0].astype(BF16)

    sink = _sink_column(sink_ref, layer, m)
    for qb in range(TM // m):
        n = tile * (TM // m) + qb
        rows = slice(qb * m, (qb + 1) * m)
        blocks = []
        for off in (-1, 0, 1):
            start = pl.multiple_of(jnp.clip(n + off, 0, nb - 1) * m, m)
            blocks.append(kv_ref[0, pl.ds(start, m), :])
        kvw = jnp.concatenate(blocks, axis=0)
        shape = (A_HEADS * m, 3 * m)
        qpos = n * m + (lax.broadcasted_iota(jnp.int32, shape, 0) & (m - 1))
        kpos = (n - 1) * m + lax.broadcasted_iota(jnp.int32, shape, 1)
        mask = (jnp.abs(qpos - kpos) <= A_WINDOW) & (kpos >= 0) & (kpos < DEC_SEQ)
        q4 = _stack_heads([q_ref[0, rows, 0:LANE], q_ref[0, rows, LANE:2 * LANE]])
        s_win = jnp.where(mask, _dot_nt(q4, kvw[:, 0:LANE]), NEG)
        s_ctx = _dot_nt(q4, ck_s[...])
        o4 = _softmax_pv([s_win, s_ctx], [kvw[:, LANE:2 * LANE], cv_s[...]], sink=sink)
        for pair in range(2):
            g = g_ref[0, rows, pair * LANE:(pair + 1) * LANE].astype(F32)
            dst_ref[rows, pair * LANE:(pair + 1) * LANE] = (_unstack_pair(o4, pair, m) * _silu(g)).astype(BF16)


def _nat_rows(q_ref, k_ref, v_ref, g_ref, ck_ref, cv_ref, ring_ref, ck_s, cv_s, bias_s, dst_ref):
    nrows = DEC_SEQ // GRID_W
    kr = NAT_ROWS
    m = NAT_RPS * GRID_W
    step = pl.program_id(1)

    @pl.when((pl.program_id(0) == 0) & (step == 0))
    def _():
        shape = (GRID_W, LANE)
        lane = lax.broadcasted_iota(jnp.int32, shape, 1)
        qcol = lax.broadcasted_iota(jnp.int32, shape, 0)
        rel = (lane & (GRID_W - 1)) - jnp.clip(qcol - NAT_COLS // 2, 0, GRID_W - NAT_COLS)
        col_ok = (rel >= 0) & (rel < NAT_COLS)
        for h in range(NAT_HEADS):
            for i in range(NAT_BIAS_ROWS):
                a = pltpu.roll(jnp.broadcast_to(ring_ref[h, i:i + 1, :], shape), 0, 1, stride=1, stride_axis=0)
                b = pltpu.roll(jnp.broadcast_to(ring_ref[h, i + 1:i + 2, :], shape), GRID_W, 1, stride=1,
                               stride_axis=0)
                bias_s[h, i] = jnp.where(col_ok, jnp.where(lane < GRID_W, a, b), NEG)

    @pl.when(step == 0)
    def _():
        ck_s[...] = ck_ref[0, 0].astype(BF16)
        cv_s[...] = cv_ref[0, 0].astype(BF16)

    key_row = lax.broadcasted_iota(jnp.int32, (1, LANE), 1) // GRID_W
    for grp in range(TM // m):
        row0 = (step * (TM // m) + grp) * NAT_RPS
        rows = slice(grp * m, (grp + 1) * m)
        ws = jnp.clip(row0 - kr // 2, 0, nrows - NAT_UNION)
        start = pl.multiple_of(ws * GRID_W, GRID_W)
        kl = k_ref[0, pl.ds(start, NAT_UNION * GRID_W), :]
        vl = v_ref[0, pl.ds(start, NAT_UNION * GRID_W), :]
        for pair in range(NAT_HEADS // 2):
            c = pair * LANE
            blocks = []
            for hh in range(2):
                for rr in range(NAT_RPS):
                    r = row0 + rr
                    lo = jnp.clip(r - kr // 2, 0, nrows - kr) - ws
                    ro = ws - r + (NAT_ROWS - 1) + NAT_BIAS_PAD
                    row = []
                    for t in range(NAT_UNION // 2):
                        kj = key_row + 2 * t
                        row.append(jnp.where((kj >= lo) & (kj < lo + kr), bias_s[2 * pair + hh, ro + 2 * t], NEG))
                    blocks.append(jnp.concatenate(row, axis=1))
            bias = jnp.concatenate(blocks, axis=0)
            q2 = _stack_heads([q_ref[0, rows, c:c + LANE]])
            s_lat = _dot_nt(q2, kl[:, c:c + LANE]) + bias
            s_ctx = _dot_nt(q2, ck_s[:, c:c + LANE])
            o2 = _softmax_pv([s_lat, s_ctx], [vl[:, c:c + LANE], cv_s[:, c:c + LANE]])
            g = g_ref[0, rows, c:c + LANE].astype(F32)
            dst_ref[rows, c:c + LANE] = (_unstack_pair(o2, 0, m) * _silu(g)).astype(BF16)


def _nat_bias_rings(rpb):
    nc = NAT_COLS - 1
    rep = lambda a, n: jnp.broadcast_to(a, a.shape[:-1] + (n,))
    ring = jnp.concatenate([rpb[..., nc:], rep(rpb[..., -1:], GRID_W - 1 - nc), rep(rpb[..., :1], GRID_W - nc),
                            rpb[..., :nc]], axis=-1)
    neg = lambda n: jnp.full(ring.shape[:2] + (n, 2 * GRID_W), NEG, F32)
    return jnp.concatenate([neg(NAT_BIAS_PAD), ring, neg(NAT_BIAS_PAD + 1)], axis=2)


def _lat_attn_rider(z, sink, ck_a, cv_a, ck_n, cv_n, rings, layer):
    kvw = A_KV_HEADS * HEAD_DIM
    hw = NAT_HEADS * HEAD_DIM

    def body(sink_ref, qa_ref, kva_ref, ga_ref, cka_ref, cva_ref, qd_ref, kd_ref, vd_ref, gd_ref, ckd_ref, cvd_ref,
             ring_ref, cka_s, cva_s, ckd_s, cvd_s, bias_s, ba_s, bd_s):
        _win_attn_rows(sink_ref, qa_ref, kva_ref, ga_ref, cka_ref, cva_ref, cka_s, cva_s, ba_s, layer)
        _nat_rows(qd_ref, kd_ref, vd_ref, gd_ref, ckd_ref, cvd_ref, ring_ref, ckd_s, cvd_s, bias_s, bd_s)

    tile = lambda col: pl.BlockSpec((1, TM, 256), lambda b, t: (b, t, col))
    whole = lambda col: pl.BlockSpec((1, DEC_SEQ, 256), lambda b, t: (b, 0, col))
    cache = lambda w: pl.BlockSpec((1, 1, PAST_LEN, w), lambda b, t: (b, layer, 0, 0))
    return _Rider(
        body=body,
        in_specs=[pl.BlockSpec(memory_space=pltpu.SMEM), tile(Z_AQ // 256), whole(Z_AK // 256), tile(Z_AG // 256),
                  cache(kvw), cache(kvw), tile(Z_DQ // 256), whole(Z_DK // 256), whole(Z_DV // 256),
                  tile(Z_DG // 256), cache(hw), cache(hw), _layer_spec(rings.shape[1:], layer)],
        args=[sink, z, z, z, ck_a, cv_a, z, z, z, z, ck_n, cv_n, rings],
        out_shape=[], out_specs=[],
        scratch=[pltpu.VMEM((PAST_LEN, kvw), BF16), pltpu.VMEM((PAST_LEN, kvw), BF16),
                 pltpu.VMEM((PAST_LEN, hw), BF16), pltpu.VMEM((PAST_LEN, hw), BF16),
                 pltpu.VMEM((NAT_HEADS, NAT_BIAS_ROWS, GRID_W, LANE), F32),
                 pltpu.VMEM((TM, BRANCH_W), BF16), pltpu.VMEM((TM, BRANCH_W), BF16)],
        branches=True)


def _mla_body(q_ref, ckv_ref, kpe_ref, g_ref, cckv_ref, ckpe_ref, wkv_ref, o_ref, k_s, v_s):
    nlat = DEC_SEQ
    step = pl.program_id(0) * pl.num_programs(1) + pl.program_id(1)

    @pl.when(step % MLA_TILES == 0)
    def _():
        rows = 512

        def fill(r0, ckv, kpe_blk):
            kv = _dot(ckv, wkv_ref[...])
            for h, kh in enumerate(_mla_keys(kv, kpe_blk)):
                k_s[h, r0:r0 + rows, :] = kh
                v_s[h, r0:r0 + rows, :] = kv[:, h * LANE:(h + 1) * LANE].astype(BF16)

        for c in range(nlat // rows):
            fill(c * rows, ckv_ref[0, c * rows:(c + 1) * rows, :], kpe_ref[0, c * rows:(c + 1) * rows, :].astype(F32))
        ckpe = ckpe_ref[0, 0]
        ckpe_blk = jnp.concatenate([jnp.zeros((PAST_LEN, KPE_LANE), F32), ckpe,
                                    jnp.zeros((PAST_LEN, LANE - KPE_LANE - MLA_ROPE), F32)], axis=1)
        fill(nlat, cckv_ref[0, 0].astype(BF16), ckpe_blk)

    def store(pair, val):
        o_ref[0, :, pair * LANE:(pair + 1) * LANE] = val

    _mla_heads(lambda h: q_ref[0, :, LANE * h:LANE * (h + 1)], lambda h: k_s[h], lambda h: v_s[h],
               lambda p: _silu(g_ref[0, :, p * LANE:(p + 1) * LANE].astype(F32)), store)


def _mla_rider(z, cache_ckv, cache_kpe, wkv, layer, nt):
    nkeys = DEC_SEQ + PAST_LEN
    assert DEC_BATCH * MLA_TILES == POST_STEPS
    bt = lambda b, t: ((b * nt + t) // MLA_TILES, (b * nt + t) % MLA_TILES)
    tile = lambda col: (lambda b, t: bt(b, t) + (col,))
    whole = lambda col: (lambda b, t: (bt(b, t)[0], 0, col))
    cache = lambda b, t: (bt(b, t)[0], layer, 0, 0)
    return _Rider(
        body=_mla_body,
        in_specs=[
            pl.BlockSpec((1, TQ_MLA, MLA_HEADS * LANE), tile(Z_QM // (MLA_HEADS * LANE))),
            pl.BlockSpec((1, DEC_SEQ, LANE), whole(Z_CKV // LANE)),
            pl.BlockSpec((1, DEC_SEQ, LANE), whole(Z_KPE // LANE)),
            pl.BlockSpec((1, TQ_MLA, 256), tile(Z_BG // 256)),
            pl.BlockSpec((1, 1, PAST_LEN, MLA_KV_LORA), cache),
            pl.BlockSpec((1, 1, PAST_LEN, MLA_ROPE), cache),
            _layer_spec((MLA_KV_LORA, 512), layer),
        ],
        args=[z, z, z, z, cache_ckv, cache_kpe, wkv],
        out_shape=[jax.ShapeDtypeStruct((DEC_BATCH, DEC_SEQ, BRANCH_W), BF16)],
        out_specs=[pl.BlockSpec((1, TQ_MLA, BRANCH_W), tile(0))],
        scratch=[pltpu.VMEM((MLA_HEADS, nkeys, LANE), BF16), pltpu.VMEM((MLA_HEADS, nkeys, LANE), BF16)])


def _s5_mats(lam_re, lam_im, log_dt, b_re, b_im, c_re, c_im):
    t = S5_T
    hp = lax.Precision.HIGHEST
    lre = jnp.minimum(lam_re, -1e-4)
    lim = lam_im
    dt = jnp.exp(log_dt)[..., None]
    er, ei = lre * dt, lim * dt
    mag = jnp.exp(er)
    are, aim = mag * jnp.cos(ei), mag * jnp.sin(ei)
    den = lre * lre + lim * lim
    qre = ((are - 1.0) * lre + aim * lim) / den
    qim = (aim * lre - (are - 1.0) * lim) / den
    bbr = qre[..., None] * b_re - qim[..., None] * b_im
    bbi = qre[..., None] * b_im + qim[..., None] * b_re
    n = jnp.arange(t + 1, dtype=F32)[:, None, None, None]
    pmag = jnp.exp(er[None] * n)
    pr, pi = pmag * jnp.cos(ei[None] * n), pmag * jnp.sin(ei[None] * n)

    def c_times_pow(d, pw_r, pw_i):
        rep = lambda a: jnp.repeat(jnp.transpose(a, (1, 2, 0)), S5_CH, axis=-1)
        til = lambda a: jnp.tile(jnp.transpose(a, (0, 2, 1)), (1, 1, t))
        cr, ci, ar, ai = til(c_re[d]), til(c_im[d]), rep(pw_r), rep(pw_i)
        return cr * ar - ci * ai, cr * ai + ci * ar

    def lag_kernels(d, wr, wi):
        return (jnp.einsum('gpk,gpx->gkx', bbr[d], wr, precision=hp)
                - jnp.einsum('gpk,gpx->gkx', bbi[d], wi, precision=hp))

    wf0 = c_times_pow(0, pr[:t, 0], pi[:t, 0])
    wf1 = c_times_pow(0, pr[1:, 0], pi[1:, 0])
    wb = c_times_pow(1, pr[1:, 1][::-1], pi[1:, 1][::-1])
    kb0 = (jnp.einsum('gpk,gcp->gkc', bbr[1], c_re[1], precision=hp)
           - jnp.einsum('gpk,gcp->gkc', bbi[1], c_im[1], precision=hp))
    kf = lag_kernels(0, *wf0) + jnp.pad(kb0, ((0, 0), (0, 0), (0, S5_TC - S5_CH)))
    kb = lag_kernels(1, *wb) * jnp.asarray(np.arange(S5_TC) >= S5_CH, F32)
    ring = jnp.concatenate([kf, kb], axis=-1)

    def state_cols(pw_r, pw_i, d):
        rep = lambda a: jnp.repeat(jnp.transpose(a, (1, 0, 2)), S5_CH, axis=1)
        til = lambda a: jnp.tile(jnp.transpose(a, (0, 2, 1)), (1, t, 1))
        ar, ai, br, bi = rep(pw_r), rep(pw_i), til(bbr[d]), til(bbi[d])
        return ar * br - ai * bi, ar * bi + ai * br

    sfr, sfi = state_cols(pr[:t, 0][::-1], pi[:t, 0][::-1], 0)
    sbr, sbi = state_cols(pr[:t, 1], pi[:t, 1], 1)
    m_state = jnp.concatenate([sfr, sbr, sfi, sbi], axis=-1)
    m_out = jnp.concatenate([wf1[0], wb[0], -wf1[1], -wb[1]], axis=1)

    a_t = jnp.stack([jnp.concatenate([pr[t, 0], pr[t, 1]], axis=-1),
                     jnp.concatenate([pi[t, 0], pi[t, 1]], axis=-1)], axis=1)
    return ring, m_state.astype(BF16), m_out.astype(BF16), a_t


def _s5_kernel(uc_ref, ul_ref, ring_ref, ms_ref, mo_ref, at_ref, h0_ref, yc_ref, yl_ref, hend_ref,
               s_s, hf_s, hb_s, m_s):
    p = S5_STATE
    u = jnp.concatenate([uc_ref[0], ul_ref[0]], axis=0)
    s = _dot(u, ms_ref[0])
    s_s[0] = s[:, 0:2 * p]
    s_s[1] = s[:, 2 * p:4 * p]
    are = at_ref[0, 0:1, :]
    aim = at_ref[0, 1:2, :]
    fwd_lane = lax.broadcasted_iota(jnp.int32, (1, 2 * p), 1) < p

    for k in range(S5_CH):
        rk = jnp.broadcast_to(ring_ref[0, k:k + 1, :], (S5_T, 2 * S5_TC))
        rolled = pltpu.roll(rk, 0, 1, stride=S5_CH, stride_axis=0)
        for blk in range(S5_TC // LANE):
            m_s[blk, pl.ds(k, S5_T, stride=S5_CH), :] = rolled[:, blk * LANE:(blk + 1) * LANE]

    def scan(row0, nk, nb, hre, him):
        for st in range(nk):
            rf = pl.ds(row0 + st, nb, stride=nk)
            rb = pl.ds(row0 + nk - 1 - st, nb, stride=nk)
            hf_s[0, rf, :] = hre
            hf_s[1, rf, :] = him
            hb_s[0, rb, :] = hre
            hb_s[1, rb, :] = him
            sre = jnp.where(fwd_lane, s_s[0, rf, :], s_s[0, rb, :])
            sim = jnp.where(fwd_lane, s_s[1, rf, :], s_s[1, rb, :])
            hre, him = are * hre - aim * him + sre, are * him + aim * hre + sim
        return hre, him

    zero = jnp.zeros((BATCH, 2 * p), F32)
    hre, him = scan(0, S5_NK_CTX, BATCH, zero, zero)
    hend_ref[0] = jnp.concatenate([hre, him], axis=1)
    scan(S5_ROWS, S5_NK_LAT, DEC_BATCH, h0_ref[0, :, 0:2 * p], h0_ref[0, :, 2 * p:4 * p])
    hst = jnp.concatenate([jnp.where(fwd_lane, hf_s[0], hb_s[0]), jnp.where(fwd_lane, hf_s[1], hb_s[1])], axis=1)
    m_intra = jnp.concatenate([m_s[blk] for blk in range(S5_TC // LANE)], axis=1).astype(BF16)
    y = _dot(u, m_intra) + _dot(hst.astype(BF16), mo_ref[0])
    yc_ref[0] = y[0:S5_ROWS]
    yl_ref[0] = y[S5_ROWS:2 * S5_ROWS]


def _s5_call(ut_ctx, ut_lat, ring, m_state, m_out, a_t, h0, layer):
    g3 = lambda g: (g, 0, 0)
    lg = lambda shape: pl.BlockSpec((None, 1) + shape, lambda g: (layer, g, 0, 0))
    yshape = jax.ShapeDtypeStruct((S5_GROUPS, S5_ROWS, S5_TC), F32)
    uspec = pl.BlockSpec((1, S5_ROWS, S5_TC), g3)
    return pl.pallas_call(
        _s5_kernel,
        out_shape=(yshape, yshape, jax.ShapeDtypeStruct((S5_GROUPS, BATCH, 4 * S5_STATE), F32)),
        grid=(S5_GROUPS,),
        in_specs=[
            uspec, uspec,
            lg((S5_CH, 2 * S5_TC)),
            lg((S5_TC, 4 * S5_STATE)),
            lg((4 * S5_STATE, S5_TC)),
            lg((2, 2 * S5_STATE)),
            lg((DEC_BATCH, 4 * S5_STATE)),
        ],
        out_specs=(uspec, uspec, pl.BlockSpec((1, BATCH, 4 * S5_STATE), g3)),
        scratch_shapes=[pltpu.VMEM((2, 2 * S5_ROWS, LANE), F32)] * 3 + [pltpu.VMEM((S5_TC // LANE, S5_TC, LANE), F32)],
        compiler_params=_cparams(("arbitrary",), 32),
        name="s5_chunks",
    )(ut_ctx, ut_lat, ring, m_state, m_out, a_t, h0)


_POST_IN = ('x', 'mod', 'ng', 'ba', 'bb', 'bd', 'y', 'u', 'cg', 'sd', 'wglu', 'wm', 'wbr', 'wo', 'fg')


class _Rider(NamedTuple):
    body: Callable
    in_specs: list
    args: list
    out_shape: list
    out_specs: list
    scratch: list
    branches: bool = False


def _post_kernel(*refs, final, rider):
    names = [n for n in _POST_IN if not (rider.branches and n in ('ba', 'bd'))]
    r = dict(zip(names, refs))
    n0, n_in, n_out = len(names), len(rider.in_specs), len(rider.out_specs)
    o_ref = refs[n0 + n_in]
    ys_s = refs[n0 + n_in + 1 + n_out]
    rider_scratch = refs[n0 + n_in + 2 + n_out:]
    rider.body(*refs[n0:n0 + n_in], *refs[n0 + n_in + 1:n0 + n_in + 1 + n_out], *rider_scratch)
    ba, bd = (rider_scratch[-2][...], rider_scratch[-1][...]) if rider.branches else (r['ba'][0], r['bd'][0])
    x_ref, mod_ref, ng_ref, y_ref = r['x'], r['mod'], r['ng'], r['y']
    x = x_ref[0]
    hb = _ada_h(x, mod_ref, ng_ref).astype(BF16)
    for m in range(S5_T // 8):
        for hf in range(2):
            vs = _slot_transpose([y_ref[8 * hf + gp, :, m * LANE:(m + 1) * LANE] for gp in range(8)])
            for s in range(8):
                ys_s[hf, pl.ds(8 * m + s, TM_CHUNKS, stride=S5_T), :] = vs[s]
    yv = jnp.concatenate([ys_s[0], ys_s[1]], axis=1) + r['sd'][...] * r['u'][0].astype(F32)
    gel = 0.5 * yv * (1.0 + jnp.tanh(math.sqrt(2.0 / math.pi) * (yv + 0.044715 * (yv * yv * yv))))
    gl = _dot(gel.astype(BF16), r['wglu'][...])
    oc = gl[:, 0:BRANCH_W] * _sigmoid(gl[:, BRANCH_W:2 * BRANCH_W])
    bc = (oc * _silu(r['cg'][0].astype(F32))).astype(BF16)
    acc = None
    for k, br in enumerate((ba, r['bb'][0], bc, bd)):
        proj = _dot(br, r['wbr'][k])
        mg = _sigmoid(_dot(hb, r['wm'][:, k * D_MODEL:(k + 1) * D_MODEL]))
        acc = mg * proj if acc is None else acc + mg * proj
    y = _dot(acc.astype(BF16), r['wo'][...])
    out = x + mod_ref[0, :, 2 * D_MODEL:3 * D_MODEL] * y
    if final:
        out = _rms(out, r['fg'][...])
    o_ref[0] = out


def _post_call(x, mods, mod_row0, p, ba, bb, bd, y_s5, z, fg, final, layer, name, rider):
    nb, length, _ = x.shape
    nt = length // TM
    assert nb * nt == POST_STEPS
    tok = lambda b, t: (b, t, 0)
    br_spec = pl.BlockSpec((1, TM, BRANCH_W), tok)
    branch_args = [bb] if rider.branches else [ba, bb, bd]
    return pl.pallas_call(
        functools.partial(_post_kernel, final=final, rider=rider),
        out_shape=tuple([jax.ShapeDtypeStruct((nb, length, D_MODEL), F32)] + rider.out_shape),
        grid=(nb, nt),
        in_specs=[
            pl.BlockSpec((1, TM, D_MODEL), tok),
            _mod_spec(layer, mod_row0),
            _layer_spec((1, D_MODEL), layer),
        ] + [br_spec] * len(branch_args) + [
            pl.BlockSpec((S5_GROUPS, TM_CHUNKS, S5_TC), lambda b, t: (0, b * nt + t, 0)),
            pl.BlockSpec((1, TM, 256), lambda b, t: (b, t, Z_CU // 256)),
            pl.BlockSpec((1, TM, 256), lambda b, t: (b, t, Z_CG // 256)),
            _layer_spec((1, BRANCH_W), layer),
            _layer_spec((BRANCH_W, 2 * BRANCH_W), layer),
            _layer_spec((D_MODEL, N_BRANCH * D_MODEL), layer),
            _layer_spec((N_BRANCH, BRANCH_W, D_MODEL), layer),
            _layer_spec((D_MODEL, D_MODEL), layer),
            pl.BlockSpec((1, D_MODEL), lambda b, t: (0, 0)),
        ] + rider.in_specs,
        out_specs=tuple([pl.BlockSpec((1, TM, D_MODEL), tok)] + rider.out_specs),
        scratch_shapes=[pltpu.VMEM((2, TM, LANE), F32)] + rider.scratch,
        compiler_params=_cparams(("arbitrary", "arbitrary"), 56),
        name=name,
    )(x, mods, p['ng'], *branch_args, y_s5, z, z, p['s5d'], p['wglu'], p['wm'], p['wbr'], p['wo'], fg, *rider.args)


def _rope_tables():
    t = np.arange(DEC_SEQ)
    row = (t // GRID_W).astype(np.float64)
    col = (t % GRID_W).astype(np.float64)

    def pattern(half):
        inv = ROPE_BASE ** (-np.arange(half, dtype=np.float64) / half)
        zeros = np.zeros((DEC_SEQ, half))
        cs, s_up, s_lo = [], [], []
        for pos in (row, col):
            ang = pos[:, None] * inv[None, :]
            c, s = np.cos(ang), np.sin(ang)
            cs += [c, c]
            s_up += [zeros, s]
            s_lo += [-s, zeros]
        return [np.concatenate(parts, axis=1) for parts in (cs, s_up, s_lo)]

    tab_a = np.stack([np.tile(part, (1, LANE // HEAD_DIM)) for part in pattern(HEAD_DIM // 4)])
    ident = [np.ones, np.zeros, np.zeros]
    tab_m = np.stack([np.concatenate([fill((DEC_SEQ, KPE_LANE)), part,
                                      fill((DEC_SEQ, LANE - KPE_LANE - MLA_ROPE))], axis=1)
                      for fill, part in zip(ident, pattern(MLA_ROPE // 4))])
    return jnp.asarray(tab_a, F32), jnp.asarray(tab_m, F32)


_W_IN_MOVES = tuple(
    [(HEAD_DIM * i, HEAD_DIM * h, HEAD_DIM) for i, h in enumerate(A_HEAD_ORDER)]
    + [(256, 256, 256), (512, 1952, 768), (R_CQ, 768, 384), (R_KPE + KPE_LANE, 1152, MLA_ROPE), (R_CU, 1440, 256)]
    + [(R_G + HEAD_DIM * i, 512 + HEAD_DIM * h, HEAD_DIM) for i, h in enumerate(A_HEAD_ORDER)]
    + [(R_G + 256, 1184, 256), (R_G + 512, 1696, 256), (R_G + 768, 2720, 256)])
W_IN_ROWS = 256


def _w_in_kernel(w_ref, o_ref):
    o_ref[0, :, R_KPE:R_CU] = jnp.zeros((W_IN_ROWS, LANE), BF16)
    for dst, src, width in _W_IN_MOVES:
        o_ref[0, :, dst:dst + width] = w_ref[0, :, src:src + width].astype(BF16)


def _w_in_call(w_in):
    return pl.pallas_call(
        _w_in_kernel,
        out_shape=jax.ShapeDtypeStruct((DEPTH, D_MODEL, R_W), BF16),
        grid=(DEPTH, D_MODEL // W_IN_ROWS),
        in_specs=[pl.BlockSpec((1, W_IN_ROWS, w_in.shape[-1]), lambda i, r: (i, r, 0))],
        out_specs=pl.BlockSpec((1, W_IN_ROWS, R_W), lambda i, r: (i, r, 0)),
        compiler_params=_cparams(("arbitrary", "arbitrary"), 32),
        name="w_in_layout",
    )(w_in)


def _cast_kernel(w_ref, o_ref):
    o_ref[...] = w_ref[...].astype(BF16)


def _cast_call(w):
    _, rows, cols = w.shape
    spec = pl.BlockSpec((1, W_IN_ROWS, cols), lambda i, r: (i, r, 0))
    return pl.pallas_call(
        _cast_kernel,
        out_shape=jax.ShapeDtypeStruct(w.shape, BF16),
        grid=(DEPTH, rows // W_IN_ROWS),
        in_specs=[spec],
        out_specs=spec,
        compiler_params=_cparams(("arbitrary", "arbitrary"), 32),
        name="w_cast",
    )(w)


def _reorder_w_q_up(w):
    w = w.reshape(MLA_Q_LORA, MLA_HEADS, MLA_NOPE + MLA_ROPE)
    w = jnp.concatenate([w, jnp.zeros((MLA_Q_LORA, MLA_HEADS, LANE - MLA_NOPE - MLA_ROPE), w.dtype)], axis=-1)
    return w.reshape(MLA_Q_LORA, MLA_HEADS * LANE)


def _reorder_w_branch(w):
    wa = w[0].reshape(A_HEADS, HEAD_DIM, D_MODEL)
    wa = jnp.concatenate([wa[h] for h in A_HEAD_ORDER], axis=0)
    return jnp.concatenate([wa[None], w[1:]], axis=0)


def kernel(x_prompt, x_sample, cache_a_k, cache_a_v, cache_mla_ckv, cache_mla_kpe, cache_na_k, cache_na_v,
           state_s5_re, state_s5_im, c, c_ctx, w_mod, b_mod, norm_g, w_in, w_merge, a_sink,
           mla_q_norm, mla_w_q_up, mla_kv_norm, mla_w_kv_up, s5_lam_re, s5_lam_im, s5_log_dt,
           s5_b_re, s5_b_im, s5_c_re, s5_c_im, s5_d, s5_w_glu, na_rpb, w_branch, w_out, final_norm_g):
    n_ctx = BATCH * SEQ
    conds = jnp.concatenate([c, c_ctx[None, :], jnp.zeros((3, D_MODEL), F32)], axis=0)
    mods = _mod_call(conds, w_mod, b_mod).reshape(DEPTH, 8, 1, 3 * D_MODEL)
    tabs = _rope_tables()
    ck_a = cache_a_k.reshape(DEC_BATCH, DEPTH, PAST_LEN, A_KV_HEADS * HEAD_DIM)
    cv_a = cache_a_v.reshape(DEC_BATCH, DEPTH, PAST_LEN, A_KV_HEADS * HEAD_DIM)
    ck_n = cache_na_k.reshape(DEC_BATCH, DEPTH, PAST_LEN, NAT_HEADS * HEAD_DIM)
    cv_n = cache_na_v.reshape(DEC_BATCH, DEPTH, PAST_LEN, NAT_HEADS * HEAD_DIM)
    fg = final_norm_g.reshape(1, D_MODEL)
    caches = None

    p = dict(
        ng=norm_g.reshape(DEPTH, 1, D_MODEL),
        w_raw=_w_in_call(w_in),
        qnorm=mla_q_norm.reshape(DEPTH, 1, MLA_Q_LORA),
        wq=jax.vmap(_reorder_w_q_up)(mla_w_q_up).astype(BF16),
        kvnorm=mla_kv_norm.reshape(DEPTH, 1, MLA_KV_LORA),
        wkv=mla_w_kv_up.astype(BF16),
        wm=_cast_call(w_merge),
        wbr=jax.vmap(_reorder_w_branch)(w_branch).astype(BF16),
        wo=_cast_call(w_out),
        wglu=s5_w_glu.astype(BF16),
        s5d=s5_d.reshape(DEPTH, 1, BRANCH_W),
    )
    ring, m_state, m_out, a_t = jax.vmap(_s5_mats)(s5_lam_re, s5_lam_im, s5_log_dt, s5_b_re, s5_b_im, s5_c_re, s5_c_im)
    nat_bias = _nat_bias_rings(na_rpb * LOG2E)
    h0 = jnp.concatenate([state_s5_re[:, :, 0], state_s5_re[:, :, 1], state_s5_im[:, :, 0], state_s5_im[:, :, 1]],
                         axis=-1)
    h0 = jnp.transpose(h0, (1, 2, 0, 3))

    yp = x_prompt.reshape(1, n_ctx, D_MODEL)
    ys = x_sample
    hends = []
    for i in range(DEPTH):
        res = _inproj_call(yp, mods, DEC_BATCH, p, None, caches, i, "inproj_ctx")
        z_ctx, ut_ctx, caches = res[0], res[1], list(res[2:])
        zc = z_ctx.reshape(BATCH, SEQ, Z_W)
        z_lat, ut_lat, oa_c, ob_c, od_c = _inproj_call(ys, mods, 0, p, tabs, None, i, "inproj_lat_ctx_attn",
                                                       _ctx_attn_rider(zc, a_sink, p['wkv'], i, DEC_SEQ // TM))

        y_ctx, y_lat, hend = _s5_call(ut_ctx, ut_lat, ring, m_state, m_out, a_t, h0, i)
        hends.append(hend)

        flat = lambda a: a.reshape(1, n_ctx, BRANCH_W)
        final = i == DEPTH - 1
        yp, ob_l = _post_call(yp, mods, DEC_BATCH, p, flat(oa_c), flat(ob_c), flat(od_c), y_ctx, z_ctx, fg, final, i,
                              "post_ctx_lat_mla", _mla_rider(z_lat, cache_mla_ckv, cache_mla_kpe, p['wkv'], i, POST_STEPS))
        ys, = _post_call(ys, mods, 0, p, None, ob_l, None, y_lat, z_lat, fg, final, i, "post_lat_win_nat",
                         _lat_attn_rider(z_lat, a_sink, ck_a, cv_a, ck_n, cv_n, nat_bias, i))

    hend = jnp.stack(hends).reshape(DEPTH, S5_GROUPS, BATCH, 2, 2, S5_STATE)
    hend = jnp.transpose(hend, (3, 2, 0, 4, 1, 5))
    ak, av, ckv, kpe, nk, nv = caches
    heads = lambda a, h: a.reshape(BATCH, DEPTH, SEQ, h, HEAD_DIM)
    return (yp.reshape(BATCH, SEQ, D_MODEL), ys, heads(ak, A_KV_HEADS), heads(av, A_KV_HEADS), ckv, kpe,
            heads(nk, NAT_HEADS), heads(nv, NAT_HEADS), hend[0], hend[1])
```

```python
import functools
import math
from typing import Callable, NamedTuple

import numpy as np
import jax
import jax.numpy as jnp
from jax import lax
from jax.experimental import pallas as pl
from jax.experimental.pallas import tpu as pltpu

F32 = jnp.float32
BF16 = jnp.bfloat16

D_MODEL = 1024
BATCH = 32
SEQ = 256
DEPTH = 2
DEC_BATCH = 4
DEC_SEQ = 2048
PAST_LEN = 512
GRID_W = 64
HEAD_DIM = 64
BRANCH_W = 256
N_BRANCH = 4
Q_BLOCK = 128
A_HEADS = 4
A_KV_HEADS = 2
A_GROUP = A_HEADS // A_KV_HEADS
A_WINDOW = 128
MLA_HEADS = 4
MLA_Q_LORA = 256
MLA_KV_LORA = 128
MLA_NOPE = 64
MLA_ROPE = 32
MLA_V = 64
S5_CH = 16
S5_GROUPS = BRANCH_W // S5_CH
S5_STATE = 64
NAT_HEADS = 4
NAT_ROWS = 8
NAT_COLS = 16
ROPE_BASE = 10000.0
EPS = 1e-6
NEG = -1e30
LOG2E = 1.4426950408889634
Q_SCALE = HEAD_DIM ** -0.5 * LOG2E
MLA_Q_SCALE = (MLA_NOPE + MLA_ROPE) ** -0.5 * LOG2E

LANE = 128

R_AV, R_CQ, R_CKV, R_KPE, R_CU, R_G, R_W = 384, 1280, 1536, 1664, 1792, 2048, 3072
KPE_LANE = 64
A_HEAD_ORDER = (0, 2, 1, 3)
Z_AQ, Z_AK, Z_AV = 0, 256, 384
Z_DQ, Z_DK, Z_DV = 512, 768, 1024
Z_CKV, Z_KPE = 1280, 1408
Z_QM = 1536
Z_AG, Z_BG, Z_CG, Z_DG = 2048, 2304, 2560, 2816
Z_CU = 3072
Z_W = 3328

S5_T = 32
S5_TC = S5_T * S5_CH
S5_ROWS = 256
S5_NK_CTX = SEQ // S5_T
S5_NK_LAT = DEC_SEQ // S5_T

TM = 512
TM_CHUNKS = TM // S5_T
TQ_MLA = 512
MLA_TILES = DEC_SEQ // TQ_MLA
POST_STEPS = BATCH * SEQ // TM
CTX_BPS = 2
NAT_RPS = 4
NAT_UNION = NAT_ROWS + NAT_RPS
NAT_BIAS_PAD = NAT_RPS
NAT_BIAS_ROWS = 2 * NAT_ROWS - 1 + 2 * NAT_BIAS_PAD


def _cparams(sem, vmem_mb):
    return pltpu.CompilerParams(dimension_semantics=sem, vmem_limit_bytes=vmem_mb * 1024 * 1024)


def _sigmoid(x):
    return 1.0 / (1.0 + jnp.exp(-x))


def _silu(x):
    return x * _sigmoid(x)


def _rms(x, g):
    return x * lax.rsqrt(jnp.mean(x * x, axis=-1, keepdims=True) + EPS) * g


def _dot(a, b):
    return jnp.dot(a, b, preferred_element_type=F32)


def _dot_nt(a, b):
    return lax.dot_general(a, b, (((1,), (1,)), ((), ())), preferred_element_type=F32)


def _slot_transpose(vs):
    lane = lax.broadcasted_iota(jnp.int32, vs[0].shape, 1)
    vs = list(vs)
    for d in (4, 2, 1):
        keep = (lane & (S5_CH * d)) == 0
        nxt = list(vs)
        for lo in range(8):
            if lo & d:
                continue
            hi = lo + d
            nxt[lo] = jnp.where(keep, vs[lo], pltpu.roll(vs[hi], S5_CH * d, 1))
            nxt[hi] = jnp.where(keep, pltpu.roll(vs[lo], LANE - S5_CH * d, 1), vs[hi])
        vs = nxt
    return vs


def _mod_kernel(c_ref, w_ref, b_ref, o_ref):
    s = _silu(c_ref[...])
    o_ref[0] = _dot(s.astype(BF16), w_ref[0].astype(BF16)) + b_ref[0]


def _mod_call(conds, w_mod, b_mod):
    nc = 512
    return pl.pallas_call(
        _mod_kernel,
        out_shape=jax.ShapeDtypeStruct((DEPTH, 8, 3 * D_MODEL), F32),
        grid=(DEPTH, 3 * D_MODEL // nc),
        in_specs=[
            pl.BlockSpec((8, D_MODEL), lambda i, j: (0, 0)),
            pl.BlockSpec((1, D_MODEL, nc), lambda i, j: (i, 0, j)),
            pl.BlockSpec((1, 1, nc), lambda i, j: (i, 0, j)),
        ],
        out_specs=pl.BlockSpec((1, 8, nc), lambda i, j: (i, 0, j)),
        compiler_params=_cparams(("arbitrary", "arbitrary"), 32),
        name="mod_rows",
    )(conds, w_mod, b_mod.reshape(DEPTH, 1, 3 * D_MODEL))


def _ada_h(x, mod_ref, ng_ref):
    shift = mod_ref[0, :, 0:D_MODEL]
    scale = mod_ref[0, :, D_MODEL:2 * D_MODEL]
    return _rms(x, ng_ref[...]) * (1.0 + scale) + shift


def _rope_block(xs, tab_ref, shift):
    return (xs * tab_ref[0] + pltpu.roll(xs, shift, 1) * tab_ref[1]
            + pltpu.roll(xs, LANE - shift, 1) * tab_ref[2])


CACHE_DIMS = ((SEQ, 128), (SEQ, 128), (SEQ, MLA_KV_LORA), (MLA_ROPE, SEQ), (SEQ, 256), (SEQ, 256))
N_CACHE = len(CACHE_DIMS)


def _inproj_kernel(*refs, rope, rider, first):
    x_ref, mod_ref, ng_ref, w_ref, qn_ref, wq_ref, kvn_ref = refs[:7]
    if rope:
        ta_ref, tm_ref = refs[7:9]
        n_in, n_out = len(rider.in_specs), len(rider.out_specs)
        z_ref, ut_ref = refs[9 + n_in:11 + n_in]
        u_s = refs[11 + n_in + n_out]
        rider.body(*refs[9:9 + n_in], *refs[11 + n_in:11 + n_in + n_out], *refs[12 + n_in + n_out:])
    else:
        z_ref, ut_ref, ak_ref, av_ref, ckv_ref, kpe_ref, nk_ref, nv_ref, u_s = refs[7 + (0 if first else N_CACHE):]
    h = _ada_h(x_ref[0], mod_ref, ng_ref)
    raw = _dot(h.astype(BF16), w_ref[...])
    for j in range(R_AV // LANE):
        blk = raw[:, j * LANE:(j + 1) * LANE]
        if j < Z_AK // LANE:
            blk = blk * Q_SCALE
        if rope:
            blk = _rope_block(blk, ta_ref, 16)
        z_ref[0, :, j * LANE:(j + 1) * LANE] = blk.astype(BF16)
    z_ref[0, :, Z_AV:Z_DQ] = raw[:, Z_AV:Z_DQ].astype(BF16)
    z_ref[0, :, Z_DQ:Z_DK] = (raw[:, Z_DQ:Z_DK] * Q_SCALE).astype(BF16)
    z_ref[0, :, Z_DK:R_CQ] = raw[:, Z_DK:R_CQ].astype(BF16)
    qn = _rms(raw[:, R_CQ:R_CKV], qn_ref[...])
    q = _dot(qn.astype(BF16), wq_ref[...]) * MLA_Q_SCALE
    kp = raw[:, R_KPE:R_CU]
    ckv = _rms(raw[:, R_CKV:R_KPE], kvn_ref[...])
    for hh in range(MLA_HEADS):
        qh = q[:, hh * LANE:(hh + 1) * LANE]
        if rope:
            qh = _rope_block(qh, tm_ref, 8)
        z_ref[0, :, Z_QM + hh * LANE:Z_QM + (hh + 1) * LANE] = qh.astype(BF16)
    z_ref[0, :, Z_CKV:Z_KPE] = ckv.astype(BF16)
    z_ref[0, :, Z_KPE:Z_QM] = (_rope_block(kp, tm_ref, 8) if rope else kp).astype(BF16)
    z_ref[0, :, Z_AG:Z_CU] = raw[:, R_G:R_W].astype(BF16)
    z_ref[0, :, Z_CU:Z_W] = raw[:, R_CU:R_G].astype(BF16)
    if not rope:
        half = TM // 2
        for e in range(2):
            rows = slice(e * half, (e + 1) * half)
            ak_ref[e, 0] = raw[rows, Z_AK:Z_AV]
            av_ref[e, 0] = raw[rows, Z_AV:Z_DQ]
            nk_ref[e, 0] = raw[rows, Z_DK:Z_DV]
            nv_ref[e, 0] = raw[rows, Z_DV:R_CQ]
            ckv_ref[e, 0] = ckv[rows]
            kpe_ref[e, 0] = kp[rows].T[KPE_LANE:KPE_LANE + MLA_ROPE]
            if first:
                for cref in (ak_ref, av_ref, nk_ref, nv_ref, ckv_ref, kpe_ref):
                    cref[e, 1:DEPTH] = jnp.zeros((DEPTH - 1,) + cref.shape[2:], F32)
    for hf in range(2):
        u_s[hf] = raw[:, R_CU + hf * LANE:R_CU + (hf + 1) * LANE]
    for m in range(S5_T // 8):
        for hf in range(2):
            outs = _slot_transpose([u_s[hf, pl.ds(8 * m + s, TM_CHUNKS, stride=S5_T), :] for s in range(8)])
            for gp in range(8):
                ut_ref[8 * hf + gp, :, m * LANE:(m + 1) * LANE] = outs[gp].astype(BF16)


def _layer_spec(shape, layer):
    zeros = (0,) * len(shape)
    return pl.BlockSpec((None,) + tuple(shape), lambda *_: (layer,) + zeros)


def _mod_spec(layer, row0):
    return pl.BlockSpec((None, 1, 1, 3 * D_MODEL), lambda b, t: (layer, row0 + b, 0, 0))


def _inproj_call(x, mods, mod_row0, p, tabs, caches, layer, name, rider=None):
    nb, length, _ = x.shape
    rope = tabs is not None
    nt = length // TM
    in_specs = [
        pl.BlockSpec((1, TM, D_MODEL), lambda b, t: (b, t, 0)),
        _mod_spec(layer, mod_row0),
        _layer_spec((1, D_MODEL), layer),
        _layer_spec((D_MODEL, R_W), layer),
        _layer_spec((1, MLA_Q_LORA), layer),
        _layer_spec((MLA_Q_LORA, MLA_HEADS * LANE), layer),
        _layer_spec((1, MLA_KV_LORA), layer),
    ]
    args = [x, mods, p['ng'], p['w_raw'], p['qnorm'], p['wq'], p['kvnorm']]
    out_shape = [jax.ShapeDtypeStruct((nb, length, Z_W), BF16),
                 jax.ShapeDtypeStruct((S5_GROUPS, S5_ROWS, S5_TC), BF16)]
    out_specs = [pl.BlockSpec((1, TM, Z_W), lambda b, t: (b, t, 0)),
                 pl.BlockSpec((S5_GROUPS, TM_CHUNKS, S5_TC), lambda b, t: (0, b * nt + t, 0))]
    aliases = {}
    scratch = [pltpu.VMEM((2, TM, LANE), F32)]
    if rope:
        assert nb * nt == POST_STEPS
        in_specs += [pl.BlockSpec((3, TM, LANE), lambda b, t: (0, t, 0))] * 2 + rider.in_specs
        args += list(tabs) + rider.args
        out_shape += rider.out_shape
        out_specs += rider.out_specs
        scratch += rider.scratch
    else:
        for k, dims in enumerate(CACHE_DIMS):
            out_shape.append(jax.ShapeDtypeStruct((BATCH, DEPTH) + dims, F32))
            if caches is None:
                out_specs.append(pl.BlockSpec((2, DEPTH) + dims, lambda b, t: (t, 0, 0, 0)))
            else:
                in_specs.append(pl.BlockSpec(memory_space=pl.ANY))
                args.append(caches[k])
                out_specs.append(pl.BlockSpec((2, 1) + dims, lambda b, t: (t, layer, 0, 0)))
                aliases[7 + k] = 2 + k
    return pl.pallas_call(
        functools.partial(_inproj_kernel, rope=rope, rider=rider, first=not rope and caches is None),
        out_shape=tuple(out_shape),
        grid=(nb, nt),
        in_specs=in_specs,
        out_specs=tuple(out_specs),
        scratch_shapes=scratch,
        input_output_aliases=aliases,
        compiler_params=_cparams(("arbitrary", "arbitrary"), 56),
        name=name,
    )(*args)


def _softmax_pv(s_list, v_list, sink=None):
    m = jnp.max(s_list[0], axis=-1, keepdims=True)
    for s in s_list[1:]:
        m = jnp.maximum(m, jnp.max(s, axis=-1, keepdims=True))
    if sink is not None:
        m = jnp.maximum(m, sink)
    den = None
    o = None
    for s, v in zip(s_list, v_list):
        e = jnp.exp2(s - m)
        d = jnp.sum(e, axis=-1, keepdims=True)
        pv = _dot(e.astype(BF16), v)
        den = d if den is None else den + d
        o = pv if o is None else o + pv
    if sink is not None:
        den = den + jnp.exp2(sink - m)
    return o / den


def _low_lanes(shape):
    return lax.broadcasted_iota(jnp.int32, shape, 1) < HEAD_DIM


def _stack_heads(blocks):
    lo = _low_lanes(blocks[0].shape)
    zero = jnp.zeros_like(blocks[0])
    parts = []
    for b in blocks:
        parts += [jnp.where(lo, b, zero), jnp.where(lo, zero, b)]
    return jnp.concatenate(parts, axis=0)


def _unstack_pair(o, idx, m):
    return jnp.where(_low_lanes((m, LANE)), o[2 * idx * m:(2 * idx + 1) * m], o[(2 * idx + 1) * m:(2 * idx + 2) * m])


def _sink_column(sink_ref, layer, m):
    blk = lax.broadcasted_iota(jnp.int32, (A_HEADS * m, 1), 0) // m
    col = jnp.full((A_HEADS * m, 1), sink_ref[layer, A_HEAD_ORDER[-1]] * LOG2E, F32)
    for i in range(A_HEADS - 1):
        col = jnp.where(blk == i, sink_ref[layer, A_HEAD_ORDER[i]] * LOG2E, col)
    return col


def _mla_keys(kv, kpe_blk):
    lo = _low_lanes(kpe_blk.shape)
    return [jnp.where(lo, kv[:, h * LANE:(h + 1) * LANE], kpe_blk).astype(BF16) for h in range(MLA_HEADS)]


def _mla_heads(q_of, k_of, v_of, gate_of, store):
    for pair in range(MLA_HEADS // 2):
        outs = [_softmax_pv([_dot_nt(q_of(h), k_of(h))], [v_of(h)]) for h in (2 * pair, 2 * pair + 1)]
        o = jnp.where(_low_lanes(outs[0].shape), pltpu.roll(outs[0], MLA_V, 1), outs[1])
        store(pair, (o * gate_of(pair)).astype(BF16))


def _ctx_attn_body(sink_ref, z_ref, wkv_ref, oa_ref, ob_ref, od_ref, e, layer):
    m = SEQ
    gate = lambda c0: _silu(z_ref[e, :, c0:c0 + LANE].astype(F32))

    def store(ref, pair, val):
        ref[e, :, pair * LANE:(pair + 1) * LANE] = val

    q4 = _stack_heads([z_ref[e, :, Z_AQ:Z_AQ + LANE], z_ref[e, :, Z_AQ + LANE:Z_AQ + 2 * LANE]])
    o4 = _softmax_pv([_dot_nt(q4, z_ref[e, :, Z_AK:Z_AV])], [z_ref[e, :, Z_AV:Z_DQ]],
                     sink=_sink_column(sink_ref, layer, m))
    for pair in range(2):
        store(oa_ref, pair, (_unstack_pair(o4, pair, m) * gate(Z_AG + pair * LANE)).astype(BF16))
    kv = _dot(z_ref[e, :, Z_CKV:Z_KPE], wkv_ref[...])
    keys = _mla_keys(kv, z_ref[e, :, Z_KPE:Z_QM].astype(F32))
    kvb = kv.astype(BF16)
    _mla_heads(lambda h: z_ref[e, :, Z_QM + LANE * h:Z_QM + LANE * (h + 1)], lambda h: keys[h],
               lambda h: kvb[:, LANE * h:LANE * (h + 1)], lambda p: gate(Z_BG + p * LANE),
               functools.partial(store, ob_ref))
    for pair in range(NAT_HEADS // 2):
        c = pair * LANE
        q2 = _stack_heads([z_ref[e, :, Z_DQ + c:Z_DQ + c + LANE]])
        o2 = _softmax_pv([_dot_nt(q2, z_ref[e, :, Z_DK + c:Z_DK + c + LANE])], [z_ref[e, :, Z_DV + c:Z_DV + c + LANE]])
        store(od_ref, pair, (_unstack_pair(o2, 0, m) * gate(Z_DG + c)).astype(BF16))


def _ctx_attn_rider(z, sink, wkv, layer, nt):
    def body(sink_ref, z_ref, wkv_ref, oa_ref, ob_ref, od_ref):
        for e in range(CTX_BPS):
            _ctx_attn_body(sink_ref, z_ref, wkv_ref, oa_ref, ob_ref, od_ref, e, layer)

    step = lambda b, t: (b * nt + t, 0, 0)
    return _Rider(
        body=body,
        in_specs=[pl.BlockSpec(memory_space=pltpu.SMEM), pl.BlockSpec((CTX_BPS, SEQ, Z_W), step),
                  _layer_spec((MLA_KV_LORA, 512), layer)],
        args=[sink, z, wkv],
        out_shape=[jax.ShapeDtypeStruct((BATCH, SEQ, BRANCH_W), BF16)] * 3,
        out_specs=[pl.BlockSpec((CTX_BPS, SEQ, BRANCH_W), step)] * 3,
        scratch=[])


def _win_attn_rows(sink_ref, q_ref, kv_ref, g_ref, ck_ref, cv_ref, ck_s, cv_s, dst_ref, layer):
    nb = DEC_SEQ // Q_BLOCK
    m = Q_BLOCK
    tile = pl.program_id(1)

    @pl.when(tile == 0)
    def _():
        ck_s[...] = ck_ref[0, 0].astype(BF16)
        cv_s[...] = cv_ref[0, 0].astype(BF16)

    sink = _sink_column(sink_ref, layer, m)
    for qb in range(TM // m):
        n = tile * (TM // m) + qb
        rows = slice(qb * m, (qb + 1) * m)
        blocks = []
        for off in (-1, 0, 1):
            start = pl.multiple_of(jnp.clip(n + off, 0, nb - 1) * m, m)
            blocks.append(kv_ref[0, pl.ds(start, m), :])
        kvw = jnp.concatenate(blocks, axis=0)
        shape = (A_HEADS * m, 3 * m)
        qpos = n * m + (lax.broadcasted_iota(jnp.int32, shape, 0) & (m - 1))
        kpos = (n - 1) * m + lax.broadcasted_iota(jnp.int32, shape, 1)
        mask = (jnp.abs(qpos - kpos) <= A_WINDOW) & (kpos >= 0) & (kpos < DEC_SEQ)
        q4 = _stack_heads([q_ref[0, rows, 0:LANE], q_ref[0, rows, LANE:2 * LANE]])
        s_win = jnp.where(mask, _dot_nt(q4, kvw[:, 0:LANE]), NEG)
        s_ctx = _dot_nt(q4, ck_s[...])
        o4 = _softmax_pv([s_win, s_ctx], [kvw[:, LANE:2 * LANE], cv_s[...]], sink=sink)
        for pair in range(2):
            g = g_ref[0, rows, pair * LANE:(pair + 1) * LANE].astype(F32)
            dst_ref[rows, pair * LANE:(pair + 1) * LANE] = (_unstack_pair(o4, pair, m) * _silu(g)).astype(BF16)


def _nat_rows(q_ref, k_ref, v_ref, g_ref, ck_ref, cv_ref, ring_ref, ck_s, cv_s, bias_s, dst_ref):
    nrows = DEC_SEQ // GRID_W
    kr = NAT_ROWS
    m = NAT_RPS * GRID_W
    step = pl.program_id(1)

    @pl.when((pl.program_id(0) == 0) & (step == 0))
    def _():
        shape = (GRID_W, LANE)
        lane = lax.broadcasted_iota(jnp.int32, shape, 1)
        qcol = lax.broadcasted_iota(jnp.int32, shape, 0)
        rel = (lane & (GRID_W - 1)) - jnp.clip(qcol - NAT_COLS // 2, 0, GRID_W - NAT_COLS)
        col_ok = (rel >= 0) & (rel < NAT_COLS)
        for h in range(NAT_HEADS):
            for i in range(NAT_BIAS_ROWS):
                a = pltpu.roll(jnp.broadcast_to(ring_ref[h, i:i + 1, :], shape), 0, 1, stride=1, stride_axis=0)
                b = pltpu.roll(jnp.broadcast_to(ring_ref[h, i + 1:i + 2, :], shape), GRID_W, 1, stride=1,
                               stride_axis=0)
                bias_s[h, i] = jnp.where(col_ok, jnp.where(lane < GRID_W, a, b), NEG)

    @pl.when(step == 0)
    def _():
        ck_s[...] = ck_ref[0, 0].astype(BF16)
        cv_s[...] = cv_ref[0, 0].astype(BF16)

    key_row = lax.broadcasted_iota(jnp.int32, (1, LANE), 1) // GRID_W
    for grp in range(TM // m):
        row0 = (step * (TM // m) + grp) * NAT_RPS
        rows = slice(grp * m, (grp + 1) * m)
        ws = jnp.clip(row0 - kr // 2, 0, nrows - NAT_UNION)
        start = pl.multiple_of(ws * GRID_W, GRID_W)
        kl = k_ref[0, pl.ds(start, NAT_UNION * GRID_W), :]
        vl = v_ref[0, pl.ds(start, NAT_UNION * GRID_W), :]
        for pair in range(NAT_HEADS // 2):
            c = pair * LANE
            blocks = []
            for hh in range(2):
                for rr in range(NAT_RPS):
                    r = row0 + rr
                    lo = jnp.clip(r - kr // 2, 0, nrows - kr) - ws
                    ro = ws - r + (NAT_ROWS - 1) + NAT_BIAS_PAD
                    row = []
                    for t in range(NAT_UNION // 2):
                        kj = key_row + 2 * t
                        row.append(jnp.where((kj >= lo) & (kj < lo + kr), bias_s[2 * pair + hh, ro + 2 * t], NEG))
                    blocks.append(jnp.concatenate(row, axis=1))
            bias = jnp.concatenate(blocks, axis=0)
            q2 = _stack_heads([q_ref[0, rows, c:c + LANE]])
            s_lat = _dot_nt(q2, kl[:, c:c + LANE]) + bias
            s_ctx = _dot_nt(q2, ck_s[:, c:c + LANE])
            o2 = _softmax_pv([s_lat, s_ctx], [vl[:, c:c + LANE], cv_s[:, c:c + LANE]])
            g = g_ref[0, rows, c:c + LANE].astype(F32)
            dst_ref[rows, c:c + LANE] = (_unstack_pair(o2, 0, m) * _silu(g)).astype(BF16)


def _nat_bias_rings(rpb):
    nc = NAT_COLS - 1
    rep = lambda a, n: jnp.broadcast_to(a, a.shape[:-1] + (n,))
    ring = jnp.concatenate([rpb[..., nc:], rep(rpb[..., -1:], GRID_W - 1 - nc), rep(rpb[..., :1], GRID_W - nc),
                            rpb[..., :nc]], axis=-1)
    neg = lambda n: jnp.full(ring.shape[:2] + (n, 2 * GRID_W), NEG, F32)
    return jnp.concatenate([neg(NAT_BIAS_PAD), ring, neg(NAT_BIAS_PAD + 1)], axis=2)


def _lat_attn_rider(z, sink, ck_a, cv_a, ck_n, cv_n, rings, layer):
    kvw = A_KV_HEADS * HEAD_DIM
    hw = NAT_HEADS * HEAD_DIM

    def body(sink_ref, qa_ref, kva_ref, ga_ref, cka_ref, cva_ref, qd_ref, kd_ref, vd_ref, gd_ref, ckd_ref, cvd_ref,
             ring_ref, cka_s, cva_s, ckd_s, cvd_s, bias_s, ba_s, bd_s):
        _win_attn_rows(sink_ref, qa_ref, kva_ref, ga_ref, cka_ref, cva_ref, cka_s, cva_s, ba_s, layer)
        _nat_rows(qd_ref, kd_ref, vd_ref, gd_ref, ckd_ref, cvd_ref, ring_ref, ckd_s, cvd_s, bias_s, bd_s)

    tile = lambda col: pl.BlockSpec((1, TM, 256), lambda b, t: (b, t, col))
    whole = lambda col: pl.BlockSpec((1, DEC_SEQ, 256), lambda b, t: (b, 0, col))
    cache = lambda w: pl.BlockSpec((1, 1, PAST_LEN, w), lambda b, t: (b, layer, 0, 0))
    return _Rider(
        body=body,
        in_specs=[pl.BlockSpec(memory_space=pltpu.SMEM), tile(Z_AQ // 256), whole(Z_AK // 256), tile(Z_AG // 256),
                  cache(kvw), cache(kvw), tile(Z_DQ // 256), whole(Z_DK // 256), whole(Z_DV // 256),
                  tile(Z_DG // 256), cache(hw), cache(hw), _layer_spec(rings.shape[1:], layer)],
        args=[sink, z, z, z, ck_a, cv_a, z, z, z, z, ck_n, cv_n, rings],
        out_shape=[], out_specs=[],
        scratch=[pltpu.VMEM((PAST_LEN, kvw), BF16), pltpu.VMEM((PAST_LEN, kvw), BF16),
                 pltpu.VMEM((PAST_LEN, hw), BF16), pltpu.VMEM((PAST_LEN, hw), BF16),
                 pltpu.VMEM((NAT_HEADS, NAT_BIAS_ROWS, GRID_W, LANE), F32),
                 pltpu.VMEM((TM, BRANCH_W), BF16), pltpu.VMEM((TM, BRANCH_W), BF16)],
        branches=True)


def _mla_body(q_ref, ckv_ref, kpe_ref, g_ref, cckv_ref, ckpe_ref, wkv_ref, o_ref, k_s, v_s):
    nlat = DEC_SEQ
    step = pl.program_id(0) * pl.num_programs(1) + pl.program_id(1)

    @pl.when(step % MLA_TILES == 0)
    def _():
        rows = 512

        def fill(r0, ckv, kpe_blk):
            kv = _dot(ckv, wkv_ref[...])
            for h, kh in enumerate(_mla_keys(kv, kpe_blk)):
                k_s[h, r0:r0 + rows, :] = kh
                v_s[h, r0:r0 + rows, :] = kv[:, h * LANE:(h + 1) * LANE].astype(BF16)

        for c in range(nlat // rows):
            fill(c * rows, ckv_ref[0, c * rows:(c + 1) * rows, :], kpe_ref[0, c * rows:(c + 1) * rows, :].astype(F32))
        ckpe = ckpe_ref[0, 0]
        ckpe_blk = jnp.concatenate([jnp.zeros((PAST_LEN, KPE_LANE), F32), ckpe,
                                    jnp.zeros((PAST_LEN, LANE - KPE_LANE - MLA_ROPE), F32)], axis=1)
        fill(nlat, cckv_ref[0, 0].astype(BF16), ckpe_blk)

    def store(pair, val):
        o_ref[0, :, pair * LANE:(pair + 1) * LANE] = val

    _mla_heads(lambda h: q_ref[0, :, LANE * h:LANE * (h + 1)], lambda h: k_s[h], lambda h: v_s[h],
               lambda p: _silu(g_ref[0, :, p * LANE:(p + 1) * LANE].astype(F32)), store)


def _mla_rider(z, cache_ckv, cache_kpe, wkv, layer, nt):
    nkeys = DEC_SEQ + PAST_LEN
    assert DEC_BATCH * MLA_TILES == POST_STEPS
    bt = lambda b, t: ((b * nt + t) // MLA_TILES, (b * nt + t) % MLA_TILES)
    tile = lambda col: (lambda b, t: bt(b, t) + (col,))
    whole = lambda col: (lambda b, t: (bt(b, t)[0], 0, col))
    cache = lambda b, t: (bt(b, t)[0], layer, 0, 0)
    return _Rider(
        body=_mla_body,
        in_specs=[
            pl.BlockSpec((1, TQ_MLA, MLA_HEADS * LANE), tile(Z_QM // (MLA_HEADS * LANE))),
            pl.BlockSpec((1, DEC_SEQ, LANE), whole(Z_CKV // LANE)),
            pl.BlockSpec((1, DEC_SEQ, LANE), whole(Z_KPE // LANE)),
            pl.BlockSpec((1, TQ_MLA, 256), tile(Z_BG // 256)),
            pl.BlockSpec((1, 1, PAST_LEN, MLA_KV_LORA), cache),
            pl.BlockSpec((1, 1, PAST_LEN, MLA_ROPE), cache),
            _layer_spec((MLA_KV_LORA, 512), layer),
        ],
        args=[z, z, z, z, cache_ckv, cache_kpe, wkv],
        out_shape=[jax.ShapeDtypeStruct((DEC_BATCH, DEC_SEQ, BRANCH_W), BF16)],
        out_specs=[pl.BlockSpec((1, TQ_MLA, BRANCH_W), tile(0))],
        scratch=[pltpu.VMEM((MLA_HEADS, nkeys, LANE), BF16), pltpu.VMEM((MLA_HEADS, nkeys, LANE), BF16)])


def _s5_mats(lam_re, lam_im, log_dt, b_re, b_im, c_re, c_im):
    t = S5_T
    hp = lax.Precision.HIGHEST
    lre = jnp.minimum(lam_re, -1e-4)
    lim = lam_im
    dt = jnp.exp(log_dt)[..., None]
    er, ei = lre * dt, lim * dt
    mag = jnp.exp(er)
    are, aim = mag * jnp.cos(ei), mag * jnp.sin(ei)
    den = lre * lre + lim * lim
    qre = ((are - 1.0) * lre + aim * lim) / den
    qim = (aim * lre - (are - 1.0) * lim) / den
    bbr = qre[..., None] * b_re - qim[..., None] * b_im
    bbi = qre[..., None] * b_im + qim[..., None] * b_re
    n = jnp.arange(t + 1, dtype=F32)[:, None, None, None]
    pmag = jnp.exp(er[None] * n)
    pr, pi = pmag * jnp.cos(ei[None] * n), pmag * jnp.sin(ei[None] * n)

    def c_times_pow(d, pw_r, pw_i):
        rep = lambda a: jnp.repeat(jnp.transpose(a, (1, 2, 0)), S5_CH, axis=-1)
        til = lambda a: jnp.tile(jnp.transpose(a, (0, 2, 1)), (1, 1, t))
        cr, ci, ar, ai = til(c_re[d]), til(c_im[d]), rep(pw_r), rep(pw_i)
        return cr * ar - ci * ai, cr * ai + ci * ar

    def lag_kernels(d, wr, wi):
        return (jnp.einsum('gpk,gpx->gkx', bbr[d], wr, precision=hp)
                - jnp.einsum('gpk,gpx->gkx', bbi[d], wi, precision=hp))

    wf0 = c_times_pow(0, pr[:t, 0], pi[:t, 0])
    wf1 = c_times_pow(0, pr[1:, 0], pi[1:, 0])
    wb = c_times_pow(1, pr[1:, 1][::-1], pi[1:, 1][::-1])
    kb0 = (jnp.einsum('gpk,gcp->gkc', bbr[1], c_re[1], precision=hp)
           - jnp.einsum('gpk,gcp->gkc', bbi[1], c_im[1], precision=hp))
    kf = lag_kernels(0, *wf0) + jnp.pad(kb0, ((0, 0), (0, 0), (0, S5_TC - S5_CH)))
    kb = lag_kernels(1, *wb) * jnp.asarray(np.arange(S5_TC) >= S5_CH, F32)
    ring = jnp.concatenate([kf, kb], axis=-1)

    def state_cols(pw_r, pw_i, d):
        rep = lambda a: jnp.repeat(jnp.transpose(a, (1, 0, 2)), S5_CH, axis=1)
        til = lambda a: jnp.tile(jnp.transpose(a, (0, 2, 1)), (1, t, 1))
        ar, ai, br, bi = rep(pw_r), rep(pw_i), til(bbr[d]), til(bbi[d])
        return ar * br - ai * bi, ar * bi + ai * br

    sfr, sfi = state_cols(pr[:t, 0][::-1], pi[:t, 0][::-1], 0)
    sbr, sbi = state_cols(pr[:t, 1], pi[:t, 1], 1)
    m_state = jnp.concatenate([sfr, sbr, sfi, sbi], axis=-1)
    m_out = jnp.concatenate([wf1[0], wb[0], -wf1[1], -wb[1]], axis=1)

    a_t = jnp.stack([jnp.concatenate([pr[t, 0], pr[t, 1]], axis=-1),
                     jnp.concatenate([pi[t, 0], pi[t, 1]], axis=-1)], axis=1)
    return ring, m_state.astype(BF16), m_out.astype(BF16), a_t


def _s5_kernel(uc_ref, ul_ref, ring_ref, ms_ref, mo_ref, at_ref, h0_ref, yc_ref, yl_ref, hend_ref,
               s_s, hf_s, hb_s, m_s):
    p = S5_STATE
    u = jnp.concatenate([uc_ref[0], ul_ref[0]], axis=0)
    s = _dot(u, ms_ref[0])
    s_s[0] = s[:, 0:2 * p]
    s_s[1] = s[:, 2 * p:4 * p]
    are = at_ref[0, 0:1, :]
    aim = at_ref[0, 1:2, :]
    fwd_lane = lax.broadcasted_iota(jnp.int32, (1, 2 * p), 1) < p

    for k in range(S5_CH):
        rk = jnp.broadcast_to(ring_ref[0, k:k + 1, :], (S5_T, 2 * S5_TC))
        rolled = pltpu.roll(rk, 0, 1, stride=S5_CH, stride_axis=0)
        for blk in range(S5_TC // LANE):
            m_s[blk, pl.ds(k, S5_T, stride=S5_CH), :] = rolled[:, blk * LANE:(blk + 1) * LANE]

    def scan(row0, nk, nb, hre, him):
        for st in range(nk):
            rf = pl.ds(row0 + st, nb, stride=nk)
            rb = pl.ds(row0 + nk - 1 - st, nb, stride=nk)
            hf_s[0, rf, :] = hre
            hf_s[1, rf, :] = him
            hb_s[0, rb, :] = hre
            hb_s[1, rb, :] = him
            sre = jnp.where(fwd_lane, s_s[0, rf, :], s_s[0, rb, :])
            sim = jnp.where(fwd_lane, s_s[1, rf, :], s_s[1, rb, :])
            hre, him = are * hre - aim * him + sre, are * him + aim * hre + sim
        return hre, him

    zero = jnp.zeros((BATCH, 2 * p), F32)
    hre, him = scan(0, S5_NK_CTX, BATCH, zero, zero)
    hend_ref[0] = jnp.concatenate([hre, him], axis=1)
    scan(S5_ROWS, S5_NK_LAT, DEC_BATCH, h0_ref[0, :, 0:2 * p], h0_ref[0, :, 2 * p:4 * p])
    hst = jnp.concatenate([jnp.where(fwd_lane, hf_s[0], hb_s[0]), jnp.where(fwd_lane, hf_s[1], hb_s[1])], axis=1)
    m_intra = jnp.concatenate([m_s[blk] for blk in range(S5_TC // LANE)], axis=1).astype(BF16)
    y = _dot(u, m_intra) + _dot(hst.astype(BF16), mo_ref[0])
    yc_ref[0] = y[0:S5_ROWS]
    yl_ref[0] = y[S5_ROWS:2 * S5_ROWS]


def _s5_call(ut_ctx, ut_lat, ring, m_state, m_out, a_t, h0, layer):
    g3 = lambda g: (g, 0, 0)
    lg = lambda shape: pl.BlockSpec((None, 1) + shape, lambda g: (layer, g, 0, 0))
    yshape = jax.ShapeDtypeStruct((S5_GROUPS, S5_ROWS, S5_TC), F32)
    uspec = pl.BlockSpec((1, S5_ROWS, S5_TC), g3)
    return pl.pallas_call(
        _s5_kernel,
        out_shape=(yshape, yshape, jax.ShapeDtypeStruct((S5_GROUPS, BATCH, 4 * S5_STATE), F32)),
        grid=(S5_GROUPS,),
        in_specs=[
            uspec, uspec,
            lg((S5_CH, 2 * S5_TC)),
            lg((S5_TC, 4 * S5_STATE)),
            lg((4 * S5_STATE, S5_TC)),
            lg((2, 2 * S5_STATE)),
            lg((DEC_BATCH, 4 * S5_STATE)),
        ],
        out_specs=(uspec, uspec, pl.BlockSpec((1, BATCH, 4 * S5_STATE), g3)),
        scratch_shapes=[pltpu.VMEM((2, 2 * S5_ROWS, LANE), F32)] * 3 + [pltpu.VMEM((S5_TC // LANE, S5_TC, LANE), F32)],
        compiler_params=_cparams(("arbitrary",), 32),
        name="s5_chunks",
    )(ut_ctx, ut_lat, ring, m_state, m_out, a_t, h0)


_POST_IN = ('x', 'mod', 'ng', 'ba', 'bb', 'bd', 'y', 'u', 'cg', 'sd', 'wglu', 'wm', 'wbr', 'wo', 'fg')


class _Rider(NamedTuple):
    body: Callable
    in_specs: list
    args: list
    out_shape: list
    out_specs: list
    scratch: list
    branches: bool = False


def _post_kernel(*refs, final, rider):
    names = [n for n in _POST_IN if not (rider.branches and n in ('ba', 'bd'))]
    r = dict(zip(names, refs))
    n0, n_in, n_out = len(names), len(rider.in_specs), len(rider.out_specs)
    o_ref = refs[n0 + n_in]
    ys_s = refs[n0 + n_in + 1 + n_out]
    rider_scratch = refs[n0 + n_in + 2 + n_out:]
    rider.body(*refs[n0:n0 + n_in], *refs[n0 + n_in + 1:n0 + n_in + 1 + n_out], *rider_scratch)
    ba, bd = (rider_scratch[-2][...], rider_scratch[-1][...]) if rider.branches else (r['ba'][0], r['bd'][0])
    x_ref, mod_ref, ng_ref, y_ref = r['x'], r['mod'], r['ng'], r['y']
    x = x_ref[0]
    hb = _ada_h(x, mod_ref, ng_ref).astype(BF16)
    for m in range(S5_T // 8):
        for hf in range(2):
            vs = _slot_transpose([y_ref[8 * hf + gp, :, m * LANE:(m + 1) * LANE] for gp in range(8)])
            for s in range(8):
                ys_s[hf, pl.ds(8 * m + s, TM_CHUNKS, stride=S5_T), :] = vs[s]
    yv = jnp.concatenate([ys_s[0], ys_s[1]], axis=1) + r['sd'][...] * r['u'][0].astype(F32)
    gel = 0.5 * yv * (1.0 + jnp.tanh(math.sqrt(2.0 / math.pi) * (yv + 0.044715 * (yv * yv * yv))))
    gl = _dot(gel.astype(BF16), r['wglu'][...])
    oc = gl[:, 0:BRANCH_W] * _sigmoid(gl[:, BRANCH_W:2 * BRANCH_W])
    bc = (oc * _silu(r['cg'][0].astype(F32))).astype(BF16)
    acc = None
    for k, br in enumerate((ba, r['bb'][0], bc, bd)):
        proj = _dot(br, r['wbr'][k])
        mg = _sigmoid(_dot(hb, r['wm'][:, k * D_MODEL:(k + 1) * D_MODEL]))
        acc = mg * proj if acc is None else acc + mg * proj
    y = _dot(acc.astype(BF16), r['wo'][...])
    out = x + mod_ref[0, :, 2 * D_MODEL:3 * D_MODEL] * y
    if final:
        out = _rms(out, r['fg'][...])
    o_ref[0] = out


def _post_call(x, mods, mod_row0, p, ba, bb, bd, y_s5, z, fg, final, layer, name, rider):
    nb, length, _ = x.shape
    nt = length // TM
    assert nb * nt == POST_STEPS
    tok = lambda b, t: (b, t, 0)
    br_spec = pl.BlockSpec((1, TM, BRANCH_W), tok)
    branch_args = [bb] if rider.branches else [ba, bb, bd]
    return pl.pallas_call(
        functools.partial(_post_kernel, final=final, rider=rider),
        out_shape=tuple([jax.ShapeDtypeStruct((nb, length, D_MODEL), F32)] + rider.out_shape),
        grid=(nb, nt),
        in_specs=[
            pl.BlockSpec((1, TM, D_MODEL), tok),
            _mod_spec(layer, mod_row0),
            _layer_spec((1, D_MODEL), layer),
        ] + [br_spec] * len(branch_args) + [
            pl.BlockSpec((S5_GROUPS, TM_CHUNKS, S5_TC), lambda b, t: (0, b * nt + t, 0)),
            pl.BlockSpec((1, TM, 256), lambda b, t: (b, t, Z_CU // 256)),
            pl.BlockSpec((1, TM, 256), lambda b, t: (b, t, Z_CG // 256)),
            _layer_spec((1, BRANCH_W), layer),
            _layer_spec((BRANCH_W, 2 * BRANCH_W), layer),
            _layer_spec((D_MODEL, N_BRANCH * D_MODEL), layer),
            _layer_spec((N_BRANCH, BRANCH_W, D_MODEL), layer),
            _layer_spec((D_MODEL, D_MODEL), layer),
            pl.BlockSpec((1, D_MODEL), lambda b, t: (0, 0)),
        ] + rider.in_specs,
        out_specs=tuple([pl.BlockSpec((1, TM, D_MODEL), tok)] + rider.out_specs),
        scratch_shapes=[pltpu.VMEM((2, TM, LANE), F32)] + rider.scratch,
        compiler_params=_cparams(("arbitrary", "arbitrary"), 56),
        name=name,
    )(x, mods, p['ng'], *branch_args, y_s5, z, z, p['s5d'], p['wglu'], p['wm'], p['wbr'], p['wo'], fg, *rider.args)


def _rope_tables():
    t = np.arange(DEC_SEQ)
    row = (t // GRID_W).astype(np.float64)
    col = (t % GRID_W).astype(np.float64)

    def pattern(half):
        inv = ROPE_BASE ** (-np.arange(half, dtype=np.float64) / half)
        zeros = np.zeros((DEC_SEQ, half))
        cs, s_up, s_lo = [], [], []
        for pos in (row, col):
            ang = pos[:, None] * inv[None, :]
            c, s = np.cos(ang), np.sin(ang)
            cs += [c, c]
            s_up += [zeros, s]
            s_lo += [-s, zeros]
        return [np.concatenate(parts, axis=1) for parts in (cs, s_up, s_lo)]

    tab_a = np.stack([np.tile(part, (1, LANE // HEAD_DIM)) for part in pattern(HEAD_DIM // 4)])
    ident = [np.ones, np.zeros, np.zeros]
    tab_m = np.stack([np.concatenate([fill((DEC_SEQ, KPE_LANE)), part,
                                      fill((DEC_SEQ, LANE - KPE_LANE - MLA_ROPE))], axis=1)
                      for fill, part in zip(ident, pattern(MLA_ROPE // 4))])
    return jnp.asarray(tab_a, F32), jnp.asarray(tab_m, F32)


_W_IN_MOVES = ((256, 256, 256), (512, 1952, 768), (R_CQ, 768, 384), (R_CU, 1440, 256),
               (R_G + 256, 1184, 256), (R_G + 512, 1696, 256), (R_G + 768, 2720, 256))
_W_IN_HEAD_MOVES = ((0, 0), (R_G, 512))
_W_IN_KPE = 1152
W_IN_ROWS = 256


def _w_in_kernel(wt_ref, o_ref):
    take = lambda src, width: wt_ref[0, src:src + width, :].T
    for dst, src, width in _W_IN_MOVES:
        o_ref[0, :, dst:dst + width] = take(src, width).astype(BF16)
    for dst, src in _W_IN_HEAD_MOVES:
        blk = take(src, A_HEADS * HEAD_DIM)
        for i, h in enumerate(A_HEAD_ORDER):
            o_ref[0, :, dst + HEAD_DIM * i:dst + HEAD_DIM * (i + 1)] = blk[:, HEAD_DIM * h:HEAD_DIM * (h + 1)].astype(BF16)
    o_ref[0, :, R_KPE:R_CU] = jnp.zeros((W_IN_ROWS, LANE), BF16)
    o_ref[0, :, R_KPE + KPE_LANE:R_KPE + KPE_LANE + MLA_ROPE] = take(_W_IN_KPE, LANE)[:, 0:MLA_ROPE].astype(BF16)


def _w_in_call(w_in):
    wt = jnp.swapaxes(w_in, 1, 2)
    return pl.pallas_call(
        _w_in_kernel,
        out_shape=jax.ShapeDtypeStruct((DEPTH, D_MODEL, R_W), BF16),
        grid=(DEPTH, D_MODEL // W_IN_ROWS),
        in_specs=[pl.BlockSpec((1, wt.shape[1], W_IN_ROWS), lambda i, r: (i, 0, r))],
        out_specs=pl.BlockSpec((1, W_IN_ROWS, R_W), lambda i, r: (i, r, 0)),
        compiler_params=_cparams(("arbitrary", "arbitrary"), 32),
        name="w_in_layout",
    )(wt)


def _cast_kernel(w_ref, o_ref):
    o_ref[...] = w_ref[...].astype(BF16)


def _cast_call(w):
    _, rows, cols = w.shape
    spec = pl.BlockSpec((1, W_IN_ROWS, cols), lambda i, r: (i, r, 0))
    return pl.pallas_call(
        _cast_kernel,
        out_shape=jax.ShapeDtypeStruct(w.shape, BF16),
        grid=(DEPTH, rows // W_IN_ROWS),
        in_specs=[spec],
        out_specs=spec,
        compiler_params=_cparams(("arbitrary", "arbitrary"), 32),
        name="w_cast",
    )(w)


def _reorder_w_q_up(w):
    w = w.reshape(MLA_Q_LORA, MLA_HEADS, MLA_NOPE + MLA_ROPE)
    w = jnp.concatenate([w, jnp.zeros((MLA_Q_LORA, MLA_HEADS, LANE - MLA_NOPE - MLA_ROPE), w.dtype)], axis=-1)
    return w.reshape(MLA_Q_LORA, MLA_HEADS * LANE)


def _reorder_w_branch(w):
    wa = w[0].reshape(A_HEADS, HEAD_DIM, D_MODEL)
    wa = jnp.concatenate([wa[h] for h in A_HEAD_ORDER], axis=0)
    return jnp.concatenate([wa[None], w[1:]], axis=0)


def kernel(x_prompt, x_sample, cache_a_k, cache_a_v, cache_mla_ckv, cache_mla_kpe, cache_na_k, cache_na_v,
           state_s5_re, state_s5_im, c, c_ctx, w_mod, b_mod, norm_g, w_in, w_merge, a_sink,
           mla_q_norm, mla_w_q_up, mla_kv_norm, mla_w_kv_up, s5_lam_re, s5_lam_im, s5_log_dt,
           s5_b_re, s5_b_im, s5_c_re, s5_c_im, s5_d, s5_w_glu, na_rpb, w_branch, w_out, final_norm_g):
    n_ctx = BATCH * SEQ
    conds = jnp.concatenate([c, c_ctx[None, :], jnp.zeros((3, D_MODEL), F32)], axis=0)
    mods = _mod_call(conds, w_mod, b_mod).reshape(DEPTH, 8, 1, 3 * D_MODEL)
    tabs = _rope_tables()
    ck_a = cache_a_k.reshape(DEC_BATCH, DEPTH, PAST_LEN, A_KV_HEADS * HEAD_DIM)
    cv_a = cache_a_v.reshape(DEC_BATCH, DEPTH, PAST_LEN, A_KV_HEADS * HEAD_DIM)
    ck_n = cache_na_k.reshape(DEC_BATCH, DEPTH, PAST_LEN, NAT_HEADS * HEAD_DIM)
    cv_n = cache_na_v.reshape(DEC_BATCH, DEPTH, PAST_LEN, NAT_HEADS * HEAD_DIM)
    fg = final_norm_g.reshape(1, D_MODEL)
    caches = None

    p = dict(
        ng=norm_g.reshape(DEPTH, 1, D_MODEL),
        w_raw=_w_in_call(w_in),
        qnorm=mla_q_norm.reshape(DEPTH, 1, MLA_Q_LORA),
        wq=jax.vmap(_reorder_w_q_up)(mla_w_q_up).astype(BF16),
        kvnorm=mla_kv_norm.reshape(DEPTH, 1, MLA_KV_LORA),
        wkv=mla_w_kv_up.astype(BF16),
        wm=_cast_call(w_merge),
        wbr=jax.vmap(_reorder_w_branch)(w_branch).astype(BF16),
        wo=_cast_call(w_out),
        wglu=s5_w_glu.astype(BF16),
        s5d=s5_d.reshape(DEPTH, 1, BRANCH_W),
    )
    ring, m_state, m_out, a_t = jax.vmap(_s5_mats)(s5_lam_re, s5_lam_im, s5_log_dt, s5_b_re, s5_b_im, s5_c_re, s5_c_im)
    nat_bias = _nat_bias_rings(na_rpb * LOG2E)
    h0 = jnp.concatenate([state_s5_re[:, :, 0], state_s5_re[:, :, 1], state_s5_im[:, :, 0], state_s5_im[:, :, 1]],
                         axis=-1)
    h0 = jnp.transpose(h0, (1, 2, 0, 3))

    yp = x_prompt.reshape(1, n_ctx, D_MODEL)
    ys = x_sample
    hends = []
    for i in range(DEPTH):
        res = _inproj_call(yp, mods, DEC_BATCH, p, None, caches, i, "inproj_ctx")
        z_ctx, ut_ctx, caches = res[0], res[1], list(res[2:])
        zc = z_ctx.reshape(BATCH, SEQ, Z_W)
        z_lat, ut_lat, oa_c, ob_c, od_c = _inproj_call(ys, mods, 0, p, tabs, None, i, "inproj_lat_ctx_attn",
                                                       _ctx_attn_rider(zc, a_sink, p['wkv'], i, DEC_SEQ // TM))

        y_ctx, y_lat, hend = _s5_call(ut_ctx, ut_lat, ring, m_state, m_out, a_t, h0, i)
        hends.append(hend)

        flat = lambda a: a.reshape(1, n_ctx, BRANCH_W)
        final = i == DEPTH - 1
        yp, ob_l = _post_call(yp, mods, DEC_BATCH, p, flat(oa_c), flat(ob_c), flat(od_c), y_ctx, z_ctx, fg, final, i,
                              "post_ctx_lat_mla", _mla_rider(z_lat, cache_mla_ckv, cache_mla_kpe, p['wkv'], i, POST_STEPS))
        ys, = _post_call(ys, mods, 0, p, None, ob_l, None, y_lat, z_lat, fg, final, i, "post_lat_win_nat",
                         _lat_attn_rider(z_lat, a_sink, ck_a, cv_a, ck_n, cv_n, nat_bias, i))

    hend = jnp.stack(hends).reshape(DEPTH, S5_GROUPS, BATCH, 2, 2, S5_STATE)
    hend = jnp.transpose(hend, (3, 2, 0, 4, 1, 5))
    ak, av, ckv, kpe, nk, nv = caches
    heads = lambda a, h: a.reshape(BATCH, DEPTH, SEQ, h, HEAD_DIM)
    return (yp.reshape(BATCH, SEQ, D_MODEL), ys, heads(ak, A_KV_HEADS), heads(av, A_KV_HEADS), ckv, jnp.swapaxes(kpe, 2, 3),
            heads(nk, NAT_HEADS), heads(nv, NAT_HEADS), hend[0], hend[1])
```

```python
import functools
import math
from typing import Callable, NamedTuple

import numpy as np
import jax
import jax.numpy as jnp
from jax import lax
from jax.experimental import pallas as pl
from jax.experimental.pallas import tpu as pltpu

F32 = jnp.float32
BF16 = jnp.bfloat16

D_MODEL = 1024
BATCH = 32
SEQ = 256
DEPTH = 2
DEC_BATCH = 4
DEC_SEQ = 2048
PAST_LEN = 512
GRID_W = 64
HEAD_DIM = 64
BRANCH_W = 256
N_BRANCH = 4
Q_BLOCK = 128
A_HEADS = 4
A_KV_HEADS = 2
A_GROUP = A_HEADS // A_KV_HEADS
A_WINDOW = 128
MLA_HEADS = 4
MLA_Q_LORA = 256
MLA_KV_LORA = 128
MLA_NOPE = 64
MLA_ROPE = 32
MLA_V = 64
S5_CH = 16
S5_GROUPS = BRANCH_W // S5_CH
S5_STATE = 64
NAT_HEADS = 4
NAT_ROWS = 8
NAT_COLS = 16
ROPE_BASE = 10000.0
EPS = 1e-6
NEG = -1e30
LOG2E = 1.4426950408889634
Q_SCALE = HEAD_DIM ** -0.5 * LOG2E
MLA_Q_SCALE = (MLA_NOPE + MLA_ROPE) ** -0.5 * LOG2E

LANE = 128

R_AV, R_CQ, R_CKV, R_KPE, R_CU, R_G, R_W = 384, 1280, 1536, 1664, 1792, 2048, 3072
KPE_LANE = 64
A_HEAD_ORDER = (0, 2, 1, 3)
Z_AQ, Z_AK, Z_AV = 0, 256, 384
Z_DQ, Z_DK, Z_DV = 512, 768, 1024
Z_CKV, Z_KPE = 1280, 1408
Z_QM = 1536
Z_AG, Z_BG, Z_CG, Z_DG = 2048, 2304, 2560, 2816
Z_CU = 3072
Z_W = 3328

S5_T = 32
S5_TC = S5_T * S5_CH
S5_ROWS = 256
S5_NK_CTX = SEQ // S5_T
S5_NK_LAT = DEC_SEQ // S5_T

TM = 512
TM_CHUNKS = TM // S5_T
TQ_MLA = 512
MLA_TILES = DEC_SEQ // TQ_MLA
POST_STEPS = BATCH * SEQ // TM
CTX_BPS = 2
NAT_RPS = 4
NAT_UNION = NAT_ROWS + NAT_RPS
NAT_BIAS_PAD = NAT_RPS
NAT_BIAS_ROWS = 2 * NAT_ROWS - 1 + 2 * NAT_BIAS_PAD


def _cparams(sem, vmem_mb):
    return pltpu.CompilerParams(dimension_semantics=sem, vmem_limit_bytes=vmem_mb * 1024 * 1024)


def _sigmoid(x):
    return 1.0 / (1.0 + jnp.exp(-x))


def _silu(x):
    return x * _sigmoid(x)


def _rms(x, g):
    return x * lax.rsqrt(jnp.mean(x * x, axis=-1, keepdims=True) + EPS) * g


def _dot(a, b):
    return jnp.dot(a, b, preferred_element_type=F32)


def _dot_nt(a, b):
    return lax.dot_general(a, b, (((1,), (1,)), ((), ())), preferred_element_type=F32)


def _slot_transpose(vs):
    lane = lax.broadcasted_iota(jnp.int32, vs[0].shape, 1)
    vs = list(vs)
    for d in (4, 2, 1):
        keep = (lane & (S5_CH * d)) == 0
        nxt = list(vs)
        for lo in range(8):
            if lo & d:
                continue
            hi = lo + d
            nxt[lo] = jnp.where(keep, vs[lo], pltpu.roll(vs[hi], S5_CH * d, 1))
            nxt[hi] = jnp.where(keep, pltpu.roll(vs[lo], LANE - S5_CH * d, 1), vs[hi])
        vs = nxt
    return vs


def _mod_kernel(c_ref, w_ref, b_ref, o_ref):
    s = _silu(c_ref[...])
    o_ref[0] = _dot(s.astype(BF16), w_ref[0].astype(BF16)) + b_ref[0]


def _mod_call(conds, w_mod, b_mod):
    nc = 512
    return pl.pallas_call(
        _mod_kernel,
        out_shape=jax.ShapeDtypeStruct((DEPTH, 8, 3 * D_MODEL), F32),
        grid=(DEPTH, 3 * D_MODEL // nc),
        in_specs=[
            pl.BlockSpec((8, D_MODEL), lambda i, j: (0, 0)),
            pl.BlockSpec((1, D_MODEL, nc), lambda i, j: (i, 0, j)),
            pl.BlockSpec((1, 1, nc), lambda i, j: (i, 0, j)),
        ],
        out_specs=pl.BlockSpec((1, 8, nc), lambda i, j: (i, 0, j)),
        compiler_params=_cparams(("arbitrary", "arbitrary"), 32),
        name="mod_rows",
    )(conds, w_mod, b_mod.reshape(DEPTH, 1, 3 * D_MODEL))


def _ada_h(x, mod_ref, ng_ref):
    shift = mod_ref[0, :, 0:D_MODEL]
    scale = mod_ref[0, :, D_MODEL:2 * D_MODEL]
    return _rms(x, ng_ref[...]) * (1.0 + scale) + shift


def _rope_block(xs, tab_ref, shift):
    return (xs * tab_ref[0] + pltpu.roll(xs, shift, 1) * tab_ref[1]
            + pltpu.roll(xs, LANE - shift, 1) * tab_ref[2])


CACHE_DIMS = ((SEQ, 128), (SEQ, 128), (SEQ, MLA_KV_LORA), (MLA_ROPE, SEQ), (SEQ, 256), (SEQ, 256))
N_CACHE = len(CACHE_DIMS)


def _inproj_kernel(*refs, rope, rider, first):
    x_ref, mod_ref, ng_ref, w_ref, qn_ref, wq_ref, kvn_ref = refs[:7]
    if rope:
        ta_ref, tm_ref = refs[7:9]
        n_in, n_out = len(rider.in_specs), len(rider.out_specs)
        z_ref, ut_ref = refs[9 + n_in:11 + n_in]
        u_s = refs[11 + n_in + n_out]
        rider.body(*refs[9:9 + n_in], *refs[11 + n_in:11 + n_in + n_out], *refs[12 + n_in + n_out:])
    else:
        n_c = 0 if first else N_CACHE
        n_in, n_out = len(rider.in_specs), len(rider.out_specs)
        base = 7 + n_c + n_in
        z_ref, ut_ref, ak_ref, av_ref, ckv_ref, kpe_ref, nk_ref, nv_ref = refs[base:base + 2 + N_CACHE]
        u_s = refs[base + 2 + N_CACHE + n_out]
        rider.body(*refs[7 + n_c:base], *refs[base + 2 + N_CACHE:base + 2 + N_CACHE + n_out],
                   *refs[base + 3 + N_CACHE + n_out:])
    h = _ada_h(x_ref[0], mod_ref, ng_ref)
    raw = _dot(h.astype(BF16), w_ref[...])
    for j in range(R_AV // LANE):
        blk = raw[:, j * LANE:(j + 1) * LANE]
        if j < Z_AK // LANE:
            blk = blk * Q_SCALE
        if rope:
            blk = _rope_block(blk, ta_ref, 16)
        z_ref[0, :, j * LANE:(j + 1) * LANE] = blk.astype(BF16)
    z_ref[0, :, Z_AV:Z_DQ] = raw[:, Z_AV:Z_DQ].astype(BF16)
    z_ref[0, :, Z_DQ:Z_DK] = (raw[:, Z_DQ:Z_DK] * Q_SCALE).astype(BF16)
    z_ref[0, :, Z_DK:R_CQ] = raw[:, Z_DK:R_CQ].astype(BF16)
    qn = _rms(raw[:, R_CQ:R_CKV], qn_ref[...])
    q = _dot(qn.astype(BF16), wq_ref[...]) * MLA_Q_SCALE
    kp = raw[:, R_KPE:R_CU]
    ckv = _rms(raw[:, R_CKV:R_KPE], kvn_ref[...])
    for hh in range(MLA_HEADS):
        qh = q[:, hh * LANE:(hh + 1) * LANE]
        if rope:
            qh = _rope_block(qh, tm_ref, 8)
        z_ref[0, :, Z_QM + hh * LANE:Z_QM + (hh + 1) * LANE] = qh.astype(BF16)
    z_ref[0, :, Z_CKV:Z_KPE] = ckv.astype(BF16)
    z_ref[0, :, Z_KPE:Z_QM] = (_rope_block(kp, tm_ref, 8) if rope else kp).astype(BF16)
    z_ref[0, :, Z_AG:Z_CU] = raw[:, R_G:R_W].astype(BF16)
    z_ref[0, :, Z_CU:Z_W] = raw[:, R_CU:R_G].astype(BF16)
    if not rope:
        half = TM // 2
        for e in range(2):
            rows = slice(e * half, (e + 1) * half)
            ak_ref[e, 0] = raw[rows, Z_AK:Z_AV]
            av_ref[e, 0] = raw[rows, Z_AV:Z_DQ]
            nk_ref[e, 0] = raw[rows, Z_DK:Z_DV]
            nv_ref[e, 0] = raw[rows, Z_DV:R_CQ]
            ckv_ref[e, 0] = ckv[rows]
            kpe_ref[e, 0] = kp[rows].T[KPE_LANE:KPE_LANE + MLA_ROPE]
            if first:
                for cref in (ak_ref, av_ref, nk_ref, nv_ref, ckv_ref, kpe_ref):
                    cref[e, 1:DEPTH] = jnp.zeros((DEPTH - 1,) + cref.shape[2:], F32)
    for hf in range(2):
        u_s[hf] = raw[:, R_CU + hf * LANE:R_CU + (hf + 1) * LANE]
    for m in range(S5_T // 8):
        for hf in range(2):
            outs = _slot_transpose([u_s[hf, pl.ds(8 * m + s, TM_CHUNKS, stride=S5_T), :] for s in range(8)])
            for gp in range(8):
                ut_ref[8 * hf + gp, :, m * LANE:(m + 1) * LANE] = outs[gp].astype(BF16)


def _layer_spec(shape, layer):
    zeros = (0,) * len(shape)
    return pl.BlockSpec((None,) + tuple(shape), lambda *_: (layer,) + zeros)


def _mod_spec(layer, row0):
    return pl.BlockSpec((None, 1, 1, 3 * D_MODEL), lambda b, t: (layer, row0 + b, 0, 0))


def _inproj_call(x, mods, mod_row0, p, tabs, caches, layer, name, rider):
    nb, length, _ = x.shape
    rope = tabs is not None
    nt = length // TM
    in_specs = [
        pl.BlockSpec((1, TM, D_MODEL), lambda b, t: (b, t, 0)),
        _mod_spec(layer, mod_row0),
        _layer_spec((1, D_MODEL), layer),
        _layer_spec((D_MODEL, R_W), layer),
        _layer_spec((1, MLA_Q_LORA), layer),
        _layer_spec((MLA_Q_LORA, MLA_HEADS * LANE), layer),
        _layer_spec((1, MLA_KV_LORA), layer),
    ]
    args = [x, mods, p['ng'], p['w_raw'], p['qnorm'], p['wq'], p['kvnorm']]
    out_shape = [jax.ShapeDtypeStruct((nb, length, Z_W), BF16),
                 jax.ShapeDtypeStruct((S5_GROUPS, S5_ROWS, S5_TC), BF16)]
    out_specs = [pl.BlockSpec((1, TM, Z_W), lambda b, t: (b, t, 0)),
                 pl.BlockSpec((S5_GROUPS, TM_CHUNKS, S5_TC), lambda b, t: (0, b * nt + t, 0))]
    aliases = {}
    scratch = [pltpu.VMEM((2, TM, LANE), F32)]
    if rope:
        assert nb * nt == POST_STEPS
        in_specs += [pl.BlockSpec((3, TM, LANE), lambda b, t: (0, t, 0))] * 2 + rider.in_specs
        args += list(tabs) + rider.args
        out_shape += rider.out_shape
        out_specs += rider.out_specs
        scratch += rider.scratch
    else:
        for k, dims in enumerate(CACHE_DIMS):
            out_shape.append(jax.ShapeDtypeStruct((BATCH, DEPTH) + dims, F32))
            if caches is None:
                out_specs.append(pl.BlockSpec((2, DEPTH) + dims, lambda b, t: (t, 0, 0, 0)))
            else:
                in_specs.append(pl.BlockSpec(memory_space=pl.ANY))
                args.append(caches[k])
                out_specs.append(pl.BlockSpec((2, 1) + dims, lambda b, t: (t, layer, 0, 0)))
                aliases[7 + k] = 2 + k
        in_specs += rider.in_specs
        args += rider.args
        out_shape += rider.out_shape
        out_specs += rider.out_specs
        scratch += rider.scratch
    return pl.pallas_call(
        functools.partial(_inproj_kernel, rope=rope, rider=rider, first=not rope and caches is None),
        out_shape=tuple(out_shape),
        grid=(nb, nt),
        in_specs=in_specs,
        out_specs=tuple(out_specs),
        scratch_shapes=scratch,
        input_output_aliases=aliases,
        compiler_params=_cparams(("arbitrary", "arbitrary"), 56),
        name=name,
    )(*args)


def _softmax_pv(s_list, v_list, sink=None):
    m = jnp.max(s_list[0], axis=-1, keepdims=True)
    for s in s_list[1:]:
        m = jnp.maximum(m, jnp.max(s, axis=-1, keepdims=True))
    if sink is not None:
        m = jnp.maximum(m, sink)
    den = None
    o = None
    for s, v in zip(s_list, v_list):
        e = jnp.exp2(s - m)
        d = jnp.sum(e, axis=-1, keepdims=True)
        pv = _dot(e.astype(BF16), v)
        den = d if den is None else den + d
        o = pv if o is None else o + pv
    if sink is not None:
        den = den + jnp.exp2(sink - m)
    return o / den


def _low_lanes(shape):
    return lax.broadcasted_iota(jnp.int32, shape, 1) < HEAD_DIM


def _stack_heads(blocks):
    lo = _low_lanes(blocks[0].shape)
    zero = jnp.zeros_like(blocks[0])
    parts = []
    for b in blocks:
        parts += [jnp.where(lo, b, zero), jnp.where(lo, zero, b)]
    return jnp.concatenate(parts, axis=0)


def _unstack_pair(o, idx, m):
    return jnp.where(_low_lanes((m, LANE)), o[2 * idx * m:(2 * idx + 1) * m], o[(2 * idx + 1) * m:(2 * idx + 2) * m])


def _sink_column(sink_ref, layer, m):
    blk = lax.broadcasted_iota(jnp.int32, (A_HEADS * m, 1), 0) // m
    col = jnp.full((A_HEADS * m, 1), sink_ref[layer, A_HEAD_ORDER[-1]] * LOG2E, F32)
    for i in range(A_HEADS - 1):
        col = jnp.where(blk == i, sink_ref[layer, A_HEAD_ORDER[i]] * LOG2E, col)
    return col


def _mla_keys(kv, kpe_blk):
    lo = _low_lanes(kpe_blk.shape)
    return [jnp.where(lo, kv[:, h * LANE:(h + 1) * LANE], kpe_blk).astype(BF16) for h in range(MLA_HEADS)]


def _mla_heads(q_of, k_of, v_of, gate_of, store):
    for pair in range(MLA_HEADS // 2):
        outs = [_softmax_pv([_dot_nt(q_of(h), k_of(h))], [v_of(h)]) for h in (2 * pair, 2 * pair + 1)]
        o = jnp.where(_low_lanes(outs[0].shape), pltpu.roll(outs[0], MLA_V, 1), outs[1])
        store(pair, (o * gate_of(pair)).astype(BF16))


def _ctx_attn_body(sink_ref, z_ref, wkv_ref, oa_ref, ob_ref, od_ref, e, layer):
    m = SEQ
    gate = lambda c0: _silu(z_ref[e, :, c0:c0 + LANE].astype(F32))

    def store(ref, pair, val):
        ref[e, :, pair * LANE:(pair + 1) * LANE] = val

    q4 = _stack_heads([z_ref[e, :, Z_AQ:Z_AQ + LANE], z_ref[e, :, Z_AQ + LANE:Z_AQ + 2 * LANE]])
    o4 = _softmax_pv([_dot_nt(q4, z_ref[e, :, Z_AK:Z_AV])], [z_ref[e, :, Z_AV:Z_DQ]],
                     sink=_sink_column(sink_ref, layer, m))
    for pair in range(2):
        store(oa_ref, pair, (_unstack_pair(o4, pair, m) * gate(Z_AG + pair * LANE)).astype(BF16))
    kv = _dot(z_ref[e, :, Z_CKV:Z_KPE], wkv_ref[...])
    keys = _mla_keys(kv, z_ref[e, :, Z_KPE:Z_QM].astype(F32))
    kvb = kv.astype(BF16)
    _mla_heads(lambda h: z_ref[e, :, Z_QM + LANE * h:Z_QM + LANE * (h + 1)], lambda h: keys[h],
               lambda h: kvb[:, LANE * h:LANE * (h + 1)], lambda p: gate(Z_BG + p * LANE),
               functools.partial(store, ob_ref))
    for pair in range(NAT_HEADS // 2):
        c = pair * LANE
        q2 = _stack_heads([z_ref[e, :, Z_DQ + c:Z_DQ + c + LANE]])
        o2 = _softmax_pv([_dot_nt(q2, z_ref[e, :, Z_DK + c:Z_DK + c + LANE])], [z_ref[e, :, Z_DV + c:Z_DV + c + LANE]])
        store(od_ref, pair, (_unstack_pair(o2, 0, m) * gate(Z_DG + c)).astype(BF16))


def _ctx_attn_rider(z, sink, wkv, layer, nt):
    def body(sink_ref, z_ref, wkv_ref, oa_ref, ob_ref, od_ref):
        for e in range(CTX_BPS):
            _ctx_attn_body(sink_ref, z_ref, wkv_ref, oa_ref, ob_ref, od_ref, e, layer)

    step = lambda b, t: (b * nt + t, 0, 0)
    return _Rider(
        body=body,
        in_specs=[pl.BlockSpec(memory_space=pltpu.SMEM), pl.BlockSpec((CTX_BPS, SEQ, Z_W), step),
                  _layer_spec((MLA_KV_LORA, 512), layer)],
        args=[sink, z, wkv],
        out_shape=[jax.ShapeDtypeStruct((BATCH, SEQ, BRANCH_W), BF16)] * 3,
        out_specs=[pl.BlockSpec((CTX_BPS, SEQ, BRANCH_W), step)] * 3,
        scratch=[])


def _win_attn_rows(sink_ref, q_ref, kv_ref, g_ref, ck_ref, cv_ref, ck_s, cv_s, dst_ref, layer):
    nb = DEC_SEQ // Q_BLOCK
    m = Q_BLOCK
    tile = pl.program_id(1)

    @pl.when(tile == 0)
    def _():
        ck_s[...] = ck_ref[0, 0].astype(BF16)
        cv_s[...] = cv_ref[0, 0].astype(BF16)

    sink = _sink_column(sink_ref, layer, m)
    for qb in range(TM // m):
        n = tile * (TM // m) + qb
        rows = slice(qb * m, (qb + 1) * m)
        blocks = []
        for off in (-1, 0, 1):
            start = pl.multiple_of(jnp.clip(n + off, 0, nb - 1) * m, m)
            blocks.append(kv_ref[0, pl.ds(start, m), :])
        kvw = jnp.concatenate(blocks, axis=0)
        shape = (A_HEADS * m, 3 * m)
        qpos = n * m + (lax.broadcasted_iota(jnp.int32, shape, 0) & (m - 1))
        kpos = (n - 1) * m + lax.broadcasted_iota(jnp.int32, shape, 1)
        mask = (jnp.abs(qpos - kpos) <= A_WINDOW) & (kpos >= 0) & (kpos < DEC_SEQ)
        q4 = _stack_heads([q_ref[0, rows, 0:LANE], q_ref[0, rows, LANE:2 * LANE]])
        s_win = jnp.where(mask, _dot_nt(q4, kvw[:, 0:LANE]), NEG)
        s_ctx = _dot_nt(q4, ck_s[...])
        o4 = _softmax_pv([s_win, s_ctx], [kvw[:, LANE:2 * LANE], cv_s[...]], sink=sink)
        for pair in range(2):
            g = g_ref[0, rows, pair * LANE:(pair + 1) * LANE].astype(F32)
            dst_ref[rows, pair * LANE:(pair + 1) * LANE] = (_unstack_pair(o4, pair, m) * _silu(g)).astype(BF16)


def _nat_rows(q_ref, k_ref, v_ref, g_ref, ck_ref, cv_ref, ring_ref, ck_s, cv_s, bias_s, dst_ref):
    nrows = DEC_SEQ // GRID_W
    kr = NAT_ROWS
    m = NAT_RPS * GRID_W
    step = pl.program_id(1)

    @pl.when((pl.program_id(0) == 0) & (step == 0))
    def _():
        shape = (GRID_W, LANE)
        lane = lax.broadcasted_iota(jnp.int32, shape, 1)
        qcol = lax.broadcasted_iota(jnp.int32, shape, 0)
        rel = (lane & (GRID_W - 1)) - jnp.clip(qcol - NAT_COLS // 2, 0, GRID_W - NAT_COLS)
        col_ok = (rel >= 0) & (rel < NAT_COLS)
        for h in range(NAT_HEADS):
            for i in range(NAT_BIAS_ROWS):
                a = pltpu.roll(jnp.broadcast_to(ring_ref[h, i:i + 1, :], shape), 0, 1, stride=1, stride_axis=0)
                b = pltpu.roll(jnp.broadcast_to(ring_ref[h, i + 1:i + 2, :], shape), GRID_W, 1, stride=1,
                               stride_axis=0)
                bias_s[h, i] = jnp.where(col_ok, jnp.where(lane < GRID_W, a, b), NEG)

    @pl.when(step == 0)
    def _():
        ck_s[...] = ck_ref[0, 0].astype(BF16)
        cv_s[...] = cv_ref[0, 0].astype(BF16)

    key_row = lax.broadcasted_iota(jnp.int32, (1, LANE), 1) // GRID_W
    for grp in range(TM // m):
        row0 = (step * (TM // m) + grp) * NAT_RPS
        rows = slice(grp * m, (grp + 1) * m)
        ws = jnp.clip(row0 - kr // 2, 0, nrows - NAT_UNION)
        start = pl.multiple_of(ws * GRID_W, GRID_W)
        kl = k_ref[0, pl.ds(start, NAT_UNION * GRID_W), :]
        vl = v_ref[0, pl.ds(start, NAT_UNION * GRID_W), :]
        for pair in range(NAT_HEADS // 2):
            c = pair * LANE
            blocks = []
            for hh in range(2):
                for rr in range(NAT_RPS):
                    r = row0 + rr
                    lo = jnp.clip(r - kr // 2, 0, nrows - kr) - ws
                    ro = ws - r + (NAT_ROWS - 1) + NAT_BIAS_PAD
                    row = []
                    for t in range(NAT_UNION // 2):
                        kj = key_row + 2 * t
                        row.append(jnp.where((kj >= lo) & (kj < lo + kr), bias_s[2 * pair + hh, ro + 2 * t], NEG))
                    blocks.append(jnp.concatenate(row, axis=1))
            bias = jnp.concatenate(blocks, axis=0)
            q2 = _stack_heads([q_ref[0, rows, c:c + LANE]])
            s_lat = _dot_nt(q2, kl[:, c:c + LANE]) + bias
            s_ctx = _dot_nt(q2, ck_s[:, c:c + LANE])
            o2 = _softmax_pv([s_lat, s_ctx], [vl[:, c:c + LANE], cv_s[:, c:c + LANE]])
            g = g_ref[0, rows, c:c + LANE].astype(F32)
            dst_ref[rows, c:c + LANE] = (_unstack_pair(o2, 0, m) * _silu(g)).astype(BF16)


def _nat_bias_rings(rpb):
    nc = NAT_COLS - 1
    rep = lambda a, n: jnp.broadcast_to(a, a.shape[:-1] + (n,))
    ring = jnp.concatenate([rpb[..., nc:], rep(rpb[..., -1:], GRID_W - 1 - nc), rep(rpb[..., :1], GRID_W - nc),
                            rpb[..., :nc]], axis=-1)
    neg = lambda n: jnp.full(ring.shape[:2] + (n, 2 * GRID_W), NEG, F32)
    return jnp.concatenate([neg(NAT_BIAS_PAD), ring, neg(NAT_BIAS_PAD + 1)], axis=2)


def _lat_attn_rider(z, sink, ck_a, cv_a, ck_n, cv_n, rings, layer):
    kvw = A_KV_HEADS * HEAD_DIM
    hw = NAT_HEADS * HEAD_DIM

    def body(sink_ref, qa_ref, kva_ref, ga_ref, cka_ref, cva_ref, qd_ref, kd_ref, vd_ref, gd_ref, ckd_ref, cvd_ref,
             ring_ref, cka_s, cva_s, ckd_s, cvd_s, bias_s, ba_s, bd_s):
        _win_attn_rows(sink_ref, qa_ref, kva_ref, ga_ref, cka_ref, cva_ref, cka_s, cva_s, ba_s, layer)
        _nat_rows(qd_ref, kd_ref, vd_ref, gd_ref, ckd_ref, cvd_ref, ring_ref, ckd_s, cvd_s, bias_s, bd_s)

    tile = lambda col: pl.BlockSpec((1, TM, 256), lambda b, t: (b, t, col))
    whole = lambda col: pl.BlockSpec((1, DEC_SEQ, 256), lambda b, t: (b, 0, col))
    cache = lambda w: pl.BlockSpec((1, 1, PAST_LEN, w), lambda b, t: (b, layer, 0, 0))
    return _Rider(
        body=body,
        in_specs=[pl.BlockSpec(memory_space=pltpu.SMEM), tile(Z_AQ // 256), whole(Z_AK // 256), tile(Z_AG // 256),
                  cache(kvw), cache(kvw), tile(Z_DQ // 256), whole(Z_DK // 256), whole(Z_DV // 256),
                  tile(Z_DG // 256), cache(hw), cache(hw), _layer_spec(rings.shape[1:], layer)],
        args=[sink, z, z, z, ck_a, cv_a, z, z, z, z, ck_n, cv_n, rings],
        out_shape=[], out_specs=[],
        scratch=[pltpu.VMEM((PAST_LEN, kvw), BF16), pltpu.VMEM((PAST_LEN, kvw), BF16),
                 pltpu.VMEM((PAST_LEN, hw), BF16), pltpu.VMEM((PAST_LEN, hw), BF16),
                 pltpu.VMEM((NAT_HEADS, NAT_BIAS_ROWS, GRID_W, LANE), F32),
                 pltpu.VMEM((TM, BRANCH_W), BF16), pltpu.VMEM((TM, BRANCH_W), BF16)],
        branches=True)


def _mla_body(q_ref, ckv_ref, kpe_ref, g_ref, cckv_ref, ckpe_ref, wkv_ref, o_ref, k_s, v_s):
    nlat = DEC_SEQ
    step = pl.program_id(0) * pl.num_programs(1) + pl.program_id(1)

    @pl.when(step % MLA_TILES == 0)
    def _():
        rows = 512

        def fill(r0, ckv, kpe_blk):
            kv = _dot(ckv, wkv_ref[...])
            for h, kh in enumerate(_mla_keys(kv, kpe_blk)):
                k_s[h, r0:r0 + rows, :] = kh
                v_s[h, r0:r0 + rows, :] = kv[:, h * LANE:(h + 1) * LANE].astype(BF16)

        for c in range(nlat // rows):
            fill(c * rows, ckv_ref[0, c * rows:(c + 1) * rows, :], kpe_ref[0, c * rows:(c + 1) * rows, :].astype(F32))
        ckpe = ckpe_ref[0, 0]
        ckpe_blk = jnp.concatenate([jnp.zeros((PAST_LEN, KPE_LANE), F32), ckpe,
                                    jnp.zeros((PAST_LEN, LANE - KPE_LANE - MLA_ROPE), F32)], axis=1)
        fill(nlat, cckv_ref[0, 0].astype(BF16), ckpe_blk)

    def store(pair, val):
        o_ref[0, :, pair * LANE:(pair + 1) * LANE] = val

    _mla_heads(lambda h: q_ref[0, :, LANE * h:LANE * (h + 1)], lambda h: k_s[h], lambda h: v_s[h],
               lambda p: _silu(g_ref[0, :, p * LANE:(p + 1) * LANE].astype(F32)), store)


def _mla_rider(z, cache_ckv, cache_kpe, wkv, layer, nt):
    nkeys = DEC_SEQ + PAST_LEN
    assert DEC_BATCH * MLA_TILES == POST_STEPS
    bt = lambda b, t: ((b * nt + t) // MLA_TILES, (b * nt + t) % MLA_TILES)
    tile = lambda col: (lambda b, t: bt(b, t) + (col,))
    whole = lambda col: (lambda b, t: (bt(b, t)[0], 0, col))
    cache = lambda b, t: (bt(b, t)[0], layer, 0, 0)
    return _Rider(
        body=_mla_body,
        in_specs=[
            pl.BlockSpec((1, TQ_MLA, MLA_HEADS * LANE), tile(Z_QM // (MLA_HEADS * LANE))),
            pl.BlockSpec((1, DEC_SEQ, LANE), whole(Z_CKV // LANE)),
            pl.BlockSpec((1, DEC_SEQ, LANE), whole(Z_KPE // LANE)),
            pl.BlockSpec((1, TQ_MLA, 256), tile(Z_BG // 256)),
            pl.BlockSpec((1, 1, PAST_LEN, MLA_KV_LORA), cache),
            pl.BlockSpec((1, 1, PAST_LEN, MLA_ROPE), cache),
            _layer_spec((MLA_KV_LORA, 512), layer),
        ],
        args=[z, z, z, z, cache_ckv, cache_kpe, wkv],
        out_shape=[jax.ShapeDtypeStruct((DEC_BATCH, DEC_SEQ, BRANCH_W), BF16)],
        out_specs=[pl.BlockSpec((1, TQ_MLA, BRANCH_W), tile(0))],
        scratch=[pltpu.VMEM((MLA_HEADS, nkeys, LANE), BF16), pltpu.VMEM((MLA_HEADS, nkeys, LANE), BF16)])


def _s5_mats(lam_re, lam_im, log_dt, b_re, b_im, c_re, c_im):
    t = S5_T
    hp = lax.Precision.HIGHEST
    lre = jnp.minimum(lam_re, -1e-4)
    lim = lam_im
    dt = jnp.exp(log_dt)[..., None]
    er, ei = lre * dt, lim * dt
    mag = jnp.exp(er)
    are, aim = mag * jnp.cos(ei), mag * jnp.sin(ei)
    den = lre * lre + lim * lim
    qre = ((are - 1.0) * lre + aim * lim) / den
    qim = (aim * lre - (are - 1.0) * lim) / den
    bbr = qre[..., None] * b_re - qim[..., None] * b_im
    bbi = qre[..., None] * b_im + qim[..., None] * b_re
    n = jnp.arange(t + 1, dtype=F32)[:, None, None, None]
    pmag = jnp.exp(er[None] * n)
    pr, pi = pmag * jnp.cos(ei[None] * n), pmag * jnp.sin(ei[None] * n)

    def c_times_pow(d, pw_r, pw_i):
        rep = lambda a: jnp.repeat(jnp.transpose(a, (1, 2, 0)), S5_CH, axis=-1)
        til = lambda a: jnp.tile(jnp.transpose(a, (0, 2, 1)), (1, 1, t))
        cr, ci, ar, ai = til(c_re[d]), til(c_im[d]), rep(pw_r), rep(pw_i)
        return cr * ar - ci * ai, cr * ai + ci * ar

    def lag_kernels(d, wr, wi):
        return (jnp.einsum('gpk,gpx->gkx', bbr[d], wr, precision=hp)
                - jnp.einsum('gpk,gpx->gkx', bbi[d], wi, precision=hp))

    wf0 = c_times_pow(0, pr[:t, 0], pi[:t, 0])
    wf1 = c_times_pow(0, pr[1:, 0], pi[1:, 0])
    wb = c_times_pow(1, pr[1:, 1][::-1], pi[1:, 1][::-1])
    kb0 = (jnp.einsum('gpk,gcp->gkc', bbr[1], c_re[1], precision=hp)
           - jnp.einsum('gpk,gcp->gkc', bbi[1], c_im[1], precision=hp))
    kf = lag_kernels(0, *wf0) + jnp.pad(kb0, ((0, 0), (0, 0), (0, S5_TC - S5_CH)))
    kb = lag_kernels(1, *wb) * jnp.asarray(np.arange(S5_TC) >= S5_CH, F32)
    ring = jnp.concatenate([kf, kb], axis=-1)

    def state_cols(pw_r, pw_i, d):
        rep = lambda a: jnp.repeat(jnp.transpose(a, (1, 0, 2)), S5_CH, axis=1)
        til = lambda a: jnp.tile(jnp.transpose(a, (0, 2, 1)), (1, t, 1))
        ar, ai, br, bi = rep(pw_r), rep(pw_i), til(bbr[d]), til(bbi[d])
        return ar * br - ai * bi, ar * bi + ai * br

    sfr, sfi = state_cols(pr[:t, 0][::-1], pi[:t, 0][::-1], 0)
    sbr, sbi = state_cols(pr[:t, 1], pi[:t, 1], 1)
    m_state = jnp.concatenate([sfr, sbr, sfi, sbi], axis=-1)
    m_out = jnp.concatenate([wf1[0], wb[0], -wf1[1], -wb[1]], axis=1)

    a_t = jnp.stack([jnp.concatenate([pr[t, 0], pr[t, 1]], axis=-1),
                     jnp.concatenate([pi[t, 0], pi[t, 1]], axis=-1)], axis=1)
    return ring, m_state.astype(BF16), m_out.astype(BF16), a_t


def _s5_kernel(uc_ref, ul_ref, ring_ref, ms_ref, mo_ref, at_ref, h0_ref, yc_ref, yl_ref, hend_ref,
               s_s, hf_s, hb_s, m_s):
    p = S5_STATE
    u = jnp.concatenate([uc_ref[0], ul_ref[0]], axis=0)
    s = _dot(u, ms_ref[0])
    s_s[0] = s[:, 0:2 * p]
    s_s[1] = s[:, 2 * p:4 * p]
    are = at_ref[0, 0:1, :]
    aim = at_ref[0, 1:2, :]
    fwd_lane = lax.broadcasted_iota(jnp.int32, (1, 2 * p), 1) < p

    for k in range(S5_CH):
        rk = jnp.broadcast_to(ring_ref[0, k:k + 1, :], (S5_T, 2 * S5_TC))
        rolled = pltpu.roll(rk, 0, 1, stride=S5_CH, stride_axis=0)
        for blk in range(S5_TC // LANE):
            m_s[blk, pl.ds(k, S5_T, stride=S5_CH), :] = rolled[:, blk * LANE:(blk + 1) * LANE]

    def scan(row0, nk, nb, hre, him):
        for st in range(nk):
            rf = pl.ds(row0 + st, nb, stride=nk)
            rb = pl.ds(row0 + nk - 1 - st, nb, stride=nk)
            hf_s[0, rf, :] = hre
            hf_s[1, rf, :] = him
            hb_s[0, rb, :] = hre
            hb_s[1, rb, :] = him
            sre = jnp.where(fwd_lane, s_s[0, rf, :], s_s[0, rb, :])
            sim = jnp.where(fwd_lane, s_s[1, rf, :], s_s[1, rb, :])
            hre, him = are * hre - aim * him + sre, are * him + aim * hre + sim
        return hre, him

    zero = jnp.zeros((BATCH, 2 * p), F32)
    hre, him = scan(0, S5_NK_CTX, BATCH, zero, zero)
    hend_ref[0] = jnp.concatenate([hre, him], axis=1)
    scan(S5_ROWS, S5_NK_LAT, DEC_BATCH, h0_ref[0, :, 0:2 * p], h0_ref[0, :, 2 * p:4 * p])
    hst = jnp.concatenate([jnp.where(fwd_lane, hf_s[0], hb_s[0]), jnp.where(fwd_lane, hf_s[1], hb_s[1])], axis=1)
    m_intra = jnp.concatenate([m_s[blk] for blk in range(S5_TC // LANE)], axis=1).astype(BF16)
    y = _dot(u, m_intra) + _dot(hst.astype(BF16), mo_ref[0])
    yc_ref[0] = y[0:S5_ROWS]
    yl_ref[0] = y[S5_ROWS:2 * S5_ROWS]


def _s5_call(ut_ctx, ut_lat, ring, m_state, m_out, a_t, h0, layer):
    g3 = lambda g: (g, 0, 0)
    lg = lambda shape: pl.BlockSpec((None, 1) + shape, lambda g: (layer, g, 0, 0))
    yshape = jax.ShapeDtypeStruct((S5_GROUPS, S5_ROWS, S5_TC), F32)
    uspec = pl.BlockSpec((1, S5_ROWS, S5_TC), g3)
    return pl.pallas_call(
        _s5_kernel,
        out_shape=(yshape, yshape, jax.ShapeDtypeStruct((S5_GROUPS, BATCH, 4 * S5_STATE), F32)),
        grid=(S5_GROUPS,),
        in_specs=[
            uspec, uspec,
            lg((S5_CH, 2 * S5_TC)),
            lg((S5_TC, 4 * S5_STATE)),
            lg((4 * S5_STATE, S5_TC)),
            lg((2, 2 * S5_STATE)),
            lg((DEC_BATCH, 4 * S5_STATE)),
        ],
        out_specs=(uspec, uspec, pl.BlockSpec((1, BATCH, 4 * S5_STATE), g3)),
        scratch_shapes=[pltpu.VMEM((2, 2 * S5_ROWS, LANE), F32)] * 3 + [pltpu.VMEM((S5_TC // LANE, S5_TC, LANE), F32)],
        compiler_params=_cparams(("arbitrary",), 32),
        name="s5_chunks",
    )(ut_ctx, ut_lat, ring, m_state, m_out, a_t, h0)


_POST_IN = ('x', 'mod', 'ng', 'ba', 'bb', 'bd', 'y', 'u', 'cg', 'sd', 'wglu', 'wm', 'wbr', 'wo', 'fg')


class _Rider(NamedTuple):
    body: Callable
    in_specs: list
    args: list
    out_shape: list
    out_specs: list
    scratch: list
    branches: bool = False


NO_RIDER = _Rider(body=lambda: None, in_specs=[], args=[], out_shape=[], out_specs=[], scratch=[])


def _post_kernel(*refs, final, rider):
    names = [n for n in _POST_IN if not (rider.branches and n in ('ba', 'bd'))]
    r = dict(zip(names, refs))
    n0, n_in, n_out = len(names), len(rider.in_specs), len(rider.out_specs)
    o_ref = refs[n0 + n_in]
    ys_s = refs[n0 + n_in + 1 + n_out]
    rider_scratch = refs[n0 + n_in + 2 + n_out:]
    rider.body(*refs[n0:n0 + n_in], *refs[n0 + n_in + 1:n0 + n_in + 1 + n_out], *rider_scratch)
    ba, bd = (rider_scratch[-2][...], rider_scratch[-1][...]) if rider.branches else (r['ba'][0], r['bd'][0])
    x_ref, mod_ref, ng_ref, y_ref = r['x'], r['mod'], r['ng'], r['y']
    x = x_ref[0]
    hb = _ada_h(x, mod_ref, ng_ref).astype(BF16)
    for m in range(S5_T // 8):
        for hf in range(2):
            vs = _slot_transpose([y_ref[8 * hf + gp, :, m * LANE:(m + 1) * LANE] for gp in range(8)])
            for s in range(8):
                ys_s[hf, pl.ds(8 * m + s, TM_CHUNKS, stride=S5_T), :] = vs[s]
    yv = jnp.concatenate([ys_s[0], ys_s[1]], axis=1) + r['sd'][...] * r['u'][0].astype(F32)
    gel = 0.5 * yv * (1.0 + jnp.tanh(math.sqrt(2.0 / math.pi) * (yv + 0.044715 * (yv * yv * yv))))
    gl = _dot(gel.astype(BF16), r['wglu'][...])
    oc = gl[:, 0:BRANCH_W] * _sigmoid(gl[:, BRANCH_W:2 * BRANCH_W])
    bc = (oc * _silu(r['cg'][0].astype(F32))).astype(BF16)
    acc = None
    for k, br in enumerate((ba, r['bb'][0], bc, bd)):
        proj = _dot(br, r['wbr'][k])
        mg = _sigmoid(_dot(hb, r['wm'][:, k * D_MODEL:(k + 1) * D_MODEL]))
        acc = mg * proj if acc is None else acc + mg * proj
    y = _dot(acc.astype(BF16), r['wo'][...])
    out = x + mod_ref[0, :, 2 * D_MODEL:3 * D_MODEL] * y
    if final:
        out = _rms(out, r['fg'][...])
    o_ref[0] = out


def _post_call(x, mods, mod_row0, p, ba, bb, bd, y_s5, z, fg, final, layer, name, rider):
    nb, length, _ = x.shape
    nt = length // TM
    assert nb * nt == POST_STEPS
    tok = lambda b, t: (b, t, 0)
    br_spec = pl.BlockSpec((1, TM, BRANCH_W), tok)
    branch_args = [bb] if rider.branches else [ba, bb, bd]
    return pl.pallas_call(
        functools.partial(_post_kernel, final=final, rider=rider),
        out_shape=tuple([jax.ShapeDtypeStruct((nb, length, D_MODEL), F32)] + rider.out_shape),
        grid=(nb, nt),
        in_specs=[
            pl.BlockSpec((1, TM, D_MODEL), tok),
            _mod_spec(layer, mod_row0),
            _layer_spec((1, D_MODEL), layer),
        ] + [br_spec] * len(branch_args) + [
            pl.BlockSpec((S5_GROUPS, TM_CHUNKS, S5_TC), lambda b, t: (0, b * nt + t, 0)),
            pl.BlockSpec((1, TM, 256), lambda b, t: (b, t, Z_CU // 256)),
            pl.BlockSpec((1, TM, 256), lambda b, t: (b, t, Z_CG // 256)),
            _layer_spec((1, BRANCH_W), layer),
            _layer_spec((BRANCH_W, 2 * BRANCH_W), layer),
            _layer_spec((D_MODEL, N_BRANCH * D_MODEL), layer),
            _layer_spec((N_BRANCH, BRANCH_W, D_MODEL), layer),
            _layer_spec((D_MODEL, D_MODEL), layer),
            pl.BlockSpec((1, D_MODEL), lambda b, t: (0, 0)),
        ] + rider.in_specs,
        out_specs=tuple([pl.BlockSpec((1, TM, D_MODEL), tok)] + rider.out_specs),
        scratch_shapes=[pltpu.VMEM((2, TM, LANE), F32)] + rider.scratch,
        compiler_params=_cparams(("arbitrary", "arbitrary"), 56),
        name=name,
    )(x, mods, p['ng'], *branch_args, y_s5, z, z, p['s5d'], p['wglu'], p['wm'], p['wbr'], p['wo'], fg, *rider.args)


def _rope_tables():
    t = np.arange(DEC_SEQ)
    row = (t // GRID_W).astype(np.float64)
    col = (t % GRID_W).astype(np.float64)

    def pattern(half):
        inv = ROPE_BASE ** (-np.arange(half, dtype=np.float64) / half)
        zeros = np.zeros((DEC_SEQ, half))
        cs, s_up, s_lo = [], [], []
        for pos in (row, col):
            ang = pos[:, None] * inv[None, :]
            c, s = np.cos(ang), np.sin(ang)
            cs += [c, c]
            s_up += [zeros, s]
            s_lo += [-s, zeros]
        return [np.concatenate(parts, axis=1) for parts in (cs, s_up, s_lo)]

    tab_a = np.stack([np.tile(part, (1, LANE // HEAD_DIM)) for part in pattern(HEAD_DIM // 4)])
    ident = [np.ones, np.zeros, np.zeros]
    tab_m = np.stack([np.concatenate([fill((DEC_SEQ, KPE_LANE)), part,
                                      fill((DEC_SEQ, LANE - KPE_LANE - MLA_ROPE))], axis=1)
                      for fill, part in zip(ident, pattern(MLA_ROPE // 4))])
    return jnp.asarray(tab_a, F32), jnp.asarray(tab_m, F32)


_W_IN_MOVES = ((256, 256, 256), (512, 1952, 768), (R_CQ, 768, 384), (R_CU, 1440, 256),
               (R_G + 256, 1184, 256), (R_G + 512, 1696, 256), (R_G + 768, 2720, 256))
_W_IN_HEAD_MOVES = ((0, 0), (R_G, 512))
_W_IN_KPE = 1152
W_IN_ROWS = 256


def _w_in_kernel(wt_ref, o_ref):
    take = lambda src, width: wt_ref[0, src:src + width, :].T
    for dst, src, width in _W_IN_MOVES:
        o_ref[0, :, dst:dst + width] = take(src, width).astype(BF16)
    for dst, src in _W_IN_HEAD_MOVES:
        blk = take(src, A_HEADS * HEAD_DIM)
        for i, h in enumerate(A_HEAD_ORDER):
            o_ref[0, :, dst + HEAD_DIM * i:dst + HEAD_DIM * (i + 1)] = blk[:, HEAD_DIM * h:HEAD_DIM * (h + 1)].astype(BF16)
    o_ref[0, :, R_KPE:R_CU] = jnp.zeros((W_IN_ROWS, LANE), BF16)
    o_ref[0, :, R_KPE + KPE_LANE:R_KPE + KPE_LANE + MLA_ROPE] = take(_W_IN_KPE, LANE)[:, 0:MLA_ROPE].astype(BF16)


def _w_in_call(w_in):
    wt = jnp.swapaxes(w_in, 1, 2)
    return pl.pallas_call(
        _w_in_kernel,
        out_shape=jax.ShapeDtypeStruct((DEPTH, D_MODEL, R_W), BF16),
        grid=(DEPTH, D_MODEL // W_IN_ROWS),
        in_specs=[pl.BlockSpec((1, wt.shape[1], W_IN_ROWS), lambda i, r: (i, 0, r))],
        out_specs=pl.BlockSpec((1, W_IN_ROWS, R_W), lambda i, r: (i, r, 0)),
        compiler_params=_cparams(("arbitrary", "arbitrary"), 32),
        name="w_in_layout",
    )(wt)


def _cast_rider(ws):
    flat = [w.reshape(-1, w.shape[-1]) for w in ws]

    def body(*refs):
        for w_ref, o_ref in zip(refs[:len(ws)], refs[len(ws):]):
            o_ref[...] = w_ref[...].astype(BF16)

    specs = [pl.BlockSpec((w.shape[0] // POST_STEPS, w.shape[1]), lambda b, t: (t, 0)) for w in flat]
    return _Rider(body=body, in_specs=specs, args=flat,
                  out_shape=[jax.ShapeDtypeStruct(w.shape, BF16) for w in flat], out_specs=specs, scratch=[])


def _reorder_w_q_up(w):
    w = w.reshape(MLA_Q_LORA, MLA_HEADS, MLA_NOPE + MLA_ROPE)
    w = jnp.concatenate([w, jnp.zeros((MLA_Q_LORA, MLA_HEADS, LANE - MLA_NOPE - MLA_ROPE), w.dtype)], axis=-1)
    return w.reshape(MLA_Q_LORA, MLA_HEADS * LANE)


def _reorder_w_branch(w):
    wa = w[0].reshape(A_HEADS, HEAD_DIM, D_MODEL)
    wa = jnp.concatenate([wa[h] for h in A_HEAD_ORDER], axis=0)
    return jnp.concatenate([wa[None], w[1:]], axis=0)


def kernel(x_prompt, x_sample, cache_a_k, cache_a_v, cache_mla_ckv, cache_mla_kpe, cache_na_k, cache_na_v,
           state_s5_re, state_s5_im, c, c_ctx, w_mod, b_mod, norm_g, w_in, w_merge, a_sink,
           mla_q_norm, mla_w_q_up, mla_kv_norm, mla_w_kv_up, s5_lam_re, s5_lam_im, s5_log_dt,
           s5_b_re, s5_b_im, s5_c_re, s5_c_im, s5_d, s5_w_glu, na_rpb, w_branch, w_out, final_norm_g):
    n_ctx = BATCH * SEQ
    conds = jnp.concatenate([c, c_ctx[None, :], jnp.zeros((3, D_MODEL), F32)], axis=0)
    mods = _mod_call(conds, w_mod, b_mod).reshape(DEPTH, 8, 1, 3 * D_MODEL)
    tabs = _rope_tables()
    ck_a = cache_a_k.reshape(DEC_BATCH, DEPTH, PAST_LEN, A_KV_HEADS * HEAD_DIM)
    cv_a = cache_a_v.reshape(DEC_BATCH, DEPTH, PAST_LEN, A_KV_HEADS * HEAD_DIM)
    ck_n = cache_na_k.reshape(DEC_BATCH, DEPTH, PAST_LEN, NAT_HEADS * HEAD_DIM)
    cv_n = cache_na_v.reshape(DEC_BATCH, DEPTH, PAST_LEN, NAT_HEADS * HEAD_DIM)
    fg = final_norm_g.reshape(1, D_MODEL)
    caches = None

    p = dict(
        ng=norm_g.reshape(DEPTH, 1, D_MODEL),
        w_raw=_w_in_call(w_in),
        qnorm=mla_q_norm.reshape(DEPTH, 1, MLA_Q_LORA),
        wq=jax.vmap(_reorder_w_q_up)(mla_w_q_up).astype(BF16),
        kvnorm=mla_kv_norm.reshape(DEPTH, 1, MLA_KV_LORA),
        wkv=mla_w_kv_up.astype(BF16),
        wbr=jax.vmap(_reorder_w_branch)(w_branch).astype(BF16),
        wglu=s5_w_glu.astype(BF16),
        s5d=s5_d.reshape(DEPTH, 1, BRANCH_W),
    )
    ring, m_state, m_out, a_t = jax.vmap(_s5_mats)(s5_lam_re, s5_lam_im, s5_log_dt, s5_b_re, s5_b_im, s5_c_re, s5_c_im)
    nat_bias = _nat_bias_rings(na_rpb * LOG2E)
    h0 = jnp.concatenate([state_s5_re[:, :, 0], state_s5_re[:, :, 1], state_s5_im[:, :, 0], state_s5_im[:, :, 1]],
                         axis=-1)
    h0 = jnp.transpose(h0, (1, 2, 0, 3))

    yp = x_prompt.reshape(1, n_ctx, D_MODEL)
    ys = x_sample
    hends = []
    for i in range(DEPTH):
        rider = _cast_rider([w_merge, w_out]) if i == 0 else NO_RIDER
        res = _inproj_call(yp, mods, DEC_BATCH, p, None, caches, i, "inproj_ctx", rider)
        z_ctx, ut_ctx, caches = res[0], res[1], list(res[2:2 + N_CACHE])
        if i == 0:
            p['wm'], p['wo'] = res[2 + N_CACHE].reshape(w_merge.shape), res[3 + N_CACHE].reshape(w_out.shape)
        zc = z_ctx.reshape(BATCH, SEQ, Z_W)
        z_lat, ut_lat, oa_c, ob_c, od_c = _inproj_call(ys, mods, 0, p, tabs, None, i, "inproj_lat_ctx_attn",
                                                       _ctx_attn_rider(zc, a_sink, p['wkv'], i, DEC_SEQ // TM))

        y_ctx, y_lat, hend = _s5_call(ut_ctx, ut_lat, ring, m_state, m_out, a_t, h0, i)
        hends.append(hend)

        flat = lambda a: a.reshape(1, n_ctx, BRANCH_W)
        final = i == DEPTH - 1
        yp, ob_l = _post_call(yp, mods, DEC_BATCH, p, flat(oa_c), flat(ob_c), flat(od_c), y_ctx, z_ctx, fg, final, i,
                              "post_ctx_lat_mla", _mla_rider(z_lat, cache_mla_ckv, cache_mla_kpe, p['wkv'], i, POST_STEPS))
        ys, = _post_call(ys, mods, 0, p, None, ob_l, None, y_lat, z_lat, fg, final, i, "post_lat_win_nat",
                         _lat_attn_rider(z_lat, a_sink, ck_a, cv_a, ck_n, cv_n, nat_bias, i))

    hend = jnp.stack(hends).reshape(DEPTH, S5_GROUPS, BATCH, 2, 2, S5_STATE)
    hend = jnp.transpose(hend, (3, 2, 0, 4, 1, 5))
    ak, av, ckv, kpe, nk, nv = caches
    heads = lambda a, h: a.reshape(BATCH, DEPTH, SEQ, h, HEAD_DIM)
    return (yp.reshape(BATCH, SEQ, D_MODEL), ys, heads(ak, A_KV_HEADS), heads(av, A_KV_HEADS), ckv, jnp.swapaxes(kpe, 2, 3),
            heads(nk, NAT_HEADS), heads(nv, NAT_HEADS), hend[0], hend[1])
```

```python
import functools
import math
from typing import Callable, NamedTuple

import numpy as np
import jax
import jax.numpy as jnp
from jax import lax
from jax.experimental import pallas as pl
from jax.experimental.pallas import tpu as pltpu

F32 = jnp.float32
BF16 = jnp.bfloat16

D_MODEL = 1024
BATCH = 32
SEQ = 256
DEPTH = 2
DEC_BATCH = 4
DEC_SEQ = 2048
PAST_LEN = 512
GRID_W = 64
HEAD_DIM = 64
BRANCH_W = 256
N_BRANCH = 4
Q_BLOCK = 128
A_HEADS = 4
A_KV_HEADS = 2
A_GROUP = A_HEADS // A_KV_HEADS
A_WINDOW = 128
MLA_HEADS = 4
MLA_Q_LORA = 256
MLA_KV_LORA = 128
MLA_NOPE = 64
MLA_ROPE = 32
MLA_V = 64
S5_CH = 16
S5_GROUPS = BRANCH_W // S5_CH
S5_STATE = 64
NAT_HEADS = 4
NAT_ROWS = 8
NAT_COLS = 16
ROPE_BASE = 10000.0
EPS = 1e-6
NEG = -1e30
LOG2E = 1.4426950408889634
Q_SCALE = HEAD_DIM ** -0.5 * LOG2E
MLA_Q_SCALE = (MLA_NOPE + MLA_ROPE) ** -0.5 * LOG2E

LANE = 128

R_AV, R_CQ, R_CKV, R_KPE, R_CU, R_G, R_W = 384, 1280, 1536, 1664, 1792, 2048, 3072
KPE_LANE = 64
A_HEAD_ORDER = (0, 2, 1, 3)
Z_AQ, Z_AK, Z_AV = 0, 256, 384
Z_DQ, Z_DK, Z_DV = 512, 768, 1024
Z_CKV, Z_KPE = 1280, 1408
Z_QM = 1536
Z_AG, Z_BG, Z_CG, Z_DG = 2048, 2304, 2560, 2816
Z_CU = 3072
Z_W = 3328

S5_T = 32
S5_TC = S5_T * S5_CH
S5_ROWS = 256
S5_NK_CTX = SEQ // S5_T
S5_NK_LAT = DEC_SEQ // S5_T

TM = 512
TM_CHUNKS = TM // S5_T
TQ_MLA = 512
MLA_TILES = DEC_SEQ // TQ_MLA
POST_STEPS = BATCH * SEQ // TM
CTX_BPS = 2
NAT_RPS = 4
NAT_UNION = NAT_ROWS + NAT_RPS
NAT_BIAS_PAD = NAT_RPS
NAT_BIAS_ROWS = 2 * NAT_ROWS - 1 + 2 * NAT_BIAS_PAD


def _cparams(sem, vmem_mb):
    return pltpu.CompilerParams(dimension_semantics=sem, vmem_limit_bytes=vmem_mb * 1024 * 1024)


def _sigmoid(x):
    return 1.0 / (1.0 + jnp.exp(-x))


def _silu(x):
    return x * _sigmoid(x)


def _rms(x, g):
    return x * lax.rsqrt(jnp.mean(x * x, axis=-1, keepdims=True) + EPS) * g


def _dot(a, b):
    return jnp.dot(a, b, preferred_element_type=F32)


def _dot_nt(a, b):
    return lax.dot_general(a, b, (((1,), (1,)), ((), ())), preferred_element_type=F32)


def _slot_transpose(vs):
    lane = lax.broadcasted_iota(jnp.int32, vs[0].shape, 1)
    vs = list(vs)
    for d in (4, 2, 1):
        keep = (lane & (S5_CH * d)) == 0
        nxt = list(vs)
        for lo in range(8):
            if lo & d:
                continue
            hi = lo + d
            nxt[lo] = jnp.where(keep, vs[lo], pltpu.roll(vs[hi], S5_CH * d, 1))
            nxt[hi] = jnp.where(keep, pltpu.roll(vs[lo], LANE - S5_CH * d, 1), vs[hi])
        vs = nxt
    return vs


def _mod_kernel(c_ref, w_ref, b_ref, o_ref):
    s = _silu(c_ref[...])
    o_ref[0] = _dot(s.astype(BF16), w_ref[0].astype(BF16)) + b_ref[0]


def _mod_call(conds, w_mod, b_mod):
    nc = 512
    return pl.pallas_call(
        _mod_kernel,
        out_shape=jax.ShapeDtypeStruct((DEPTH, 8, 3 * D_MODEL), F32),
        grid=(DEPTH, 3 * D_MODEL // nc),
        in_specs=[
            pl.BlockSpec((8, D_MODEL), lambda i, j: (0, 0)),
            pl.BlockSpec((1, D_MODEL, nc), lambda i, j: (i, 0, j)),
            pl.BlockSpec((1, 1, nc), lambda i, j: (i, 0, j)),
        ],
        out_specs=pl.BlockSpec((1, 8, nc), lambda i, j: (i, 0, j)),
        compiler_params=_cparams(("arbitrary", "arbitrary"), 32),
        name="mod_rows",
    )(conds, w_mod, b_mod.reshape(DEPTH, 1, 3 * D_MODEL))


def _ada_h(x, mod_ref, ng_ref):
    shift = mod_ref[0, :, 0:D_MODEL]
    scale = mod_ref[0, :, D_MODEL:2 * D_MODEL]
    return _rms(x, ng_ref[...]) * (1.0 + scale) + shift


def _rope_block(xs, tab_ref, shift):
    return (xs * tab_ref[0] + pltpu.roll(xs, shift, 1) * tab_ref[1]
            + pltpu.roll(xs, LANE - shift, 1) * tab_ref[2])


CACHE_DIMS = ((SEQ, 128), (SEQ, 128), (SEQ, MLA_KV_LORA), (MLA_ROPE, SEQ), (SEQ, 256), (SEQ, 256))
N_CACHE = len(CACHE_DIMS)


def _inproj_kernel(*refs, rope, rider, first):
    x_ref, mod_ref, ng_ref, w_ref, qn_ref, wq_ref, kvn_ref = refs[:7]
    if rope:
        ta_ref, tm_ref = refs[7:9]
        n_in, n_out = len(rider.in_specs), len(rider.out_specs)
        z_ref, ut_ref = refs[9 + n_in:11 + n_in]
        u_s = refs[11 + n_in + n_out]
        rider.body(*refs[9:9 + n_in], *refs[11 + n_in:11 + n_in + n_out], *refs[12 + n_in + n_out:])
    else:
        n_c = 0 if first else N_CACHE
        n_in, n_out = len(rider.in_specs), len(rider.out_specs)
        base = 7 + n_c + n_in
        z_ref, ut_ref, ak_ref, av_ref, ckv_ref, kpe_ref, nk_ref, nv_ref = refs[base:base + 2 + N_CACHE]
        u_s = refs[base + 2 + N_CACHE + n_out]
        rider.body(*refs[7 + n_c:base], *refs[base + 2 + N_CACHE:base + 2 + N_CACHE + n_out],
                   *refs[base + 3 + N_CACHE + n_out:])
    h = _ada_h(x_ref[0], mod_ref, ng_ref)
    raw = _dot(h.astype(BF16), w_ref[...])
    for j in range(R_AV // LANE):
        blk = raw[:, j * LANE:(j + 1) * LANE]
        if j < Z_AK // LANE:
            blk = blk * Q_SCALE
        if rope:
            blk = _rope_block(blk, ta_ref, 16)
        z_ref[0, :, j * LANE:(j + 1) * LANE] = blk.astype(BF16)
    z_ref[0, :, Z_AV:Z_DQ] = raw[:, Z_AV:Z_DQ].astype(BF16)
    z_ref[0, :, Z_DQ:Z_DK] = (raw[:, Z_DQ:Z_DK] * Q_SCALE).astype(BF16)
    z_ref[0, :, Z_DK:R_CQ] = raw[:, Z_DK:R_CQ].astype(BF16)
    qn = _rms(raw[:, R_CQ:R_CKV], qn_ref[...])
    q = _dot(qn.astype(BF16), wq_ref[...]) * MLA_Q_SCALE
    kp = raw[:, R_KPE:R_CU]
    ckv = _rms(raw[:, R_CKV:R_KPE], kvn_ref[...])
    for hh in range(MLA_HEADS):
        qh = q[:, hh * LANE:(hh + 1) * LANE]
        if rope:
            qh = _rope_block(qh, tm_ref, 8)
        z_ref[0, :, Z_QM + hh * LANE:Z_QM + (hh + 1) * LANE] = qh.astype(BF16)
    z_ref[0, :, Z_CKV:Z_KPE] = ckv.astype(BF16)
    z_ref[0, :, Z_KPE:Z_QM] = (_rope_block(kp, tm_ref, 8) if rope else kp).astype(BF16)
    z_ref[0, :, Z_AG:Z_CU] = raw[:, R_G:R_W].astype(BF16)
    z_ref[0, :, Z_CU:Z_W] = raw[:, R_CU:R_G].astype(BF16)
    if not rope:
        half = TM // 2
        for e in range(2):
            rows = slice(e * half, (e + 1) * half)
            ak_ref[e, 0] = raw[rows, Z_AK:Z_AV]
            av_ref[e, 0] = raw[rows, Z_AV:Z_DQ]
            nk_ref[e, 0] = raw[rows, Z_DK:Z_DV]
            nv_ref[e, 0] = raw[rows, Z_DV:R_CQ]
            ckv_ref[e, 0] = ckv[rows]
            kpe_ref[e, 0] = kp[rows].T[KPE_LANE:KPE_LANE + MLA_ROPE]
            if first:
                for cref in (ak_ref, av_ref, nk_ref, nv_ref, ckv_ref, kpe_ref):
                    cref[e, 1:DEPTH] = jnp.zeros((DEPTH - 1,) + cref.shape[2:], F32)
    for hf in range(2):
        u_s[hf] = raw[:, R_CU + hf * LANE:R_CU + (hf + 1) * LANE]
    for m in range(S5_T // 8):
        for hf in range(2):
            outs = _slot_transpose([u_s[hf, pl.ds(8 * m + s, TM_CHUNKS, stride=S5_T), :] for s in range(8)])
            for gp in range(8):
                ut_ref[8 * hf + gp, :, m * LANE:(m + 1) * LANE] = outs[gp].astype(BF16)


def _layer_spec(shape, layer):
    zeros = (0,) * len(shape)
    return pl.BlockSpec((None,) + tuple(shape), lambda *_: (layer,) + zeros)


def _mod_spec(layer, row0):
    return pl.BlockSpec((None, 1, 1, 3 * D_MODEL), lambda b, t: (layer, row0 + b, 0, 0))


def _inproj_call(x, mods, mod_row0, p, tabs, caches, layer, name, rider):
    nb, length, _ = x.shape
    rope = tabs is not None
    nt = length // TM
    in_specs = [
        pl.BlockSpec((1, TM, D_MODEL), lambda b, t: (b, t, 0)),
        _mod_spec(layer, mod_row0),
        _layer_spec((1, D_MODEL), layer),
        _layer_spec((D_MODEL, R_W), layer),
        _layer_spec((1, MLA_Q_LORA), layer),
        _layer_spec((MLA_Q_LORA, MLA_HEADS * LANE), layer),
        _layer_spec((1, MLA_KV_LORA), layer),
    ]
    args = [x, mods, p['ng'], p['w_raw'], p['qnorm'], p['wq'], p['kvnorm']]
    out_shape = [jax.ShapeDtypeStruct((nb, length, Z_W), BF16),
                 jax.ShapeDtypeStruct((S5_GROUPS, S5_ROWS, S5_TC), BF16)]
    out_specs = [pl.BlockSpec((1, TM, Z_W), lambda b, t: (b, t, 0)),
                 pl.BlockSpec((S5_GROUPS, TM_CHUNKS, S5_TC), lambda b, t: (0, b * nt + t, 0))]
    aliases = {}
    scratch = [pltpu.VMEM((2, TM, LANE), F32)]
    if rope:
        assert nb * nt == POST_STEPS
        in_specs += [pl.BlockSpec((3, TM, LANE), lambda b, t: (0, t, 0))] * 2 + rider.in_specs
        args += list(tabs) + rider.args
        out_shape += rider.out_shape
        out_specs += rider.out_specs
        scratch += rider.scratch
    else:
        for k, dims in enumerate(CACHE_DIMS):
            out_shape.append(jax.ShapeDtypeStruct((BATCH, DEPTH) + dims, F32))
            if caches is None:
                out_specs.append(pl.BlockSpec((2, DEPTH) + dims, lambda b, t: (t, 0, 0, 0)))
            else:
                in_specs.append(pl.BlockSpec(memory_space=pl.ANY))
                args.append(caches[k])
                out_specs.append(pl.BlockSpec((2, 1) + dims, lambda b, t: (t, layer, 0, 0)))
                aliases[7 + k] = 2 + k
        in_specs += rider.in_specs
        args += rider.args
        out_shape += rider.out_shape
        out_specs += rider.out_specs
        scratch += rider.scratch
    return pl.pallas_call(
        functools.partial(_inproj_kernel, rope=rope, rider=rider, first=not rope and caches is None),
        out_shape=tuple(out_shape),
        grid=(nb, nt),
        in_specs=in_specs,
        out_specs=tuple(out_specs),
        scratch_shapes=scratch,
        input_output_aliases=aliases,
        compiler_params=_cparams(("arbitrary", "arbitrary"), 56),
        name=name,
    )(*args)


def _softmax_pv(s_list, v_list, sink=None):
    m = jnp.max(s_list[0], axis=-1, keepdims=True)
    for s in s_list[1:]:
        m = jnp.maximum(m, jnp.max(s, axis=-1, keepdims=True))
    if sink is not None:
        m = jnp.maximum(m, sink)
    den = None
    o = None
    for s, v in zip(s_list, v_list):
        e = jnp.exp2(s - m)
        d = jnp.sum(e, axis=-1, keepdims=True)
        pv = _dot(e.astype(BF16), v)
        den = d if den is None else den + d
        o = pv if o is None else o + pv
    if sink is not None:
        den = den + jnp.exp2(sink - m)
    return o / den


def _low_lanes(shape):
    return lax.broadcasted_iota(jnp.int32, shape, 1) < HEAD_DIM


def _stack_heads(blocks):
    lo = _low_lanes(blocks[0].shape)
    zero = jnp.zeros_like(blocks[0])
    parts = []
    for b in blocks:
        parts += [jnp.where(lo, b, zero), jnp.where(lo, zero, b)]
    return jnp.concatenate(parts, axis=0)


def _unstack_pair(o, idx, m):
    return jnp.where(_low_lanes((m, LANE)), o[2 * idx * m:(2 * idx + 1) * m], o[(2 * idx + 1) * m:(2 * idx + 2) * m])


def _sink_column(sink_ref, layer, m):
    blk = lax.broadcasted_iota(jnp.int32, (A_HEADS * m, 1), 0) // m
    col = jnp.full((A_HEADS * m, 1), sink_ref[layer, A_HEAD_ORDER[-1]] * LOG2E, F32)
    for i in range(A_HEADS - 1):
        col = jnp.where(blk == i, sink_ref[layer, A_HEAD_ORDER[i]] * LOG2E, col)
    return col


def _mla_keys(kv, kpe_blk):
    lo = _low_lanes(kpe_blk.shape)
    return [jnp.where(lo, kv[:, h * LANE:(h + 1) * LANE], kpe_blk).astype(BF16) for h in range(MLA_HEADS)]


def _mla_heads(q_of, k_of, v_of, gate_of, store):
    for pair in range(MLA_HEADS // 2):
        outs = [_softmax_pv([_dot_nt(q_of(h), k_of(h))], [v_of(h)]) for h in (2 * pair, 2 * pair + 1)]
        o = jnp.where(_low_lanes(outs[0].shape), pltpu.roll(outs[0], MLA_V, 1), outs[1])
        store(pair, (o * gate_of(pair)).astype(BF16))


def _ctx_attn_body(sink_ref, z_ref, wkv_ref, oa_ref, ob_ref, od_ref, e, layer):
    m = SEQ
    gate = lambda c0: _silu(z_ref[e, :, c0:c0 + LANE].astype(F32))

    def store(ref, pair, val):
        ref[e, :, pair * LANE:(pair + 1) * LANE] = val

    q4 = _stack_heads([z_ref[e, :, Z_AQ:Z_AQ + LANE], z_ref[e, :, Z_AQ + LANE:Z_AQ + 2 * LANE]])
    o4 = _softmax_pv([_dot_nt(q4, z_ref[e, :, Z_AK:Z_AV])], [z_ref[e, :, Z_AV:Z_DQ]],
                     sink=_sink_column(sink_ref, layer, m))
    for pair in range(2):
        store(oa_ref, pair, (_unstack_pair(o4, pair, m) * gate(Z_AG + pair * LANE)).astype(BF16))
    kv = _dot(z_ref[e, :, Z_CKV:Z_KPE], wkv_ref[...])
    keys = _mla_keys(kv, z_ref[e, :, Z_KPE:Z_QM].astype(F32))
    kvb = kv.astype(BF16)
    _mla_heads(lambda h: z_ref[e, :, Z_QM + LANE * h:Z_QM + LANE * (h + 1)], lambda h: keys[h],
               lambda h: kvb[:, LANE * h:LANE * (h + 1)], lambda p: gate(Z_BG + p * LANE),
               functools.partial(store, ob_ref))
    for pair in range(NAT_HEADS // 2):
        c = pair * LANE
        q2 = _stack_heads([z_ref[e, :, Z_DQ + c:Z_DQ + c + LANE]])
        o2 = _softmax_pv([_dot_nt(q2, z_ref[e, :, Z_DK + c:Z_DK + c + LANE])], [z_ref[e, :, Z_DV + c:Z_DV + c + LANE]])
        store(od_ref, pair, (_unstack_pair(o2, 0, m) * gate(Z_DG + c)).astype(BF16))


def _ctx_attn_rider(z, sink, wkv, layer, nt):
    def body(sink_ref, z_ref, wkv_ref, oa_ref, ob_ref, od_ref):
        for e in range(CTX_BPS):
            _ctx_attn_body(sink_ref, z_ref, wkv_ref, oa_ref, ob_ref, od_ref, e, layer)

    step = lambda b, t: (b * nt + t, 0, 0)
    return _Rider(
        body=body,
        in_specs=[pl.BlockSpec(memory_space=pltpu.SMEM), pl.BlockSpec((CTX_BPS, SEQ, Z_W), step),
                  _layer_spec((MLA_KV_LORA, 512), layer)],
        args=[sink, z, wkv],
        out_shape=[jax.ShapeDtypeStruct((BATCH, SEQ, BRANCH_W), BF16)] * 3,
        out_specs=[pl.BlockSpec((CTX_BPS, SEQ, BRANCH_W), step)] * 3,
        scratch=[])


def _win_attn_rows(sink_ref, q_ref, kv_ref, g_ref, ck_ref, cv_ref, ck_s, cv_s, dst_ref, layer):
    nb = DEC_SEQ // Q_BLOCK
    m = Q_BLOCK
    tile = pl.program_id(1)

    @pl.when(tile == 0)
    def _():
        ck_s[...] = ck_ref[0, 0].astype(BF16)
        cv_s[...] = cv_ref[0, 0].astype(BF16)

    sink = _sink_column(sink_ref, layer, m)
    for qb in range(TM // m):
        n = tile * (TM // m) + qb
        rows = slice(qb * m, (qb + 1) * m)
        blocks = []
        for off in (-1, 0, 1):
            start = pl.multiple_of(jnp.clip(n + off, 0, nb - 1) * m, m)
            blocks.append(kv_ref[0, pl.ds(start, m), :])
        kvw = jnp.concatenate(blocks, axis=0)
        shape = (A_HEADS * m, 3 * m)
        qpos = n * m + (lax.broadcasted_iota(jnp.int32, shape, 0) & (m - 1))
        kpos = (n - 1) * m + lax.broadcasted_iota(jnp.int32, shape, 1)
        mask = (jnp.abs(qpos - kpos) <= A_WINDOW) & (kpos >= 0) & (kpos < DEC_SEQ)
        q4 = _stack_heads([q_ref[0, rows, 0:LANE], q_ref[0, rows, LANE:2 * LANE]])
        s_win = jnp.where(mask, _dot_nt(q4, kvw[:, 0:LANE]), NEG)
        s_ctx = _dot_nt(q4, ck_s[...])
        o4 = _softmax_pv([s_win, s_ctx], [kvw[:, LANE:2 * LANE], cv_s[...]], sink=sink)
        for pair in range(2):
            g = g_ref[0, rows, pair * LANE:(pair + 1) * LANE].astype(F32)
            dst_ref[rows, pair * LANE:(pair + 1) * LANE] = (_unstack_pair(o4, pair, m) * _silu(g)).astype(BF16)


def _nat_rows(q_ref, k_ref, v_ref, g_ref, ck_ref, cv_ref, ring_ref, ck_s, cv_s, bias_s, dst_ref):
    nrows = DEC_SEQ // GRID_W
    kr = NAT_ROWS
    m = NAT_RPS * GRID_W
    step = pl.program_id(1)

    @pl.when((pl.program_id(0) == 0) & (step == 0))
    def _():
        shape = (GRID_W, LANE)
        lane = lax.broadcasted_iota(jnp.int32, shape, 1)
        qcol = lax.broadcasted_iota(jnp.int32, shape, 0)
        rel = (lane & (GRID_W - 1)) - jnp.clip(qcol - NAT_COLS // 2, 0, GRID_W - NAT_COLS)
        col_ok = (rel >= 0) & (rel < NAT_COLS)
        for h in range(NAT_HEADS):
            for i in range(NAT_BIAS_ROWS):
                a = pltpu.roll(jnp.broadcast_to(ring_ref[h, i:i + 1, :], shape), 0, 1, stride=1, stride_axis=0)
                b = pltpu.roll(jnp.broadcast_to(ring_ref[h, i + 1:i + 2, :], shape), GRID_W, 1, stride=1,
                               stride_axis=0)
                bias_s[h, i] = jnp.where(col_ok, jnp.where(lane < GRID_W, a, b), NEG)

    @pl.when(step == 0)
    def _():
        ck_s[...] = ck_ref[0, 0].astype(BF16)
        cv_s[...] = cv_ref[0, 0].astype(BF16)

    key_row = lax.broadcasted_iota(jnp.int32, (1, LANE), 1) // GRID_W
    for grp in range(TM // m):
        row0 = (step * (TM // m) + grp) * NAT_RPS
        rows = slice(grp * m, (grp + 1) * m)
        ws = jnp.clip(row0 - kr // 2, 0, nrows - NAT_UNION)
        start = pl.multiple_of(ws * GRID_W, GRID_W)
        kl = k_ref[0, pl.ds(start, NAT_UNION * GRID_W), :]
        vl = v_ref[0, pl.ds(start, NAT_UNION * GRID_W), :]
        for pair in range(NAT_HEADS // 2):
            c = pair * LANE
            blocks = []
            for hh in range(2):
                for rr in range(NAT_RPS):
                    r = row0 + rr
                    lo = jnp.clip(r - kr // 2, 0, nrows - kr) - ws
                    ro = ws - r + (NAT_ROWS - 1) + NAT_BIAS_PAD
                    row = []
                    for t in range(NAT_UNION // 2):
                        kj = key_row + 2 * t
                        row.append(jnp.where((kj >= lo) & (kj < lo + kr), bias_s[2 * pair + hh, ro + 2 * t], NEG))
                    blocks.append(jnp.concatenate(row, axis=1))
            bias = jnp.concatenate(blocks, axis=0)
            q2 = _stack_heads([q_ref[0, rows, c:c + LANE]])
            s_lat = _dot_nt(q2, kl[:, c:c + LANE]) + bias
            s_ctx = _dot_nt(q2, ck_s[:, c:c + LANE])
            o2 = _softmax_pv([s_lat, s_ctx], [vl[:, c:c + LANE], cv_s[:, c:c + LANE]])
            g = g_ref[0, rows, c:c + LANE].astype(F32)
            dst_ref[rows, c:c + LANE] = (_unstack_pair(o2, 0, m) * _silu(g)).astype(BF16)


def _nat_bias_rings(rpb):
    nc = NAT_COLS - 1
    rep = lambda a, n: jnp.broadcast_to(a, a.shape[:-1] + (n,))
    ring = jnp.concatenate([rpb[..., nc:], rep(rpb[..., -1:], GRID_W - 1 - nc), rep(rpb[..., :1], GRID_W - nc),
                            rpb[..., :nc]], axis=-1)
    neg = lambda n: jnp.full(ring.shape[:2] + (n, 2 * GRID_W), NEG, F32)
    return jnp.concatenate([neg(NAT_BIAS_PAD), ring, neg(NAT_BIAS_PAD + 1)], axis=2)


def _lat_attn_rider(z, sink, ck_a, cv_a, ck_n, cv_n, rings, layer):
    kvw = A_KV_HEADS * HEAD_DIM
    hw = NAT_HEADS * HEAD_DIM

    def body(sink_ref, qa_ref, kva_ref, ga_ref, cka_ref, cva_ref, qd_ref, kd_ref, vd_ref, gd_ref, ckd_ref, cvd_ref,
             ring_ref, cka_s, cva_s, ckd_s, cvd_s, bias_s, ba_s, bd_s):
        _win_attn_rows(sink_ref, qa_ref, kva_ref, ga_ref, cka_ref, cva_ref, cka_s, cva_s, ba_s, layer)
        _nat_rows(qd_ref, kd_ref, vd_ref, gd_ref, ckd_ref, cvd_ref, ring_ref, ckd_s, cvd_s, bias_s, bd_s)

    tile = lambda col: pl.BlockSpec((1, TM, 256), lambda b, t: (b, t, col))
    whole = lambda col: pl.BlockSpec((1, DEC_SEQ, 256), lambda b, t: (b, 0, col))
    cache = lambda w: pl.BlockSpec((1, 1, PAST_LEN, w), lambda b, t: (b, layer, 0, 0))
    return _Rider(
        body=body,
        in_specs=[pl.BlockSpec(memory_space=pltpu.SMEM), tile(Z_AQ // 256), whole(Z_AK // 256), tile(Z_AG // 256),
                  cache(kvw), cache(kvw), tile(Z_DQ // 256), whole(Z_DK // 256), whole(Z_DV // 256),
                  tile(Z_DG // 256), cache(hw), cache(hw), _layer_spec(rings.shape[1:], layer)],
        args=[sink, z, z, z, ck_a, cv_a, z, z, z, z, ck_n, cv_n, rings],
        out_shape=[], out_specs=[],
        scratch=[pltpu.VMEM((PAST_LEN, kvw), BF16), pltpu.VMEM((PAST_LEN, kvw), BF16),
                 pltpu.VMEM((PAST_LEN, hw), BF16), pltpu.VMEM((PAST_LEN, hw), BF16),
                 pltpu.VMEM((NAT_HEADS, NAT_BIAS_ROWS, GRID_W, LANE), F32),
                 pltpu.VMEM((TM, BRANCH_W), BF16), pltpu.VMEM((TM, BRANCH_W), BF16)],
        branches=True)


def _mla_body(q_ref, ckv_ref, kpe_ref, g_ref, cckv_ref, ckpe_ref, wkv_ref, o_ref, k_s, v_s):
    nlat = DEC_SEQ
    step = pl.program_id(0) * pl.num_programs(1) + pl.program_id(1)

    @pl.when(step % MLA_TILES == 0)
    def _():
        rows = 512

        def fill(r0, ckv, kpe_blk):
            kv = _dot(ckv, wkv_ref[...])
            for h, kh in enumerate(_mla_keys(kv, kpe_blk)):
                k_s[h, r0:r0 + rows, :] = kh
                v_s[h, r0:r0 + rows, :] = kv[:, h * LANE:(h + 1) * LANE].astype(BF16)

        for c in range(nlat // rows):
            fill(c * rows, ckv_ref[0, c * rows:(c + 1) * rows, :], kpe_ref[0, c * rows:(c + 1) * rows, :].astype(F32))
        ckpe = ckpe_ref[0, 0]
        ckpe_blk = jnp.concatenate([jnp.zeros((PAST_LEN, KPE_LANE), F32), ckpe,
                                    jnp.zeros((PAST_LEN, LANE - KPE_LANE - MLA_ROPE), F32)], axis=1)
        fill(nlat, cckv_ref[0, 0].astype(BF16), ckpe_blk)

    def store(pair, val):
        o_ref[0, :, pair * LANE:(pair + 1) * LANE] = val

    _mla_heads(lambda h: q_ref[0, :, LANE * h:LANE * (h + 1)], lambda h: k_s[h], lambda h: v_s[h],
               lambda p: _silu(g_ref[0, :, p * LANE:(p + 1) * LANE].astype(F32)), store)


def _mla_rider(z, cache_ckv, cache_kpe, wkv, layer, nt):
    nkeys = DEC_SEQ + PAST_LEN
    assert DEC_BATCH * MLA_TILES == POST_STEPS
    bt = lambda b, t: ((b * nt + t) // MLA_TILES, (b * nt + t) % MLA_TILES)
    tile = lambda col: (lambda b, t: bt(b, t) + (col,))
    whole = lambda col: (lambda b, t: (bt(b, t)[0], 0, col))
    cache = lambda b, t: (bt(b, t)[0], layer, 0, 0)
    return _Rider(
        body=_mla_body,
        in_specs=[
            pl.BlockSpec((1, TQ_MLA, MLA_HEADS * LANE), tile(Z_QM // (MLA_HEADS * LANE))),
            pl.BlockSpec((1, DEC_SEQ, LANE), whole(Z_CKV // LANE)),
            pl.BlockSpec((1, DEC_SEQ, LANE), whole(Z_KPE // LANE)),
            pl.BlockSpec((1, TQ_MLA, 256), tile(Z_BG // 256)),
            pl.BlockSpec((1, 1, PAST_LEN, MLA_KV_LORA), cache),
            pl.BlockSpec((1, 1, PAST_LEN, MLA_ROPE), cache),
            _layer_spec((MLA_KV_LORA, 512), layer),
        ],
        args=[z, z, z, z, cache_ckv, cache_kpe, wkv],
        out_shape=[jax.ShapeDtypeStruct((DEC_BATCH, DEC_SEQ, BRANCH_W), BF16)],
        out_specs=[pl.BlockSpec((1, TQ_MLA, BRANCH_W), tile(0))],
        scratch=[pltpu.VMEM((MLA_HEADS, nkeys, LANE), BF16), pltpu.VMEM((MLA_HEADS, nkeys, LANE), BF16)])


def _s5_mats(lam_re, lam_im, log_dt, b_re, b_im, c_re, c_im):
    t = S5_T
    hp = lax.Precision.HIGHEST
    lre = jnp.minimum(lam_re, -1e-4)
    lim = lam_im
    dt = jnp.exp(log_dt)[..., None]
    er, ei = lre * dt, lim * dt
    mag = jnp.exp(er)
    are, aim = mag * jnp.cos(ei), mag * jnp.sin(ei)
    den = lre * lre + lim * lim
    qre = ((are - 1.0) * lre + aim * lim) / den
    qim = (aim * lre - (are - 1.0) * lim) / den
    bbr = qre[..., None] * b_re - qim[..., None] * b_im
    bbi = qre[..., None] * b_im + qim[..., None] * b_re
    n = jnp.arange(t + 1, dtype=F32)[:, None, None, None]
    pmag = jnp.exp(er[None] * n)
    pr, pi = pmag * jnp.cos(ei[None] * n), pmag * jnp.sin(ei[None] * n)

    def c_times_pow(d, pw_r, pw_i):
        rep = lambda a: jnp.repeat(jnp.transpose(a, (1, 2, 0)), S5_CH, axis=-1)
        til = lambda a: jnp.tile(jnp.transpose(a, (0, 2, 1)), (1, 1, t))
        cr, ci, ar, ai = til(c_re[d]), til(c_im[d]), rep(pw_r), rep(pw_i)
        return cr * ar - ci * ai, cr * ai + ci * ar

    def lag_kernels(d, wr, wi):
        return (jnp.einsum('gpk,gpx->gkx', bbr[d], wr, precision=hp)
                - jnp.einsum('gpk,gpx->gkx', bbi[d], wi, precision=hp))

    wf0 = c_times_pow(0, pr[:t, 0], pi[:t, 0])
    wf1 = c_times_pow(0, pr[1:, 0], pi[1:, 0])
    wb = c_times_pow(1, pr[1:, 1][::-1], pi[1:, 1][::-1])
    kb0 = (jnp.einsum('gpk,gcp->gkc', bbr[1], c_re[1], precision=hp)
           - jnp.einsum('gpk,gcp->gkc', bbi[1], c_im[1], precision=hp))
    kf = lag_kernels(0, *wf0) + jnp.pad(kb0, ((0, 0), (0, 0), (0, S5_TC - S5_CH)))
    kb = lag_kernels(1, *wb) * jnp.asarray(np.arange(S5_TC) >= S5_CH, F32)
    ring = jnp.concatenate([kf, kb], axis=-1)

    def state_cols(pw_r, pw_i, d):
        rep = lambda a: jnp.repeat(jnp.transpose(a, (1, 0, 2)), S5_CH, axis=1)
        til = lambda a: jnp.tile(jnp.transpose(a, (0, 2, 1)), (1, t, 1))
        ar, ai, br, bi = rep(pw_r), rep(pw_i), til(bbr[d]), til(bbi[d])
        return ar * br - ai * bi, ar * bi + ai * br

    sfr, sfi = state_cols(pr[:t, 0][::-1], pi[:t, 0][::-1], 0)
    sbr, sbi = state_cols(pr[:t, 1], pi[:t, 1], 1)
    m_state = jnp.concatenate([sfr, sbr, sfi, sbi], axis=-1)
    m_out = jnp.concatenate([wf1[0], wb[0], -wf1[1], -wb[1]], axis=1)

    a_t = jnp.stack([jnp.concatenate([pr[t, 0], pr[t, 1]], axis=-1),
                     jnp.concatenate([pi[t, 0], pi[t, 1]], axis=-1)], axis=1)
    return ring, m_state.astype(BF16), m_out.astype(BF16), a_t


def _s5_body(u_ref, ring_ref, ms_ref, mo_ref, at_ref, h0_ref, y_ref, hend_ref, s_s, hf_s, hb_s, m_s, *, nk, nb):
    p = S5_STATE
    u = u_ref[0]
    s = _dot(u, ms_ref[0])
    s_s[0] = s[:, 0:2 * p]
    s_s[1] = s[:, 2 * p:4 * p]
    are = at_ref[0, 0:1, :]
    aim = at_ref[0, 1:2, :]
    fwd_lane = lax.broadcasted_iota(jnp.int32, (1, 2 * p), 1) < p

    for k in range(S5_CH):
        rk = jnp.broadcast_to(ring_ref[0, k:k + 1, :], (S5_T, 2 * S5_TC))
        rolled = pltpu.roll(rk, 0, 1, stride=S5_CH, stride_axis=0)
        for blk in range(S5_TC // LANE):
            m_s[blk, pl.ds(k, S5_T, stride=S5_CH), :] = rolled[:, blk * LANE:(blk + 1) * LANE]

    if h0_ref is None:
        hre = him = jnp.zeros((nb, 2 * p), F32)
    else:
        hre, him = h0_ref[0, :, 0:2 * p], h0_ref[0, :, 2 * p:4 * p]
    for st in range(nk):
        rf = pl.ds(st, nb, stride=nk)
        rb = pl.ds(nk - 1 - st, nb, stride=nk)
        hf_s[0, rf, :] = hre
        hf_s[1, rf, :] = him
        hb_s[0, rb, :] = hre
        hb_s[1, rb, :] = him
        sre = jnp.where(fwd_lane, s_s[0, rf, :], s_s[0, rb, :])
        sim = jnp.where(fwd_lane, s_s[1, rf, :], s_s[1, rb, :])
        hre, him = are * hre - aim * him + sre, are * him + aim * hre + sim
    if hend_ref is not None:
        hend_ref[0] = jnp.concatenate([hre, him], axis=1)
    hst = jnp.concatenate([jnp.where(fwd_lane, hf_s[0], hb_s[0]), jnp.where(fwd_lane, hf_s[1], hb_s[1])], axis=1)
    m_intra = jnp.concatenate([m_s[blk] for blk in range(S5_TC // LANE)], axis=1).astype(BF16)
    y_ref[0] = _dot(u, m_intra) + _dot(hst.astype(BF16), mo_ref[0])


def _s5_rider(ut, ring, m_state, m_out, a_t, h0, layer, nt):
    assert S5_GROUPS == POST_STEPS
    ctx = h0 is None
    g3 = lambda b, t: (b * nt + t, 0, 0)
    lg = lambda shape: pl.BlockSpec((None, 1) + shape, lambda b, t: (layer, b * nt + t, 0, 0))
    uspec = pl.BlockSpec((1, S5_ROWS, S5_TC), g3)

    def body(*refs):
        if ctx:
            u, rg, ms, mo, at, y, hend, s_s, hf_s, hb_s, m_s = refs
            _s5_body(u, rg, ms, mo, at, None, y, hend, s_s, hf_s, hb_s, m_s, nk=S5_NK_CTX, nb=BATCH)
        else:
            u, rg, ms, mo, at, h0_ref, y, s_s, hf_s, hb_s, m_s = refs
            _s5_body(u, rg, ms, mo, at, h0_ref, y, None, s_s, hf_s, hb_s, m_s, nk=S5_NK_LAT, nb=DEC_BATCH)

    return _Rider(
        body=body,
        in_specs=[uspec, lg((S5_CH, 2 * S5_TC)), lg((S5_TC, 4 * S5_STATE)), lg((4 * S5_STATE, S5_TC)),
                  lg((2, 2 * S5_STATE))] + ([] if ctx else [lg((DEC_BATCH, 4 * S5_STATE))]),
        args=[ut, ring, m_state, m_out, a_t] + ([] if ctx else [h0]),
        out_shape=[jax.ShapeDtypeStruct((S5_GROUPS, S5_ROWS, S5_TC), F32)]
        + ([jax.ShapeDtypeStruct((S5_GROUPS, BATCH, 4 * S5_STATE), F32)] if ctx else []),
        out_specs=[uspec] + ([pl.BlockSpec((1, BATCH, 4 * S5_STATE), g3)] if ctx else []),
        scratch=[pltpu.VMEM((2, S5_ROWS, LANE), F32)] * 3 + [pltpu.VMEM((S5_TC // LANE, S5_TC, LANE), F32)])


_POST_IN = ('x', 'mod', 'ng', 'ba', 'bb', 'bd', 'y', 'u', 'cg', 'sd', 'wglu', 'wm', 'wbr', 'wo', 'fg')


class _Rider(NamedTuple):
    body: Callable
    in_specs: list
    args: list
    out_shape: list
    out_specs: list
    scratch: list
    branches: bool = False


NO_RIDER = _Rider(body=lambda: None, in_specs=[], args=[], out_shape=[], out_specs=[], scratch=[])


def _both_riders(r1, r2):
    i1, i2, o1, o2, s1 = len(r1.in_specs), len(r2.in_specs), len(r1.out_specs), len(r2.out_specs), len(r1.scratch)

    def body(*refs):
        ins, outs, scr = refs[:i1 + i2], refs[i1 + i2:i1 + i2 + o1 + o2], refs[i1 + i2 + o1 + o2:]
        r1.body(*ins[:i1], *outs[:o1], *scr[:s1])
        r2.body(*ins[i1:], *outs[o1:], *scr[s1:])

    return _Rider(body=body, in_specs=r1.in_specs + r2.in_specs, args=r1.args + r2.args,
                  out_shape=r1.out_shape + r2.out_shape, out_specs=r1.out_specs + r2.out_specs,
                  scratch=r1.scratch + r2.scratch)


def _post_kernel(*refs, final, rider):
    names = [n for n in _POST_IN if not (rider.branches and n in ('ba', 'bd'))]
    r = dict(zip(names, refs))
    n0, n_in, n_out = len(names), len(rider.in_specs), len(rider.out_specs)
    o_ref = refs[n0 + n_in]
    ys_s = refs[n0 + n_in + 1 + n_out]
    rider_scratch = refs[n0 + n_in + 2 + n_out:]
    rider.body(*refs[n0:n0 + n_in], *refs[n0 + n_in + 1:n0 + n_in + 1 + n_out], *rider_scratch)
    ba, bd = (rider_scratch[-2][...], rider_scratch[-1][...]) if rider.branches else (r['ba'][0], r['bd'][0])
    x_ref, mod_ref, ng_ref, y_ref = r['x'], r['mod'], r['ng'], r['y']
    x = x_ref[0]
    hb = _ada_h(x, mod_ref, ng_ref).astype(BF16)
    for m in range(S5_T // 8):
        for hf in range(2):
            vs = _slot_transpose([y_ref[8 * hf + gp, :, m * LANE:(m + 1) * LANE] for gp in range(8)])
            for s in range(8):
                ys_s[hf, pl.ds(8 * m + s, TM_CHUNKS, stride=S5_T), :] = vs[s]
    yv = jnp.concatenate([ys_s[0], ys_s[1]], axis=1) + r['sd'][...] * r['u'][0].astype(F32)
    gel = 0.5 * yv * (1.0 + jnp.tanh(math.sqrt(2.0 / math.pi) * (yv + 0.044715 * (yv * yv * yv))))
    gl = _dot(gel.astype(BF16), r['wglu'][...])
    oc = gl[:, 0:BRANCH_W] * _sigmoid(gl[:, BRANCH_W:2 * BRANCH_W])
    bc = (oc * _silu(r['cg'][0].astype(F32))).astype(BF16)
    acc = None
    for k, br in enumerate((ba, r['bb'][0], bc, bd)):
        proj = _dot(br, r['wbr'][k])
        mg = _sigmoid(_dot(hb, r['wm'][:, k * D_MODEL:(k + 1) * D_MODEL]))
        acc = mg * proj if acc is None else acc + mg * proj
    y = _dot(acc.astype(BF16), r['wo'][...])
    out = x + mod_ref[0, :, 2 * D_MODEL:3 * D_MODEL] * y
    if final:
        out = _rms(out, r['fg'][...])
    o_ref[0] = out


def _post_call(x, mods, mod_row0, p, ba, bb, bd, y_s5, z, fg, final, layer, name, rider):
    nb, length, _ = x.shape
    nt = length // TM
    assert nb * nt == POST_STEPS
    tok = lambda b, t: (b, t, 0)
    br_spec = pl.BlockSpec((1, TM, BRANCH_W), tok)
    branch_args = [bb] if rider.branches else [ba, bb, bd]
    return pl.pallas_call(
        functools.partial(_post_kernel, final=final, rider=rider),
        out_shape=tuple([jax.ShapeDtypeStruct((nb, length, D_MODEL), F32)] + rider.out_shape),
        grid=(nb, nt),
        in_specs=[
            pl.BlockSpec((1, TM, D_MODEL), tok),
            _mod_spec(layer, mod_row0),
            _layer_spec((1, D_MODEL), layer),
        ] + [br_spec] * len(branch_args) + [
            pl.BlockSpec((S5_GROUPS, TM_CHUNKS, S5_TC), lambda b, t: (0, b * nt + t, 0)),
            pl.BlockSpec((1, TM, 256), lambda b, t: (b, t, Z_CU // 256)),
            pl.BlockSpec((1, TM, 256), lambda b, t: (b, t, Z_CG // 256)),
            _layer_spec((1, BRANCH_W), layer),
            _layer_spec((BRANCH_W, 2 * BRANCH_W), layer),
            _layer_spec((D_MODEL, N_BRANCH * D_MODEL), layer),
            _layer_spec((N_BRANCH, BRANCH_W, D_MODEL), layer),
            _layer_spec((D_MODEL, D_MODEL), layer),
            pl.BlockSpec((1, D_MODEL), lambda b, t: (0, 0)),
        ] + rider.in_specs,
        out_specs=tuple([pl.BlockSpec((1, TM, D_MODEL), tok)] + rider.out_specs),
        scratch_shapes=[pltpu.VMEM((2, TM, LANE), F32)] + rider.scratch,
        compiler_params=_cparams(("arbitrary", "arbitrary"), 56),
        name=name,
    )(x, mods, p['ng'], *branch_args, y_s5, z, z, p['s5d'], p['wglu'], p['wm'], p['wbr'], p['wo'], fg, *rider.args)


def _rope_tables():
    t = np.arange(DEC_SEQ)
    row = (t // GRID_W).astype(np.float64)
    col = (t % GRID_W).astype(np.float64)

    def pattern(half):
        inv = ROPE_BASE ** (-np.arange(half, dtype=np.float64) / half)
        zeros = np.zeros((DEC_SEQ, half))
        cs, s_up, s_lo = [], [], []
        for pos in (row, col):
            ang = pos[:, None] * inv[None, :]
            c, s = np.cos(ang), np.sin(ang)
            cs += [c, c]
            s_up += [zeros, s]
            s_lo += [-s, zeros]
        return [np.concatenate(parts, axis=1) for parts in (cs, s_up, s_lo)]

    tab_a = np.stack([np.tile(part, (1, LANE // HEAD_DIM)) for part in pattern(HEAD_DIM // 4)])
    ident = [np.ones, np.zeros, np.zeros]
    tab_m = np.stack([np.concatenate([fill((DEC_SEQ, KPE_LANE)), part,
                                      fill((DEC_SEQ, LANE - KPE_LANE - MLA_ROPE))], axis=1)
                      for fill, part in zip(ident, pattern(MLA_ROPE // 4))])
    return jnp.asarray(tab_a, F32), jnp.asarray(tab_m, F32)


_W_IN_MOVES = ((256, 256, 256), (512, 1952, 768), (R_CQ, 768, 384), (R_CU, 1440, 256),
               (R_G + 256, 1184, 256), (R_G + 512, 1696, 256), (R_G + 768, 2720, 256))
_W_IN_HEAD_MOVES = ((0, 0), (R_G, 512))
_W_IN_KPE = 1152
W_IN_ROWS = 256


def _w_in_kernel(wt_ref, o_ref):
    take = lambda src, width: wt_ref[0, src:src + width, :].T
    for dst, src, width in _W_IN_MOVES:
        o_ref[0, :, dst:dst + width] = take(src, width).astype(BF16)
    for dst, src in _W_IN_HEAD_MOVES:
        blk = take(src, A_HEADS * HEAD_DIM)
        for i, h in enumerate(A_HEAD_ORDER):
            o_ref[0, :, dst + HEAD_DIM * i:dst + HEAD_DIM * (i + 1)] = blk[:, HEAD_DIM * h:HEAD_DIM * (h + 1)].astype(BF16)
    o_ref[0, :, R_KPE:R_CU] = jnp.zeros((W_IN_ROWS, LANE), BF16)
    o_ref[0, :, R_KPE + KPE_LANE:R_KPE + KPE_LANE + MLA_ROPE] = take(_W_IN_KPE, LANE)[:, 0:MLA_ROPE].astype(BF16)


def _w_in_call(w_in):
    wt = jnp.swapaxes(w_in, 1, 2)
    return pl.pallas_call(
        _w_in_kernel,
        out_shape=jax.ShapeDtypeStruct((DEPTH, D_MODEL, R_W), BF16),
        grid=(DEPTH, D_MODEL // W_IN_ROWS),
        in_specs=[pl.BlockSpec((1, wt.shape[1], W_IN_ROWS), lambda i, r: (i, 0, r))],
        out_specs=pl.BlockSpec((1, W_IN_ROWS, R_W), lambda i, r: (i, r, 0)),
        compiler_params=_cparams(("arbitrary", "arbitrary"), 32),
        name="w_in_layout",
    )(wt)


def _cast_rider(ws):
    flat = [w.reshape(-1, w.shape[-1]) for w in ws]

    def body(*refs):
        for w_ref, o_ref in zip(refs[:len(ws)], refs[len(ws):]):
            o_ref[...] = w_ref[...].astype(BF16)

    specs = [pl.BlockSpec((w.shape[0] // POST_STEPS, w.shape[1]), lambda b, t: (t, 0)) for w in flat]
    return _Rider(body=body, in_specs=specs, args=flat,
                  out_shape=[jax.ShapeDtypeStruct(w.shape, BF16) for w in flat], out_specs=specs, scratch=[])


def _reorder_w_q_up(w):
    w = w.reshape(MLA_Q_LORA, MLA_HEADS, MLA_NOPE + MLA_ROPE)
    w = jnp.concatenate([w, jnp.zeros((MLA_Q_LORA, MLA_HEADS, LANE - MLA_NOPE - MLA_ROPE), w.dtype)], axis=-1)
    return w.reshape(MLA_Q_LORA, MLA_HEADS * LANE)


def _reorder_w_branch(w):
    wa = w[0].reshape(A_HEADS, HEAD_DIM, D_MODEL)
    wa = jnp.concatenate([wa[h] for h in A_HEAD_ORDER], axis=0)
    return jnp.concatenate([wa[None], w[1:]], axis=0)


def kernel(x_prompt, x_sample, cache_a_k, cache_a_v, cache_mla_ckv, cache_mla_kpe, cache_na_k, cache_na_v,
           state_s5_re, state_s5_im, c, c_ctx, w_mod, b_mod, norm_g, w_in, w_merge, a_sink,
           mla_q_norm, mla_w_q_up, mla_kv_norm, mla_w_kv_up, s5_lam_re, s5_lam_im, s5_log_dt,
           s5_b_re, s5_b_im, s5_c_re, s5_c_im, s5_d, s5_w_glu, na_rpb, w_branch, w_out, final_norm_g):
    n_ctx = BATCH * SEQ
    conds = jnp.concatenate([c, c_ctx[None, :], jnp.zeros((3, D_MODEL), F32)], axis=0)
    mods = _mod_call(conds, w_mod, b_mod).reshape(DEPTH, 8, 1, 3 * D_MODEL)
    tabs = _rope_tables()
    ck_a = cache_a_k.reshape(DEC_BATCH, DEPTH, PAST_LEN, A_KV_HEADS * HEAD_DIM)
    cv_a = cache_a_v.reshape(DEC_BATCH, DEPTH, PAST_LEN, A_KV_HEADS * HEAD_DIM)
    ck_n = cache_na_k.reshape(DEC_BATCH, DEPTH, PAST_LEN, NAT_HEADS * HEAD_DIM)
    cv_n = cache_na_v.reshape(DEC_BATCH, DEPTH, PAST_LEN, NAT_HEADS * HEAD_DIM)
    fg = final_norm_g.reshape(1, D_MODEL)
    caches = None

    p = dict(
        ng=norm_g.reshape(DEPTH, 1, D_MODEL),
        w_raw=_w_in_call(w_in),
        qnorm=mla_q_norm.reshape(DEPTH, 1, MLA_Q_LORA),
        wq=jax.vmap(_reorder_w_q_up)(mla_w_q_up).astype(BF16),
        kvnorm=mla_kv_norm.reshape(DEPTH, 1, MLA_KV_LORA),
        wkv=mla_w_kv_up.astype(BF16),
        wbr=jax.vmap(_reorder_w_branch)(w_branch).astype(BF16),
        wglu=s5_w_glu.astype(BF16),
        s5d=s5_d.reshape(DEPTH, 1, BRANCH_W),
    )
    ring, m_state, m_out, a_t = jax.vmap(_s5_mats)(s5_lam_re, s5_lam_im, s5_log_dt, s5_b_re, s5_b_im, s5_c_re, s5_c_im)
    nat_bias = _nat_bias_rings(na_rpb * LOG2E)
    h0 = jnp.concatenate([state_s5_re[:, :, 0], state_s5_re[:, :, 1], state_s5_im[:, :, 0], state_s5_im[:, :, 1]],
                         axis=-1)
    h0 = jnp.transpose(h0, (1, 2, 0, 3))

    yp = x_prompt.reshape(1, n_ctx, D_MODEL)
    ys = x_sample
    hends = []
    for i in range(DEPTH):
        rider = _cast_rider([w_merge, w_out]) if i == 0 else NO_RIDER
        res = _inproj_call(yp, mods, DEC_BATCH, p, None, caches, i, "inproj_ctx", rider)
        z_ctx, ut_ctx, caches = res[0], res[1], list(res[2:2 + N_CACHE])
        if i == 0:
            p['wm'], p['wo'] = res[2 + N_CACHE].reshape(w_merge.shape), res[3 + N_CACHE].reshape(w_out.shape)
        zc = z_ctx.reshape(BATCH, SEQ, Z_W)
        nt_lat = DEC_SEQ // TM
        rider = _both_riders(_ctx_attn_rider(zc, a_sink, p['wkv'], i, nt_lat),
                             _s5_rider(ut_ctx, ring, m_state, m_out, a_t, None, i, nt_lat))
        z_lat, ut_lat, oa_c, ob_c, od_c, y_ctx, hend = _inproj_call(ys, mods, 0, p, tabs, None, i,
                                                                    "inproj_lat_ctx_attn_s5", rider)
        hends.append(hend)

        flat = lambda a: a.reshape(1, n_ctx, BRANCH_W)
        final = i == DEPTH - 1
        rider = _both_riders(_mla_rider(z_lat, cache_mla_ckv, cache_mla_kpe, p['wkv'], i, POST_STEPS),
                             _s5_rider(ut_lat, ring, m_state, m_out, a_t, h0, i, POST_STEPS))
        yp, ob_l, y_lat = _post_call(yp, mods, DEC_BATCH, p, flat(oa_c), flat(ob_c), flat(od_c), y_ctx, z_ctx, fg, final,
                                     i, "post_ctx_lat_mla_s5", rider)
        ys, = _post_call(ys, mods, 0, p, None, ob_l, None, y_lat, z_lat, fg, final, i, "post_lat_win_nat",
                         _lat_attn_rider(z_lat, a_sink, ck_a, cv_a, ck_n, cv_n, nat_bias, i))

    hend = jnp.stack(hends).reshape(DEPTH, S5_GROUPS, BATCH, 2, 2, S5_STATE)
    hend = jnp.transpose(hend, (3, 2, 0, 4, 1, 5))
    ak, av, ckv, kpe, nk, nv = caches
    heads = lambda a, h: a.reshape(BATCH, DEPTH, SEQ, h, HEAD_DIM)
    return (yp.reshape(BATCH, SEQ, D_MODEL), ys, heads(ak, A_KV_HEADS), heads(av, A_KV_HEADS), ckv, jnp.swapaxes(kpe, 2, 3),
            heads(nk, NAT_HEADS), heads(nv, NAT_HEADS), hend[0], hend[1])
```

```python
import functools
import math
from typing import Callable, NamedTuple

import numpy as np
import jax
import jax.numpy as jnp
from jax import lax
from jax.experimental import pallas as pl
from jax.experimental.pallas import tpu as pltpu

F32 = jnp.float32
BF16 = jnp.bfloat16

D_MODEL = 1024
BATCH = 32
SEQ = 256
DEPTH = 2
DEC_BATCH = 4
DEC_SEQ = 2048
PAST_LEN = 512
GRID_W = 64
HEAD_DIM = 64
BRANCH_W = 256
N_BRANCH = 4
Q_BLOCK = 128
A_HEADS = 4
A_KV_HEADS = 2
A_GROUP = A_HEADS // A_KV_HEADS
A_WINDOW = 128
MLA_HEADS = 4
MLA_Q_LORA = 256
MLA_KV_LORA = 128
MLA_NOPE = 64
MLA_ROPE = 32
MLA_V = 64
S5_CH = 16
S5_GROUPS = BRANCH_W // S5_CH
S5_STATE = 64
NAT_HEADS = 4
NAT_ROWS = 8
NAT_COLS = 16
ROPE_BASE = 10000.0
EPS = 1e-6
NEG = -1e30
LOG2E = 1.4426950408889634
Q_SCALE = HEAD_DIM ** -0.5 * LOG2E
MLA_Q_SCALE = (MLA_NOPE + MLA_ROPE) ** -0.5 * LOG2E

LANE = 128

R_AV, R_CQ, R_CKV, R_KPE, R_CU, R_G, R_W = 384, 1280, 1536, 1664, 1792, 2048, 3072
KPE_LANE = 64
A_HEAD_ORDER = (0, 2, 1, 3)
Z_AQ, Z_AK, Z_AV = 0, 256, 384
Z_DQ, Z_DK, Z_DV = 512, 768, 1024
Z_CKV, Z_KPE = 1280, 1408
Z_QM = 1536
Z_AG, Z_BG, Z_CG, Z_DG = 2048, 2304, 2560, 2816
Z_CU = 3072
Z_W = 3328

S5_T = 32
S5_TC = S5_T * S5_CH
S5_ROWS = 256
S5_NK_CTX = SEQ // S5_T
S5_NK_LAT = DEC_SEQ // S5_T

TM = 512
TM_CHUNKS = TM // S5_T
TQ_MLA = 512
MLA_TILES = DEC_SEQ // TQ_MLA
POST_STEPS = BATCH * SEQ // TM
CTX_BPS = 2
NAT_RPS = 4
NAT_UNION = NAT_ROWS + NAT_RPS
NAT_BIAS_PAD = NAT_RPS
NAT_BIAS_ROWS = 2 * NAT_ROWS - 1 + 2 * NAT_BIAS_PAD


def _cparams(sem, vmem_mb):
    return pltpu.CompilerParams(dimension_semantics=sem, vmem_limit_bytes=vmem_mb * 1024 * 1024)


def _sigmoid(x):
    return 1.0 / (1.0 + jnp.exp(-x))


def _silu(x):
    return x * _sigmoid(x)


def _rms(x, g):
    return x * lax.rsqrt(jnp.mean(x * x, axis=-1, keepdims=True) + EPS) * g


def _dot(a, b):
    return jnp.dot(a, b, preferred_element_type=F32)


def _dot_nt(a, b):
    return lax.dot_general(a, b, (((1,), (1,)), ((), ())), preferred_element_type=F32)


def _slot_transpose(vs):
    lane = lax.broadcasted_iota(jnp.int32, vs[0].shape, 1)
    vs = list(vs)
    for d in (4, 2, 1):
        keep = (lane & (S5_CH * d)) == 0
        nxt = list(vs)
        for lo in range(8):
            if lo & d:
                continue
            hi = lo + d
            nxt[lo] = jnp.where(keep, vs[lo], pltpu.roll(vs[hi], S5_CH * d, 1))
            nxt[hi] = jnp.where(keep, pltpu.roll(vs[lo], LANE - S5_CH * d, 1), vs[hi])
        vs = nxt
    return vs


def _mod_kernel(c_ref, w_ref, b_ref, o_ref):
    s = _silu(c_ref[...])
    o_ref[0] = _dot(s.astype(BF16), w_ref[0].astype(BF16)) + b_ref[0]


def _mod_call(conds, w_mod, b_mod):
    nc = 512
    return pl.pallas_call(
        _mod_kernel,
        out_shape=jax.ShapeDtypeStruct((DEPTH, 8, 3 * D_MODEL), F32),
        grid=(DEPTH, 3 * D_MODEL // nc),
        in_specs=[
            pl.BlockSpec((8, D_MODEL), lambda i, j: (0, 0)),
            pl.BlockSpec((1, D_MODEL, nc), lambda i, j: (i, 0, j)),
            pl.BlockSpec((1, 1, nc), lambda i, j: (i, 0, j)),
        ],
        out_specs=pl.BlockSpec((1, 8, nc), lambda i, j: (i, 0, j)),
        compiler_params=_cparams(("arbitrary", "arbitrary"), 32),
        name="mod_rows",
    )(conds, w_mod, b_mod.reshape(DEPTH, 1, 3 * D_MODEL))


def _ada_h(x, mod_ref, ng_ref):
    shift = mod_ref[0, :, 0:D_MODEL]
    scale = mod_ref[0, :, D_MODEL:2 * D_MODEL]
    return _rms(x, ng_ref[...]) * (1.0 + scale) + shift


def _rope_block(xs, tab_ref, shift):
    return (xs * tab_ref[0] + pltpu.roll(xs, shift, 1) * tab_ref[1]
            + pltpu.roll(xs, LANE - shift, 1) * tab_ref[2])


CACHE_DIMS = ((SEQ, 128), (SEQ, 128), (SEQ, MLA_KV_LORA), (MLA_ROPE, SEQ), (SEQ, 256), (SEQ, 256))
N_CACHE = len(CACHE_DIMS)


def _inproj_kernel(*refs, rope, rider, first):
    x_ref, mod_ref, ng_ref, w_ref, qn_ref, wq_ref, kvn_ref = refs[:7]
    if rope:
        ta_ref, tm_ref = refs[7:9]
        n_in, n_out = len(rider.in_specs), len(rider.out_specs)
        z_ref, ut_ref = refs[9 + n_in:11 + n_in]
        u_s = refs[11 + n_in + n_out]
        rider.body(*refs[9:9 + n_in], *refs[11 + n_in:11 + n_in + n_out], *refs[12 + n_in + n_out:])
    else:
        n_c = 0 if first else N_CACHE
        n_in, n_out = len(rider.in_specs), len(rider.out_specs)
        base = 7 + n_c + n_in
        z_ref, ut_ref, ak_ref, av_ref, ckv_ref, kpe_ref, nk_ref, nv_ref = refs[base:base + 2 + N_CACHE]
        u_s = refs[base + 2 + N_CACHE + n_out]
        rider.body(*refs[7 + n_c:base], *refs[base + 2 + N_CACHE:base + 2 + N_CACHE + n_out],
                   *refs[base + 3 + N_CACHE + n_out:])
    h = _ada_h(x_ref[0], mod_ref, ng_ref)
    raw = _dot(h.astype(BF16), w_ref[...])
    for j in range(R_AV // LANE):
        blk = raw[:, j * LANE:(j + 1) * LANE]
        if j < Z_AK // LANE:
            blk = blk * Q_SCALE
        if rope:
            blk = _rope_block(blk, ta_ref, 16)
        z_ref[0, :, j * LANE:(j + 1) * LANE] = blk.astype(BF16)
    z_ref[0, :, Z_AV:Z_DQ] = raw[:, Z_AV:Z_DQ].astype(BF16)
    z_ref[0, :, Z_DQ:Z_DK] = (raw[:, Z_DQ:Z_DK] * Q_SCALE).astype(BF16)
    z_ref[0, :, Z_DK:R_CQ] = raw[:, Z_DK:R_CQ].astype(BF16)
    qn = _rms(raw[:, R_CQ:R_CKV], qn_ref[...])
    q = _dot(qn.astype(BF16), wq_ref[...]) * MLA_Q_SCALE
    kp = raw[:, R_KPE:R_CU]
    ckv = _rms(raw[:, R_CKV:R_KPE], kvn_ref[...])
    for hh in range(MLA_HEADS):
        qh = q[:, hh * LANE:(hh + 1) * LANE]
        if rope:
            qh = _rope_block(qh, tm_ref, 8)
        z_ref[0, :, Z_QM + hh * LANE:Z_QM + (hh + 1) * LANE] = qh.astype(BF16)
    z_ref[0, :, Z_CKV:Z_KPE] = ckv.astype(BF16)
    z_ref[0, :, Z_KPE:Z_QM] = (_rope_block(kp, tm_ref, 8) if rope else kp).astype(BF16)
    z_ref[0, :, Z_AG:Z_CU] = raw[:, R_G:R_W].astype(BF16)
    z_ref[0, :, Z_CU:Z_W] = raw[:, R_CU:R_G].astype(BF16)
    if not rope:
        half = TM // 2
        for e in range(2):
            rows = slice(e * half, (e + 1) * half)
            ak_ref[e, 0] = raw[rows, Z_AK:Z_AV]
            av_ref[e, 0] = raw[rows, Z_AV:Z_DQ]
            nk_ref[e, 0] = raw[rows, Z_DK:Z_DV]
            nv_ref[e, 0] = raw[rows, Z_DV:R_CQ]
            ckv_ref[e, 0] = ckv[rows]
            kpe_ref[e, 0] = kp[rows].T[KPE_LANE:KPE_LANE + MLA_ROPE]
            if first:
                for cref in (ak_ref, av_ref, nk_ref, nv_ref, ckv_ref, kpe_ref):
                    cref[e, 1:DEPTH] = jnp.zeros((DEPTH - 1,) + cref.shape[2:], F32)
    for hf in range(2):
        u_s[hf] = raw[:, R_CU + hf * LANE:R_CU + (hf + 1) * LANE]
    for m in range(S5_T // 8):
        for hf in range(2):
            outs = _slot_transpose([u_s[hf, pl.ds(8 * m + s, TM_CHUNKS, stride=S5_T), :] for s in range(8)])
            for gp in range(8):
                ut_ref[8 * hf + gp, :, m * LANE:(m + 1) * LANE] = outs[gp].astype(BF16)


def _layer_spec(shape, layer):
    zeros = (0,) * len(shape)
    return pl.BlockSpec((None,) + tuple(shape), lambda *_: (layer,) + zeros)


def _mod_spec(layer, row0):
    return pl.BlockSpec((None, 1, 1, 3 * D_MODEL), lambda b, t: (layer, row0 + b, 0, 0))


def _inproj_call(x, mods, mod_row0, p, tabs, caches, layer, name, rider):
    nb, length, _ = x.shape
    rope = tabs is not None
    nt = length // TM
    in_specs = [
        pl.BlockSpec((1, TM, D_MODEL), lambda b, t: (b, t, 0)),
        _mod_spec(layer, mod_row0),
        _layer_spec((1, D_MODEL), layer),
        _layer_spec((D_MODEL, R_W), layer),
        _layer_spec((1, MLA_Q_LORA), layer),
        _layer_spec((MLA_Q_LORA, MLA_HEADS * LANE), layer),
        _layer_spec((1, MLA_KV_LORA), layer),
    ]
    args = [x, mods, p['ng'], p['w_raw'], p['qnorm'], p['wq'], p['kvnorm']]
    out_shape = [jax.ShapeDtypeStruct((nb, length, Z_W), BF16),
                 jax.ShapeDtypeStruct((S5_GROUPS, S5_ROWS, S5_TC), BF16)]
    out_specs = [pl.BlockSpec((1, TM, Z_W), lambda b, t: (b, t, 0)),
                 pl.BlockSpec((S5_GROUPS, TM_CHUNKS, S5_TC), lambda b, t: (0, b * nt + t, 0))]
    aliases = {}
    scratch = [pltpu.VMEM((2, TM, LANE), F32)]
    if rope:
        assert nb * nt == POST_STEPS
        in_specs += [pl.BlockSpec((3, TM, LANE), lambda b, t: (0, t, 0))] * 2 + rider.in_specs
        args += list(tabs) + rider.args
        out_shape += rider.out_shape
        out_specs += rider.out_specs
        scratch += rider.scratch
    else:
        for k, dims in enumerate(CACHE_DIMS):
            out_shape.append(jax.ShapeDtypeStruct((BATCH, DEPTH) + dims, F32))
            if caches is None:
                out_specs.append(pl.BlockSpec((2, DEPTH) + dims, lambda b, t: (t, 0, 0, 0)))
            else:
                in_specs.append(pl.BlockSpec(memory_space=pl.ANY))
                args.append(caches[k])
                out_specs.append(pl.BlockSpec((2, 1) + dims, lambda b, t: (t, layer, 0, 0)))
                aliases[7 + k] = 2 + k
        in_specs += rider.in_specs
        args += rider.args
        out_shape += rider.out_shape
        out_specs += rider.out_specs
        scratch += rider.scratch
    return pl.pallas_call(
        functools.partial(_inproj_kernel, rope=rope, rider=rider, first=not rope and caches is None),
        out_shape=tuple(out_shape),
        grid=(nb, nt),
        in_specs=in_specs,
        out_specs=tuple(out_specs),
        scratch_shapes=scratch,
        input_output_aliases=aliases,
        compiler_params=_cparams(("arbitrary", "arbitrary"), 56),
        name=name,
    )(*args)


def _softmax_pv(s_list, v_list, sink=None):
    m = jnp.max(s_list[0], axis=-1, keepdims=True)
    for s in s_list[1:]:
        m = jnp.maximum(m, jnp.max(s, axis=-1, keepdims=True))
    if sink is not None:
        m = jnp.maximum(m, sink)
    den = None
    o = None
    for s, v in zip(s_list, v_list):
        e = jnp.exp2(s - m)
        d = jnp.sum(e, axis=-1, keepdims=True)
        pv = _dot(e.astype(BF16), v)
        den = d if den is None else den + d
        o = pv if o is None else o + pv
    if sink is not None:
        den = den + jnp.exp2(sink - m)
    return o / den


def _low_lanes(shape):
    return lax.broadcasted_iota(jnp.int32, shape, 1) < HEAD_DIM


def _stack_heads(blocks):
    lo = _low_lanes(blocks[0].shape)
    zero = jnp.zeros_like(blocks[0])
    parts = []
    for b in blocks:
        parts += [jnp.where(lo, b, zero), jnp.where(lo, zero, b)]
    return jnp.concatenate(parts, axis=0)


def _unstack_pair(o, idx, m):
    return jnp.where(_low_lanes((m, LANE)), o[2 * idx * m:(2 * idx + 1) * m], o[(2 * idx + 1) * m:(2 * idx + 2) * m])


def _sink_column(sink_ref, layer, m):
    blk = lax.broadcasted_iota(jnp.int32, (A_HEADS * m, 1), 0) // m
    col = jnp.full((A_HEADS * m, 1), sink_ref[layer, A_HEAD_ORDER[-1]] * LOG2E, F32)
    for i in range(A_HEADS - 1):
        col = jnp.where(blk == i, sink_ref[layer, A_HEAD_ORDER[i]] * LOG2E, col)
    return col


def _mla_keys(kv, kpe_blk):
    lo = _low_lanes(kpe_blk.shape)
    return [jnp.where(lo, kv[:, h * LANE:(h + 1) * LANE], kpe_blk).astype(BF16) for h in range(MLA_HEADS)]


def _mla_heads(q_of, k_of, v_of, gate_of, store):
    for pair in range(MLA_HEADS // 2):
        outs = [_softmax_pv([_dot_nt(q_of(h), k_of(h))], [v_of(h)]) for h in (2 * pair, 2 * pair + 1)]
        o = jnp.where(_low_lanes(outs[0].shape), pltpu.roll(outs[0], MLA_V, 1), outs[1])
        store(pair, (o * gate_of(pair)).astype(BF16))


def _ctx_attn_body(sink_ref, z_ref, wkv_ref, oa_ref, ob_ref, od_ref, e, layer):
    m = SEQ
    gate = lambda c0: _silu(z_ref[e, :, c0:c0 + LANE].astype(F32))

    def store(ref, pair, val):
        ref[e, :, pair * LANE:(pair + 1) * LANE] = val

    q4 = _stack_heads([z_ref[e, :, Z_AQ:Z_AQ + LANE], z_ref[e, :, Z_AQ + LANE:Z_AQ + 2 * LANE]])
    o4 = _softmax_pv([_dot_nt(q4, z_ref[e, :, Z_AK:Z_AV])], [z_ref[e, :, Z_AV:Z_DQ]],
                     sink=_sink_column(sink_ref, layer, m))
    for pair in range(2):
        store(oa_ref, pair, (_unstack_pair(o4, pair, m) * gate(Z_AG + pair * LANE)).astype(BF16))
    kv = _dot(z_ref[e, :, Z_CKV:Z_KPE], wkv_ref[...])
    keys = _mla_keys(kv, z_ref[e, :, Z_KPE:Z_QM].astype(F32))
    kvb = kv.astype(BF16)
    _mla_heads(lambda h: z_ref[e, :, Z_QM + LANE * h:Z_QM + LANE * (h + 1)], lambda h: keys[h],
               lambda h: kvb[:, LANE * h:LANE * (h + 1)], lambda p: gate(Z_BG + p * LANE),
               functools.partial(store, ob_ref))
    for pair in range(NAT_HEADS // 2):
        c = pair * LANE
        q2 = _stack_heads([z_ref[e, :, Z_DQ + c:Z_DQ + c + LANE]])
        o2 = _softmax_pv([_dot_nt(q2, z_ref[e, :, Z_DK + c:Z_DK + c + LANE])], [z_ref[e, :, Z_DV + c:Z_DV + c + LANE]])
        store(od_ref, pair, (_unstack_pair(o2, 0, m) * gate(Z_DG + c)).astype(BF16))


def _ctx_attn_rider(z, sink, wkv, layer, nt):
    def body(sink_ref, z_ref, wkv_ref, oa_ref, ob_ref, od_ref):
        for e in range(CTX_BPS):
            _ctx_attn_body(sink_ref, z_ref, wkv_ref, oa_ref, ob_ref, od_ref, e, layer)

    step = lambda b, t: (b * nt + t, 0, 0)
    return _Rider(
        body=body,
        in_specs=[pl.BlockSpec(memory_space=pltpu.SMEM), pl.BlockSpec((CTX_BPS, SEQ, Z_W), step),
                  _layer_spec((MLA_KV_LORA, 512), layer)],
        args=[sink, z, wkv],
        out_shape=[jax.ShapeDtypeStruct((BATCH, SEQ, BRANCH_W), BF16)] * 3,
        out_specs=[pl.BlockSpec((CTX_BPS, SEQ, BRANCH_W), step)] * 3,
        scratch=[])


def _win_attn_rows(sink_ref, q_ref, kv_ref, g_ref, ck_ref, cv_ref, ck_s, cv_s, dst_ref, layer):
    nb = DEC_SEQ // Q_BLOCK
    m = Q_BLOCK
    tile = pl.program_id(1)

    @pl.when(tile == 0)
    def _():
        ck_s[...] = ck_ref[0, 0].astype(BF16)
        cv_s[...] = cv_ref[0, 0].astype(BF16)

    sink = _sink_column(sink_ref, layer, m)
    shape = (A_HEADS * m, 3 * m)
    band = jnp.abs((lax.broadcasted_iota(jnp.int32, shape, 0) & (m - 1))
                   - (lax.broadcasted_iota(jnp.int32, shape, 1) - m)) <= A_WINDOW
    key_off = lax.broadcasted_iota(jnp.int32, (1, 3 * m), 1) - m
    for qb in range(TM // m):
        n = tile * (TM // m) + qb
        rows = slice(qb * m, (qb + 1) * m)
        blocks = []
        for off in (-1, 0, 1):
            start = pl.multiple_of(jnp.clip(n + off, 0, nb - 1) * m, m)
            blocks.append(kv_ref[0, pl.ds(start, m), :])
        kvw = jnp.concatenate(blocks, axis=0)
        kpos = n * m + key_off
        in_seq = jnp.where((kpos >= 0) & (kpos < DEC_SEQ), 0.0, NEG)
        q4 = _stack_heads([q_ref[0, rows, 0:LANE], q_ref[0, rows, LANE:2 * LANE]])
        s_win = jnp.where(band, _dot_nt(q4, kvw[:, 0:LANE]) + in_seq, NEG)
        s_ctx = _dot_nt(q4, ck_s[...])
        o4 = _softmax_pv([s_win, s_ctx], [kvw[:, LANE:2 * LANE], cv_s[...]], sink=sink)
        for pair in range(2):
            g = g_ref[0, rows, pair * LANE:(pair + 1) * LANE].astype(F32)
            dst_ref[rows, pair * LANE:(pair + 1) * LANE] = (_unstack_pair(o4, pair, m) * _silu(g)).astype(BF16)


def _nat_rows(q_ref, k_ref, v_ref, g_ref, ck_ref, cv_ref, ring_ref, ck_s, cv_s, bias_s, dst_ref):
    nrows = DEC_SEQ // GRID_W
    kr = NAT_ROWS
    m = NAT_RPS * GRID_W
    step = pl.program_id(1)

    @pl.when((pl.program_id(0) == 0) & (step == 0))
    def _():
        shape = (GRID_W, LANE)
        lane = lax.broadcasted_iota(jnp.int32, shape, 1)
        qcol = lax.broadcasted_iota(jnp.int32, shape, 0)
        rel = (lane & (GRID_W - 1)) - jnp.clip(qcol - NAT_COLS // 2, 0, GRID_W - NAT_COLS)
        col_ok = (rel >= 0) & (rel < NAT_COLS)
        for h in range(NAT_HEADS):
            for i in range(NAT_BIAS_ROWS):
                a = pltpu.roll(jnp.broadcast_to(ring_ref[h, i:i + 1, :], shape), 0, 1, stride=1, stride_axis=0)
                b = pltpu.roll(jnp.broadcast_to(ring_ref[h, i + 1:i + 2, :], shape), GRID_W, 1, stride=1,
                               stride_axis=0)
                bias_s[h, i] = jnp.where(col_ok, jnp.where(lane < GRID_W, a, b), NEG)

    @pl.when(step == 0)
    def _():
        ck_s[...] = ck_ref[0, 0].astype(BF16)
        cv_s[...] = cv_ref[0, 0].astype(BF16)

    key_row = lax.broadcasted_iota(jnp.int32, (1, LANE), 1) // GRID_W
    for grp in range(TM // m):
        row0 = (step * (TM // m) + grp) * NAT_RPS
        rows = slice(grp * m, (grp + 1) * m)
        ws = jnp.clip(row0 - kr // 2, 0, nrows - NAT_UNION)
        start = pl.multiple_of(ws * GRID_W, GRID_W)
        kl = k_ref[0, pl.ds(start, NAT_UNION * GRID_W), :]
        vl = v_ref[0, pl.ds(start, NAT_UNION * GRID_W), :]
        for pair in range(NAT_HEADS // 2):
            c = pair * LANE
            blocks = []
            for hh in range(2):
                for rr in range(NAT_RPS):
                    r = row0 + rr
                    lo = jnp.clip(r - kr // 2, 0, nrows - kr) - ws
                    ro = ws - r + (NAT_ROWS - 1) + NAT_BIAS_PAD
                    row = []
                    for t in range(NAT_UNION // 2):
                        kj = key_row + 2 * t
                        row.append(jnp.where((kj >= lo) & (kj < lo + kr), bias_s[2 * pair + hh, ro + 2 * t], NEG))
                    blocks.append(jnp.concatenate(row, axis=1))
            bias = jnp.concatenate(blocks, axis=0)
            q2 = _stack_heads([q_ref[0, rows, c:c + LANE]])
            s_lat = _dot_nt(q2, kl[:, c:c + LANE]) + bias
            s_ctx = _dot_nt(q2, ck_s[:, c:c + LANE])
            o2 = _softmax_pv([s_lat, s_ctx], [vl[:, c:c + LANE], cv_s[:, c:c + LANE]])
            g = g_ref[0, rows, c:c + LANE].astype(F32)
            dst_ref[rows, c:c + LANE] = (_unstack_pair(o2, 0, m) * _silu(g)).astype(BF16)


def _nat_bias_rings(rpb):
    nc = NAT_COLS - 1
    rep = lambda a, n: jnp.broadcast_to(a, a.shape[:-1] + (n,))
    ring = jnp.concatenate([rpb[..., nc:], rep(rpb[..., -1:], GRID_W - 1 - nc), rep(rpb[..., :1], GRID_W - nc),
                            rpb[..., :nc]], axis=-1)
    neg = lambda n: jnp.full(ring.shape[:2] + (n, 2 * GRID_W), NEG, F32)
    return jnp.concatenate([neg(NAT_BIAS_PAD), ring, neg(NAT_BIAS_PAD + 1)], axis=2)


def _lat_attn_rider(z, sink, ck_a, cv_a, ck_n, cv_n, rings, layer):
    kvw = A_KV_HEADS * HEAD_DIM
    hw = NAT_HEADS * HEAD_DIM

    def body(sink_ref, qa_ref, kva_ref, ga_ref, cka_ref, cva_ref, qd_ref, kd_ref, vd_ref, gd_ref, ckd_ref, cvd_ref,
             ring_ref, cka_s, cva_s, ckd_s, cvd_s, bias_s, ba_s, bd_s):
        _win_attn_rows(sink_ref, qa_ref, kva_ref, ga_ref, cka_ref, cva_ref, cka_s, cva_s, ba_s, layer)
        _nat_rows(qd_ref, kd_ref, vd_ref, gd_ref, ckd_ref, cvd_ref, ring_ref, ckd_s, cvd_s, bias_s, bd_s)

    tile = lambda col: pl.BlockSpec((1, TM, 256), lambda b, t: (b, t, col))
    whole = lambda col: pl.BlockSpec((1, DEC_SEQ, 256), lambda b, t: (b, 0, col))
    cache = lambda w: pl.BlockSpec((1, 1, PAST_LEN, w), lambda b, t: (b, layer, 0, 0))
    return _Rider(
        body=body,
        in_specs=[pl.BlockSpec(memory_space=pltpu.SMEM), tile(Z_AQ // 256), whole(Z_AK // 256), tile(Z_AG // 256),
                  cache(kvw), cache(kvw), tile(Z_DQ // 256), whole(Z_DK // 256), whole(Z_DV // 256),
                  tile(Z_DG // 256), cache(hw), cache(hw), _layer_spec(rings.shape[1:], layer)],
        args=[sink, z, z, z, ck_a, cv_a, z, z, z, z, ck_n, cv_n, rings],
        out_shape=[], out_specs=[],
        scratch=[pltpu.VMEM((PAST_LEN, kvw), BF16), pltpu.VMEM((PAST_LEN, kvw), BF16),
                 pltpu.VMEM((PAST_LEN, hw), BF16), pltpu.VMEM((PAST_LEN, hw), BF16),
                 pltpu.VMEM((NAT_HEADS, NAT_BIAS_ROWS, GRID_W, LANE), F32),
                 pltpu.VMEM((TM, BRANCH_W), BF16), pltpu.VMEM((TM, BRANCH_W), BF16)],
        branches=True)


def _mla_body(q_ref, ckv_ref, kpe_ref, g_ref, cckv_ref, ckpe_ref, wkv_ref, o_ref, k_s, v_s):
    nlat = DEC_SEQ
    step = pl.program_id(0) * pl.num_programs(1) + pl.program_id(1)

    @pl.when(step % MLA_TILES == 0)
    def _():
        rows = 512

        def fill(r0, ckv, kpe_blk):
            kv = _dot(ckv, wkv_ref[...])
            for h, kh in enumerate(_mla_keys(kv, kpe_blk)):
                k_s[h, r0:r0 + rows, :] = kh
                v_s[h, r0:r0 + rows, :] = kv[:, h * LANE:(h + 1) * LANE].astype(BF16)

        for c in range(nlat // rows):
            fill(c * rows, ckv_ref[0, c * rows:(c + 1) * rows, :], kpe_ref[0, c * rows:(c + 1) * rows, :].astype(F32))
        ckpe = ckpe_ref[0, 0]
        ckpe_blk = jnp.concatenate([jnp.zeros((PAST_LEN, KPE_LANE), F32), ckpe,
                                    jnp.zeros((PAST_LEN, LANE - KPE_LANE - MLA_ROPE), F32)], axis=1)
        fill(nlat, cckv_ref[0, 0].astype(BF16), ckpe_blk)

    def store(pair, val):
        o_ref[0, :, pair * LANE:(pair + 1) * LANE] = val

    _mla_heads(lambda h: q_ref[0, :, LANE * h:LANE * (h + 1)], lambda h: k_s[h], lambda h: v_s[h],
               lambda p: _silu(g_ref[0, :, p * LANE:(p + 1) * LANE].astype(F32)), store)


def _mla_rider(z, cache_ckv, cache_kpe, wkv, layer, nt):
    nkeys = DEC_SEQ + PAST_LEN
    assert DEC_BATCH * MLA_TILES == POST_STEPS
    bt = lambda b, t: ((b * nt + t) // MLA_TILES, (b * nt + t) % MLA_TILES)
    tile = lambda col: (lambda b, t: bt(b, t) + (col,))
    whole = lambda col: (lambda b, t: (bt(b, t)[0], 0, col))
    cache = lambda b, t: (bt(b, t)[0], layer, 0, 0)
    return _Rider(
        body=_mla_body,
        in_specs=[
            pl.BlockSpec((1, TQ_MLA, MLA_HEADS * LANE), tile(Z_QM // (MLA_HEADS * LANE))),
            pl.BlockSpec((1, DEC_SEQ, LANE), whole(Z_CKV // LANE)),
            pl.BlockSpec((1, DEC_SEQ, LANE), whole(Z_KPE // LANE)),
            pl.BlockSpec((1, TQ_MLA, 256), tile(Z_BG // 256)),
            pl.BlockSpec((1, 1, PAST_LEN, MLA_KV_LORA), cache),
            pl.BlockSpec((1, 1, PAST_LEN, MLA_ROPE), cache),
            _layer_spec((MLA_KV_LORA, 512), layer),
        ],
        args=[z, z, z, z, cache_ckv, cache_kpe, wkv],
        out_shape=[jax.ShapeDtypeStruct((DEC_BATCH, DEC_SEQ, BRANCH_W), BF16)],
        out_specs=[pl.BlockSpec((1, TQ_MLA, BRANCH_W), tile(0))],
        scratch=[pltpu.VMEM((MLA_HEADS, nkeys, LANE), BF16), pltpu.VMEM((MLA_HEADS, nkeys, LANE), BF16)])


def _s5_mats(lam_re, lam_im, log_dt, b_re, b_im, c_re, c_im):
    t = S5_T
    hp = lax.Precision.HIGHEST
    lre = jnp.minimum(lam_re, -1e-4)
    lim = lam_im
    dt = jnp.exp(log_dt)[..., None]
    er, ei = lre * dt, lim * dt
    mag = jnp.exp(er)
    are, aim = mag * jnp.cos(ei), mag * jnp.sin(ei)
    den = lre * lre + lim * lim
    qre = ((are - 1.0) * lre + aim * lim) / den
    qim = (aim * lre - (are - 1.0) * lim) / den
    bbr = qre[..., None] * b_re - qim[..., None] * b_im
    bbi = qre[..., None] * b_im + qim[..., None] * b_re
    n = jnp.arange(t + 1, dtype=F32)[:, None, None, None]
    pmag = jnp.exp(er[None] * n)
    pr, pi = pmag * jnp.cos(ei[None] * n), pmag * jnp.sin(ei[None] * n)

    def c_times_pow(d, pw_r, pw_i):
        rep = lambda a: jnp.repeat(jnp.transpose(a, (1, 2, 0)), S5_CH, axis=-1)
        til = lambda a: jnp.tile(jnp.transpose(a, (0, 2, 1)), (1, 1, t))
        cr, ci, ar, ai = til(c_re[d]), til(c_im[d]), rep(pw_r), rep(pw_i)
        return cr * ar - ci * ai, cr * ai + ci * ar

    def lag_kernels(d, wr, wi):
        return (jnp.einsum('gpk,gpx->gkx', bbr[d], wr, precision=hp)
                - jnp.einsum('gpk,gpx->gkx', bbi[d], wi, precision=hp))

    wf0 = c_times_pow(0, pr[:t, 0], pi[:t, 0])
    wf1 = c_times_pow(0, pr[1:, 0], pi[1:, 0])
    wb = c_times_pow(1, pr[1:, 1][::-1], pi[1:, 1][::-1])
    kb0 = (jnp.einsum('gpk,gcp->gkc', bbr[1], c_re[1], precision=hp)
           - jnp.einsum('gpk,gcp->gkc', bbi[1], c_im[1], precision=hp))
    kf = lag_kernels(0, *wf0) + jnp.pad(kb0, ((0, 0), (0, 0), (0, S5_TC - S5_CH)))
    kb = lag_kernels(1, *wb) * jnp.asarray(np.arange(S5_TC) >= S5_CH, F32)
    ring = jnp.concatenate([kf, kb], axis=-1)

    def state_cols(pw_r, pw_i, d):
        rep = lambda a: jnp.repeat(jnp.transpose(a, (1, 0, 2)), S5_CH, axis=1)
        til = lambda a: jnp.tile(jnp.transpose(a, (0, 2, 1)), (1, t, 1))
        ar, ai, br, bi = rep(pw_r), rep(pw_i), til(bbr[d]), til(bbi[d])
        return ar * br - ai * bi, ar * bi + ai * br

    sfr, sfi = state_cols(pr[:t, 0][::-1], pi[:t, 0][::-1], 0)
    sbr, sbi = state_cols(pr[:t, 1], pi[:t, 1], 1)
    m_state = jnp.concatenate([sfr, sbr, sfi, sbi], axis=-1)
    m_out = jnp.concatenate([wf1[0], wb[0], -wf1[1], -wb[1]], axis=1)

    a_t = jnp.stack([jnp.concatenate([pr[t, 0], pr[t, 1]], axis=-1),
                     jnp.concatenate([pi[t, 0], pi[t, 1]], axis=-1)], axis=1)
    return ring, m_state.astype(BF16), m_out.astype(BF16), a_t


def _s5_body(u_ref, ring_ref, ms_ref, mo_ref, at_ref, h0_ref, y_ref, hend_ref, s_s, hf_s, hb_s, m_s, *, nk, nb):
    p = S5_STATE
    u = u_ref[0]
    s = _dot(u, ms_ref[0])
    s_s[0] = s[:, 0:2 * p]
    s_s[1] = s[:, 2 * p:4 * p]
    are = at_ref[0, 0:1, :]
    aim = at_ref[0, 1:2, :]
    fwd_lane = lax.broadcasted_iota(jnp.int32, (1, 2 * p), 1) < p

    for k in range(S5_CH):
        rk = jnp.broadcast_to(ring_ref[0, k:k + 1, :], (S5_T, 2 * S5_TC))
        rolled = pltpu.roll(rk, 0, 1, stride=S5_CH, stride_axis=0)
        for blk in range(S5_TC // LANE):
            m_s[blk, pl.ds(k, S5_T, stride=S5_CH), :] = rolled[:, blk * LANE:(blk + 1) * LANE]

    if h0_ref is None:
        hre = him = jnp.zeros((nb, 2 * p), F32)
    else:
        hre, him = h0_ref[0, :, 0:2 * p], h0_ref[0, :, 2 * p:4 * p]
    for st in range(nk):
        rf = pl.ds(st, nb, stride=nk)
        rb = pl.ds(nk - 1 - st, nb, stride=nk)
        hf_s[0, rf, :] = hre
        hf_s[1, rf, :] = him
        hb_s[0, rb, :] = hre
        hb_s[1, rb, :] = him
        sre = jnp.where(fwd_lane, s_s[0, rf, :], s_s[0, rb, :])
        sim = jnp.where(fwd_lane, s_s[1, rf, :], s_s[1, rb, :])
        hre, him = are * hre - aim * him + sre, are * him + aim * hre + sim
    if hend_ref is not None:
        hend_ref[0] = jnp.concatenate([hre, him], axis=1)
    hst = jnp.concatenate([jnp.where(fwd_lane, hf_s[0], hb_s[0]), jnp.where(fwd_lane, hf_s[1], hb_s[1])], axis=1)
    m_intra = jnp.concatenate([m_s[blk] for blk in range(S5_TC // LANE)], axis=1).astype(BF16)
    y_ref[0] = _dot(u, m_intra) + _dot(hst.astype(BF16), mo_ref[0])


def _s5_rider(ut, ring, m_state, m_out, a_t, h0, layer, nt):
    assert S5_GROUPS == POST_STEPS
    ctx = h0 is None
    g3 = lambda b, t: (b * nt + t, 0, 0)
    lg = lambda shape: pl.BlockSpec((None, 1) + shape, lambda b, t: (layer, b * nt + t, 0, 0))
    uspec = pl.BlockSpec((1, S5_ROWS, S5_TC), g3)

    def body(*refs):
        if ctx:
            u, rg, ms, mo, at, y, hend, s_s, hf_s, hb_s, m_s = refs
            _s5_body(u, rg, ms, mo, at, None, y, hend, s_s, hf_s, hb_s, m_s, nk=S5_NK_CTX, nb=BATCH)
        else:
            u, rg, ms, mo, at, h0_ref, y, s_s, hf_s, hb_s, m_s = refs
            _s5_body(u, rg, ms, mo, at, h0_ref, y, None, s_s, hf_s, hb_s, m_s, nk=S5_NK_LAT, nb=DEC_BATCH)

    return _Rider(
        body=body,
        in_specs=[uspec, lg((S5_CH, 2 * S5_TC)), lg((S5_TC, 4 * S5_STATE)), lg((4 * S5_STATE, S5_TC)),
                  lg((2, 2 * S5_STATE))] + ([] if ctx else [lg((DEC_BATCH, 4 * S5_STATE))]),
        args=[ut, ring, m_state, m_out, a_t] + ([] if ctx else [h0]),
        out_shape=[jax.ShapeDtypeStruct((S5_GROUPS, S5_ROWS, S5_TC), F32)]
        + ([jax.ShapeDtypeStruct((S5_GROUPS, BATCH, 4 * S5_STATE), F32)] if ctx else []),
        out_specs=[uspec] + ([pl.BlockSpec((1, BATCH, 4 * S5_STATE), g3)] if ctx else []),
        scratch=[pltpu.VMEM((2, S5_ROWS, LANE), F32)] * 3 + [pltpu.VMEM((S5_TC // LANE, S5_TC, LANE), F32)])


_POST_IN = ('x', 'mod', 'ng', 'ba', 'bb', 'bd', 'y', 'u', 'cg', 'sd', 'wglu', 'wm', 'wbr', 'wo', 'fg')


class _Rider(NamedTuple):
    body: Callable
    in_specs: list
    args: list
    out_shape: list
    out_specs: list
    scratch: list
    branches: bool = False


NO_RIDER = _Rider(body=lambda: None, in_specs=[], args=[], out_shape=[], out_specs=[], scratch=[])


def _both_riders(r1, r2):
    i1, i2, o1, o2, s1 = len(r1.in_specs), len(r2.in_specs), len(r1.out_specs), len(r2.out_specs), len(r1.scratch)

    def body(*refs):
        ins, outs, scr = refs[:i1 + i2], refs[i1 + i2:i1 + i2 + o1 + o2], refs[i1 + i2 + o1 + o2:]
        r1.body(*ins[:i1], *outs[:o1], *scr[:s1])
        r2.body(*ins[i1:], *outs[o1:], *scr[s1:])

    return _Rider(body=body, in_specs=r1.in_specs + r2.in_specs, args=r1.args + r2.args,
                  out_shape=r1.out_shape + r2.out_shape, out_specs=r1.out_specs + r2.out_specs,
                  scratch=r1.scratch + r2.scratch)


def _post_kernel(*refs, final, rider):
    names = [n for n in _POST_IN if not (rider.branches and n in ('ba', 'bd'))]
    r = dict(zip(names, refs))
    n0, n_in, n_out = len(names), len(rider.in_specs), len(rider.out_specs)
    o_ref = refs[n0 + n_in]
    ys_s = refs[n0 + n_in + 1 + n_out]
    rider_scratch = refs[n0 + n_in + 2 + n_out:]
    rider.body(*refs[n0:n0 + n_in], *refs[n0 + n_in + 1:n0 + n_in + 1 + n_out], *rider_scratch)
    ba, bd = (rider_scratch[-2][...], rider_scratch[-1][...]) if rider.branches else (r['ba'][0], r['bd'][0])
    x_ref, mod_ref, ng_ref, y_ref = r['x'], r['mod'], r['ng'], r['y']
    x = x_ref[0]
    hb = _ada_h(x, mod_ref, ng_ref).astype(BF16)
    for m in range(S5_T // 8):
        for hf in range(2):
            vs = _slot_transpose([y_ref[8 * hf + gp, :, m * LANE:(m + 1) * LANE] for gp in range(8)])
            for s in range(8):
                ys_s[hf, pl.ds(8 * m + s, TM_CHUNKS, stride=S5_T), :] = vs[s]
    yv = jnp.concatenate([ys_s[0], ys_s[1]], axis=1) + r['sd'][...] * r['u'][0].astype(F32)
    gel = 0.5 * yv * (1.0 + jnp.tanh(math.sqrt(2.0 / math.pi) * (yv + 0.044715 * (yv * yv * yv))))
    gl = _dot(gel.astype(BF16), r['wglu'][...])
    oc = gl[:, 0:BRANCH_W] * _sigmoid(gl[:, BRANCH_W:2 * BRANCH_W])
    bc = (oc * _silu(r['cg'][0].astype(F32))).astype(BF16)
    acc = None
    for k, br in enumerate((ba, r['bb'][0], bc, bd)):
        proj = _dot(br, r['wbr'][k])
        mg = _sigmoid(_dot(hb, r['wm'][:, k * D_MODEL:(k + 1) * D_MODEL]))
        acc = mg * proj if acc is None else acc + mg * proj
    y = _dot(acc.astype(BF16), r['wo'][...])
    out = x + mod_ref[0, :, 2 * D_MODEL:3 * D_MODEL] * y
    if final:
        out = _rms(out, r['fg'][...])
    o_ref[0] = out


def _post_call(x, mods, mod_row0, p, ba, bb, bd, y_s5, z, fg, final, layer, name, rider):
    nb, length, _ = x.shape
    nt = length // TM
    assert nb * nt == POST_STEPS
    tok = lambda b, t: (b, t, 0)
    br_spec = pl.BlockSpec((1, TM, BRANCH_W), tok)
    branch_args = [bb] if rider.branches else [ba, bb, bd]
    return pl.pallas_call(
        functools.partial(_post_kernel, final=final, rider=rider),
        out_shape=tuple([jax.ShapeDtypeStruct((nb, length, D_MODEL), F32)] + rider.out_shape),
        grid=(nb, nt),
        in_specs=[
            pl.BlockSpec((1, TM, D_MODEL), tok),
            _mod_spec(layer, mod_row0),
            _layer_spec((1, D_MODEL), layer),
        ] + [br_spec] * len(branch_args) + [
            pl.BlockSpec((S5_GROUPS, TM_CHUNKS, S5_TC), lambda b, t: (0, b * nt + t, 0)),
            pl.BlockSpec((1, TM, 256), lambda b, t: (b, t, Z_CU // 256)),
            pl.BlockSpec((1, TM, 256), lambda b, t: (b, t, Z_CG // 256)),
            _layer_spec((1, BRANCH_W), layer),
            _layer_spec((BRANCH_W, 2 * BRANCH_W), layer),
            _layer_spec((D_MODEL, N_BRANCH * D_MODEL), layer),
            _layer_spec((N_BRANCH, BRANCH_W, D_MODEL), layer),
            _layer_spec((D_MODEL, D_MODEL), layer),
            pl.BlockSpec((1, D_MODEL), lambda b, t: (0, 0)),
        ] + rider.in_specs,
        out_specs=tuple([pl.BlockSpec((1, TM, D_MODEL), tok)] + rider.out_specs),
        scratch_shapes=[pltpu.VMEM((2, TM, LANE), F32)] + rider.scratch,
        compiler_params=_cparams(("arbitrary", "arbitrary"), 56),
        name=name,
    )(x, mods, p['ng'], *branch_args, y_s5, z, z, p['s5d'], p['wglu'], p['wm'], p['wbr'], p['wo'], fg, *rider.args)


def _rope_tables():
    t = np.arange(DEC_SEQ)
    row = (t // GRID_W).astype(np.float64)
    col = (t % GRID_W).astype(np.float64)

    def pattern(half):
        inv = ROPE_BASE ** (-np.arange(half, dtype=np.float64) / half)
        zeros = np.zeros((DEC_SEQ, half))
        cs, s_up, s_lo = [], [], []
        for pos in (row, col):
            ang = pos[:, None] * inv[None, :]
            c, s = np.cos(ang), np.sin(ang)
            cs += [c, c]
            s_up += [zeros, s]
            s_lo += [-s, zeros]
        return [np.concatenate(parts, axis=1) for parts in (cs, s_up, s_lo)]

    tab_a = np.stack([np.tile(part, (1, LANE // HEAD_DIM)) for part in pattern(HEAD_DIM // 4)])
    ident = [np.ones, np.zeros, np.zeros]
    tab_m = np.stack([np.concatenate([fill((DEC_SEQ, KPE_LANE)), part,
                                      fill((DEC_SEQ, LANE - KPE_LANE - MLA_ROPE))], axis=1)
                      for fill, part in zip(ident, pattern(MLA_ROPE // 4))])
    return jnp.asarray(tab_a, F32), jnp.asarray(tab_m, F32)


_W_IN_MOVES = ((256, 256, 256), (512, 1952, 768), (R_CQ, 768, 384), (R_CU, 1440, 256),
               (R_G + 256, 1184, 256), (R_G + 512, 1696, 256), (R_G + 768, 2720, 256))
_W_IN_HEAD_MOVES = ((0, 0), (R_G, 512))
_W_IN_KPE = 1152
W_IN_ROWS = 256


def _w_in_kernel(wt_ref, o_ref):
    take = lambda src, width: wt_ref[0, src:src + width, :].T
    for dst, src, width in _W_IN_MOVES:
        o_ref[0, :, dst:dst + width] = take(src, width).astype(BF16)
    for dst, src in _W_IN_HEAD_MOVES:
        blk = take(src, A_HEADS * HEAD_DIM)
        for i, h in enumerate(A_HEAD_ORDER):
            o_ref[0, :, dst + HEAD_DIM * i:dst + HEAD_DIM * (i + 1)] = blk[:, HEAD_DIM * h:HEAD_DIM * (h + 1)].astype(BF16)
    o_ref[0, :, R_KPE:R_CU] = jnp.zeros((W_IN_ROWS, LANE), BF16)
    o_ref[0, :, R_KPE + KPE_LANE:R_KPE + KPE_LANE + MLA_ROPE] = take(_W_IN_KPE, LANE)[:, 0:MLA_ROPE].astype(BF16)


def _w_in_call(w_in):
    wt = jnp.swapaxes(w_in, 1, 2)
    return pl.pallas_call(
        _w_in_kernel,
        out_shape=jax.ShapeDtypeStruct((DEPTH, D_MODEL, R_W), BF16),
        grid=(DEPTH, D_MODEL // W_IN_ROWS),
        in_specs=[pl.BlockSpec((1, wt.shape[1], W_IN_ROWS), lambda i, r: (i, 0, r))],
        out_specs=pl.BlockSpec((1, W_IN_ROWS, R_W), lambda i, r: (i, r, 0)),
        compiler_params=_cparams(("arbitrary", "arbitrary"), 32),
        name="w_in_layout",
    )(wt)


def _cast_rider(ws):
    flat = [w.reshape(-1, w.shape[-1]) for w in ws]

    def body(*refs):
        for w_ref, o_ref in zip(refs[:len(ws)], refs[len(ws):]):
            o_ref[...] = w_ref[...].astype(BF16)

    specs = [pl.BlockSpec((w.shape[0] // POST_STEPS, w.shape[1]), lambda b, t: (t, 0)) for w in flat]
    return _Rider(body=body, in_specs=specs, args=flat,
                  out_shape=[jax.ShapeDtypeStruct(w.shape, BF16) for w in flat], out_specs=specs, scratch=[])


def _reorder_w_q_up(w):
    w = w.reshape(MLA_Q_LORA, MLA_HEADS, MLA_NOPE + MLA_ROPE)
    w = jnp.concatenate([w, jnp.zeros((MLA_Q_LORA, MLA_HEADS, LANE - MLA_NOPE - MLA_ROPE), w.dtype)], axis=-1)
    return w.reshape(MLA_Q_LORA, MLA_HEADS * LANE)


def _reorder_w_branch(w):
    wa = w[0].reshape(A_HEADS, HEAD_DIM, D_MODEL)
    wa = jnp.concatenate([wa[h] for h in A_HEAD_ORDER], axis=0)
    return jnp.concatenate([wa[None], w[1:]], axis=0)


def kernel(x_prompt, x_sample, cache_a_k, cache_a_v, cache_mla_ckv, cache_mla_kpe, cache_na_k, cache_na_v,
           state_s5_re, state_s5_im, c, c_ctx, w_mod, b_mod, norm_g, w_in, w_merge, a_sink,
           mla_q_norm, mla_w_q_up, mla_kv_norm, mla_w_kv_up, s5_lam_re, s5_lam_im, s5_log_dt,
           s5_b_re, s5_b_im, s5_c_re, s5_c_im, s5_d, s5_w_glu, na_rpb, w_branch, w_out, final_norm_g):
    n_ctx = BATCH * SEQ
    conds = jnp.concatenate([c, c_ctx[None, :], jnp.zeros((3, D_MODEL), F32)], axis=0)
    mods = _mod_call(conds, w_mod, b_mod).reshape(DEPTH, 8, 1, 3 * D_MODEL)
    tabs = _rope_tables()
    ck_a = cache_a_k.reshape(DEC_BATCH, DEPTH, PAST_LEN, A_KV_HEADS * HEAD_DIM)
    cv_a = cache_a_v.reshape(DEC_BATCH, DEPTH, PAST_LEN, A_KV_HEADS * HEAD_DIM)
    ck_n = cache_na_k.reshape(DEC_BATCH, DEPTH, PAST_LEN, NAT_HEADS * HEAD_DIM)
    cv_n = cache_na_v.reshape(DEC_BATCH, DEPTH, PAST_LEN, NAT_HEADS * HEAD_DIM)
    fg = final_norm_g.reshape(1, D_MODEL)
    caches = None

    p = dict(
        ng=norm_g.reshape(DEPTH, 1, D_MODEL),
        w_raw=_w_in_call(w_in),
        qnorm=mla_q_norm.reshape(DEPTH, 1, MLA_Q_LORA),
        wq=jax.vmap(_reorder_w_q_up)(mla_w_q_up).astype(BF16),
        kvnorm=mla_kv_norm.reshape(DEPTH, 1, MLA_KV_LORA),
        wkv=mla_w_kv_up.astype(BF16),
        wbr=jax.vmap(_reorder_w_branch)(w_branch).astype(BF16),
        wglu=s5_w_glu.astype(BF16),
        s5d=s5_d.reshape(DEPTH, 1, BRANCH_W),
    )
    ring, m_state, m_out, a_t = jax.vmap(_s5_mats)(s5_lam_re, s5_lam_im, s5_log_dt, s5_b_re, s5_b_im, s5_c_re, s5_c_im)
    nat_bias = _nat_bias_rings(na_rpb * LOG2E)
    h0 = jnp.concatenate([state_s5_re[:, :, 0], state_s5_re[:, :, 1], state_s5_im[:, :, 0], state_s5_im[:, :, 1]],
                         axis=-1)
    h0 = jnp.transpose(h0, (1, 2, 0, 3))

    yp = x_prompt.reshape(1, n_ctx, D_MODEL)
    ys = x_sample
    hends = []
    for i in range(DEPTH):
        rider = _cast_rider([w_merge, w_out]) if i == 0 else NO_RIDER
        res = _inproj_call(yp, mods, DEC_BATCH, p, None, caches, i, "inproj_ctx", rider)
        z_ctx, ut_ctx, caches = res[0], res[1], list(res[2:2 + N_CACHE])
        if i == 0:
            p['wm'], p['wo'] = res[2 + N_CACHE].reshape(w_merge.shape), res[3 + N_CACHE].reshape(w_out.shape)
        zc = z_ctx.reshape(BATCH, SEQ, Z_W)
        nt_lat = DEC_SEQ // TM
        rider = _both_riders(_ctx_attn_rider(zc, a_sink, p['wkv'], i, nt_lat),
                             _s5_rider(ut_ctx, ring, m_state, m_out, a_t, None, i, nt_lat))
        z_lat, ut_lat, oa_c, ob_c, od_c, y_ctx, hend = _inproj_call(ys, mods, 0, p, tabs, None, i,
                                                                    "inproj_lat_ctx_attn_s5", rider)
        hends.append(hend)

        flat = lambda a: a.reshape(1, n_ctx, BRANCH_W)
        final = i == DEPTH - 1
        rider = _both_riders(_mla_rider(z_lat, cache_mla_ckv, cache_mla_kpe, p['wkv'], i, POST_STEPS),
                             _s5_rider(ut_lat, ring, m_state, m_out, a_t, h0, i, POST_STEPS))
        yp, ob_l, y_lat = _post_call(yp, mods, DEC_BATCH, p, flat(oa_c), flat(ob_c), flat(od_c), y_ctx, z_ctx, fg, final,
                                     i, "post_ctx_lat_mla_s5", rider)
        ys, = _post_call(ys, mods, 0, p, None, ob_l, None, y_lat, z_lat, fg, final, i, "post_lat_win_nat",
                         _lat_attn_rider(z_lat, a_sink, ck_a, cv_a, ck_n, cv_n, nat_bias, i))

    hend = jnp.stack(hends).reshape(DEPTH, S5_GROUPS, BATCH, 2, 2, S5_STATE)
    hend = jnp.transpose(hend, (3, 2, 0, 4, 1, 5))
    ak, av, ckv, kpe, nk, nv = caches
    heads = lambda a, h: a.reshape(BATCH, DEPTH, SEQ, h, HEAD_DIM)
    return (yp.reshape(BATCH, SEQ, D_MODEL), ys, heads(ak, A_KV_HEADS), heads(av, A_KV_HEADS), ckv, jnp.swapaxes(kpe, 2, 3),
            heads(nk, NAT_HEADS), heads(nv, NAT_HEADS), hend[0], hend[1])
```

```python
import functools
import math
from typing import Callable, NamedTuple

import numpy as np
import jax
import jax.numpy as jnp
from jax import lax
from jax.experimental import pallas as pl
from jax.experimental.pallas import tpu as pltpu

F32 = jnp.float32
BF16 = jnp.bfloat16

D_MODEL = 1024
BATCH = 32
SEQ = 256
DEPTH = 2
DEC_BATCH = 4
DEC_SEQ = 2048
PAST_LEN = 512
GRID_W = 64
HEAD_DIM = 64
BRANCH_W = 256
N_BRANCH = 4
Q_BLOCK = 128
A_HEADS = 4
A_KV_HEADS = 2
A_GROUP = A_HEADS // A_KV_HEADS
A_WINDOW = 128
MLA_HEADS = 4
MLA_Q_LORA = 256
MLA_KV_LORA = 128
MLA_NOPE = 64
MLA_ROPE = 32
MLA_V = 64
S5_CH = 16
S5_GROUPS = BRANCH_W // S5_CH
S5_STATE = 64
NAT_HEADS = 4
NAT_ROWS = 8
NAT_COLS = 16
ROPE_BASE = 10000.0
EPS = 1e-6
NEG = -1e30
LOG2E = 1.4426950408889634
Q_SCALE = HEAD_DIM ** -0.5 * LOG2E
MLA_Q_SCALE = (MLA_NOPE + MLA_ROPE) ** -0.5 * LOG2E

LANE = 128

R_AV, R_CQ, R_CKV, R_KPE, R_CU, R_G, R_W = 384, 1280, 1536, 1664, 1792, 2048, 3072
KPE_LANE = 64
A_HEAD_ORDER = (0, 2, 1, 3)
Z_AQ, Z_AK, Z_AV = 0, 256, 384
Z_DQ, Z_DK, Z_DV = 512, 768, 1024
Z_CKV, Z_KPE = 1280, 1408
Z_QM = 1536
Z_AG, Z_BG, Z_CG, Z_DG = 2048, 2304, 2560, 2816
Z_CU = 3072
Z_W = 3328

S5_T = 32
S5_TC = S5_T * S5_CH
S5_ROWS = 256
S5_NK_CTX = SEQ // S5_T
S5_NK_LAT = DEC_SEQ // S5_T

TM = 512
TM_CHUNKS = TM // S5_T
TQ_MLA = 512
MLA_TILES = DEC_SEQ // TQ_MLA
POST_STEPS = BATCH * SEQ // TM
CTX_BPS = 2
NAT_RPS = 4
NAT_UNION = NAT_ROWS + NAT_RPS
NAT_BIAS_PAD = NAT_RPS
NAT_BIAS_ROWS = 2 * NAT_ROWS - 1 + 2 * NAT_BIAS_PAD


def _cparams(sem, vmem_mb):
    return pltpu.CompilerParams(dimension_semantics=sem, vmem_limit_bytes=vmem_mb * 1024 * 1024)


def _sigmoid(x):
    return 1.0 / (1.0 + jnp.exp(-x))


def _silu(x):
    return x * _sigmoid(x)


def _rms(x, g):
    return x * lax.rsqrt(jnp.mean(x * x, axis=-1, keepdims=True) + EPS) * g


def _dot(a, b):
    return jnp.dot(a, b, preferred_element_type=F32)


def _dot_nt(a, b):
    return lax.dot_general(a, b, (((1,), (1,)), ((), ())), preferred_element_type=F32)


def _slot_transpose(vs):
    lane = lax.broadcasted_iota(jnp.int32, vs[0].shape, 1)
    vs = list(vs)
    for d in (4, 2, 1):
        keep = (lane & (S5_CH * d)) == 0
        nxt = list(vs)
        for lo in range(8):
            if lo & d:
                continue
            hi = lo + d
            nxt[lo] = jnp.where(keep, vs[lo], pltpu.roll(vs[hi], S5_CH * d, 1))
            nxt[hi] = jnp.where(keep, pltpu.roll(vs[lo], LANE - S5_CH * d, 1), vs[hi])
        vs = nxt
    return vs


def _mod_kernel(c_ref, w_ref, b_ref, o_ref):
    s = _silu(c_ref[...])
    o_ref[0] = _dot(s.astype(BF16), w_ref[0].astype(BF16)) + b_ref[0]


def _mod_call(conds, w_mod, b_mod):
    nc = 512
    return pl.pallas_call(
        _mod_kernel,
        out_shape=jax.ShapeDtypeStruct((DEPTH, 8, 3 * D_MODEL), F32),
        grid=(DEPTH, 3 * D_MODEL // nc),
        in_specs=[
            pl.BlockSpec((8, D_MODEL), lambda i, j: (0, 0)),
            pl.BlockSpec((1, D_MODEL, nc), lambda i, j: (i, 0, j)),
            pl.BlockSpec((1, 1, nc), lambda i, j: (i, 0, j)),
        ],
        out_specs=pl.BlockSpec((1, 8, nc), lambda i, j: (i, 0, j)),
        compiler_params=_cparams(("arbitrary", "arbitrary"), 32),
        name="mod_rows",
    )(conds, w_mod, b_mod.reshape(DEPTH, 1, 3 * D_MODEL))


def _ada_h(x, mod_ref, ng_ref):
    shift = mod_ref[0, :, 0:D_MODEL]
    scale = mod_ref[0, :, D_MODEL:2 * D_MODEL]
    return _rms(x, ng_ref[...]) * (1.0 + scale) + shift


def _rope_block(xs, tab_ref, shift):
    return (xs * tab_ref[0] + pltpu.roll(xs, shift, 1) * tab_ref[1]
            + pltpu.roll(xs, LANE - shift, 1) * tab_ref[2])


CACHE_DIMS = ((SEQ, 128), (SEQ, 128), (SEQ, MLA_KV_LORA), (MLA_ROPE, SEQ), (SEQ, 256), (SEQ, 256))
N_CACHE = len(CACHE_DIMS)


def _inproj_kernel(*refs, rope, rider, first):
    x_ref, mod_ref, ng_ref, w_ref, qn_ref, wq_ref, kvn_ref = refs[:7]
    if rope:
        ta_ref, tm_ref = refs[7:9]
        n_in, n_out = len(rider.in_specs), len(rider.out_specs)
        z_ref, ut_ref = refs[9 + n_in:11 + n_in]
        u_s = refs[11 + n_in + n_out]
        rider.body(*refs[9:9 + n_in], *refs[11 + n_in:11 + n_in + n_out], *refs[12 + n_in + n_out:])
    else:
        n_c = 0 if first else N_CACHE
        n_in, n_out = len(rider.in_specs), len(rider.out_specs)
        base = 7 + n_c + n_in
        z_ref, ut_ref, ak_ref, av_ref, ckv_ref, kpe_ref, nk_ref, nv_ref = refs[base:base + 2 + N_CACHE]
        u_s = refs[base + 2 + N_CACHE + n_out]
        rider.body(*refs[7 + n_c:base], *refs[base + 2 + N_CACHE:base + 2 + N_CACHE + n_out],
                   *refs[base + 3 + N_CACHE + n_out:])
    h = _ada_h(x_ref[0], mod_ref, ng_ref)
    raw = _dot(h.astype(BF16), w_ref[...])
    for j in range(R_AV // LANE):
        blk = raw[:, j * LANE:(j + 1) * LANE]
        if j < Z_AK // LANE:
            blk = blk * Q_SCALE
        if rope:
            blk = _rope_block(blk, ta_ref, 16)
        z_ref[0, :, j * LANE:(j + 1) * LANE] = blk.astype(BF16)
    z_ref[0, :, Z_AV:Z_DQ] = raw[:, Z_AV:Z_DQ].astype(BF16)
    z_ref[0, :, Z_DQ:Z_DK] = (raw[:, Z_DQ:Z_DK] * Q_SCALE).astype(BF16)
    z_ref[0, :, Z_DK:R_CQ] = raw[:, Z_DK:R_CQ].astype(BF16)
    qn = _rms(raw[:, R_CQ:R_CKV], qn_ref[...])
    q = _dot(qn.astype(BF16), wq_ref[...]) * MLA_Q_SCALE
    kp = raw[:, R_KPE:R_CU]
    ckv = _rms(raw[:, R_CKV:R_KPE], kvn_ref[...])
    for hh in range(MLA_HEADS):
        qh = q[:, hh * LANE:(hh + 1) * LANE]
        if rope:
            qh = _rope_block(qh, tm_ref, 8)
        z_ref[0, :, Z_QM + hh * LANE:Z_QM + (hh + 1) * LANE] = qh.astype(BF16)
    z_ref[0, :, Z_CKV:Z_KPE] = ckv.astype(BF16)
    z_ref[0, :, Z_KPE:Z_QM] = (_rope_block(kp, tm_ref, 8) if rope else kp).astype(BF16)
    z_ref[0, :, Z_AG:Z_CU] = raw[:, R_G:R_W].astype(BF16)
    z_ref[0, :, Z_CU:Z_W] = raw[:, R_CU:R_G].astype(BF16)
    if not rope:
        half = TM // 2
        for e in range(2):
            rows = slice(e * half, (e + 1) * half)
            ak_ref[e, 0] = raw[rows, Z_AK:Z_AV]
            av_ref[e, 0] = raw[rows, Z_AV:Z_DQ]
            nk_ref[e, 0] = raw[rows, Z_DK:Z_DV]
            nv_ref[e, 0] = raw[rows, Z_DV:R_CQ]
            ckv_ref[e, 0] = ckv[rows]
            kpe_ref[e, 0] = kp[rows].T[KPE_LANE:KPE_LANE + MLA_ROPE]
            if first:
                for cref in (ak_ref, av_ref, nk_ref, nv_ref, ckv_ref, kpe_ref):
                    cref[e, 1:DEPTH] = jnp.zeros((DEPTH - 1,) + cref.shape[2:], F32)
    for hf in range(2):
        u_s[hf] = raw[:, R_CU + hf * LANE:R_CU + (hf + 1) * LANE]
    for m in range(S5_T // 8):
        for hf in range(2):
            outs = _slot_transpose([u_s[hf, pl.ds(8 * m + s, TM_CHUNKS, stride=S5_T), :] for s in range(8)])
            for gp in range(8):
                ut_ref[8 * hf + gp, :, m * LANE:(m + 1) * LANE] = outs[gp].astype(BF16)


def _layer_spec(shape, layer):
    zeros = (0,) * len(shape)
    return pl.BlockSpec((None,) + tuple(shape), lambda *_: (layer,) + zeros, pipeline_mode=pl.Buffered(1))


def _mod_spec(layer, row0):
    return pl.BlockSpec((None, 1, 1, 3 * D_MODEL), lambda b, t: (layer, row0 + b, 0, 0))


def _inproj_call(x, mods, mod_row0, p, tabs, caches, layer, name, rider):
    nb, length, _ = x.shape
    rope = tabs is not None
    nt = length // TM
    in_specs = [
        pl.BlockSpec((1, TM, D_MODEL), lambda b, t: (b, t, 0)),
        _mod_spec(layer, mod_row0),
        _layer_spec((1, D_MODEL), layer),
        _layer_spec((D_MODEL, R_W), layer),
        _layer_spec((1, MLA_Q_LORA), layer),
        _layer_spec((MLA_Q_LORA, MLA_HEADS * LANE), layer),
        _layer_spec((1, MLA_KV_LORA), layer),
    ]
    args = [x, mods, p['ng'], p['w_raw'], p['qnorm'], p['wq'], p['kvnorm']]
    out_shape = [jax.ShapeDtypeStruct((nb, length, Z_W), BF16),
                 jax.ShapeDtypeStruct((S5_GROUPS, S5_ROWS, S5_TC), BF16)]
    out_specs = [pl.BlockSpec((1, TM, Z_W), lambda b, t: (b, t, 0)),
                 pl.BlockSpec((S5_GROUPS, TM_CHUNKS, S5_TC), lambda b, t: (0, b * nt + t, 0))]
    aliases = {}
    scratch = [pltpu.VMEM((2, TM, LANE), F32)]
    if rope:
        assert nb * nt == POST_STEPS
        in_specs += [pl.BlockSpec((3, TM, LANE), lambda b, t: (0, t, 0))] * 2 + rider.in_specs
        args += list(tabs) + rider.args
        out_shape += rider.out_shape
        out_specs += rider.out_specs
        scratch += rider.scratch
    else:
        for k, dims in enumerate(CACHE_DIMS):
            out_shape.append(jax.ShapeDtypeStruct((BATCH, DEPTH) + dims, F32))
            if caches is None:
                out_specs.append(pl.BlockSpec((2, DEPTH) + dims, lambda b, t: (t, 0, 0, 0)))
            else:
                in_specs.append(pl.BlockSpec(memory_space=pl.ANY))
                args.append(caches[k])
                out_specs.append(pl.BlockSpec((2, 1) + dims, lambda b, t: (t, layer, 0, 0)))
                aliases[7 + k] = 2 + k
        in_specs += rider.in_specs
        args += rider.args
        out_shape += rider.out_shape
        out_specs += rider.out_specs
        scratch += rider.scratch
    return pl.pallas_call(
        functools.partial(_inproj_kernel, rope=rope, rider=rider, first=not rope and caches is None),
        out_shape=tuple(out_shape),
        grid=(nb, nt),
        in_specs=in_specs,
        out_specs=tuple(out_specs),
        scratch_shapes=scratch,
        input_output_aliases=aliases,
        compiler_params=_cparams(("arbitrary", "arbitrary"), 56),
        name=name,
    )(*args)


def _softmax_pv(s_list, v_list, sink=None):
    m = jnp.max(s_list[0], axis=-1, keepdims=True)
    for s in s_list[1:]:
        m = jnp.maximum(m, jnp.max(s, axis=-1, keepdims=True))
    if sink is not None:
        m = jnp.maximum(m, sink)
    den = None
    o = None
    for s, v in zip(s_list, v_list):
        e = jnp.exp2(s - m)
        d = jnp.sum(e, axis=-1, keepdims=True)
        pv = _dot(e.astype(BF16), v)
        den = d if den is None else den + d
        o = pv if o is None else o + pv
    if sink is not None:
        den = den + jnp.exp2(sink - m)
    return o / den


def _low_lanes(shape):
    return lax.broadcasted_iota(jnp.int32, shape, 1) < HEAD_DIM


def _stack_heads(blocks):
    lo = _low_lanes(blocks[0].shape)
    zero = jnp.zeros_like(blocks[0])
    parts = []
    for b in blocks:
        parts += [jnp.where(lo, b, zero), jnp.where(lo, zero, b)]
    return jnp.concatenate(parts, axis=0)


def _unstack_pair(o, idx, m):
    return jnp.where(_low_lanes((m, LANE)), o[2 * idx * m:(2 * idx + 1) * m], o[(2 * idx + 1) * m:(2 * idx + 2) * m])


def _sink_column(sink_ref, layer, m):
    blk = lax.broadcasted_iota(jnp.int32, (A_HEADS * m, 1), 0) // m
    col = jnp.full((A_HEADS * m, 1), sink_ref[layer, A_HEAD_ORDER[-1]] * LOG2E, F32)
    for i in range(A_HEADS - 1):
        col = jnp.where(blk == i, sink_ref[layer, A_HEAD_ORDER[i]] * LOG2E, col)
    return col


def _mla_keys(kv, kpe_blk):
    lo = _low_lanes(kpe_blk.shape)
    return [jnp.where(lo, kv[:, h * LANE:(h + 1) * LANE], kpe_blk).astype(BF16) for h in range(MLA_HEADS)]


def _mla_heads(q_of, k_of, v_of, gate_of, store):
    for pair in range(MLA_HEADS // 2):
        outs = [_softmax_pv([_dot_nt(q_of(h), k_of(h))], [v_of(h)]) for h in (2 * pair, 2 * pair + 1)]
        o = jnp.where(_low_lanes(outs[0].shape), pltpu.roll(outs[0], MLA_V, 1), outs[1])
        store(pair, (o * gate_of(pair)).astype(BF16))


def _ctx_attn_body(sink_ref, z_ref, wkv_ref, oa_ref, ob_ref, od_ref, e, layer):
    m = SEQ
    gate = lambda c0: _silu(z_ref[e, :, c0:c0 + LANE].astype(F32))

    def store(ref, pair, val):
        ref[e, :, pair * LANE:(pair + 1) * LANE] = val

    q4 = _stack_heads([z_ref[e, :, Z_AQ:Z_AQ + LANE], z_ref[e, :, Z_AQ + LANE:Z_AQ + 2 * LANE]])
    o4 = _softmax_pv([_dot_nt(q4, z_ref[e, :, Z_AK:Z_AV])], [z_ref[e, :, Z_AV:Z_DQ]],
                     sink=_sink_column(sink_ref, layer, m))
    for pair in range(2):
        store(oa_ref, pair, (_unstack_pair(o4, pair, m) * gate(Z_AG + pair * LANE)).astype(BF16))
    kv = _dot(z_ref[e, :, Z_CKV:Z_KPE], wkv_ref[...])
    keys = _mla_keys(kv, z_ref[e, :, Z_KPE:Z_QM].astype(F32))
    kvb = kv.astype(BF16)
    _mla_heads(lambda h: z_ref[e, :, Z_QM + LANE * h:Z_QM + LANE * (h + 1)], lambda h: keys[h],
               lambda h: kvb[:, LANE * h:LANE * (h + 1)], lambda p: gate(Z_BG + p * LANE),
               functools.partial(store, ob_ref))
    for pair in range(NAT_HEADS // 2):
        c = pair * LANE
        q2 = _stack_heads([z_ref[e, :, Z_DQ + c:Z_DQ + c + LANE]])
        o2 = _softmax_pv([_dot_nt(q2, z_ref[e, :, Z_DK + c:Z_DK + c + LANE])], [z_ref[e, :, Z_DV + c:Z_DV + c + LANE]])
        store(od_ref, pair, (_unstack_pair(o2, 0, m) * gate(Z_DG + c)).astype(BF16))


def _ctx_attn_rider(z, sink, wkv, layer, nt):
    def body(sink_ref, z_ref, wkv_ref, oa_ref, ob_ref, od_ref):
        for e in range(CTX_BPS):
            _ctx_attn_body(sink_ref, z_ref, wkv_ref, oa_ref, ob_ref, od_ref, e, layer)

    step = lambda b, t: (b * nt + t, 0, 0)
    return _Rider(
        body=body,
        in_specs=[pl.BlockSpec(memory_space=pltpu.SMEM), pl.BlockSpec((CTX_BPS, SEQ, Z_W), step),
                  _layer_spec((MLA_KV_LORA, 512), layer)],
        args=[sink, z, wkv],
        out_shape=[jax.ShapeDtypeStruct((BATCH, SEQ, BRANCH_W), BF16)] * 3,
        out_specs=[pl.BlockSpec((CTX_BPS, SEQ, BRANCH_W), step)] * 3,
        scratch=[])


def _win_attn_rows(sink_ref, q_ref, kv_ref, g_ref, ck_ref, cv_ref, ck_s, cv_s, dst_ref, layer):
    nb = DEC_SEQ // Q_BLOCK
    m = Q_BLOCK
    tile = pl.program_id(1)

    @pl.when(tile == 0)
    def _():
        ck_s[...] = ck_ref[0, 0].astype(BF16)
        cv_s[...] = cv_ref[0, 0].astype(BF16)

    sink = _sink_column(sink_ref, layer, m)
    shape = (A_HEADS * m, 3 * m)
    band = jnp.abs((lax.broadcasted_iota(jnp.int32, shape, 0) & (m - 1))
                   - (lax.broadcasted_iota(jnp.int32, shape, 1) - m)) <= A_WINDOW
    key_off = lax.broadcasted_iota(jnp.int32, (1, 3 * m), 1) - m
    for qb in range(TM // m):
        n = tile * (TM // m) + qb
        rows = slice(qb * m, (qb + 1) * m)
        blocks = []
        for off in (-1, 0, 1):
            start = pl.multiple_of(jnp.clip(n + off, 0, nb - 1) * m, m)
            blocks.append(kv_ref[0, pl.ds(start, m), :])
        kvw = jnp.concatenate(blocks, axis=0)
        kpos = n * m + key_off
        in_seq = jnp.where((kpos >= 0) & (kpos < DEC_SEQ), 0.0, NEG)
        q4 = _stack_heads([q_ref[0, rows, 0:LANE], q_ref[0, rows, LANE:2 * LANE]])
        s_win = jnp.where(band, _dot_nt(q4, kvw[:, 0:LANE]) + in_seq, NEG)
        s_ctx = _dot_nt(q4, ck_s[...])
        o4 = _softmax_pv([s_win, s_ctx], [kvw[:, LANE:2 * LANE], cv_s[...]], sink=sink)
        for pair in range(2):
            g = g_ref[0, rows, pair * LANE:(pair + 1) * LANE].astype(F32)
            dst_ref[rows, pair * LANE:(pair + 1) * LANE] = (_unstack_pair(o4, pair, m) * _silu(g)).astype(BF16)


def _nat_rows(q_ref, k_ref, v_ref, g_ref, ck_ref, cv_ref, ring_ref, ck_s, cv_s, bias_s, dst_ref):
    nrows = DEC_SEQ // GRID_W
    kr = NAT_ROWS
    m = NAT_RPS * GRID_W
    step = pl.program_id(1)

    @pl.when((pl.program_id(0) == 0) & (step == 0))
    def _():
        shape = (GRID_W, LANE)
        lane = lax.broadcasted_iota(jnp.int32, shape, 1)
        qcol = lax.broadcasted_iota(jnp.int32, shape, 0)
        rel = (lane & (GRID_W - 1)) - jnp.clip(qcol - NAT_COLS // 2, 0, GRID_W - NAT_COLS)
        col_ok = (rel >= 0) & (rel < NAT_COLS)
        for h in range(NAT_HEADS):
            for i in range(NAT_BIAS_ROWS):
                a = pltpu.roll(jnp.broadcast_to(ring_ref[h, i:i + 1, :], shape), 0, 1, stride=1, stride_axis=0)
                b = pltpu.roll(jnp.broadcast_to(ring_ref[h, i + 1:i + 2, :], shape), GRID_W, 1, stride=1,
                               stride_axis=0)
                bias_s[h, i] = jnp.where(col_ok, jnp.where(lane < GRID_W, a, b), NEG)

    @pl.when(step == 0)
    def _():
        ck_s[...] = ck_ref[0, 0].astype(BF16)
        cv_s[...] = cv_ref[0, 0].astype(BF16)

    key_row = lax.broadcasted_iota(jnp.int32, (1, LANE), 1) // GRID_W
    for grp in range(TM // m):
        row0 = (step * (TM // m) + grp) * NAT_RPS
        rows = slice(grp * m, (grp + 1) * m)
        ws = jnp.clip(row0 - kr // 2, 0, nrows - NAT_UNION)
        start = pl.multiple_of(ws * GRID_W, GRID_W)
        kl = k_ref[0, pl.ds(start, NAT_UNION * GRID_W), :]
        vl = v_ref[0, pl.ds(start, NAT_UNION * GRID_W), :]
        for pair in range(NAT_HEADS // 2):
            c = pair * LANE
            blocks = []
            for hh in range(2):
                for rr in range(NAT_RPS):
                    r = row0 + rr
                    lo = jnp.clip(r - kr // 2, 0, nrows - kr) - ws
                    ro = ws - r + (NAT_ROWS - 1) + NAT_BIAS_PAD
                    row = []
                    for t in range(NAT_UNION // 2):
                        kj = key_row + 2 * t
                        row.append(jnp.where((kj >= lo) & (kj < lo + kr), bias_s[2 * pair + hh, ro + 2 * t], NEG))
                    blocks.append(jnp.concatenate(row, axis=1))
            bias = jnp.concatenate(blocks, axis=0)
            q2 = _stack_heads([q_ref[0, rows, c:c + LANE]])
            s_lat = _dot_nt(q2, kl[:, c:c + LANE]) + bias
            s_ctx = _dot_nt(q2, ck_s[:, c:c + LANE])
            o2 = _softmax_pv([s_lat, s_ctx], [vl[:, c:c + LANE], cv_s[:, c:c + LANE]])
            g = g_ref[0, rows, c:c + LANE].astype(F32)
            dst_ref[rows, c:c + LANE] = (_unstack_pair(o2, 0, m) * _silu(g)).astype(BF16)


def _nat_bias_rings(rpb):
    nc = NAT_COLS - 1
    rep = lambda a, n: jnp.broadcast_to(a, a.shape[:-1] + (n,))
    ring = jnp.concatenate([rpb[..., nc:], rep(rpb[..., -1:], GRID_W - 1 - nc), rep(rpb[..., :1], GRID_W - nc),
                            rpb[..., :nc]], axis=-1)
    neg = lambda n: jnp.full(ring.shape[:2] + (n, 2 * GRID_W), NEG, F32)
    return jnp.concatenate([neg(NAT_BIAS_PAD), ring, neg(NAT_BIAS_PAD + 1)], axis=2)


def _lat_attn_rider(z, sink, ck_a, cv_a, ck_n, cv_n, rings, layer):
    kvw = A_KV_HEADS * HEAD_DIM
    hw = NAT_HEADS * HEAD_DIM

    def body(sink_ref, qa_ref, kva_ref, ga_ref, cka_ref, cva_ref, qd_ref, kd_ref, vd_ref, gd_ref, ckd_ref, cvd_ref,
             ring_ref, cka_s, cva_s, ckd_s, cvd_s, bias_s, ba_s, bd_s):
        _win_attn_rows(sink_ref, qa_ref, kva_ref, ga_ref, cka_ref, cva_ref, cka_s, cva_s, ba_s, layer)
        _nat_rows(qd_ref, kd_ref, vd_ref, gd_ref, ckd_ref, cvd_ref, ring_ref, ckd_s, cvd_s, bias_s, bd_s)

    tile = lambda col: pl.BlockSpec((1, TM, 256), lambda b, t: (b, t, col))
    whole = lambda col: pl.BlockSpec((1, DEC_SEQ, 256), lambda b, t: (b, 0, col))
    cache = lambda w: pl.BlockSpec((1, 1, PAST_LEN, w), lambda b, t: (b, layer, 0, 0))
    return _Rider(
        body=body,
        in_specs=[pl.BlockSpec(memory_space=pltpu.SMEM), tile(Z_AQ // 256), whole(Z_AK // 256), tile(Z_AG // 256),
                  cache(kvw), cache(kvw), tile(Z_DQ // 256), whole(Z_DK // 256), whole(Z_DV // 256),
                  tile(Z_DG // 256), cache(hw), cache(hw), _layer_spec(rings.shape[1:], layer)],
        args=[sink, z, z, z, ck_a, cv_a, z, z, z, z, ck_n, cv_n, rings],
        out_shape=[], out_specs=[],
        scratch=[pltpu.VMEM((PAST_LEN, kvw), BF16), pltpu.VMEM((PAST_LEN, kvw), BF16),
                 pltpu.VMEM((PAST_LEN, hw), BF16), pltpu.VMEM((PAST_LEN, hw), BF16),
                 pltpu.VMEM((NAT_HEADS, NAT_BIAS_ROWS, GRID_W, LANE), F32),
                 pltpu.VMEM((TM, BRANCH_W), BF16), pltpu.VMEM((TM, BRANCH_W), BF16)],
        branches=True)


def _mla_body(q_ref, ckv_ref, kpe_ref, g_ref, cckv_ref, ckpe_ref, wkv_ref, o_ref, k_s, v_s):
    nlat = DEC_SEQ
    step = pl.program_id(0) * pl.num_programs(1) + pl.program_id(1)

    @pl.when(step % MLA_TILES == 0)
    def _():
        rows = 512

        def fill(r0, ckv, kpe_blk):
            kv = _dot(ckv, wkv_ref[...])
            for h, kh in enumerate(_mla_keys(kv, kpe_blk)):
                k_s[h, r0:r0 + rows, :] = kh
                v_s[h, r0:r0 + rows, :] = kv[:, h * LANE:(h + 1) * LANE].astype(BF16)

        for c in range(nlat // rows):
            fill(c * rows, ckv_ref[0, c * rows:(c + 1) * rows, :], kpe_ref[0, c * rows:(c + 1) * rows, :].astype(F32))
        ckpe = ckpe_ref[0, 0]
        ckpe_blk = jnp.concatenate([jnp.zeros((PAST_LEN, KPE_LANE), F32), ckpe,
                                    jnp.zeros((PAST_LEN, LANE - KPE_LANE - MLA_ROPE), F32)], axis=1)
        fill(nlat, cckv_ref[0, 0].astype(BF16), ckpe_blk)

    def store(pair, val):
        o_ref[0, :, pair * LANE:(pair + 1) * LANE] = val

    _mla_heads(lambda h: q_ref[0, :, LANE * h:LANE * (h + 1)], lambda h: k_s[h], lambda h: v_s[h],
               lambda p: _silu(g_ref[0, :, p * LANE:(p + 1) * LANE].astype(F32)), store)


def _mla_rider(z, cache_ckv, cache_kpe, wkv, layer, nt):
    nkeys = DEC_SEQ + PAST_LEN
    assert DEC_BATCH * MLA_TILES == POST_STEPS
    bt = lambda b, t: ((b * nt + t) // MLA_TILES, (b * nt + t) % MLA_TILES)
    tile = lambda col: (lambda b, t: bt(b, t) + (col,))
    whole = lambda col: (lambda b, t: (bt(b, t)[0], 0, col))
    cache = lambda b, t: (bt(b, t)[0], layer, 0, 0)
    return _Rider(
        body=_mla_body,
        in_specs=[
            pl.BlockSpec((1, TQ_MLA, MLA_HEADS * LANE), tile(Z_QM // (MLA_HEADS * LANE))),
            pl.BlockSpec((1, DEC_SEQ, LANE), whole(Z_CKV // LANE)),
            pl.BlockSpec((1, DEC_SEQ, LANE), whole(Z_KPE // LANE)),
            pl.BlockSpec((1, TQ_MLA, 256), tile(Z_BG // 256)),
            pl.BlockSpec((1, 1, PAST_LEN, MLA_KV_LORA), cache),
            pl.BlockSpec((1, 1, PAST_LEN, MLA_ROPE), cache),
            _layer_spec((MLA_KV_LORA, 512), layer),
        ],
        args=[z, z, z, z, cache_ckv, cache_kpe, wkv],
        out_shape=[jax.ShapeDtypeStruct((DEC_BATCH, DEC_SEQ, BRANCH_W), BF16)],
        out_specs=[pl.BlockSpec((1, TQ_MLA, BRANCH_W), tile(0))],
        scratch=[pltpu.VMEM((MLA_HEADS, nkeys, LANE), BF16), pltpu.VMEM((MLA_HEADS, nkeys, LANE), BF16)])


def _s5_mats(lam_re, lam_im, log_dt, b_re, b_im, c_re, c_im):
    t = S5_T
    hp = lax.Precision.HIGHEST
    lre = jnp.minimum(lam_re, -1e-4)
    lim = lam_im
    dt = jnp.exp(log_dt)[..., None]
    er, ei = lre * dt, lim * dt
    mag = jnp.exp(er)
    are, aim = mag * jnp.cos(ei), mag * jnp.sin(ei)
    den = lre * lre + lim * lim
    qre = ((are - 1.0) * lre + aim * lim) / den
    qim = (aim * lre - (are - 1.0) * lim) / den
    bbr = qre[..., None] * b_re - qim[..., None] * b_im
    bbi = qre[..., None] * b_im + qim[..., None] * b_re
    n = jnp.arange(t + 1, dtype=F32)[:, None, None, None]
    pmag = jnp.exp(er[None] * n)
    pr, pi = pmag * jnp.cos(ei[None] * n), pmag * jnp.sin(ei[None] * n)

    def c_times_pow(d, pw_r, pw_i):
        rep = lambda a: jnp.repeat(jnp.transpose(a, (1, 2, 0)), S5_CH, axis=-1)
        til = lambda a: jnp.tile(jnp.transpose(a, (0, 2, 1)), (1, 1, t))
        cr, ci, ar, ai = til(c_re[d]), til(c_im[d]), rep(pw_r), rep(pw_i)
        return cr * ar - ci * ai, cr * ai + ci * ar

    def lag_kernels(d, wr, wi):
        return (jnp.einsum('gpk,gpx->gkx', bbr[d], wr, precision=hp)
                - jnp.einsum('gpk,gpx->gkx', bbi[d], wi, precision=hp))

    wf0 = c_times_pow(0, pr[:t, 0], pi[:t, 0])
    wf1 = c_times_pow(0, pr[1:, 0], pi[1:, 0])
    wb = c_times_pow(1, pr[1:, 1][::-1], pi[1:, 1][::-1])
    kb0 = (jnp.einsum('gpk,gcp->gkc', bbr[1], c_re[1], precision=hp)
           - jnp.einsum('gpk,gcp->gkc', bbi[1], c_im[1], precision=hp))
    kf = lag_kernels(0, *wf0) + jnp.pad(kb0, ((0, 0), (0, 0), (0, S5_TC - S5_CH)))
    kb = lag_kernels(1, *wb) * jnp.asarray(np.arange(S5_TC) >= S5_CH, F32)
    ring = jnp.concatenate([kf, kb], axis=-1)

    def state_cols(pw_r, pw_i, d):
        rep = lambda a: jnp.repeat(jnp.transpose(a, (1, 0, 2)), S5_CH, axis=1)
        til = lambda a: jnp.tile(jnp.transpose(a, (0, 2, 1)), (1, t, 1))
        ar, ai, br, bi = rep(pw_r), rep(pw_i), til(bbr[d]), til(bbi[d])
        return ar * br - ai * bi, ar * bi + ai * br

    sfr, sfi = state_cols(pr[:t, 0][::-1], pi[:t, 0][::-1], 0)
    sbr, sbi = state_cols(pr[:t, 1], pi[:t, 1], 1)
    m_state = jnp.concatenate([sfr, sbr, sfi, sbi], axis=-1)
    m_out = jnp.concatenate([wf1[0], wb[0], -wf1[1], -wb[1]], axis=1)

    a_t = jnp.stack([jnp.concatenate([pr[t, 0], pr[t, 1]], axis=-1),
                     jnp.concatenate([pi[t, 0], pi[t, 1]], axis=-1)], axis=1)
    return ring, m_state.astype(BF16), m_out.astype(BF16), a_t


def _s5_body(u_ref, ring_ref, ms_ref, mo_ref, at_ref, h0_ref, y_ref, hend_ref, s_s, hf_s, hb_s, m_s, *, nk, nb):
    p = S5_STATE
    u = u_ref[0]
    s = _dot(u, ms_ref[0])
    s_s[0] = s[:, 0:2 * p]
    s_s[1] = s[:, 2 * p:4 * p]
    are = at_ref[0, 0:1, :]
    aim = at_ref[0, 1:2, :]
    fwd_lane = lax.broadcasted_iota(jnp.int32, (1, 2 * p), 1) < p

    for k in range(S5_CH):
        rk = jnp.broadcast_to(ring_ref[0, k:k + 1, :], (S5_T, 2 * S5_TC))
        rolled = pltpu.roll(rk, 0, 1, stride=S5_CH, stride_axis=0)
        for blk in range(S5_TC // LANE):
            m_s[blk, pl.ds(k, S5_T, stride=S5_CH), :] = rolled[:, blk * LANE:(blk + 1) * LANE]

    if h0_ref is None:
        hre = him = jnp.zeros((nb, 2 * p), F32)
    else:
        hre, him = h0_ref[0, :, 0:2 * p], h0_ref[0, :, 2 * p:4 * p]
    for st in range(nk):
        rf = pl.ds(st, nb, stride=nk)
        rb = pl.ds(nk - 1 - st, nb, stride=nk)
        hf_s[0, rf, :] = hre
        hf_s[1, rf, :] = him
        hb_s[0, rb, :] = hre
        hb_s[1, rb, :] = him
        sre = jnp.where(fwd_lane, s_s[0, rf, :], s_s[0, rb, :])
        sim = jnp.where(fwd_lane, s_s[1, rf, :], s_s[1, rb, :])
        hre, him = are * hre - aim * him + sre, are * him + aim * hre + sim
    if hend_ref is not None:
        hend_ref[0] = jnp.concatenate([hre, him], axis=1)
    hst = jnp.concatenate([jnp.where(fwd_lane, hf_s[0], hb_s[0]), jnp.where(fwd_lane, hf_s[1], hb_s[1])], axis=1)
    m_intra = jnp.concatenate([m_s[blk] for blk in range(S5_TC // LANE)], axis=1).astype(BF16)
    y_ref[0] = _dot(u, m_intra) + _dot(hst.astype(BF16), mo_ref[0])


def _s5_rider(ut, ring, m_state, m_out, a_t, h0, layer, nt):
    assert S5_GROUPS == POST_STEPS
    ctx = h0 is None
    g3 = lambda b, t: (b * nt + t, 0, 0)
    lg = lambda shape: pl.BlockSpec((None, 1) + shape, lambda b, t: (layer, b * nt + t, 0, 0))
    uspec = pl.BlockSpec((1, S5_ROWS, S5_TC), g3)

    def body(*refs):
        if ctx:
            u, rg, ms, mo, at, y, hend, s_s, hf_s, hb_s, m_s = refs
            _s5_body(u, rg, ms, mo, at, None, y, hend, s_s, hf_s, hb_s, m_s, nk=S5_NK_CTX, nb=BATCH)
        else:
            u, rg, ms, mo, at, h0_ref, y, s_s, hf_s, hb_s, m_s = refs
            _s5_body(u, rg, ms, mo, at, h0_ref, y, None, s_s, hf_s, hb_s, m_s, nk=S5_NK_LAT, nb=DEC_BATCH)

    return _Rider(
        body=body,
        in_specs=[uspec, lg((S5_CH, 2 * S5_TC)), lg((S5_TC, 4 * S5_STATE)), lg((4 * S5_STATE, S5_TC)),
                  lg((2, 2 * S5_STATE))] + ([] if ctx else [lg((DEC_BATCH, 4 * S5_STATE))]),
        args=[ut, ring, m_state, m_out, a_t] + ([] if ctx else [h0]),
        out_shape=[jax.ShapeDtypeStruct((S5_GROUPS, S5_ROWS, S5_TC), F32)]
        + ([jax.ShapeDtypeStruct((S5_GROUPS, BATCH, 4 * S5_STATE), F32)] if ctx else []),
        out_specs=[uspec] + ([pl.BlockSpec((1, BATCH, 4 * S5_STATE), g3)] if ctx else []),
        scratch=[pltpu.VMEM((2, S5_ROWS, LANE), F32)] * 3 + [pltpu.VMEM((S5_TC // LANE, S5_TC, LANE), F32)])


_POST_IN = ('x', 'mod', 'ng', 'ba', 'bb', 'bd', 'y', 'u', 'cg', 'sd', 'wglu', 'wm', 'wbr', 'wo', 'fg')


class _Rider(NamedTuple):
    body: Callable
    in_specs: list
    args: list
    out_shape: list
    out_specs: list
    scratch: list
    branches: bool = False


NO_RIDER = _Rider(body=lambda: None, in_specs=[], args=[], out_shape=[], out_specs=[], scratch=[])


def _both_riders(r1, r2):
    i1, i2, o1, o2, s1 = len(r1.in_specs), len(r2.in_specs), len(r1.out_specs), len(r2.out_specs), len(r1.scratch)

    def body(*refs):
        ins, outs, scr = refs[:i1 + i2], refs[i1 + i2:i1 + i2 + o1 + o2], refs[i1 + i2 + o1 + o2:]
        r1.body(*ins[:i1], *outs[:o1], *scr[:s1])
        r2.body(*ins[i1:], *outs[o1:], *scr[s1:])

    return _Rider(body=body, in_specs=r1.in_specs + r2.in_specs, args=r1.args + r2.args,
                  out_shape=r1.out_shape + r2.out_shape, out_specs=r1.out_specs + r2.out_specs,
                  scratch=r1.scratch + r2.scratch)


def _post_kernel(*refs, final, rider):
    names = [n for n in _POST_IN if not (rider.branches and n in ('ba', 'bd'))]
    r = dict(zip(names, refs))
    n0, n_in, n_out = len(names), len(rider.in_specs), len(rider.out_specs)
    o_ref = refs[n0 + n_in]
    ys_s = refs[n0 + n_in + 1 + n_out]
    rider_scratch = refs[n0 + n_in + 2 + n_out:]
    rider.body(*refs[n0:n0 + n_in], *refs[n0 + n_in + 1:n0 + n_in + 1 + n_out], *rider_scratch)
    ba, bd = (rider_scratch[-2][...], rider_scratch[-1][...]) if rider.branches else (r['ba'][0], r['bd'][0])
    x_ref, mod_ref, ng_ref, y_ref = r['x'], r['mod'], r['ng'], r['y']
    x = x_ref[0]
    hb = _ada_h(x, mod_ref, ng_ref).astype(BF16)
    for m in range(S5_T // 8):
        for hf in range(2):
            vs = _slot_transpose([y_ref[8 * hf + gp, :, m * LANE:(m + 1) * LANE] for gp in range(8)])
            for s in range(8):
                ys_s[hf, pl.ds(8 * m + s, TM_CHUNKS, stride=S5_T), :] = vs[s]
    yv = jnp.concatenate([ys_s[0], ys_s[1]], axis=1) + r['sd'][...] * r['u'][0].astype(F32)
    gel = 0.5 * yv * (1.0 + jnp.tanh(math.sqrt(2.0 / math.pi) * (yv + 0.044715 * (yv * yv * yv))))
    gl = _dot(gel.astype(BF16), r['wglu'][...])
    oc = gl[:, 0:BRANCH_W] * _sigmoid(gl[:, BRANCH_W:2 * BRANCH_W])
    bc = (oc * _silu(r['cg'][0].astype(F32))).astype(BF16)
    acc = None
    for k, br in enumerate((ba, r['bb'][0], bc, bd)):
        proj = _dot(br, r['wbr'][k])
        mg = _sigmoid(_dot(hb, r['wm'][:, k * D_MODEL:(k + 1) * D_MODEL]))
        acc = mg * proj if acc is None else acc + mg * proj
    y = _dot(acc.astype(BF16), r['wo'][...])
    out = x + mod_ref[0, :, 2 * D_MODEL:3 * D_MODEL] * y
    if final:
        out = _rms(out, r['fg'][...])
    o_ref[0] = out


def _post_call(x, mods, mod_row0, p, ba, bb, bd, y_s5, z, fg, final, layer, name, rider):
    nb, length, _ = x.shape
    nt = length // TM
    assert nb * nt == POST_STEPS
    tok = lambda b, t: (b, t, 0)
    br_spec = pl.BlockSpec((1, TM, BRANCH_W), tok)
    branch_args = [bb] if rider.branches else [ba, bb, bd]
    return pl.pallas_call(
        functools.partial(_post_kernel, final=final, rider=rider),
        out_shape=tuple([jax.ShapeDtypeStruct((nb, length, D_MODEL), F32)] + rider.out_shape),
        grid=(nb, nt),
        in_specs=[
            pl.BlockSpec((1, TM, D_MODEL), tok),
            _mod_spec(layer, mod_row0),
            _layer_spec((1, D_MODEL), layer),
        ] + [br_spec] * len(branch_args) + [
            pl.BlockSpec((S5_GROUPS, TM_CHUNKS, S5_TC), lambda b, t: (0, b * nt + t, 0)),
            pl.BlockSpec((1, TM, 256), lambda b, t: (b, t, Z_CU // 256)),
            pl.BlockSpec((1, TM, 256), lambda b, t: (b, t, Z_CG // 256)),
            _layer_spec((1, BRANCH_W), layer),
            _layer_spec((BRANCH_W, 2 * BRANCH_W), layer),
            _layer_spec((D_MODEL, N_BRANCH * D_MODEL), layer),
            _layer_spec((N_BRANCH, BRANCH_W, D_MODEL), layer),
            _layer_spec((D_MODEL, D_MODEL), layer),
            pl.BlockSpec((1, D_MODEL), lambda b, t: (0, 0)),
        ] + rider.in_specs,
        out_specs=tuple([pl.BlockSpec((1, TM, D_MODEL), tok)] + rider.out_specs),
        scratch_shapes=[pltpu.VMEM((2, TM, LANE), F32)] + rider.scratch,
        compiler_params=_cparams(("arbitrary", "arbitrary"), 56),
        name=name,
    )(x, mods, p['ng'], *branch_args, y_s5, z, z, p['s5d'], p['wglu'], p['wm'], p['wbr'], p['wo'], fg, *rider.args)


def _rope_tables():
    t = np.arange(DEC_SEQ)
    row = (t // GRID_W).astype(np.float64)
    col = (t % GRID_W).astype(np.float64)

    def pattern(half):
        inv = ROPE_BASE ** (-np.arange(half, dtype=np.float64) / half)
        zeros = np.zeros((DEC_SEQ, half))
        cs, s_up, s_lo = [], [], []
        for pos in (row, col):
            ang = pos[:, None] * inv[None, :]
            c, s = np.cos(ang), np.sin(ang)
            cs += [c, c]
            s_up += [zeros, s]
            s_lo += [-s, zeros]
        return [np.concatenate(parts, axis=1) for parts in (cs, s_up, s_lo)]

    tab_a = np.stack([np.tile(part, (1, LANE // HEAD_DIM)) for part in pattern(HEAD_DIM // 4)])
    ident = [np.ones, np.zeros, np.zeros]
    tab_m = np.stack([np.concatenate([fill((DEC_SEQ, KPE_LANE)), part,
                                      fill((DEC_SEQ, LANE - KPE_LANE - MLA_ROPE))], axis=1)
                      for fill, part in zip(ident, pattern(MLA_ROPE // 4))])
    return jnp.asarray(tab_a, F32), jnp.asarray(tab_m, F32)


_W_IN_MOVES = ((256, 256, 256), (512, 1952, 768), (R_CQ, 768, 384), (R_CU, 1440, 256),
               (R_G + 256, 1184, 256), (R_G + 512, 1696, 256), (R_G + 768, 2720, 256))
_W_IN_HEAD_MOVES = ((0, 0), (R_G, 512))
_W_IN_KPE = 1152
W_IN_ROWS = 256


def _w_in_kernel(wt_ref, o_ref):
    take = lambda src, width: wt_ref[0, src:src + width, :].T
    for dst, src, width in _W_IN_MOVES:
        o_ref[0, :, dst:dst + width] = take(src, width).astype(BF16)
    for dst, src in _W_IN_HEAD_MOVES:
        blk = take(src, A_HEADS * HEAD_DIM)
        for i, h in enumerate(A_HEAD_ORDER):
            o_ref[0, :, dst + HEAD_DIM * i:dst + HEAD_DIM * (i + 1)] = blk[:, HEAD_DIM * h:HEAD_DIM * (h + 1)].astype(BF16)
    o_ref[0, :, R_KPE:R_CU] = jnp.zeros((W_IN_ROWS, LANE), BF16)
    o_ref[0, :, R_KPE + KPE_LANE:R_KPE + KPE_LANE + MLA_ROPE] = take(_W_IN_KPE, LANE)[:, 0:MLA_ROPE].astype(BF16)


def _w_in_call(w_in):
    wt = jnp.swapaxes(w_in, 1, 2)
    return pl.pallas_call(
        _w_in_kernel,
        out_shape=jax.ShapeDtypeStruct((DEPTH, D_MODEL, R_W), BF16),
        grid=(DEPTH, D_MODEL // W_IN_ROWS),
        in_specs=[pl.BlockSpec((1, wt.shape[1], W_IN_ROWS), lambda i, r: (i, 0, r))],
        out_specs=pl.BlockSpec((1, W_IN_ROWS, R_W), lambda i, r: (i, r, 0)),
        compiler_params=_cparams(("arbitrary", "arbitrary"), 32),
        name="w_in_layout",
    )(wt)


def _cast_rider(ws):
    flat = [w.reshape(-1, w.shape[-1]) for w in ws]

    def body(*refs):
        for w_ref, o_ref in zip(refs[:len(ws)], refs[len(ws):]):
            o_ref[...] = w_ref[...].astype(BF16)

    specs = [pl.BlockSpec((w.shape[0] // POST_STEPS, w.shape[1]), lambda b, t: (t, 0)) for w in flat]
    return _Rider(body=body, in_specs=specs, args=flat,
                  out_shape=[jax.ShapeDtypeStruct(w.shape, BF16) for w in flat], out_specs=specs, scratch=[])


def _reorder_w_q_up(w):
    w = w.reshape(MLA_Q_LORA, MLA_HEADS, MLA_NOPE + MLA_ROPE)
    w = jnp.concatenate([w, jnp.zeros((MLA_Q_LORA, MLA_HEADS, LANE - MLA_NOPE - MLA_ROPE), w.dtype)], axis=-1)
    return w.reshape(MLA_Q_LORA, MLA_HEADS * LANE)


def _reorder_w_branch(w):
    wa = w[0].reshape(A_HEADS, HEAD_DIM, D_MODEL)
    wa = jnp.concatenate([wa[h] for h in A_HEAD_ORDER], axis=0)
    return jnp.concatenate([wa[None], w[1:]], axis=0)


def kernel(x_prompt, x_sample, cache_a_k, cache_a_v, cache_mla_ckv, cache_mla_kpe, cache_na_k, cache_na_v,
           state_s5_re, state_s5_im, c, c_ctx, w_mod, b_mod, norm_g, w_in, w_merge, a_sink,
           mla_q_norm, mla_w_q_up, mla_kv_norm, mla_w_kv_up, s5_lam_re, s5_lam_im, s5_log_dt,
           s5_b_re, s5_b_im, s5_c_re, s5_c_im, s5_d, s5_w_glu, na_rpb, w_branch, w_out, final_norm_g):
    n_ctx = BATCH * SEQ
    conds = jnp.concatenate([c, c_ctx[None, :], jnp.zeros((3, D_MODEL), F32)], axis=0)
    mods = _mod_call(conds, w_mod, b_mod).reshape(DEPTH, 8, 1, 3 * D_MODEL)
    tabs = _rope_tables()
    ck_a = cache_a_k.reshape(DEC_BATCH, DEPTH, PAST_LEN, A_KV_HEADS * HEAD_DIM)
    cv_a = cache_a_v.reshape(DEC_BATCH, DEPTH, PAST_LEN, A_KV_HEADS * HEAD_DIM)
    ck_n = cache_na_k.reshape(DEC_BATCH, DEPTH, PAST_LEN, NAT_HEADS * HEAD_DIM)
    cv_n = cache_na_v.reshape(DEC_BATCH, DEPTH, PAST_LEN, NAT_HEADS * HEAD_DIM)
    fg = final_norm_g.reshape(1, D_MODEL)
    caches = None

    p = dict(
        ng=norm_g.reshape(DEPTH, 1, D_MODEL),
        w_raw=_w_in_call(w_in),
        qnorm=mla_q_norm.reshape(DEPTH, 1, MLA_Q_LORA),
        wq=jax.vmap(_reorder_w_q_up)(mla_w_q_up).astype(BF16),
        kvnorm=mla_kv_norm.reshape(DEPTH, 1, MLA_KV_LORA),
        wkv=mla_w_kv_up.astype(BF16),
        wbr=jax.vmap(_reorder_w_branch)(w_branch).astype(BF16),
        wglu=s5_w_glu.astype(BF16),
        s5d=s5_d.reshape(DEPTH, 1, BRANCH_W),
    )
    ring, m_state, m_out, a_t = jax.vmap(_s5_mats)(s5_lam_re, s5_lam_im, s5_log_dt, s5_b_re, s5_b_im, s5_c_re, s5_c_im)
    nat_bias = _nat_bias_rings(na_rpb * LOG2E)
    h0 = jnp.concatenate([state_s5_re[:, :, 0], state_s5_re[:, :, 1], state_s5_im[:, :, 0], state_s5_im[:, :, 1]],
                         axis=-1)
    h0 = jnp.transpose(h0, (1, 2, 0, 3))

    yp = x_prompt.reshape(1, n_ctx, D_MODEL)
    ys = x_sample
    hends = []
    for i in range(DEPTH):
        rider = _cast_rider([w_merge, w_out]) if i == 0 else NO_RIDER
        res = _inproj_call(yp, mods, DEC_BATCH, p, None, caches, i, "inproj_ctx", rider)
        z_ctx, ut_ctx, caches = res[0], res[1], list(res[2:2 + N_CACHE])
        if i == 0:
            p['wm'], p['wo'] = res[2 + N_CACHE].reshape(w_merge.shape), res[3 + N_CACHE].reshape(w_out.shape)
        zc = z_ctx.reshape(BATCH, SEQ, Z_W)
        nt_lat = DEC_SEQ // TM
        rider = _both_riders(_ctx_attn_rider(zc, a_sink, p['wkv'], i, nt_lat),
                             _s5_rider(ut_ctx, ring, m_state, m_out, a_t, None, i, nt_lat))
        z_lat, ut_lat, oa_c, ob_c, od_c, y_ctx, hend = _inproj_call(ys, mods, 0, p, tabs, None, i,
                                                                    "inproj_lat_ctx_attn_s5", rider)
        hends.append(hend)

        flat = lambda a: a.reshape(1, n_ctx, BRANCH_W)
        final = i == DEPTH - 1
        rider = _both_riders(_mla_rider(z_lat, cache_mla_ckv, cache_mla_kpe, p['wkv'], i, POST_STEPS),
                             _s5_rider(ut_lat, ring, m_state, m_out, a_t, h0, i, POST_STEPS))
        yp, ob_l, y_lat = _post_call(yp, mods, DEC_BATCH, p, flat(oa_c), flat(ob_c), flat(od_c), y_ctx, z_ctx, fg, final,
                                     i, "post_ctx_lat_mla_s5", rider)
        ys, = _post_call(ys, mods, 0, p, None, ob_l, None, y_lat, z_lat, fg, final, i, "post_lat_win_nat",
                         _lat_attn_rider(z_lat, a_sink, ck_a, cv_a, ck_n, cv_n, nat_bias, i))

    hend = jnp.stack(hends).reshape(DEPTH, S5_GROUPS, BATCH, 2, 2, S5_STATE)
    hend = jnp.transpose(hend, (3, 2, 0, 4, 1, 5))
    ak, av, ckv, kpe, nk, nv = caches
    heads = lambda a, h: a.reshape(BATCH, DEPTH, SEQ, h, HEAD_DIM)
    return (yp.reshape(BATCH, SEQ, D_MODEL), ys, heads(ak, A_KV_HEADS), heads(av, A_KV_HEADS), ckv, jnp.swapaxes(kpe, 2, 3),
            heads(nk, NAT_HEADS), heads(nv, NAT_HEADS), hend[0], hend[1])
```
